```python
import jax
import jax.numpy as jnp
from jax import lax
import numpy as np

D_MODEL = 1024
BATCH = 8
SEQ = 2048
DEPTH = 1
DEC_BATCH = 32
DEC_SEQ = 64
PAST_LEN = 4096

CHUNK = 64
MIX_WIDTH = D_MODEL
A_WIDTH = MIX_WIDTH // 2
A_HEAD_DIM = 64
A_HEADS = A_WIDTH // A_HEAD_DIM
A_KV_HEADS = A_HEADS // 4
A_GROUP = A_HEADS // A_KV_HEADS
A_KV_WIDTH = A_KV_HEADS * A_HEAD_DIM
WINDOW = 128
WIN_CHUNKS = WINDOW // CHUNK
ROT_DIM = A_HEAD_DIM // 4
ROPE_THETA = 500000.0
M_WIDTH = MIX_WIDTH - A_WIDTH
M_HEADS = 4
M_HEAD_DIM = M_WIDTH // M_HEADS
CONV_WIDTH = 4
N_GROUPS = 4
EXPERTS_PER_GROUP = 8
N_EXPERTS = N_GROUPS * EXPERTS_PER_GROUP
TOP_K_INNER = 2
D_EXPERT = D_MODEL // 2
MOE_BLOCK = 128
EPS = 1e-6
SPLIT_SIZES = (A_WIDTH, A_KV_WIDTH, A_KV_WIDTH, 2 * M_WIDTH, M_WIDTH, M_HEADS, M_HEADS, M_WIDTH)
SPLIT_OFFSETS = tuple(int(o) for o in np.cumsum(SPLIT_SIZES)[:-1])
D_IN = sum(SPLIT_SIZES)

kernel_name = 'stream_hybrid_swa_sink_mlstm_hmoe_step'


def rmsnorm(x, g):
    xf = x.astype(jnp.float32)
    y = xf * lax.rsqrt(jnp.mean(xf * xf, axis=-1, keepdims=True) + EPS)
    return (y * g.astype(jnp.float32)).astype(x.dtype)


def modulation(c, w_ada, b_ada):
    mod = (jax.nn.silu(c) @ w_ada + b_ada)[:, None, :]
    return jnp.split(mod, 6, axis=-1)


def ada_norm(x, g, shift, scale):
    return rmsnorm(x, g) * (1 + scale) + shift


def rope_partial(x, pos):
    inv_freq = ROPE_THETA ** (-jnp.arange(0, ROT_DIM, 2, dtype=jnp.float32) / ROT_DIM)
    ang = pos.astype(jnp.float32)[:, None] * inv_freq[None, :]
    cos = jnp.cos(ang)[None, :, None, :]
    sin = jnp.sin(ang)[None, :, None, :]
    xr = x[..., :ROT_DIM].astype(jnp.float32)
    x1, x2 = xr[..., :ROT_DIM // 2], xr[..., ROT_DIM // 2:]
    rot = jnp.concatenate([x1 * cos - x2 * sin, x2 * cos + x1 * sin], axis=-1)
    return jnp.concatenate([rot.astype(x.dtype), x[..., ROT_DIM:]], axis=-1)


def project(h, w_in, pos):
    b, L, _ = h.shape
    q, k, v, qk_m, v_m, ig, fg, o_m = jnp.split(h @ w_in, SPLIT_OFFSETS, axis=-1)
    q = rope_partial(q.reshape(b, L, A_HEADS, A_HEAD_DIM), pos).reshape(b, L, A_KV_HEADS, A_GROUP, A_HEAD_DIM)
    k = rope_partial(k.reshape(b, L, A_KV_HEADS, A_HEAD_DIM), pos)
    v = v.reshape(b, L, A_KV_HEADS, A_HEAD_DIM)
    return q, k, v, qk_m, v_m, ig, fg, o_m


def sink_attend(q, k, v, sinks, mask):
    s = jnp.einsum('...qhgd,...khd->...hgqk', q, k).astype(jnp.float32) * (A_HEAD_DIM ** -0.5)
    if mask is not None:
        s = jnp.where(mask, s, -jnp.inf)
    snk = sinks.astype(jnp.float32).reshape(A_KV_HEADS, A_GROUP, 1, 1)
    mx = jnp.maximum(jnp.max(s, axis=-1, keepdims=True), snk)
    p = jnp.exp(s - mx)
    p = p / (jnp.sum(p, axis=-1, keepdims=True) + jnp.exp(snk - mx))
    return jnp.einsum('...hgqk,...khd->...qhgd', p.astype(v.dtype), v)


def band_attention(q, k, v, sinks):
    b, s_len = q.shape[:2]
    n = s_len // CHUNK
    nk = (WIN_CHUNKS + 1) * CHUNK

    def bands(a):
        ap = jnp.pad(a, ((0, 0), (WIN_CHUNKS * CHUNK, 0), (0, 0), (0, 0)))
        ap = ap.reshape(b, n + WIN_CHUNKS, CHUNK, A_KV_HEADS, A_HEAD_DIM)
        return jnp.concatenate([ap[:, j:j + n] for j in range(WIN_CHUNKS + 1)], axis=2)

    qb = q.reshape(b, n, CHUNK, A_KV_HEADS, A_GROUP, A_HEAD_DIM)
    key_pos = jnp.arange(n)[:, None] * CHUNK + jnp.arange(nk)[None, :] - WIN_CHUNKS * CHUNK
    mask = (key_pos >= 0)[None, :, None, None, None, :]
    out = sink_attend(qb, bands(k), bands(v), sinks, mask)
    return out.reshape(b, s_len, A_WIDTH)


def causal_conv(x, past, w, bias):
    L = x.shape[1]
    xp = jnp.concatenate([past.astype(x.dtype), x], axis=1)
    y = bias + xp[:, 0:L] * w[0]
    for j in range(1, CONV_WIDTH):
        y = y + xp[:, j:j + L] * w[j]
    return y, xp[:, -(CONV_WIDTH - 1):]


def mlstm_prepare(qk_conv, v_m, ig, fg, b_ig, b_fg):
    b, L = v_m.shape[:2]
    q, k = jnp.split(jax.nn.silu(qk_conv).astype(jnp.float32), 2, axis=-1)
    q = q.reshape(b, L, M_HEADS, M_HEAD_DIM) * (M_HEAD_DIM ** -0.5)
    k = k.reshape(b, L, M_HEADS, M_HEAD_DIM)
    v = v_m.astype(jnp.float32).reshape(b, L, M_HEADS, M_HEAD_DIM)
    i_log = ig.astype(jnp.float32) + b_ig.astype(jnp.float32)
    f_log = jax.nn.log_sigmoid(fg.astype(jnp.float32) + b_fg.astype(jnp.float32))
    return q, k, v, i_log, f_log


def mlstm_chunk(C, n, m, q, k, v, i_log, f_log):
    L = q.shape[1]
    bcum = jnp.cumsum(f_log, axis=1).swapaxes(1, 2)
    it = i_log.swapaxes(1, 2)
    causal = jnp.tril(jnp.ones((L, L), dtype=bool))
    dmat = jnp.where(causal, bcum[..., :, None] - bcum[..., None, :] + it[..., None, :], -jnp.inf)
    inter = bcum + m[..., None]
    m_t = jnp.maximum(inter, jnp.max(dmat, axis=-1))
    w_intra = jnp.exp(dmat - m_t[..., None])
    w_inter = jnp.exp(inter - m_t)
    scores = jnp.einsum('blhd,bshd->bhls', q, k) * w_intra
    num = (jnp.einsum('bhls,bshd->blhd', scores, v)
           + w_inter.swapaxes(1, 2)[..., None] * jnp.einsum('bhvd,blhd->blhv', C, q))
    den = jnp.sum(scores, axis=-1) + w_inter * jnp.einsum('bhd,blhd->bhl', n, q)
    h = num / jnp.maximum(jnp.abs(den), jnp.exp(-m_t)).swapaxes(1, 2)[..., None]
    m_new = m_t[..., -1]
    w_end = jnp.exp(bcum[..., -1:] - bcum + it - m_new[..., None])
    decay = jnp.exp(bcum[..., -1] + m - m_new)
    kw = k * w_end.swapaxes(1, 2)[..., None]
    C_new = decay[..., None, None] * C + jnp.einsum('bshv,bshd->bhvd', v, kw)
    n_new = decay[..., None] * n + jnp.sum(kw, axis=1)
    return h, C_new, n_new, m_new


def mlstm_prompt(q, k, v, i_log, f_log):
    b, s_len = q.shape[:2]
    nc = s_len // CHUNK

    def chunks(a):
        return a.reshape((b, nc, CHUNK) + a.shape[2:]).swapaxes(0, 1)

    init = (jnp.zeros((b, M_HEADS, M_HEAD_DIM, M_HEAD_DIM), jnp.float32),
            jnp.zeros((b, M_HEADS, M_HEAD_DIM), jnp.float32),
            jnp.zeros((b, M_HEADS), jnp.float32))

    def step(carry, xs):
        h, C, nv, m = mlstm_chunk(*carry, *xs)
        return (C, nv, m), h

    (C, nv, m), hs = lax.scan(step, init, (chunks(q), chunks(k), chunks(v), chunks(i_log), chunks(f_log)))
    return hs.swapaxes(0, 1).reshape(b, s_len, M_HEADS, M_HEAD_DIM), C, nv, m


def mlstm_output(h, o_pre, g_mh, dtype):
    b, L = h.shape[:2]
    hn = h * lax.rsqrt(jnp.mean(h * h, axis=-1, keepdims=True) + EPS)
    hn = hn.reshape(b, L, M_WIDTH) * g_mh.astype(jnp.float32)
    return (hn * jax.nn.sigmoid(o_pre.astype(jnp.float32))).astype(dtype)


def hier_moe(h, w_rg, b_rg, w_re, b_re, w_g, w_u, w_d):
    b, L, d = h.shape
    t = h.reshape(b * L, d)
    n_tok = t.shape[0]
    lg = (t @ w_rg).astype(jnp.float32) + b_rg.astype(jnp.float32)
    grp = jnp.argmax(lg, axis=-1)
    p_grp = jnp.take_along_axis(jax.nn.softmax(lg, axis=-1), grp[:, None], axis=-1)
    le = ((t @ w_re).astype(jnp.float32) + b_re.astype(jnp.float32)).reshape(n_tok, N_GROUPS, EXPERTS_PER_GROUP)
    le = jnp.take_along_axis(le, grp[:, None, None], axis=1)[:, 0]
    top_v, top_i = lax.top_k(le, TOP_K_INNER)
    gates = (p_grp * jax.nn.softmax(top_v, axis=-1)).reshape(-1)
    eid = (grp[:, None] * EXPERTS_PER_GROUP + top_i).reshape(-1)
    m_rows = eid.shape[0]
    order = jnp.argsort(eid)
    e_sorted = eid[order]
    counts = jnp.bincount(eid, length=N_EXPERTS)
    padded = (counts + MOE_BLOCK - 1) // MOE_BLOCK * MOE_BLOCK
    pad_end = jnp.cumsum(padded)
    dest = (pad_end - padded)[e_sorted] + jnp.arange(m_rows) - (jnp.cumsum(counts) - counts)[e_sorted]
    n_blocks = -(-m_rows // MOE_BLOCK) + N_EXPERTS
    rows = n_blocks * MOE_BLOCK
    src_tok = order // TOP_K_INNER
    xs = jnp.zeros((rows, d), t.dtype).at[dest].set(t[src_tok])
    blk_e = jnp.minimum(jnp.searchsorted(pad_end, jnp.arange(n_blocks) * MOE_BLOCK, side='right'), N_EXPERTS - 1)

    def expert_block(args):
        xb, e = args
        a = jax.nn.silu(xb @ w_g[e]) * (xb @ w_u[e])
        return a @ w_d[e]

    out = lax.map(expert_block, (xs.reshape(n_blocks, MOE_BLOCK, d), blk_e)).reshape(rows, d)
    out_rows = out[dest] * gates[order][:, None].astype(out.dtype)
    y = jnp.zeros_like(t).at[src_tok].add(out_rows)
    return y.reshape(b, L, d)


def residual_update(x, mods, att, hm, o_m, g_mh, w_out, g_ffn, moe_w):
    mixed = jnp.concatenate([att, mlstm_output(hm, o_m, g_mh, x.dtype)], axis=-1) @ w_out
    x = x + mods[2] * mixed
    return x + mods[5] * hier_moe(ada_norm(x, g_ffn, mods[3], mods[4]), *moe_w)


def setup_inputs(seed: int = 0) -> dict:
    key = jax.random.key(seed)
    ks = iter(jax.random.split(key, 32))

    def nrm(shape, scale):
        return jax.random.normal(next(ks), shape, jnp.float32) * scale

    return {
        'x_prompt': nrm((BATCH, SEQ, D_MODEL), 1.0),
        'x_sample': nrm((DEC_BATCH, DEC_SEQ, D_MODEL), 1.0),
        'c_prompt': nrm((BATCH, D_MODEL), 1.0),
        'c_sample': nrm((DEC_BATCH, D_MODEL), 1.0),
        'cache_win_k': nrm((DEPTH, DEC_BATCH, WINDOW, A_KV_HEADS, A_HEAD_DIM), 1.0),
        'cache_win_v': nrm((DEPTH, DEC_BATCH, WINDOW, A_KV_HEADS, A_HEAD_DIM), 1.0),
        'state_conv': nrm((DEPTH, DEC_BATCH, CONV_WIDTH - 1, 2 * M_WIDTH), 1.0),
        'state_C': nrm((DEPTH, DEC_BATCH, M_HEADS, M_HEAD_DIM, M_HEAD_DIM), 0.05),
        'state_n': nrm((DEPTH, DEC_BATCH, M_HEADS, M_HEAD_DIM), 0.2),
        'state_m': nrm((DEPTH, DEC_BATCH, M_HEADS), 1.0),
        'w_ada': nrm((DEPTH, D_MODEL, 6 * D_MODEL), 0.5 * D_MODEL ** -0.5),
        'b_ada': nrm((DEPTH, 6 * D_MODEL), 0.01),
        'g_norm_mix': 1.0 + nrm((DEPTH, D_MODEL), 0.05),
        'g_norm_ffn': 1.0 + nrm((DEPTH, D_MODEL), 0.05),
        'w_in': nrm((DEPTH, D_MODEL, D_IN), D_MODEL ** -0.5),
        'attn_sinks': nrm((DEPTH, A_HEADS), 1.0),
        'conv_w': nrm((DEPTH, CONV_WIDTH, 2 * M_WIDTH), CONV_WIDTH ** -0.5),
        'conv_b': nrm((DEPTH, 2 * M_WIDTH), 0.01),
        'b_igate': nrm((DEPTH, M_HEADS), 0.1),
        'b_fgate': jnp.linspace(3.0, 6.0, M_HEADS)[None, :] + nrm((DEPTH, M_HEADS), 0.1),
        'g_mhnorm': 1.0 + nrm((DEPTH, M_WIDTH), 0.05),
        'w_out': nrm((DEPTH, MIX_WIDTH, D_MODEL), MIX_WIDTH ** -0.5),
        'w_router_group': nrm((DEPTH, D_MODEL, N_GROUPS), D_MODEL ** -0.5),
        'b_router_group': nrm((DEPTH, N_GROUPS), 0.01),
        'w_router_expert': nrm((DEPTH, D_MODEL, N_EXPERTS), D_MODEL ** -0.5),
        'b_router_expert': nrm((DEPTH, N_EXPERTS), 0.01),
        'w_exp_gate': nrm((DEPTH, N_EXPERTS, D_MODEL, D_EXPERT), D_MODEL ** -0.5),
        'w_exp_up': nrm((DEPTH, N_EXPERTS, D_MODEL, D_EXPERT), D_MODEL ** -0.5),
        'w_exp_down': nrm((DEPTH, N_EXPERTS, D_EXPERT, D_MODEL), D_EXPERT ** -0.5),
        'g_final': 1.0 + nrm((D_MODEL,), 0.05),
    }


def reference(x_prompt, x_sample, c_prompt, c_sample, cache_win_k, cache_win_v, state_conv, state_C, state_n, state_m,
              w_ada, b_ada, g_norm_mix, g_norm_ffn, w_in, attn_sinks, conv_w, conv_b, b_igate, b_fgate, g_mhnorm, w_out,
              w_router_group, b_router_group, w_router_expert, b_router_expert, w_exp_gate, w_exp_up, w_exp_down, g_final):
    bp, sp = x_prompt.shape[:2]
    bs, ls = x_sample.shape[:2]
    pos_p = jnp.arange(sp)
    pos_s = PAST_LEN + jnp.arange(ls)
    xp, xs = x_prompt, x_sample
    new_p, new_s = [], []
    for l in range(DEPTH):
        moe_w = (w_router_group[l], b_router_group[l], w_router_expert[l], b_router_expert[l],
                 w_exp_gate[l], w_exp_up[l], w_exp_down[l])
        mods = modulation(c_prompt, w_ada[l], b_ada[l])
        q, k, v, qk_m, v_m, ig, fg, o_m = project(ada_norm(xp, g_norm_mix[l], mods[0], mods[1]), w_in[l], pos_p)
        att = band_attention(q, k, v, attn_sinks[l])
        qk_c, conv_new = causal_conv(qk_m, jnp.zeros((bp, CONV_WIDTH - 1, 2 * M_WIDTH), qk_m.dtype), conv_w[l], conv_b[l])
        hm, C_new, n_new, m_new = mlstm_prompt(*mlstm_prepare(qk_c, v_m, ig, fg, b_igate[l], b_fgate[l]))
        xp = residual_update(xp, mods, att, hm, o_m, g_mhnorm[l], w_out[l], g_norm_ffn[l], moe_w)
        new_p.append((k[:, -WINDOW:], v[:, -WINDOW:], conv_new, C_new, n_new, m_new))
        mods = modulation(c_sample, w_ada[l], b_ada[l])
        q, k, v, qk_m, v_m, ig, fg, o_m = project(ada_norm(xs, g_norm_mix[l], mods[0], mods[1]), w_in[l], pos_s)
        k_all = jnp.concatenate([cache_win_k[l].astype(k.dtype), k], axis=1)
        v_all = jnp.concatenate([cache_win_v[l].astype(v.dtype), v], axis=1)
        att = sink_attend(q, k_all, v_all, attn_sinks[l], None).reshape(bs, ls, A_WIDTH)
        qk_c, conv_new = causal_conv(qk_m, state_conv[l], conv_w[l], conv_b[l])
        hm, C_new, n_new, m_new = mlstm_chunk(state_C[l].astype(jnp.float32), state_n[l].astype(jnp.float32),
                                              state_m[l].astype(jnp.float32),
                                              *mlstm_prepare(qk_c, v_m, ig, fg, b_igate[l], b_fgate[l]))
        xs = residual_update(xs, mods, att, hm, o_m, g_mhnorm[l], w_out[l], g_norm_ffn[l], moe_w)
        new_s.append((k_all[:, -WINDOW:], v_all[:, -WINDOW:], conv_new, C_new, n_new, m_new))
    win_k_p, win_v_p, conv_p, C_p, n_p, m_p = [jnp.stack(a) for a in zip(*new_p)]
    win_k_s, win_v_s, conv_s, C_s, n_s, m_s = [jnp.stack(a) for a in zip(*new_s)]
    y_prompt = rmsnorm(xp, g_final)
    y_sample = rmsnorm(xs, g_final)
    return (y_prompt, y_sample, win_k_p, win_v_p, conv_p, C_p, n_p, m_p, win_k_s, win_v_s, conv_s, C_s, n_s, m_s)
```

```python
import functools

import numpy as np
import jax
import jax.numpy as jnp
from jax import lax
from jax.experimental import pallas as pl
from jax.experimental.pallas import tpu as pltpu

F32 = jnp.float32
BF16 = jnp.bfloat16
HIGHEST = lax.Precision.HIGHEST

LANES = 128
SUBLANES = 8

D_MODEL = 1024
BATCH = 8
SEQ = 2048
DEC_BATCH = 32
DEC_SEQ = 64
PAST_LEN = 4096
CHUNK = 64
A_WIDTH = 512
A_HEAD_DIM = 64
A_HEADS = 8
A_KV_HEADS = 2
A_GROUP = 4
A_KV_WIDTH = 128
WINDOW = 128
ROT_DIM = 16
ROPE_THETA = 500000.0
M_WIDTH = 512
M_HEADS = 4
M_HEAD_DIM = 128
CONV_WIDTH = 4
N_GROUPS = 4
EXPERTS_PER_GROUP = 8
N_EXPERTS = 32
D_EXPERT = 512
EPS = 1e-6

N_P = BATCH * SEQ
N_S = DEC_BATCH * DEC_SEQ
N_TOK = N_P + N_S
N_CHUNKS = N_TOK // CHUNK
TM = 512
N_TILES = N_TOK // TM
P_TILES = N_P // TM
CH_PER_TILE = TM // CHUNK
ROW_SUB = D_MODEL // LANES
N_ASSIGN = 2 * N_TOK
MOE_R = 256
MOE_BLOCKS = N_ASSIGN // MOE_R + N_EXPERTS
MOE_ROWS = MOE_BLOCKS * MOE_R
C_Q, C_K, C_V, C_QKM, C_VM, C_OM, C_G = 0, 512, 640, 768, 1792, 2304, 2816
D_IN_PAD = 2944
VMEM_LIMIT = 48 * 1024 * 1024


def _sigmoid(x):
    return 1.0 / (1.0 + jnp.exp(-x))


def _mod_kernel(c_ref, w_ref, b_ref, o_ref):
    c = c_ref[...]
    s = c * _sigmoid(c)
    o_ref[...] = jnp.dot(s, w_ref[...], preferred_element_type=F32, precision=HIGHEST) + b_ref[...]


def _modulation(c_all, w_ada, b_ada):
    n = c_all.shape[0]
    bn = 512
    return pl.pallas_call(
        _mod_kernel,
        grid=(6 * D_MODEL // bn,),
        in_specs=[pl.BlockSpec((n, D_MODEL), lambda j: (0, 0)),
                  pl.BlockSpec((D_MODEL, bn), lambda j: (0, j)),
                  pl.BlockSpec((1, bn), lambda j: (0, j))],
        out_specs=pl.BlockSpec((n, bn), lambda j: (0, j)),
        out_shape=jax.ShapeDtypeStruct((n, 6 * D_MODEL), F32),
        name="modulation",
    )(c_all, w_ada, b_ada.reshape(1, -1))


def _xp_spec():
    def idx(i):
        t = jnp.minimum(i, P_TILES - 1)
        return (t // (SEQ // TM), t % (SEQ // TM), 0)
    return pl.BlockSpec((1, TM, D_MODEL), idx)


def _xs_spec():
    return pl.BlockSpec((CH_PER_TILE, DEC_SEQ, D_MODEL), lambda i: (jnp.maximum(i - P_TILES, 0), 0, 0))


def _mod_spec(comp):
    return pl.BlockSpec((CH_PER_TILE, 1, D_MODEL), lambda i: (i, 0, comp))


def _load_x(xp_ref, xs_ref, xbuf):
    i = pl.program_id(0)

    @pl.when(i < P_TILES)
    def _():
        xbuf[...] = xp_ref[0]

    @pl.when(i >= P_TILES)
    def _():
        xbuf[...] = xs_ref[...].reshape(TM, D_MODEL)

    return xbuf[...]


def _per_chunk(m_ref):
    m = m_ref[...]
    return jnp.broadcast_to(m, (CH_PER_TILE, CHUNK, D_MODEL)).reshape(TM, D_MODEL)


def _rmsnorm(x, g):
    return x * lax.rsqrt(jnp.mean(x * x, axis=-1, keepdims=True) + EPS) * g


def _rope(x, cos, sa, sb):
    n = x.shape[1]
    rep = n // LANES
    if rep > 1:
        cos = jnp.concatenate([cos] * rep, axis=1)
        sa = jnp.concatenate([sa] * rep, axis=1)
        sb = jnp.concatenate([sb] * rep, axis=1)
    return x * cos + pltpu.roll(x, n - ROT_DIM // 2, 1) * sa + pltpu.roll(x, ROT_DIM // 2, 1) * sb


def _inproj_kernel(xp_ref, xs_ref, sh_ref, sc_ref, g_ref, w_ref, cos_ref, sa_ref, sb_ref,
                   q_ref, kv_ref, qkm_ref, vm_ref, om_ref, gt_ref, xbuf):
    x = _load_x(xp_ref, xs_ref, xbuf)
    h = _rmsnorm(x, g_ref[...]) * (1.0 + _per_chunk(sc_ref)) + _per_chunk(sh_ref)
    hb = h.astype(BF16)

    def proj(a, b):
        return jnp.dot(hb, w_ref[:, a:b], preferred_element_type=F32)

    cos, sa, sb = cos_ref[...], sa_ref[...], sb_ref[...]
    q_ref[...] = _rope(proj(C_Q, C_K), cos, sa, sb)
    kv_ref[:, :A_KV_WIDTH] = _rope(proj(C_K, C_V), cos, sa, sb)
    kv_ref[:, A_KV_WIDTH:] = proj(C_V, C_QKM)
    qkm_ref[...] = proj(C_QKM, C_VM)
    vm_ref[...] = proj(C_VM, C_OM)
    om_ref[...] = proj(C_OM, C_G)
    gt_ref[...] = proj(C_G, D_IN_PAD)


def _rope_tables():
    pos = np.concatenate([np.arange(SEQ), np.tile(PAST_LEN + np.arange(DEC_SEQ), CH_PER_TILE)]).astype(np.float64)
    inv_freq = ROPE_THETA ** (-np.arange(0, ROT_DIM, 2, dtype=np.float64) / ROT_DIM)
    lane = np.arange(LANES)
    hl = lane % A_HEAD_DIM
    ang = pos[:, None] * inv_freq[hl % (ROT_DIM // 2)][None, :]
    rot = (hl < ROT_DIM)[None, :]
    lo = (hl < ROT_DIM // 2)[None, :]
    cos = np.where(rot, np.cos(ang), 1.0)
    sa = np.where(lo, -np.sin(ang), 0.0)
    sb = np.where(rot & ~lo, np.sin(ang), 0.0)
    return [jnp.asarray(t, F32) for t in (cos, sa, sb)]


def _inproj(x_prompt, x_sample, modc, g_mix, w_in_r):
    cos, sa, sb = _rope_tables()
    tab_spec = pl.BlockSpec((TM, LANES), lambda i: (jnp.where(i < P_TILES, i % (SEQ // TM), SEQ // TM), 0))

    def out(n):
        return pl.BlockSpec((TM, n), lambda i: (i, 0)), jax.ShapeDtypeStruct((N_TOK, n), F32)

    outs = [out(A_WIDTH), out(2 * A_KV_WIDTH), out(2 * M_WIDTH), out(M_WIDTH), out(M_WIDTH), out(LANES)]
    return pl.pallas_call(
        _inproj_kernel,
        grid=(N_TILES,),
        in_specs=[_xp_spec(), _xs_spec(), _mod_spec(0), _mod_spec(1),
                  pl.BlockSpec((1, D_MODEL), lambda i: (0, 0)),
                  pl.BlockSpec((D_MODEL, D_IN_PAD), lambda i: (0, 0)),
                  tab_spec, tab_spec, tab_spec],
        out_specs=[o[0] for o in outs],
        out_shape=[o[1] for o in outs],
        scratch_shapes=[pltpu.VMEM((TM, D_MODEL), F32)],
        compiler_params=pltpu.CompilerParams(vmem_limit_bytes=VMEM_LIMIT),
        name="inproj",
    )(x_prompt, x_sample, modc, modc, g_mix, w_in_r, cos, sa, sb)


def _attn_kernel(sink_ref, q_ref, prev_ref, cur_ref, o_ref, *, tq, banded):
    tk = tq + WINDOW
    q = q_ref[...]
    kv = jnp.concatenate([prev_ref[...], cur_ref[...]], axis=0)
    if banded:
        j = pl.program_id(1)
        rq = lax.broadcasted_iota(jnp.int32, (tq, tk), 0) // CHUNK
        ck = lax.broadcasted_iota(jnp.int32, (tq, tk), 1) // CHUNK
        first = jnp.where(j > 0, 0, WINDOW // CHUNK)
        mask = (ck >= rq) & (ck <= rq + WINDOW // CHUNK) & (ck >= first)
        mask = jnp.concatenate([mask] * A_GROUP, axis=0)
    outs = []
    for g in range(A_KV_HEADS):
        kg = kv[:, g * A_HEAD_DIM:(g + 1) * A_HEAD_DIM].astype(BF16)
        vg = kv[:, A_KV_WIDTH + g * A_HEAD_DIM:A_KV_WIDTH + (g + 1) * A_HEAD_DIM].astype(BF16)
        heads = [g * A_GROUP + i for i in range(A_GROUP)]
        qg = jnp.concatenate([q[:, h * A_HEAD_DIM:(h + 1) * A_HEAD_DIM] for h in heads], axis=0).astype(BF16)
        s = lax.dot_general(qg, kg, (((1,), (1,)), ((), ())), preferred_element_type=F32) * (A_HEAD_DIM ** -0.5)
        if banded:
            s = jnp.where(mask, s, -jnp.inf)
        snk = jnp.concatenate([jnp.full((tq, 1), sink_ref[h], F32) for h in heads], axis=0)
        mx = jnp.maximum(jnp.max(s, axis=-1, keepdims=True), snk)
        p = jnp.exp(s - mx)
        den = jnp.sum(p, axis=-1, keepdims=True) + jnp.exp(snk - mx)
        o = jnp.dot(p.astype(BF16), vg, preferred_element_type=F32) / den
        outs += [o[i * tq:(i + 1) * tq] for i in range(A_GROUP)]
    o_ref[...] = jnp.concatenate(outs, axis=1)


def _attention(sinks, q, kv, cache_kv):
    smem = pl.BlockSpec(memory_space=pltpu.SMEM)
    tq = 256
    nq = SEQ // tq
    att = pl.pallas_call(
        functools.partial(_attn_kernel, tq=tq, banded=True),
        grid=(BATCH, nq),
        in_specs=[smem,
                  pl.BlockSpec((tq, A_WIDTH), lambda b, j: (b * nq + j, 0)),
                  pl.BlockSpec((WINDOW, 2 * A_KV_WIDTH),
                               lambda b, j: (jnp.maximum((b * nq + j) * (tq // WINDOW) - 1, 0), 0)),
                  pl.BlockSpec((tq, 2 * A_KV_WIDTH), lambda b, j: (b * nq + j, 0))],
        out_specs=pl.BlockSpec((tq, A_WIDTH), lambda b, j: (b * nq + j, 0)),
        out_shape=jax.ShapeDtypeStruct((N_P, A_WIDTH), F32),
        name="attn_prompt",
    )(sinks, q, kv, kv)
    off = N_P // DEC_SEQ
    att_s = pl.pallas_call(
        functools.partial(_attn_kernel, tq=DEC_SEQ, banded=False),
        grid=(DEC_BATCH,),
        in_specs=[smem,
                  pl.BlockSpec((DEC_SEQ, A_WIDTH), lambda b: (off + b, 0)),
                  pl.BlockSpec((WINDOW, 2 * A_KV_WIDTH), lambda b: (b, 0)),
                  pl.BlockSpec((DEC_SEQ, 2 * A_KV_WIDTH), lambda b: (off + b, 0))],
        out_specs=pl.BlockSpec((DEC_SEQ, A_WIDTH), lambda b: (b, 0)),
        out_shape=jax.ShapeDtypeStruct((N_S, A_WIDTH), F32),
        name="attn_sample",
    )(sinks, q, cache_kv, kv)
    return att, att_s


def _conv4(x, w, b):
    y = b + x * w[CONV_WIDTH - 1:CONV_WIDTH]
    for j in range(1, CONV_WIDTH):
        y = y + pltpu.roll(x, j, 0) * w[CONV_WIDTH - 1 - j:CONV_WIDTH - j]
    return y


def _mlstm_kernel(qkm_ref, vm_ref, om_ref, gt_ref, cw_ref, cb_ref, gb_ref, gmh_ref,
                  carry0_ref, c0_ref, n0_ref, m0_ref,
                  hm_ref, cout_ref, nout_ref, mout_ref, c_s, n_s, m_s, carry_s, *, L):
    c = pl.program_id(1)

    @pl.when(c == 0)
    def _():
        c_s[...] = c0_ref[0]
        n_s[...] = n0_ref[0]
        m_s[...] = m0_ref[0]
        carry_s[...] = carry0_ref[0]

    x = qkm_ref[...]
    w = cw_ref[...]
    b = cb_ref[...]
    y = _conv4(x, w, b)
    y8 = _conv4(jnp.concatenate([carry_s[...], x[:SUBLANES]], axis=0), w, b)
    y = jnp.concatenate([y8[SUBLANES:], y[SUBLANES:]], axis=0)
    carry_s[...] = x[L - SUBLANES:]
    a = y * _sigmoid(y)
    qa = a[:, :M_WIDTH] * (M_HEAD_DIM ** -0.5)
    ka = a[:, M_WIDTH:]
    v = vm_ref[...]
    om = om_ref[...]
    gmh = gmh_ref[...]

    z = gt_ref[...] + gb_ref[...]
    lane = lax.broadcasted_iota(jnp.int32, (L, LANES), 1)
    f_log = jnp.minimum(z, 0.0) - jnp.log1p(jnp.exp(-jnp.abs(z)))
    val = jnp.where(lane < M_HEADS, z, f_log)
    row = lax.broadcasted_iota(jnp.int32, (L, L), 0)
    col = lax.broadcasted_iota(jnp.int32, (L, L), 1)
    causal = row >= col
    cum = jnp.dot(causal.astype(F32), val, preferred_element_type=F32, precision=HIGHEST)
    G = jnp.where(lane < M_HEADS, val, cum)
    sel = (lax.broadcasted_iota(jnp.int32, (SUBLANES, LANES), 0)
           == lax.broadcasted_iota(jnp.int32, (SUBLANES, LANES), 1)).astype(F32)
    GT = lax.dot_general(sel, G, (((1,), (1,)), ((), ())), preferred_element_type=F32, precision=HIGHEST)

    m_row = m_s[...]
    lane1 = lax.broadcasted_iota(jnp.int32, (1, LANES), 1)
    outs = []
    for h in range(M_HEADS):
        a_col = G[:, M_HEADS + h:M_HEADS + h + 1]
        i_col = G[:, h:h + 1]
        i_row = GT[h:h + 1, :]
        b_row = GT[M_HEADS + h:M_HEADS + h + 1, :]
        m_prev = m_row[:, h:h + 1]
        dm = jnp.where(causal, a_col - b_row + i_row, -jnp.inf)
        inter = a_col + m_prev
        m_t = jnp.maximum(inter, jnp.max(dm, axis=-1, keepdims=True))
        w_intra = jnp.exp(dm - m_t)
        w_inter = jnp.exp(inter - m_t)
        sl = slice(h * M_HEAD_DIM, (h + 1) * M_HEAD_DIM)
        qh, kh, vh = qa[:, sl], ka[:, sl], v[:, sl]
        qb = qh.astype(BF16)
        scores = lax.dot_general(qb, kh.astype(BF16), (((1,), (1,)), ((), ())), preferred_element_type=F32) * w_intra
        ch = c_s[h]
        nh = n_s[h:h + 1, :]
        num = (jnp.dot(scores.astype(BF16), vh.astype(BF16), preferred_element_type=F32)
               + w_inter * lax.dot_general(qb, ch.astype(BF16), (((1,), (1,)), ((), ())), preferred_element_type=F32))
        den = jnp.sum(scores, axis=-1, keepdims=True) + w_inter * jnp.sum(qh * nh, axis=-1, keepdims=True)
        hv = num / jnp.maximum(jnp.abs(den), jnp.exp(-m_t))
        hn = hv * lax.rsqrt(jnp.mean(hv * hv, axis=-1, keepdims=True) + EPS)
        outs.append(hn * gmh[:, sl] * _sigmoid(om[:, sl]))
        m_new = m_t[L - 1:L, :]
        a_last = a_col[L - 1:L, :]
        w_end = jnp.exp(a_last - a_col + i_col - m_new)
        decay = jnp.exp(a_last + m_prev - m_new)
        kw = kh * w_end
        c_s[h] = decay * ch + lax.dot_general(vh.astype(BF16), kw.astype(BF16), (((0,), (0,)), ((), ())),
                                              preferred_element_type=F32)
        n_s[h:h + 1, :] = decay * nh + jnp.sum(kw, axis=0, keepdims=True)
        m_row = jnp.where(lane1 == h, m_new, m_row)
    m_s[...] = m_row
    hm_ref[...] = jnp.concatenate(outs, axis=1)

    @pl.when(c == pl.num_programs(1) - 1)
    def _():
        cout_ref[0] = c_s[...]
        nout_ref[0] = n_s[...]
        mout_ref[0] = m_s[...]


def _mlstm(qkm, vm, om, gates, conv_w, conv_b, gbias, g_mh, carry0, c0, n0, m0, *, nb, L, nc, row_off, name):
    off = row_off // L

    def tok(n):
        return pl.BlockSpec((L, n), lambda b, c: (off + b * nc + c, 0))

    def const(shape):
        return pl.BlockSpec(shape, lambda b, c: (0,) * len(shape))

    in_specs = [tok(2 * M_WIDTH), tok(M_WIDTH), tok(M_WIDTH), tok(LANES),
                const((CONV_WIDTH, 2 * M_WIDTH)), const((1, 2 * M_WIDTH)), const((1, LANES)), const((1, M_WIDTH)),
                pl.BlockSpec((1, SUBLANES, 2 * M_WIDTH), lambda b, c: (b, 0, 0)),
                pl.BlockSpec((1, M_HEADS, M_HEAD_DIM, M_HEAD_DIM), lambda b, c: (b, 0, 0, 0)),
                pl.BlockSpec((1, M_HEADS, M_HEAD_DIM), lambda b, c: (b, 0, 0)),
                pl.BlockSpec((1, 1, LANES), lambda b, c: (b, 0, 0))]
    args = [qkm, vm, om, gates, conv_w, conv_b, gbias, g_mh, carry0, c0, n0, m0]
    return pl.pallas_call(
        functools.partial(_mlstm_kernel, L=L),
        grid=(nb, nc),
        in_specs=in_specs,
        out_specs=[pl.BlockSpec((L, M_WIDTH), lambda b, c: (b * nc + c, 0)),
                   pl.BlockSpec((1, M_HEADS, M_HEAD_DIM, M_HEAD_DIM), lambda b, c: (b, 0, 0, 0)),
                   pl.BlockSpec((1, M_HEADS, M_HEAD_DIM), lambda b, c: (b, 0, 0)),
                   pl.BlockSpec((1, 1, LANES), lambda b, c: (b, 0, 0))],
        out_shape=[jax.ShapeDtypeStruct((nb * nc * L, M_WIDTH), F32),
                   jax.ShapeDtypeStruct((nb, M_HEADS, M_HEAD_DIM, M_HEAD_DIM), F32),
                   jax.ShapeDtypeStruct((nb, M_HEADS, M_HEAD_DIM), F32),
                   jax.ShapeDtypeStruct((nb, 1, LANES), F32)],
        scratch_shapes=[pltpu.VMEM((M_HEADS, M_HEAD_DIM, M_HEAD_DIM), F32),
                        pltpu.VMEM((M_HEADS, M_HEAD_DIM), F32),
                        pltpu.VMEM((1, LANES), F32),
                        pltpu.VMEM((SUBLANES, 2 * M_WIDTH), F32)],
        compiler_params=pltpu.CompilerParams(dimension_semantics=("arbitrary", "arbitrary")),
        name=name,
    )(*args)


def _outproj_kernel(xp_ref, xs_ref, attp_ref, atts_ref, hmp_ref, hms_ref, gate_ref, sh_ref, sc_ref, g_ref,
                    w_ref, wr_ref, br_ref, x2_ref, h2_ref, ri_ref, xbuf, mixbuf):
    x = _load_x(xp_ref, xs_ref, xbuf)
    i = pl.program_id(0)

    @pl.when(i < P_TILES)
    def _():
        mixbuf[:, :A_WIDTH] = attp_ref[...].astype(BF16)
        mixbuf[:, A_WIDTH:] = hmp_ref[...].astype(BF16)

    @pl.when(i >= P_TILES)
    def _():
        mixbuf[:, :A_WIDTH] = atts_ref[...].astype(BF16)
        mixbuf[:, A_WIDTH:] = hms_ref[...].astype(BF16)

    mixed = jnp.dot(mixbuf[...], w_ref[...], preferred_element_type=F32)
    x2 = x + _per_chunk(gate_ref) * mixed
    x2_ref[...] = x2
    h2 = _rmsnorm(x2, g_ref[...]) * (1.0 + _per_chunk(sc_ref)) + _per_chunk(sh_ref)
    for s in range(ROW_SUB):
        h2_ref[pl.ds(s, TM, stride=ROW_SUB), :] = h2[:, s * LANES:(s + 1) * LANES]

    lg = jnp.dot(h2, wr_ref[...], preferred_element_type=F32, precision=HIGHEST) + br_ref[...]
    lane = lax.broadcasted_iota(jnp.int32, (TM, LANES), 1)
    lanef = lane.astype(F32)
    ninf = -jnp.inf

    def first_argmax(vals):
        mx = jnp.max(vals, axis=-1, keepdims=True)
        return mx, jnp.min(jnp.where(vals == mx, lanef, float(LANES)), axis=-1, keepdims=True)

    is_grp = lane < N_GROUPS
    gmax, grp = first_argmax(jnp.where(is_grp, lg, ninf))
    p_grp = 1.0 / jnp.sum(jnp.where(is_grp, jnp.exp(lg - gmax), 0.0), axis=-1, keepdims=True)
    base = N_GROUPS + grp * EXPERTS_PER_GROUP
    in_grp = (lanef >= base) & (lanef < base + EXPERTS_PER_GROUP)
    el = jnp.where(in_grp, lg, ninf)
    v1, i1 = first_argmax(el)
    v2, i2 = first_argmax(jnp.where(lanef == i1, ninf, el))
    e = jnp.exp(v2 - v1)
    g1 = p_grp / (1.0 + e)
    g2 = p_grp * e / (1.0 + e)
    ri = jnp.where(lane == 0, i1 - N_GROUPS,
                   jnp.where(lane == 1, i2 - N_GROUPS,
                             jnp.where(lane == 2, g1, jnp.where(lane == 3, g2, 0.0))))
    ri_ref[...] = ri


def _outproj(x_prompt, x_sample, att_p, att_s, hm_p, hm_s, modc, g_ffn, w_out_b, w_router, b_router):
    def tok(n):
        return pl.BlockSpec((TM, n), lambda i: (i, 0))

    def tok_p(n):
        return pl.BlockSpec((TM, n), lambda i: (jnp.minimum(i, P_TILES - 1), 0))

    def tok_s(n):
        return pl.BlockSpec((TM, n), lambda i: (jnp.maximum(i - P_TILES, 0), 0))

    def const(shape):
        return pl.BlockSpec(shape, lambda i: (0,) * len(shape))

    return pl.pallas_call(
        _outproj_kernel,
        grid=(N_TILES,),
        in_specs=[_xp_spec(), _xs_spec(), tok_p(A_WIDTH), tok_s(A_WIDTH), tok_p(M_WIDTH), tok_s(M_WIDTH),
                  _mod_spec(2), _mod_spec(3), _mod_spec(4),
                  const((1, D_MODEL)), const((D_MODEL, D_MODEL)), const((D_MODEL, LANES)), const((1, LANES))],
        out_specs=[tok(D_MODEL), pl.BlockSpec((TM * ROW_SUB, LANES), lambda i: (i, 0)), tok(LANES)],
        out_shape=[jax.ShapeDtypeStruct((N_TOK, D_MODEL), F32),
                   jax.ShapeDtypeStruct((N_TOK * ROW_SUB, LANES), F32),
                   jax.ShapeDtypeStruct((N_TOK, LANES), F32)],
        scratch_shapes=[pltpu.VMEM((TM, D_MODEL), F32), pltpu.VMEM((TM, D_MODEL), BF16)],
        compiler_params=pltpu.CompilerParams(vmem_limit_bytes=VMEM_LIMIT),
        name="outproj",
    )(x_prompt, x_sample, att_p, att_s, hm_p, hm_s, modc, modc, modc, g_ffn, w_out_b, w_router, b_router)


def _row(ref, r):
    return ref.at[pl.ds(pl.multiple_of(r * ROW_SUB, ROW_SUB), ROW_SUB), :]


def _dispatch_kernel(dest_ref, h_hbm, xs_in, xs_out, sem):
    del xs_in
    i = pl.program_id(0)

    def copies(t):
        tok = i * TM + t
        return [pltpu.make_async_copy(_row(h_hbm, tok), _row(xs_out, dest_ref[2 * tok + k]), sem) for k in range(2)]

    def issue(t, carry):
        for cp in copies(t):
            cp.start()
        return carry

    def wait(t, carry):
        for cp in copies(t):
            cp.wait()
        return carry

    lax.fori_loop(0, TM, issue, 0)
    lax.fori_loop(0, TM, wait, 0)


def _dispatch(dest, h2t):
    xs0 = jnp.zeros((MOE_ROWS * ROW_SUB, LANES), F32)
    return pl.pallas_call(
        _dispatch_kernel,
        grid_spec=pltpu.PrefetchScalarGridSpec(
            num_scalar_prefetch=1,
            grid=(N_TILES,),
            in_specs=[pl.BlockSpec(memory_space=pl.ANY), pl.BlockSpec(memory_space=pl.ANY)],
            out_specs=pl.BlockSpec(memory_space=pl.ANY),
            scratch_shapes=[pltpu.SemaphoreType.DMA]),
        out_shape=jax.ShapeDtypeStruct((MOE_ROWS * ROW_SUB, LANES), F32),
        input_output_aliases={2: 0},
        name="moe_dispatch",
    )(dest, h2t, xs0)


def _expert_kernel(be_ref, bv_ref, xs_ref, wg_ref, wu_ref, wd_ref, o_ref, wg_s, wu_s, wd_s):
    b = pl.program_id(0)
    prev = be_ref[jnp.maximum(b - 1, 0)]

    @pl.when((b == 0) | (be_ref[b] != prev))
    def _():
        wg_s[...] = wg_ref[0].astype(BF16)
        wu_s[...] = wu_ref[0].astype(BF16)
        wd_s[...] = wd_ref[0].astype(BF16)

    @pl.when(bv_ref[b] > 0)
    def _():
        x = jnp.concatenate([xs_ref[pl.ds(s, MOE_R, stride=ROW_SUB), :] for s in range(ROW_SUB)], axis=1)
        xb = x.astype(BF16)
        g = jnp.dot(xb, wg_s[...], preferred_element_type=F32)
        u = jnp.dot(xb, wu_s[...], preferred_element_type=F32)
        a = (g * _sigmoid(g) * u).astype(BF16)
        o = jnp.dot(a, wd_s[...], preferred_element_type=F32)
        for s in range(ROW_SUB):
            o_ref[pl.ds(s, MOE_R, stride=ROW_SUB), :] = o[:, s * LANES:(s + 1) * LANES]

    @pl.when(bv_ref[b] == 0)
    def _():
        o_ref[...] = jnp.zeros_like(o_ref)


def _experts(blk_e, blk_valid, xs, w_g, w_u, w_d):
    return pl.pallas_call(
        _expert_kernel,
        grid_spec=pltpu.PrefetchScalarGridSpec(
            num_scalar_prefetch=2,
            grid=(MOE_BLOCKS,),
            in_specs=[pl.BlockSpec((MOE_R * ROW_SUB, LANES), lambda b, be, bv: (b, 0)),
                      pl.BlockSpec((1, D_MODEL, D_EXPERT), lambda b, be, bv: (be[b], 0, 0)),
                      pl.BlockSpec((1, D_MODEL, D_EXPERT), lambda b, be, bv: (be[b], 0, 0)),
                      pl.BlockSpec((1, D_EXPERT, D_MODEL), lambda b, be, bv: (be[b], 0, 0))],
            out_specs=pl.BlockSpec((MOE_R * ROW_SUB, LANES), lambda b, be, bv: (b, 0)),
            scratch_shapes=[pltpu.VMEM((D_MODEL, D_EXPERT), BF16),
                            pltpu.VMEM((D_MODEL, D_EXPERT), BF16),
                            pltpu.VMEM((D_EXPERT, D_MODEL), BF16)]),
        out_shape=jax.ShapeDtypeStruct((MOE_ROWS * ROW_SUB, LANES), F32),
        compiler_params=pltpu.CompilerParams(dimension_semantics=("arbitrary",), vmem_limit_bytes=VMEM_LIMIT),
        name="moe_experts",
    )(blk_e, blk_valid, xs, w_g, w_u, w_d)


def _combine_kernel(dest_ref, o_hbm, x2_ref, ri_ref, gate_ref, gf_ref, yp_ref, ys_ref, obuf, sem):
    i = pl.program_id(0)

    def copies(t):
        tok = i * TM + t
        return [pltpu.make_async_copy(_row(o_hbm, dest_ref[2 * tok + k]), _row(obuf.at[k], t), sem)
                for k in range(2)]

    def issue(t, carry):
        for cp in copies(t):
            cp.start()
        return carry

    def wait(t, carry):
        for cp in copies(t):
            cp.wait()
        return carry

    lax.fori_loop(0, TM, issue, 0)
    lax.fori_loop(0, TM, wait, 0)

    ri = ri_ref[...]
    g1 = ri[:, 2:3]
    g2 = ri[:, 3:4]
    moe = jnp.concatenate(
        [g1 * obuf[0, pl.ds(s, TM, stride=ROW_SUB), :] + g2 * obuf[1, pl.ds(s, TM, stride=ROW_SUB), :]
         for s in range(ROW_SUB)], axis=1)
    x3 = x2_ref[...] + _per_chunk(gate_ref) * moe
    y = _rmsnorm(x3, gf_ref[...])

    @pl.when(i < P_TILES)
    def _():
        yp_ref[0] = y

    @pl.when(i >= P_TILES)
    def _():
        ys_ref[...] = y.reshape(CH_PER_TILE, DEC_SEQ, D_MODEL)


def _combine(dest, o_rows, x2, rinfo, modc, g_final):
    xp_spec, xs_spec = _xp_spec(), _xs_spec()
    return pl.pallas_call(
        _combine_kernel,
        grid_spec=pltpu.PrefetchScalarGridSpec(
            num_scalar_prefetch=1,
            grid=(N_TILES,),
            in_specs=[pl.BlockSpec(memory_space=pl.ANY),
                      pl.BlockSpec((TM, D_MODEL), lambda i, d: (i, 0)),
                      pl.BlockSpec((TM, LANES), lambda i, d: (i, 0)),
                      pl.BlockSpec((CH_PER_TILE, 1, D_MODEL), lambda i, d: (i, 0, 5)),
                      pl.BlockSpec((1, D_MODEL), lambda i, d: (0, 0))],
            out_specs=[pl.BlockSpec(xp_spec.block_shape, lambda i, d: xp_spec.index_map(i)),
                       pl.BlockSpec(xs_spec.block_shape, lambda i, d: xs_spec.index_map(i))],
            scratch_shapes=[pltpu.VMEM((2, TM * ROW_SUB, LANES), F32), pltpu.SemaphoreType.DMA]),
        out_shape=[jax.ShapeDtypeStruct((BATCH, SEQ, D_MODEL), F32),
                   jax.ShapeDtypeStruct((DEC_BATCH, DEC_SEQ, D_MODEL), F32)],
        compiler_params=pltpu.CompilerParams(dimension_semantics=("arbitrary",), vmem_limit_bytes=VMEM_LIMIT),
        name="moe_combine",
    )(dest, o_rows, x2, rinfo, modc, g_final)


def _moe_plan(rinfo):
    eid = rinfo[:, :2].astype(jnp.int32).reshape(-1)
    onehot = (eid[:, None] == jnp.arange(N_EXPERTS, dtype=jnp.int32)[None, :]).astype(jnp.int32)
    csum = jnp.cumsum(onehot, axis=0)
    rank = jnp.sum(onehot * (csum - 1), axis=1)
    counts = csum[-1]
    padded = (counts + MOE_R - 1) // MOE_R * MOE_R
    pad_end = jnp.cumsum(padded)
    base = pad_end - padded
    dest = base[eid] + rank
    blk_start = jnp.arange(MOE_BLOCKS, dtype=jnp.int32) * MOE_R
    blk_e = jnp.minimum(jnp.searchsorted(pad_end, blk_start, side="right"), N_EXPERTS - 1).astype(jnp.int32)
    blk_valid = jnp.clip(counts[blk_e] - (blk_start - base[blk_e]), 0, MOE_R).astype(jnp.int32)
    return dest.astype(jnp.int32), blk_e, blk_valid


def kernel(x_prompt, x_sample, c_prompt, c_sample, cache_win_k, cache_win_v, state_conv, state_C, state_n, state_m, w_ada, b_ada, g_norm_mix, g_norm_ffn, w_in, attn_sinks, conv_w, conv_b, b_igate, b_fgate, g_mhnorm, w_out, w_router_group, b_router_group, w_router_expert, b_router_expert, w_exp_gate, w_exp_up, w_exp_down, g_final):
    l = 0
    mod = _modulation(jnp.concatenate([c_prompt, c_sample], axis=0), w_ada[l], b_ada[l])
    chunk_stream = np.concatenate([np.repeat(np.arange(BATCH), SEQ // CHUNK), BATCH + np.arange(DEC_BATCH)])
    modc = mod[chunk_stream].reshape(N_CHUNKS, 1, 6 * D_MODEL)

    wi = w_in[l]
    s_q, s_k, s_v, s_qkm, s_vm, s_ig, s_fg = 0, 512, 640, 768, 1792, 2304, 2308
    s_om = 2312
    w_in_r = jnp.concatenate(
        [wi[:, s_q:s_qkm], wi[:, s_qkm:s_vm], wi[:, s_vm:s_ig], wi[:, s_om:], wi[:, s_ig:s_om],
         jnp.zeros((D_MODEL, LANES - 2 * M_HEADS), F32)], axis=1).astype(BF16)
    q, kv, qkm, vm, om, gates = _inproj(x_prompt, x_sample, modc, g_norm_mix[l].reshape(1, -1), w_in_r)

    cache_kv = jnp.concatenate([cache_win_k[l].reshape(DEC_BATCH * WINDOW, A_KV_WIDTH),
                                cache_win_v[l].reshape(DEC_BATCH * WINDOW, A_KV_WIDTH)], axis=1)
    att_p, att_s = _attention(attn_sinks[l], q, kv, cache_kv)

    gbias = jnp.concatenate([b_igate[l], b_fgate[l], jnp.zeros((LANES - 2 * M_HEADS,), F32)]).reshape(1, LANES)
    common = (qkm, vm, om, gates, conv_w[l], conv_b[l].reshape(1, -1), gbias, g_mhnorm[l].reshape(1, -1))
    zeros_p = (jnp.zeros((BATCH, SUBLANES, 2 * M_WIDTH), F32),
               jnp.zeros((BATCH, M_HEADS, M_HEAD_DIM, M_HEAD_DIM), F32),
               jnp.zeros((BATCH, M_HEADS, M_HEAD_DIM), F32),
               jnp.zeros((BATCH, 1, LANES), F32))
    LP = 256
    hm_p, C_p, n_p, m_p = _mlstm(*common, *zeros_p, nb=BATCH, L=LP, nc=SEQ // LP, row_off=0, name="mlstm_prompt")
    carry_s = jnp.concatenate([jnp.zeros((DEC_BATCH, SUBLANES - (CONV_WIDTH - 1), 2 * M_WIDTH), F32),
                               state_conv[l]], axis=1)
    m0_s = jnp.pad(state_m[l], ((0, 0), (0, LANES - M_HEADS))).reshape(DEC_BATCH, 1, LANES)
    hm_s, C_s, n_s, m_s = _mlstm(*common, carry_s, state_C[l], state_n[l], m0_s,
                                 nb=DEC_BATCH, L=DEC_SEQ, nc=1, row_off=N_P, name="mlstm_sample")

    w_router = jnp.concatenate([w_router_group[l], w_router_expert[l],
                                jnp.zeros((D_MODEL, LANES - N_GROUPS - N_EXPERTS), F32)], axis=1)
    b_router = jnp.concatenate([b_router_group[l], b_router_expert[l],
                                jnp.zeros((LANES - N_GROUPS - N_EXPERTS,), F32)]).reshape(1, LANES)
    x2, h2t, rinfo = _outproj(x_prompt, x_sample, att_p, att_s, hm_p, hm_s, modc, g_norm_ffn[l].reshape(1, -1),
                              w_out[l].astype(BF16), w_router, b_router)

    dest, blk_e, blk_valid = _moe_plan(rinfo)
    xs_rows = _dispatch(dest, h2t)
    o_rows = _experts(blk_e, blk_valid, xs_rows, w_exp_gate[l], w_exp_up[l], w_exp_down[l])
    y_prompt, y_sample = _combine(dest, o_rows, x2, rinfo, modc, g_final.reshape(1, -1))

    kv_p = kv[:N_P].reshape(BATCH, SEQ, 2 * A_KV_WIDTH)[:, -WINDOW:]
    win_k_p = kv_p[..., :A_KV_WIDTH].reshape(1, BATCH, WINDOW, A_KV_HEADS, A_HEAD_DIM)
    win_v_p = kv_p[..., A_KV_WIDTH:].reshape(1, BATCH, WINDOW, A_KV_HEADS, A_HEAD_DIM)
    kv_s = kv[N_P:].reshape(DEC_BATCH, DEC_SEQ, 2 * A_KV_WIDTH)
    win_k_s = jnp.concatenate([cache_win_k[l][:, DEC_SEQ:],
                               kv_s[..., :A_KV_WIDTH].reshape(DEC_BATCH, DEC_SEQ, A_KV_HEADS, A_HEAD_DIM)], axis=1)[None]
    win_v_s = jnp.concatenate([cache_win_v[l][:, DEC_SEQ:],
                               kv_s[..., A_KV_WIDTH:].reshape(DEC_BATCH, DEC_SEQ, A_KV_HEADS, A_HEAD_DIM)], axis=1)[None]
    conv_p = qkm[:N_P].reshape(BATCH, SEQ, 2 * M_WIDTH)[:, -(CONV_WIDTH - 1):][None]
    conv_s = qkm[N_P:].reshape(DEC_BATCH, DEC_SEQ, 2 * M_WIDTH)[:, -(CONV_WIDTH - 1):][None]
    return (y_prompt, y_sample,
            win_k_p, win_v_p, conv_p, C_p[None], n_p[None], m_p[:, 0, :M_HEADS][None],
            win_k_s, win_v_s, conv_s, C_s[None], n_s[None], m_s[:, 0, :M_HEADS][None])
```

```python
import functools

import numpy as np
import jax
import jax.numpy as jnp
from jax import lax
from jax.experimental import pallas as pl
from jax.experimental.pallas import tpu as pltpu

F32 = jnp.float32
BF16 = jnp.bfloat16
HIGHEST = lax.Precision.HIGHEST

LANES = 128
SUBLANES = 8

D_MODEL = 1024
BATCH = 8
SEQ = 2048
DEC_BATCH = 32
DEC_SEQ = 64
PAST_LEN = 4096
CHUNK = 64
A_WIDTH = 512
A_HEAD_DIM = 64
A_HEADS = 8
A_KV_HEADS = 2
A_GROUP = 4
A_KV_WIDTH = 128
WINDOW = 128
ROT_DIM = 16
ROPE_THETA = 500000.0
M_WIDTH = 512
M_HEADS = 4
M_HEAD_DIM = 128
CONV_WIDTH = 4
N_GROUPS = 4
EXPERTS_PER_GROUP = 8
N_EXPERTS = 32
D_EXPERT = 512
EPS = 1e-6

N_P = BATCH * SEQ
N_S = DEC_BATCH * DEC_SEQ
N_TOK = N_P + N_S
N_CHUNKS = N_TOK // CHUNK
TM = 512
N_TILES = N_TOK // TM
P_TILES = N_P // TM
CH_PER_TILE = TM // CHUNK
ROW_SUB = D_MODEL // LANES
N_ASSIGN = 2 * N_TOK
MOE_R = 256
MOE_BLOCKS = N_ASSIGN // MOE_R + N_EXPERTS
MOE_ROWS = MOE_BLOCKS * MOE_R
C_Q, C_K, C_V, C_QKM, C_VM, C_OM, C_G = 0, 512, 640, 768, 1792, 2304, 2816
D_IN_PAD = 2944
VMEM_LIMIT = 48 * 1024 * 1024


def _sigmoid(x):
    return 1.0 / (1.0 + jnp.exp(-x))


def _mod_kernel(c_ref, w_ref, b_ref, o_ref):
    c = c_ref[...]
    s = c * _sigmoid(c)
    o_ref[...] = jnp.dot(s, w_ref[...], preferred_element_type=F32, precision=HIGHEST) + b_ref[...]


def _modulation(c_all, w_ada, b_ada):
    n = c_all.shape[0]
    bn = 512
    return pl.pallas_call(
        _mod_kernel,
        grid=(6 * D_MODEL // bn,),
        in_specs=[pl.BlockSpec((n, D_MODEL), lambda j: (0, 0)),
                  pl.BlockSpec((D_MODEL, bn), lambda j: (0, j)),
                  pl.BlockSpec((1, bn), lambda j: (0, j))],
        out_specs=pl.BlockSpec((n, bn), lambda j: (0, j)),
        out_shape=jax.ShapeDtypeStruct((n, 6 * D_MODEL), F32),
        name="modulation",
    )(c_all, w_ada, b_ada.reshape(1, -1))


def _xp_spec():
    def idx(i):
        t = jnp.minimum(i, P_TILES - 1)
        return (t // (SEQ // TM), t % (SEQ // TM), 0)
    return pl.BlockSpec((1, TM, D_MODEL), idx)


def _xs_spec():
    return pl.BlockSpec((CH_PER_TILE, DEC_SEQ, D_MODEL), lambda i: (jnp.maximum(i - P_TILES, 0), 0, 0))


def _mod_spec(comp):
    return pl.BlockSpec((CH_PER_TILE, 1, D_MODEL), lambda i: (i, 0, comp))


def _load_x(xp_ref, xs_ref, xbuf):
    i = pl.program_id(0)

    @pl.when(i < P_TILES)
    def _():
        xbuf[...] = xp_ref[0]

    @pl.when(i >= P_TILES)
    def _():
        xbuf[...] = xs_ref[...].reshape(TM, D_MODEL)

    return xbuf[...]


def _per_chunk(m_ref):
    m = m_ref[...]
    return jnp.broadcast_to(m, (CH_PER_TILE, CHUNK, D_MODEL)).reshape(TM, D_MODEL)


def _rmsnorm(x, g):
    return x * lax.rsqrt(jnp.mean(x * x, axis=-1, keepdims=True) + EPS) * g


def _rope(x, cos, sa, sb):
    n = x.shape[1]
    rep = n // LANES
    if rep > 1:
        cos = jnp.concatenate([cos] * rep, axis=1)
        sa = jnp.concatenate([sa] * rep, axis=1)
        sb = jnp.concatenate([sb] * rep, axis=1)
    return x * cos + pltpu.roll(x, n - ROT_DIM // 2, 1) * sa + pltpu.roll(x, ROT_DIM // 2, 1) * sb


def _inproj_kernel(xp_ref, xs_ref, sh_ref, sc_ref, g_ref, w_ref, cos_ref, sa_ref, sb_ref,
                   q_ref, kv_ref, qkm_ref, vm_ref, om_ref, gt_ref, xbuf):
    x = _load_x(xp_ref, xs_ref, xbuf)
    h = _rmsnorm(x, g_ref[...]) * (1.0 + _per_chunk(sc_ref)) + _per_chunk(sh_ref)
    hb = h.astype(BF16)

    def proj(a, b):
        return jnp.dot(hb, w_ref[:, a:b], preferred_element_type=F32)

    cos, sa, sb = cos_ref[...], sa_ref[...], sb_ref[...]
    q_ref[...] = _rope(proj(C_Q, C_K), cos, sa, sb)
    kv_ref[:, :A_KV_WIDTH] = _rope(proj(C_K, C_V), cos, sa, sb)
    kv_ref[:, A_KV_WIDTH:] = proj(C_V, C_QKM)
    qkm_ref[...] = proj(C_QKM, C_VM)
    vm_ref[...] = proj(C_VM, C_OM)
    om_ref[...] = proj(C_OM, C_G)
    gt_ref[...] = proj(C_G, D_IN_PAD)


def _rope_tables():
    pos = np.concatenate([np.arange(SEQ), np.tile(PAST_LEN + np.arange(DEC_SEQ), CH_PER_TILE)]).astype(np.float64)
    inv_freq = ROPE_THETA ** (-np.arange(0, ROT_DIM, 2, dtype=np.float64) / ROT_DIM)
    lane = np.arange(LANES)
    hl = lane % A_HEAD_DIM
    ang = pos[:, None] * inv_freq[hl % (ROT_DIM // 2)][None, :]
    rot = (hl < ROT_DIM)[None, :]
    lo = (hl < ROT_DIM // 2)[None, :]
    cos = np.where(rot, np.cos(ang), 1.0)
    sa = np.where(lo, -np.sin(ang), 0.0)
    sb = np.where(rot & ~lo, np.sin(ang), 0.0)
    return [jnp.asarray(t, F32) for t in (cos, sa, sb)]


def _inproj(x_prompt, x_sample, modc, g_mix, w_in_r):
    cos, sa, sb = _rope_tables()
    tab_spec = pl.BlockSpec((TM, LANES), lambda i: (jnp.where(i < P_TILES, i % (SEQ // TM), SEQ // TM), 0))

    def out(n):
        return pl.BlockSpec((TM, n), lambda i: (i, 0)), jax.ShapeDtypeStruct((N_TOK, n), F32)

    outs = [out(A_WIDTH), out(2 * A_KV_WIDTH), out(2 * M_WIDTH), out(M_WIDTH), out(M_WIDTH), out(LANES)]
    return pl.pallas_call(
        _inproj_kernel,
        grid=(N_TILES,),
        in_specs=[_xp_spec(), _xs_spec(), _mod_spec(0), _mod_spec(1),
                  pl.BlockSpec((1, D_MODEL), lambda i: (0, 0)),
                  pl.BlockSpec((D_MODEL, D_IN_PAD), lambda i: (0, 0)),
                  tab_spec, tab_spec, tab_spec],
        out_specs=[o[0] for o in outs],
        out_shape=[o[1] for o in outs],
        scratch_shapes=[pltpu.VMEM((TM, D_MODEL), F32)],
        compiler_params=pltpu.CompilerParams(vmem_limit_bytes=VMEM_LIMIT),
        name="inproj",
    )(x_prompt, x_sample, modc, modc, g_mix, w_in_r, cos, sa, sb)


def _attn_kernel(sink_ref, q_ref, prev_ref, cur_ref, o_ref, *, tq, banded):
    tk = tq + WINDOW
    q = q_ref[...]
    kv = jnp.concatenate([prev_ref[...], cur_ref[...]], axis=0)
    if banded:
        j = pl.program_id(1)
        rq = lax.broadcasted_iota(jnp.int32, (tq, tk), 0) // CHUNK
        ck = lax.broadcasted_iota(jnp.int32, (tq, tk), 1) // CHUNK
        first = jnp.where(j > 0, 0, WINDOW // CHUNK)
        mask = (ck >= rq) & (ck <= rq + WINDOW // CHUNK) & (ck >= first)
        mask = jnp.concatenate([mask] * A_GROUP, axis=0)
    outs = []
    for g in range(A_KV_HEADS):
        kg = kv[:, g * A_HEAD_DIM:(g + 1) * A_HEAD_DIM].astype(BF16)
        vg = kv[:, A_KV_WIDTH + g * A_HEAD_DIM:A_KV_WIDTH + (g + 1) * A_HEAD_DIM].astype(BF16)
        heads = [g * A_GROUP + i for i in range(A_GROUP)]
        qg = jnp.concatenate([q[:, h * A_HEAD_DIM:(h + 1) * A_HEAD_DIM] for h in heads], axis=0).astype(BF16)
        s = lax.dot_general(qg, kg, (((1,), (1,)), ((), ())), preferred_element_type=F32) * (A_HEAD_DIM ** -0.5)
        if banded:
            s = jnp.where(mask, s, -jnp.inf)
        snk = jnp.concatenate([jnp.full((tq, 1), sink_ref[h], F32) for h in heads], axis=0)
        mx = jnp.maximum(jnp.max(s, axis=-1, keepdims=True), snk)
        p = jnp.exp(s - mx)
        den = jnp.sum(p, axis=-1, keepdims=True) + jnp.exp(snk - mx)
        o = jnp.dot(p.astype(BF16), vg, preferred_element_type=F32) / den
        outs += [o[i * tq:(i + 1) * tq] for i in range(A_GROUP)]
    o_ref[...] = jnp.concatenate(outs, axis=1)


def _attention(sinks, q, kv, cache_kv):
    smem = pl.BlockSpec(memory_space=pltpu.SMEM)
    tq = 256
    nq = SEQ // tq
    att = pl.pallas_call(
        functools.partial(_attn_kernel, tq=tq, banded=True),
        grid=(BATCH, nq),
        in_specs=[smem,
                  pl.BlockSpec((tq, A_WIDTH), lambda b, j: (b * nq + j, 0)),
                  pl.BlockSpec((WINDOW, 2 * A_KV_WIDTH),
                               lambda b, j: (jnp.maximum((b * nq + j) * (tq // WINDOW) - 1, 0), 0)),
                  pl.BlockSpec((tq, 2 * A_KV_WIDTH), lambda b, j: (b * nq + j, 0))],
        out_specs=pl.BlockSpec((tq, A_WIDTH), lambda b, j: (b * nq + j, 0)),
        out_shape=jax.ShapeDtypeStruct((N_P, A_WIDTH), F32),
        name="attn_prompt",
    )(sinks, q, kv, kv)
    off = N_P // DEC_SEQ
    att_s = pl.pallas_call(
        functools.partial(_attn_kernel, tq=DEC_SEQ, banded=False),
        grid=(DEC_BATCH,),
        in_specs=[smem,
                  pl.BlockSpec((DEC_SEQ, A_WIDTH), lambda b: (off + b, 0)),
                  pl.BlockSpec((WINDOW, 2 * A_KV_WIDTH), lambda b: (b, 0)),
                  pl.BlockSpec((DEC_SEQ, 2 * A_KV_WIDTH), lambda b: (off + b, 0))],
        out_specs=pl.BlockSpec((DEC_SEQ, A_WIDTH), lambda b: (b, 0)),
        out_shape=jax.ShapeDtypeStruct((N_S, A_WIDTH), F32),
        name="attn_sample",
    )(sinks, q, cache_kv, kv)
    return att, att_s


def _conv4(x, w, b):
    y = b + x * w[CONV_WIDTH - 1:CONV_WIDTH]
    for j in range(1, CONV_WIDTH):
        y = y + pltpu.roll(x, j, 0) * w[CONV_WIDTH - 1 - j:CONV_WIDTH - j]
    return y


def _mlstm_kernel(qkm_ref, vm_ref, om_ref, gt_ref, cw_ref, cb_ref, gb_ref, gmh_ref,
                  carry0_ref, c0_ref, n0_ref, m0_ref,
                  hm_ref, cout_ref, nout_ref, mout_ref, c_s, n_s, m_s, carry_s, *, L):
    c = pl.program_id(1)

    @pl.when(c == 0)
    def _():
        c_s[...] = c0_ref[0]
        n_s[...] = n0_ref[0]
        m_s[...] = m0_ref[0]
        carry_s[...] = carry0_ref[0]

    x = qkm_ref[...]
    w = cw_ref[...]
    b = cb_ref[...]
    y = _conv4(x, w, b)
    y8 = _conv4(jnp.concatenate([carry_s[...], x[:SUBLANES]], axis=0), w, b)
    y = jnp.concatenate([y8[SUBLANES:], y[SUBLANES:]], axis=0)
    carry_s[...] = x[L - SUBLANES:]
    a = y * _sigmoid(y)
    qa = a[:, :M_WIDTH] * (M_HEAD_DIM ** -0.5)
    ka = a[:, M_WIDTH:]
    v = vm_ref[...]
    om = om_ref[...]
    gmh = gmh_ref[...]

    z = gt_ref[...] + gb_ref[...]
    lane = lax.broadcasted_iota(jnp.int32, (L, LANES), 1)
    f_log = jnp.minimum(z, 0.0) - jnp.log1p(jnp.exp(-jnp.abs(z)))
    val = jnp.where(lane < M_HEADS, z, f_log)
    row = lax.broadcasted_iota(jnp.int32, (L, L), 0)
    col = lax.broadcasted_iota(jnp.int32, (L, L), 1)
    causal = row >= col
    cum = jnp.dot(causal.astype(F32), val, preferred_element_type=F32, precision=HIGHEST)
    G = jnp.where(lane < M_HEADS, val, cum)
    sel = (lax.broadcasted_iota(jnp.int32, (SUBLANES, LANES), 0)
           == lax.broadcasted_iota(jnp.int32, (SUBLANES, LANES), 1)).astype(F32)
    GT = lax.dot_general(sel, G, (((1,), (1,)), ((), ())), preferred_element_type=F32, precision=HIGHEST)

    m_row = m_s[...]
    lane1 = lax.broadcasted_iota(jnp.int32, (1, LANES), 1)
    outs = []
    for h in range(M_HEADS):
        a_col = G[:, M_HEADS + h:M_HEADS + h + 1]
        i_col = G[:, h:h + 1]
        i_row = GT[h:h + 1, :]
        b_row = GT[M_HEADS + h:M_HEADS + h + 1, :]
        m_prev = m_row[:, h:h + 1]
        dm = jnp.where(causal, a_col - b_row + i_row, -jnp.inf)
        inter = a_col + m_prev
        m_t = jnp.maximum(inter, jnp.max(dm, axis=-1, keepdims=True))
        w_intra = jnp.exp(dm - m_t)
        w_inter = jnp.exp(inter - m_t)
        sl = slice(h * M_HEAD_DIM, (h + 1) * M_HEAD_DIM)
        qh, kh, vh = qa[:, sl], ka[:, sl], v[:, sl]
        qb = qh.astype(BF16)
        scores = lax.dot_general(qb, kh.astype(BF16), (((1,), (1,)), ((), ())), preferred_element_type=F32) * w_intra
        ch = c_s[h]
        nh = n_s[h:h + 1, :]
        num = (jnp.dot(scores.astype(BF16), vh.astype(BF16), preferred_element_type=F32)
               + w_inter * lax.dot_general(qb, ch.astype(BF16), (((1,), (1,)), ((), ())), preferred_element_type=F32))
        den = jnp.sum(scores, axis=-1, keepdims=True) + w_inter * jnp.sum(qh * nh, axis=-1, keepdims=True)
        hv = num / jnp.maximum(jnp.abs(den), jnp.exp(-m_t))
        hn = hv * lax.rsqrt(jnp.mean(hv * hv, axis=-1, keepdims=True) + EPS)
        outs.append(hn * gmh[:, sl] * _sigmoid(om[:, sl]))
        m_new = m_t[L - 1:L, :]
        a_last = a_col[L - 1:L, :]
        w_end = jnp.exp(a_last - a_col + i_col - m_new)
        decay = jnp.exp(a_last + m_prev - m_new)
        kw = kh * w_end
        c_s[h] = decay * ch + lax.dot_general(vh.astype(BF16), kw.astype(BF16), (((0,), (0,)), ((), ())),
                                              preferred_element_type=F32)
        n_s[h:h + 1, :] = decay * nh + jnp.sum(kw, axis=0, keepdims=True)
        m_row = jnp.where(lane1 == h, m_new, m_row)
    m_s[...] = m_row
    hm_ref[...] = jnp.concatenate(outs, axis=1)

    @pl.when(c == pl.num_programs(1) - 1)
    def _():
        cout_ref[0] = c_s[...]
        nout_ref[0] = n_s[...]
        mout_ref[0] = m_s[...]


def _mlstm(qkm, vm, om, gates, conv_w, conv_b, gbias, g_mh, carry0, c0, n0, m0, *, nb, L, nc, row_off, name):
    off = row_off // L

    def tok(n):
        return pl.BlockSpec((L, n), lambda b, c: (off + b * nc + c, 0))

    def const(shape):
        return pl.BlockSpec(shape, lambda b, c: (0,) * len(shape))

    in_specs = [tok(2 * M_WIDTH), tok(M_WIDTH), tok(M_WIDTH), tok(LANES),
                const((CONV_WIDTH, 2 * M_WIDTH)), const((1, 2 * M_WIDTH)), const((1, LANES)), const((1, M_WIDTH)),
                pl.BlockSpec((1, SUBLANES, 2 * M_WIDTH), lambda b, c: (b, 0, 0)),
                pl.BlockSpec((1, M_HEADS, M_HEAD_DIM, M_HEAD_DIM), lambda b, c: (b, 0, 0, 0)),
                pl.BlockSpec((1, M_HEADS, M_HEAD_DIM), lambda b, c: (b, 0, 0)),
                pl.BlockSpec((1, 1, LANES), lambda b, c: (b, 0, 0))]
    args = [qkm, vm, om, gates, conv_w, conv_b, gbias, g_mh, carry0, c0, n0, m0]
    return pl.pallas_call(
        functools.partial(_mlstm_kernel, L=L),
        grid=(nb, nc),
        in_specs=in_specs,
        out_specs=[pl.BlockSpec((L, M_WIDTH), lambda b, c: (b * nc + c, 0)),
                   pl.BlockSpec((1, M_HEADS, M_HEAD_DIM, M_HEAD_DIM), lambda b, c: (b, 0, 0, 0)),
                   pl.BlockSpec((1, M_HEADS, M_HEAD_DIM), lambda b, c: (b, 0, 0)),
                   pl.BlockSpec((1, 1, LANES), lambda b, c: (b, 0, 0))],
        out_shape=[jax.ShapeDtypeStruct((nb * nc * L, M_WIDTH), F32),
                   jax.ShapeDtypeStruct((nb, M_HEADS, M_HEAD_DIM, M_HEAD_DIM), F32),
                   jax.ShapeDtypeStruct((nb, M_HEADS, M_HEAD_DIM), F32),
                   jax.ShapeDtypeStruct((nb, 1, LANES), F32)],
        scratch_shapes=[pltpu.VMEM((M_HEADS, M_HEAD_DIM, M_HEAD_DIM), F32),
                        pltpu.VMEM((M_HEADS, M_HEAD_DIM), F32),
                        pltpu.VMEM((1, LANES), F32),
                        pltpu.VMEM((SUBLANES, 2 * M_WIDTH), F32)],
        compiler_params=pltpu.CompilerParams(dimension_semantics=("arbitrary", "arbitrary")),
        name=name,
    )(*args)


def _outproj_kernel(xp_ref, xs_ref, attp_ref, atts_ref, hmp_ref, hms_ref, gate_ref, sh_ref, sc_ref, g_ref,
                    w_ref, wr_ref, br_ref, x2_ref, h2_ref, ri_ref, cnt_ref, xbuf, mixbuf):
    x = _load_x(xp_ref, xs_ref, xbuf)
    i = pl.program_id(0)

    @pl.when(i < P_TILES)
    def _():
        mixbuf[:, :A_WIDTH] = attp_ref[...].astype(BF16)
        mixbuf[:, A_WIDTH:] = hmp_ref[...].astype(BF16)

    @pl.when(i >= P_TILES)
    def _():
        mixbuf[:, :A_WIDTH] = atts_ref[...].astype(BF16)
        mixbuf[:, A_WIDTH:] = hms_ref[...].astype(BF16)

    mixed = jnp.dot(mixbuf[...], w_ref[...], preferred_element_type=F32)
    x2 = x + _per_chunk(gate_ref) * mixed
    x2_ref[...] = x2
    h2 = _rmsnorm(x2, g_ref[...]) * (1.0 + _per_chunk(sc_ref)) + _per_chunk(sh_ref)
    for s in range(ROW_SUB):
        h2_ref[pl.ds(s, TM, stride=ROW_SUB), :] = h2[:, s * LANES:(s + 1) * LANES]

    lg = jnp.dot(h2, wr_ref[...], preferred_element_type=F32, precision=HIGHEST) + br_ref[...]
    lane = lax.broadcasted_iota(jnp.int32, (TM, LANES), 1)
    lanef = lane.astype(F32)
    ninf = -jnp.inf

    def first_argmax(vals):
        mx = jnp.max(vals, axis=-1, keepdims=True)
        return mx, jnp.min(jnp.where(vals == mx, lanef, float(LANES)), axis=-1, keepdims=True)

    is_grp = lane < N_GROUPS
    gmax, grp = first_argmax(jnp.where(is_grp, lg, ninf))
    p_grp = 1.0 / jnp.sum(jnp.where(is_grp, jnp.exp(lg - gmax), 0.0), axis=-1, keepdims=True)
    base = N_GROUPS + grp * EXPERTS_PER_GROUP
    in_grp = (lanef >= base) & (lanef < base + EXPERTS_PER_GROUP)
    el = jnp.where(in_grp, lg, ninf)
    v1, i1 = first_argmax(el)
    v2, i2 = first_argmax(jnp.where(lanef == i1, ninf, el))
    e = jnp.exp(v2 - v1)
    g1 = p_grp / (1.0 + e)
    g2 = p_grp * e / (1.0 + e)
    e1 = i1 - N_GROUPS
    e2 = i2 - N_GROUPS
    oh1 = lanef == e1
    oh2 = lanef == e2
    oh = jnp.where(oh1 | oh2, 1.0, 0.0)
    earlier = (lax.broadcasted_iota(jnp.int32, (TM, TM), 0) > lax.broadcasted_iota(jnp.int32, (TM, TM), 1))
    before = jnp.dot(earlier.astype(BF16), oh.astype(BF16), preferred_element_type=F32)
    r1 = jnp.sum(jnp.where(oh1, before, 0.0), axis=-1, keepdims=True)
    r2 = jnp.sum(jnp.where(oh2, before, 0.0), axis=-1, keepdims=True)
    cnt_ref[...] = jnp.broadcast_to(jnp.sum(oh, axis=0, keepdims=True), (SUBLANES, LANES))
    cols = (e1, e2, g1, g2, r1, r2)
    ri = jnp.zeros((TM, LANES), F32)
    for k, cval in enumerate(cols):
        ri = jnp.where(lane == k, cval, ri)
    ri_ref[...] = ri


def _outproj(x_prompt, x_sample, att_p, att_s, hm_p, hm_s, modc, g_ffn, w_out_b, w_router, b_router):
    def tok(n):
        return pl.BlockSpec((TM, n), lambda i: (i, 0))

    def tok_p(n):
        return pl.BlockSpec((TM, n), lambda i: (jnp.minimum(i, P_TILES - 1), 0))

    def tok_s(n):
        return pl.BlockSpec((TM, n), lambda i: (jnp.maximum(i - P_TILES, 0), 0))

    def const(shape):
        return pl.BlockSpec(shape, lambda i: (0,) * len(shape))

    return pl.pallas_call(
        _outproj_kernel,
        grid=(N_TILES,),
        in_specs=[_xp_spec(), _xs_spec(), tok_p(A_WIDTH), tok_s(A_WIDTH), tok_p(M_WIDTH), tok_s(M_WIDTH),
                  _mod_spec(2), _mod_spec(3), _mod_spec(4),
                  const((1, D_MODEL)), const((D_MODEL, D_MODEL)), const((D_MODEL, LANES)), const((1, LANES))],
        out_specs=[tok(D_MODEL), pl.BlockSpec((TM * ROW_SUB, LANES), lambda i: (i, 0)), tok(LANES),
                   pl.BlockSpec((SUBLANES, LANES), lambda i: (i, 0))],
        out_shape=[jax.ShapeDtypeStruct((N_TOK, D_MODEL), F32),
                   jax.ShapeDtypeStruct((N_TOK * ROW_SUB, LANES), F32),
                   jax.ShapeDtypeStruct((N_TOK, LANES), F32),
                   jax.ShapeDtypeStruct((N_TILES * SUBLANES, LANES), F32)],
        scratch_shapes=[pltpu.VMEM((TM, D_MODEL), F32), pltpu.VMEM((TM, D_MODEL), BF16)],
        compiler_params=pltpu.CompilerParams(vmem_limit_bytes=VMEM_LIMIT),
        name="outproj",
    )(x_prompt, x_sample, att_p, att_s, hm_p, hm_s, modc, modc, modc, g_ffn, w_out_b, w_router, b_router)


def _row(ref, r):
    return ref.at[pl.ds(pl.multiple_of(r * ROW_SUB, ROW_SUB), ROW_SUB), :]


def _dispatch_kernel(dest_ref, zblk_ref, h_ref, xs_out, zbuf, sem, zsem):
    i = pl.program_id(0)
    blk_rows = MOE_R * ROW_SUB

    @pl.when(i == 0)
    def _():
        zbuf[...] = jnp.zeros_like(zbuf)

        def zero_copy(e):
            start = pl.multiple_of(jnp.maximum(zblk_ref[e], 0) * blk_rows, blk_rows)
            return pltpu.make_async_copy(zbuf, xs_out.at[pl.ds(start, blk_rows), :], zsem)

        def z_start(e, carry):
            @pl.when(zblk_ref[e] >= 0)
            def _():
                zero_copy(e).start()
            return carry

        def z_wait(e, carry):
            @pl.when(zblk_ref[e] >= 0)
            def _():
                zero_copy(e).wait()
            return carry

        lax.fori_loop(0, N_EXPERTS, z_start, 0)
        lax.fori_loop(0, N_EXPERTS, z_wait, 0)

        def tail_copy(b):
            return pltpu.make_async_copy(zbuf, xs_out.at[pl.ds(pl.multiple_of(b * blk_rows, blk_rows), blk_rows), :], zsem)

        def t_start(b, carry):
            tail_copy(b).start()
            return carry

        def t_wait(b, carry):
            tail_copy(b).wait()
            return carry

        lax.fori_loop(zblk_ref[N_EXPERTS], MOE_BLOCKS, t_start, 0)
        lax.fori_loop(zblk_ref[N_EXPERTS], MOE_BLOCKS, t_wait, 0)

    def copies(t):
        tok = i * TM + t
        return [pltpu.make_async_copy(_row(h_ref, t), _row(xs_out, dest_ref[2 * tok + k]), sem) for k in range(2)]

    def issue(t, carry):
        for cp in copies(t):
            cp.start()
        return carry

    def wait(t, carry):
        for cp in copies(t):
            cp.wait()
        return carry

    lax.fori_loop(0, TM, issue, 0, unroll=8)
    lax.fori_loop(0, TM, wait, 0, unroll=8)


def _dispatch(dest, zblk, h2t):
    return pl.pallas_call(
        _dispatch_kernel,
        grid_spec=pltpu.PrefetchScalarGridSpec(
            num_scalar_prefetch=2,
            grid=(N_TILES,),
            in_specs=[pl.BlockSpec((TM * ROW_SUB, LANES), lambda i, d, z: (i, 0))],
            out_specs=pl.BlockSpec(memory_space=pl.ANY),
            scratch_shapes=[pltpu.VMEM((MOE_R * ROW_SUB, LANES), F32),
                            pltpu.SemaphoreType.DMA, pltpu.SemaphoreType.DMA]),
        out_shape=jax.ShapeDtypeStruct((MOE_ROWS * ROW_SUB, LANES), F32),
        compiler_params=pltpu.CompilerParams(dimension_semantics=("arbitrary",)),
        name="moe_dispatch",
    )(dest, zblk, h2t)


def _expert_kernel(be_ref, bv_ref, xs_ref, wg_ref, wu_ref, wd_ref, o_ref, wg_s, wu_s, wd_s):
    b = pl.program_id(0)
    prev = be_ref[jnp.maximum(b - 1, 0)]

    @pl.when((b == 0) | (be_ref[b] != prev))
    def _():
        wg_s[...] = wg_ref[0].astype(BF16)
        wu_s[...] = wu_ref[0].astype(BF16)
        wd_s[...] = wd_ref[0].astype(BF16)

    @pl.when(bv_ref[b] > 0)
    def _():
        x = jnp.concatenate([xs_ref[pl.ds(s, MOE_R, stride=ROW_SUB), :] for s in range(ROW_SUB)], axis=1)
        xb = x.astype(BF16)
        g = jnp.dot(xb, wg_s[...], preferred_element_type=F32)
        u = jnp.dot(xb, wu_s[...], preferred_element_type=F32)
        a = (g * _sigmoid(g) * u).astype(BF16)
        o = jnp.dot(a, wd_s[...], preferred_element_type=F32)
        for s in range(ROW_SUB):
            o_ref[pl.ds(s, MOE_R, stride=ROW_SUB), :] = o[:, s * LANES:(s + 1) * LANES]

    @pl.when(bv_ref[b] == 0)
    def _():
        o_ref[...] = jnp.zeros_like(o_ref)


def _experts(blk_e, blk_valid, xs, w_g, w_u, w_d):
    return pl.pallas_call(
        _expert_kernel,
        grid_spec=pltpu.PrefetchScalarGridSpec(
            num_scalar_prefetch=2,
            grid=(MOE_BLOCKS,),
            in_specs=[pl.BlockSpec((MOE_R * ROW_SUB, LANES), lambda b, be, bv: (b, 0)),
                      pl.BlockSpec((1, D_MODEL, D_EXPERT), lambda b, be, bv: (be[b], 0, 0)),
                      pl.BlockSpec((1, D_MODEL, D_EXPERT), lambda b, be, bv: (be[b], 0, 0)),
                      pl.BlockSpec((1, D_EXPERT, D_MODEL), lambda b, be, bv: (be[b], 0, 0))],
            out_specs=pl.BlockSpec((MOE_R * ROW_SUB, LANES), lambda b, be, bv: (b, 0)),
            scratch_shapes=[pltpu.VMEM((D_MODEL, D_EXPERT), BF16),
                            pltpu.VMEM((D_MODEL, D_EXPERT), BF16),
                            pltpu.VMEM((D_EXPERT, D_MODEL), BF16)]),
        out_shape=jax.ShapeDtypeStruct((MOE_ROWS * ROW_SUB, LANES), F32),
        compiler_params=pltpu.CompilerParams(dimension_semantics=("arbitrary",), vmem_limit_bytes=VMEM_LIMIT),
        name="moe_experts",
    )(blk_e, blk_valid, xs, w_g, w_u, w_d)


def _combine_kernel(dest_ref, o_hbm, x2_ref, ri_ref, gate_ref, gf_ref, yp_ref, ys_ref, obuf, sem):
    i = pl.program_id(0)

    def copies(t):
        tok = i * TM + t
        return [pltpu.make_async_copy(_row(o_hbm, dest_ref[2 * tok + k]), _row(obuf.at[k], t), sem)
                for k in range(2)]

    def issue(t, carry):
        for cp in copies(t):
            cp.start()
        return carry

    def wait(t, carry):
        for cp in copies(t):
            cp.wait()
        return carry

    lax.fori_loop(0, TM, issue, 0)
    lax.fori_loop(0, TM, wait, 0)

    ri = ri_ref[...]
    g1 = ri[:, 2:3]
    g2 = ri[:, 3:4]
    moe = jnp.concatenate(
        [g1 * obuf[0, pl.ds(s, TM, stride=ROW_SUB), :] + g2 * obuf[1, pl.ds(s, TM, stride=ROW_SUB), :]
         for s in range(ROW_SUB)], axis=1)
    x3 = x2_ref[...] + _per_chunk(gate_ref) * moe
    y = _rmsnorm(x3, gf_ref[...])

    @pl.when(i < P_TILES)
    def _():
        yp_ref[0] = y

    @pl.when(i >= P_TILES)
    def _():
        ys_ref[...] = y.reshape(CH_PER_TILE, DEC_SEQ, D_MODEL)


def _combine(dest, o_rows, x2, rinfo, modc, g_final):
    xp_spec, xs_spec = _xp_spec(), _xs_spec()
    return pl.pallas_call(
        _combine_kernel,
        grid_spec=pltpu.PrefetchScalarGridSpec(
            num_scalar_prefetch=1,
            grid=(N_TILES,),
            in_specs=[pl.BlockSpec(memory_space=pl.ANY),
                      pl.BlockSpec((TM, D_MODEL), lambda i, d: (i, 0)),
                      pl.BlockSpec((TM, LANES), lambda i, d: (i, 0)),
                      pl.BlockSpec((CH_PER_TILE, 1, D_MODEL), lambda i, d: (i, 0, 5)),
                      pl.BlockSpec((1, D_MODEL), lambda i, d: (0, 0))],
            out_specs=[pl.BlockSpec(xp_spec.block_shape, lambda i, d: xp_spec.index_map(i)),
                       pl.BlockSpec(xs_spec.block_shape, lambda i, d: xs_spec.index_map(i))],
            scratch_shapes=[pltpu.VMEM((2, TM * ROW_SUB, LANES), F32), pltpu.SemaphoreType.DMA]),
        out_shape=[jax.ShapeDtypeStruct((BATCH, SEQ, D_MODEL), F32),
                   jax.ShapeDtypeStruct((DEC_BATCH, DEC_SEQ, D_MODEL), F32)],
        compiler_params=pltpu.CompilerParams(dimension_semantics=("arbitrary",), vmem_limit_bytes=VMEM_LIMIT),
        name="moe_combine",
    )(dest, o_rows, x2, rinfo, modc, g_final)


def _moe_plan(rinfo, cnt):
    tile_cnt = cnt.reshape(N_TILES, SUBLANES, LANES)[:, 0, :N_EXPERTS].astype(jnp.int32)
    counts = jnp.sum(tile_cnt, axis=0)
    padded = (counts + MOE_R - 1) // MOE_R * MOE_R
    pad_end = jnp.cumsum(padded)
    base = pad_end - padded
    tile_base = base[None, :] + jnp.cumsum(tile_cnt, axis=0) - tile_cnt
    eid = rinfo[:, 0:2].astype(jnp.int32).reshape(N_TILES, TM, 2, 1)
    rank = rinfo[:, 4:6].astype(jnp.int32).reshape(N_TILES, TM, 2)
    experts = jnp.arange(N_EXPERTS, dtype=jnp.int32)
    dest = jnp.sum(jnp.where(eid == experts, tile_base[:, None, None, :], 0), axis=-1) + rank
    blk_start = jnp.arange(MOE_BLOCKS, dtype=jnp.int32) * MOE_R
    blk_e = jnp.minimum(jnp.sum((pad_end[None, :] <= blk_start[:, None]).astype(jnp.int32), axis=1), N_EXPERTS - 1)
    blk_valid = jnp.clip(counts[blk_e] - (blk_start - base[blk_e]), 0, MOE_R).astype(jnp.int32)
    zblk = jnp.where(counts % MOE_R != 0, (base + counts) // MOE_R, -1)
    zblk = jnp.concatenate([zblk, pad_end[-1:] // MOE_R]).astype(jnp.int32)
    return dest.reshape(-1), blk_e, blk_valid, zblk


def kernel(x_prompt, x_sample, c_prompt, c_sample, cache_win_k, cache_win_v, state_conv, state_C, state_n, state_m, w_ada, b_ada, g_norm_mix, g_norm_ffn, w_in, attn_sinks, conv_w, conv_b, b_igate, b_fgate, g_mhnorm, w_out, w_router_group, b_router_group, w_router_expert, b_router_expert, w_exp_gate, w_exp_up, w_exp_down, g_final):
    l = 0
    mod = _modulation(jnp.concatenate([c_prompt, c_sample], axis=0), w_ada[l], b_ada[l])
    chunk_stream = np.concatenate([np.repeat(np.arange(BATCH), SEQ // CHUNK), BATCH + np.arange(DEC_BATCH)])
    modc = mod[chunk_stream].reshape(N_CHUNKS, 1, 6 * D_MODEL)

    wi = w_in[l]
    s_q, s_k, s_v, s_qkm, s_vm, s_ig, s_fg = 0, 512, 640, 768, 1792, 2304, 2308
    s_om = 2312
    w_in_r = jnp.concatenate(
        [wi[:, s_q:s_qkm], wi[:, s_qkm:s_vm], wi[:, s_vm:s_ig], wi[:, s_om:], wi[:, s_ig:s_om],
         jnp.zeros((D_MODEL, LANES - 2 * M_HEADS), F32)], axis=1).astype(BF16)
    q, kv, qkm, vm, om, gates = _inproj(x_prompt, x_sample, modc, g_norm_mix[l].reshape(1, -1), w_in_r)

    cache_kv = jnp.concatenate([cache_win_k[l].reshape(DEC_BATCH * WINDOW, A_KV_WIDTH),
                                cache_win_v[l].reshape(DEC_BATCH * WINDOW, A_KV_WIDTH)], axis=1)
    att_p, att_s = _attention(attn_sinks[l], q, kv, cache_kv)

    gbias = jnp.concatenate([b_igate[l], b_fgate[l], jnp.zeros((LANES - 2 * M_HEADS,), F32)]).reshape(1, LANES)
    common = (qkm, vm, om, gates, conv_w[l], conv_b[l].reshape(1, -1), gbias, g_mhnorm[l].reshape(1, -1))
    zeros_p = (jnp.zeros((BATCH, SUBLANES, 2 * M_WIDTH), F32),
               jnp.zeros((BATCH, M_HEADS, M_HEAD_DIM, M_HEAD_DIM), F32),
               jnp.zeros((BATCH, M_HEADS, M_HEAD_DIM), F32),
               jnp.zeros((BATCH, 1, LANES), F32))
    LP = 256
    hm_p, C_p, n_p, m_p = _mlstm(*common, *zeros_p, nb=BATCH, L=LP, nc=SEQ // LP, row_off=0, name="mlstm_prompt")
    carry_s = jnp.concatenate([jnp.zeros((DEC_BATCH, SUBLANES - (CONV_WIDTH - 1), 2 * M_WIDTH), F32),
                               state_conv[l]], axis=1)
    m0_s = jnp.pad(state_m[l], ((0, 0), (0, LANES - M_HEADS))).reshape(DEC_BATCH, 1, LANES)
    hm_s, C_s, n_s, m_s = _mlstm(*common, carry_s, state_C[l], state_n[l], m0_s,
                                 nb=DEC_BATCH, L=DEC_SEQ, nc=1, row_off=N_P, name="mlstm_sample")

    w_router = jnp.concatenate([w_router_group[l], w_router_expert[l],
                                jnp.zeros((D_MODEL, LANES - N_GROUPS - N_EXPERTS), F32)], axis=1)
    b_router = jnp.concatenate([b_router_group[l], b_router_expert[l],
                                jnp.zeros((LANES - N_GROUPS - N_EXPERTS,), F32)]).reshape(1, LANES)
    x2, h2t, rinfo, cnt = _outproj(x_prompt, x_sample, att_p, att_s, hm_p, hm_s, modc, g_norm_ffn[l].reshape(1, -1),
                                   w_out[l].astype(BF16), w_router, b_router)

    dest, blk_e, blk_valid, zblk = _moe_plan(rinfo, cnt)
    xs_rows = _dispatch(dest, zblk, h2t)
    o_rows = _experts(blk_e, blk_valid, xs_rows, w_exp_gate[l], w_exp_up[l], w_exp_down[l])
    y_prompt, y_sample = _combine(dest, o_rows, x2, rinfo, modc, g_final.reshape(1, -1))

    kv_p = kv[:N_P].reshape(BATCH, SEQ, 2 * A_KV_WIDTH)[:, -WINDOW:]
    win_k_p = kv_p[..., :A_KV_WIDTH].reshape(1, BATCH, WINDOW, A_KV_HEADS, A_HEAD_DIM)
    win_v_p = kv_p[..., A_KV_WIDTH:].reshape(1, BATCH, WINDOW, A_KV_HEADS, A_HEAD_DIM)
    kv_s = kv[N_P:].reshape(DEC_BATCH, DEC_SEQ, 2 * A_KV_WIDTH)
    win_k_s = jnp.concatenate([cache_win_k[l][:, DEC_SEQ:],
                               kv_s[..., :A_KV_WIDTH].reshape(DEC_BATCH, DEC_SEQ, A_KV_HEADS, A_HEAD_DIM)], axis=1)[None]
    win_v_s = jnp.concatenate([cache_win_v[l][:, DEC_SEQ:],
                               kv_s[..., A_KV_WIDTH:].reshape(DEC_BATCH, DEC_SEQ, A_KV_HEADS, A_HEAD_DIM)], axis=1)[None]
    conv_p = qkm[:N_P].reshape(BATCH, SEQ, 2 * M_WIDTH)[:, -(CONV_WIDTH - 1):][None]
    conv_s = qkm[N_P:].reshape(DEC_BATCH, DEC_SEQ, 2 * M_WIDTH)[:, -(CONV_WIDTH - 1):][None]
    return (y_prompt, y_sample,
            win_k_p, win_v_p, conv_p, C_p[None], n_p[None], m_p[:, 0, :M_HEADS][None],
            win_k_s, win_v_s, conv_s, C_s[None], n_s[None], m_s[:, 0, :M_HEADS][None])
```

```python
import functools

import numpy as np
import jax
import jax.numpy as jnp
from jax import lax
from jax.experimental import pallas as pl
from jax.experimental.pallas import tpu as pltpu

F32 = jnp.float32
BF16 = jnp.bfloat16
HIGHEST = lax.Precision.HIGHEST

LANES = 128
SUBLANES = 8

D_MODEL = 1024
BATCH = 8
SEQ = 2048
DEC_BATCH = 32
DEC_SEQ = 64
PAST_LEN = 4096
CHUNK = 64
A_WIDTH = 512
A_HEAD_DIM = 64
A_HEADS = 8
A_KV_HEADS = 2
A_GROUP = 4
A_KV_WIDTH = 128
WINDOW = 128
ROT_DIM = 16
ROPE_THETA = 500000.0
M_WIDTH = 512
M_HEADS = 4
M_HEAD_DIM = 128
CONV_WIDTH = 4
N_GROUPS = 4
EXPERTS_PER_GROUP = 8
N_EXPERTS = 32
D_EXPERT = 512
EPS = 1e-6

N_P = BATCH * SEQ
N_S = DEC_BATCH * DEC_SEQ
N_TOK = N_P + N_S
N_CHUNKS = N_TOK // CHUNK
TM = 512
N_TILES = N_TOK // TM
P_TILES = N_P // TM
CH_PER_TILE = TM // CHUNK
ROW_SUB = D_MODEL // LANES
N_ASSIGN = 2 * N_TOK
MOE_R = 256
MOE_BLOCKS = N_ASSIGN // MOE_R + N_EXPERTS
MOE_ROWS = MOE_BLOCKS * MOE_R
C_Q, C_K, C_V, C_QKM, C_VM, C_OM, C_G = 0, 512, 640, 768, 1792, 2304, 2816
D_IN_PAD = 2944
VMEM_LIMIT = 48 * 1024 * 1024


def _sigmoid(x):
    return 1.0 / (1.0 + jnp.exp(-x))


def _mod_kernel(c_ref, w_ref, b_ref, o_ref):
    c = c_ref[...]
    s = c * _sigmoid(c)
    o_ref[...] = jnp.dot(s, w_ref[...], preferred_element_type=F32, precision=HIGHEST) + b_ref[...]


def _modulation(c_all, w_ada, b_ada):
    n = c_all.shape[0]
    bn = 512
    return pl.pallas_call(
        _mod_kernel,
        grid=(6 * D_MODEL // bn,),
        in_specs=[pl.BlockSpec((n, D_MODEL), lambda j: (0, 0)),
                  pl.BlockSpec((D_MODEL, bn), lambda j: (0, j)),
                  pl.BlockSpec((1, bn), lambda j: (0, j))],
        out_specs=pl.BlockSpec((n, bn), lambda j: (0, j)),
        out_shape=jax.ShapeDtypeStruct((n, 6 * D_MODEL), F32),
        name="modulation",
    )(c_all, w_ada, b_ada.reshape(1, -1))


def _xp_spec():
    def idx(i):
        t = jnp.minimum(i, P_TILES - 1)
        return (t // (SEQ // TM), t % (SEQ // TM), 0)
    return pl.BlockSpec((1, TM, D_MODEL), idx)


def _xs_spec():
    return pl.BlockSpec((CH_PER_TILE, DEC_SEQ, D_MODEL), lambda i: (jnp.maximum(i - P_TILES, 0), 0, 0))


def _mod_spec(comp):
    return pl.BlockSpec((CH_PER_TILE, 1, D_MODEL), lambda i: (i, 0, comp))


def _load_x(xp_ref, xs_ref, xbuf):
    i = pl.program_id(0)

    @pl.when(i < P_TILES)
    def _():
        xbuf[...] = xp_ref[0]

    @pl.when(i >= P_TILES)
    def _():
        xbuf[...] = xs_ref[...].reshape(TM, D_MODEL)

    return xbuf[...]


def _per_chunk(m_ref):
    m = m_ref[...]
    return jnp.broadcast_to(m, (CH_PER_TILE, CHUNK, D_MODEL)).reshape(TM, D_MODEL)


def _rmsnorm(x, g):
    return x * lax.rsqrt(jnp.mean(x * x, axis=-1, keepdims=True) + EPS) * g


def _rope(x, cos, sa, sb):
    n = x.shape[1]
    rep = n // LANES
    if rep > 1:
        cos = jnp.concatenate([cos] * rep, axis=1)
        sa = jnp.concatenate([sa] * rep, axis=1)
        sb = jnp.concatenate([sb] * rep, axis=1)
    return x * cos + pltpu.roll(x, n - ROT_DIM // 2, 1) * sa + pltpu.roll(x, ROT_DIM // 2, 1) * sb


def _inproj_kernel(xp_ref, xs_ref, sh_ref, sc_ref, g_ref, w_ref, cos_ref, sa_ref, sb_ref,
                   q_ref, kv_ref, qkm_ref, vm_ref, om_ref, gt_ref, xbuf):
    x = _load_x(xp_ref, xs_ref, xbuf)
    h = _rmsnorm(x, g_ref[...]) * (1.0 + _per_chunk(sc_ref)) + _per_chunk(sh_ref)
    hb = h.astype(BF16)

    def proj(a, b):
        return jnp.dot(hb, w_ref[:, a:b], preferred_element_type=F32)

    cos, sa, sb = cos_ref[...], sa_ref[...], sb_ref[...]
    q_ref[...] = _rope(proj(C_Q, C_K), cos, sa, sb)
    kv_ref[:, :A_KV_WIDTH] = _rope(proj(C_K, C_V), cos, sa, sb)
    kv_ref[:, A_KV_WIDTH:] = proj(C_V, C_QKM)
    qkm_ref[...] = proj(C_QKM, C_VM)
    vm_ref[...] = proj(C_VM, C_OM)
    om_ref[...] = proj(C_OM, C_G)
    gt_ref[...] = proj(C_G, D_IN_PAD)


def _rope_tables():
    pos = np.concatenate([np.arange(SEQ), np.tile(PAST_LEN + np.arange(DEC_SEQ), CH_PER_TILE)]).astype(np.float64)
    inv_freq = ROPE_THETA ** (-np.arange(0, ROT_DIM, 2, dtype=np.float64) / ROT_DIM)
    lane = np.arange(LANES)
    hl = lane % A_HEAD_DIM
    ang = pos[:, None] * inv_freq[hl % (ROT_DIM // 2)][None, :]
    rot = (hl < ROT_DIM)[None, :]
    lo = (hl < ROT_DIM // 2)[None, :]
    cos = np.where(rot, np.cos(ang), 1.0)
    sa = np.where(lo, -np.sin(ang), 0.0)
    sb = np.where(rot & ~lo, np.sin(ang), 0.0)
    return [jnp.asarray(t, F32) for t in (cos, sa, sb)]


def _inproj(x_prompt, x_sample, modc, g_mix, w_in_r):
    cos, sa, sb = _rope_tables()
    tab_spec = pl.BlockSpec((TM, LANES), lambda i: (jnp.where(i < P_TILES, i % (SEQ // TM), SEQ // TM), 0))

    def out(n):
        return pl.BlockSpec((TM, n), lambda i: (i, 0)), jax.ShapeDtypeStruct((N_TOK, n), F32)

    outs = [out(A_WIDTH), out(2 * A_KV_WIDTH), out(2 * M_WIDTH), out(M_WIDTH), out(M_WIDTH), out(LANES)]
    return pl.pallas_call(
        _inproj_kernel,
        grid=(N_TILES,),
        in_specs=[_xp_spec(), _xs_spec(), _mod_spec(0), _mod_spec(1),
                  pl.BlockSpec((1, D_MODEL), lambda i: (0, 0)),
                  pl.BlockSpec((D_MODEL, D_IN_PAD), lambda i: (0, 0)),
                  tab_spec, tab_spec, tab_spec],
        out_specs=[o[0] for o in outs],
        out_shape=[o[1] for o in outs],
        scratch_shapes=[pltpu.VMEM((TM, D_MODEL), F32)],
        compiler_params=pltpu.CompilerParams(vmem_limit_bytes=VMEM_LIMIT),
        name="inproj",
    )(x_prompt, x_sample, modc, modc, g_mix, w_in_r, cos, sa, sb)


def _attn_chunk_kernel(sink_ref, q_ref, prev_ref, cur_ref, o_ref):
    q = q_ref[...] * (A_HEAD_DIM ** -0.5)
    kv = jnp.concatenate([prev_ref[...], cur_ref[...]], axis=0)
    for g in range(A_KV_HEADS):
        kg = kv[:, g * A_HEAD_DIM:(g + 1) * A_HEAD_DIM].astype(BF16)
        vg = kv[:, A_KV_WIDTH + g * A_HEAD_DIM:A_KV_WIDTH + (g + 1) * A_HEAD_DIM].astype(BF16)
        heads = [g * A_GROUP + i for i in range(A_GROUP)]
        qc = jnp.concatenate([q[:, h * A_HEAD_DIM:(h + 1) * A_HEAD_DIM] for h in heads], axis=0).astype(BF16)
        snk = jnp.concatenate([jnp.full((CHUNK, 1), sink_ref[h], F32) for h in heads], axis=0)
        s = lax.dot_general(qc, kg, (((1,), (1,)), ((), ())), preferred_element_type=F32)
        mx = jnp.maximum(jnp.max(s, axis=-1, keepdims=True), snk)
        p = jnp.exp(s - mx)
        den = jnp.sum(p, axis=-1, keepdims=True) + jnp.exp(snk - mx)
        o = jnp.dot(p.astype(BF16), vg, preferred_element_type=F32) / den
        for i, h in enumerate(heads):
            o_ref[:, h * A_HEAD_DIM:(h + 1) * A_HEAD_DIM] = o[i * CHUNK:(i + 1) * CHUNK]


ATT_TQ = 256
ATT_QB = 2 * CHUNK


def _attn_band_kernel(sink_ref, q_ref, prev_ref, cur_ref, o_ref, att_t):
    nk = ATT_QB + WINDOW
    q = (q_ref[...] * (A_HEAD_DIM ** -0.5)).astype(BF16)
    kv = jnp.concatenate([prev_ref[...], cur_ref[...]], axis=0)
    k2 = kv[:, :A_KV_WIDTH]
    k2r = pltpu.roll(k2, A_HEAD_DIM, 1)
    low = lax.broadcasted_iota(jnp.int32, k2.shape, 1) < A_HEAD_DIM
    k_placed = {(0, 0): jnp.where(low, k2, 0.0), (0, 1): jnp.where(low, 0.0, k2r),
                (1, 0): jnp.where(low, k2r, 0.0), (1, 1): jnp.where(low, 0.0, k2)}
    k_placed = {key: val.astype(BF16) for key, val in k_placed.items()}
    v_t = kv[:, A_KV_WIDTH:].T.astype(BF16)
    key_chunk = lax.broadcasted_iota(jnp.int32, (nk, ATT_QB), 0) // CHUNK
    q_chunk = lax.broadcasted_iota(jnp.int32, (nk, ATT_QB), 1) // CHUNK
    band = (key_chunk >= q_chunk) & (key_chunk <= q_chunk + WINDOW // CHUNK)
    has_history = pl.program_id(1) > 0
    for blk in range(ATT_TQ // ATT_QB):
        keys = slice(blk * ATT_QB, blk * ATT_QB + nk)
        qrows = slice(blk * ATT_QB, (blk + 1) * ATT_QB)
        valid = band & ((key_chunk >= WINDOW // CHUNK) | has_history) if blk == 0 else band
        for h in range(A_HEADS):
            g = h // A_GROUP
            s_t = lax.dot_general(k_placed[(g, h % 2)][keys], q[qrows, (h // 2) * LANES:(h // 2 + 1) * LANES],
                                  (((1,), (1,)), ((), ())), preferred_element_type=F32)
            s_t = jnp.where(valid, s_t, -jnp.inf)
            snk = sink_ref[h]
            mx = jnp.maximum(jnp.max(s_t, axis=0, keepdims=True), snk)
            p_t = jnp.exp(s_t - mx)
            den = jnp.sum(p_t, axis=0, keepdims=True) + jnp.exp(snk - mx)
            o_t = jnp.dot(v_t[:, keys], p_t.astype(BF16), preferred_element_type=F32)
            att_t[h * A_HEAD_DIM:(h + 1) * A_HEAD_DIM, qrows] = o_t[g * A_HEAD_DIM:(g + 1) * A_HEAD_DIM] / den
    o_ref[...] = att_t[...].T


def _attention(sinks, q, kv, cache_kv):
    smem = pl.BlockSpec(memory_space=pltpu.SMEM)
    tq = ATT_TQ
    nq = SEQ // tq
    att = pl.pallas_call(
        _attn_band_kernel,
        grid=(BATCH, nq),
        scratch_shapes=[pltpu.VMEM((A_WIDTH, ATT_TQ), F32)],
        in_specs=[smem,
                  pl.BlockSpec((tq, A_WIDTH), lambda b, j: (b * nq + j, 0)),
                  pl.BlockSpec((WINDOW, 2 * A_KV_WIDTH),
                               lambda b, j: (jnp.maximum((b * nq + j) * (tq // WINDOW) - 1, 0), 0)),
                  pl.BlockSpec((tq, 2 * A_KV_WIDTH), lambda b, j: (b * nq + j, 0))],
        out_specs=pl.BlockSpec((tq, A_WIDTH), lambda b, j: (b * nq + j, 0)),
        out_shape=jax.ShapeDtypeStruct((N_P, A_WIDTH), F32),
        name="attn_prompt",
    )(sinks, q, kv, kv)
    off = N_P // DEC_SEQ
    att_s = pl.pallas_call(
        _attn_chunk_kernel,
        grid=(DEC_BATCH,),
        in_specs=[smem,
                  pl.BlockSpec((DEC_SEQ, A_WIDTH), lambda b: (off + b, 0)),
                  pl.BlockSpec((WINDOW, 2 * A_KV_WIDTH), lambda b: (b, 0)),
                  pl.BlockSpec((DEC_SEQ, 2 * A_KV_WIDTH), lambda b: (off + b, 0))],
        out_specs=pl.BlockSpec((DEC_SEQ, A_WIDTH), lambda b: (b, 0)),
        out_shape=jax.ShapeDtypeStruct((N_S, A_WIDTH), F32),
        name="attn_sample",
    )(sinks, q, cache_kv, kv)
    return att, att_s


def _conv4(x, w, b):
    y = b + x * w[CONV_WIDTH - 1:CONV_WIDTH]
    for j in range(1, CONV_WIDTH):
        y = y + pltpu.roll(x, j, 0) * w[CONV_WIDTH - 1 - j:CONV_WIDTH - j]
    return y


def _mlstm_kernel(qkm_ref, vm_ref, om_ref, gt_ref, cw_ref, cb_ref, gb_ref, gmh_ref,
                  carry0_ref, c0_ref, n0_ref, m0_ref,
                  hm_ref, cout_ref, nout_ref, mout_ref, c_s, n_s, m_s, carry_s, *, L):
    c = pl.program_id(1)

    @pl.when(c == 0)
    def _():
        c_s[...] = c0_ref[0]
        n_s[...] = n0_ref[0]
        m_s[...] = m0_ref[0]
        carry_s[...] = carry0_ref[0]

    x = qkm_ref[...]
    w = cw_ref[...]
    b = cb_ref[...]
    y = _conv4(x, w, b)
    y8 = _conv4(jnp.concatenate([carry_s[...], x[:SUBLANES]], axis=0), w, b)
    y = jnp.concatenate([y8[SUBLANES:], y[SUBLANES:]], axis=0)
    carry_s[...] = x[L - SUBLANES:]
    a = y * _sigmoid(y)
    qa = a[:, :M_WIDTH] * (M_HEAD_DIM ** -0.5)
    ka = a[:, M_WIDTH:]
    v = vm_ref[...]
    om = om_ref[...]
    gmh = gmh_ref[...]

    z = gt_ref[...] + gb_ref[...]
    lane = lax.broadcasted_iota(jnp.int32, (L, LANES), 1)
    f_log = jnp.minimum(z, 0.0) - jnp.log1p(jnp.exp(-jnp.abs(z)))
    val = jnp.where(lane < M_HEADS, z, f_log)
    row = lax.broadcasted_iota(jnp.int32, (L, L), 0)
    col = lax.broadcasted_iota(jnp.int32, (L, L), 1)
    causal = row >= col
    cum = jnp.dot(causal.astype(F32), val, preferred_element_type=F32, precision=HIGHEST)
    G = jnp.where(lane < M_HEADS, val, cum)
    sel = (lax.broadcasted_iota(jnp.int32, (SUBLANES, LANES), 0)
           == lax.broadcasted_iota(jnp.int32, (SUBLANES, LANES), 1)).astype(F32)
    GT = lax.dot_general(sel, G, (((1,), (1,)), ((), ())), preferred_element_type=F32, precision=HIGHEST)

    m_row = m_s[...]
    lane1 = lax.broadcasted_iota(jnp.int32, (1, LANES), 1)
    outs = []
    for h in range(M_HEADS):
        a_col = G[:, M_HEADS + h:M_HEADS + h + 1]
        i_col = G[:, h:h + 1]
        i_row = GT[h:h + 1, :]
        b_row = GT[M_HEADS + h:M_HEADS + h + 1, :]
        m_prev = m_row[:, h:h + 1]
        dm = jnp.where(causal, a_col - b_row + i_row, -jnp.inf)
        inter = a_col + m_prev
        m_t = jnp.maximum(inter, jnp.max(dm, axis=-1, keepdims=True))
        w_intra = jnp.exp(dm - m_t)
        w_inter = jnp.exp(inter - m_t)
        sl = slice(h * M_HEAD_DIM, (h + 1) * M_HEAD_DIM)
        qh, kh, vh = qa[:, sl], ka[:, sl], v[:, sl]
        qb = qh.astype(BF16)
        scores = lax.dot_general(qb, kh.astype(BF16), (((1,), (1,)), ((), ())), preferred_element_type=F32) * w_intra
        ch = c_s[h]
        nh = n_s[h:h + 1, :]
        num = (jnp.dot(scores.astype(BF16), vh.astype(BF16), preferred_element_type=F32)
               + w_inter * lax.dot_general(qb, ch.astype(BF16), (((1,), (1,)), ((), ())), preferred_element_type=F32))
        den = jnp.sum(scores, axis=-1, keepdims=True) + w_inter * jnp.sum(qh * nh, axis=-1, keepdims=True)
        hv = num / jnp.maximum(jnp.abs(den), jnp.exp(-m_t))
        hn = hv * lax.rsqrt(jnp.mean(hv * hv, axis=-1, keepdims=True) + EPS)
        outs.append(hn * gmh[:, sl] * _sigmoid(om[:, sl]))
        m_new = m_t[L - 1:L, :]
        a_last = a_col[L - 1:L, :]
        w_end = jnp.exp(a_last - a_col + i_col - m_new)
        decay = jnp.exp(a_last + m_prev - m_new)
        kw = kh * w_end
        c_s[h] = decay * ch + lax.dot_general(vh.astype(BF16), kw.astype(BF16), (((0,), (0,)), ((), ())),
                                              preferred_element_type=F32)
        n_s[h:h + 1, :] = decay * nh + jnp.sum(kw, axis=0, keepdims=True)
        m_row = jnp.where(lane1 == h, m_new, m_row)
    m_s[...] = m_row
    hm_ref[...] = jnp.concatenate(outs, axis=1)

    @pl.when(c == pl.num_programs(1) - 1)
    def _():
        cout_ref[0] = c_s[...]
        nout_ref[0] = n_s[...]
        mout_ref[0] = m_s[...]


def _mlstm(qkm, vm, om, gates, conv_w, conv_b, gbias, g_mh, carry0, c0, n0, m0, *, nb, L, nc, row_off, name):
    off = row_off // L

    def tok(n):
        return pl.BlockSpec((L, n), lambda b, c: (off + b * nc + c, 0))

    def const(shape):
        return pl.BlockSpec(shape, lambda b, c: (0,) * len(shape))

    in_specs = [tok(2 * M_WIDTH), tok(M_WIDTH), tok(M_WIDTH), tok(LANES),
                const((CONV_WIDTH, 2 * M_WIDTH)), const((1, 2 * M_WIDTH)), const((1, LANES)), const((1, M_WIDTH)),
                pl.BlockSpec((1, SUBLANES, 2 * M_WIDTH), lambda b, c: (b, 0, 0)),
                pl.BlockSpec((1, M_HEADS, M_HEAD_DIM, M_HEAD_DIM), lambda b, c: (b, 0, 0, 0)),
                pl.BlockSpec((1, M_HEADS, M_HEAD_DIM), lambda b, c: (b, 0, 0)),
                pl.BlockSpec((1, 1, LANES), lambda b, c: (b, 0, 0))]
    args = [qkm, vm, om, gates, conv_w, conv_b, gbias, g_mh, carry0, c0, n0, m0]
    return pl.pallas_call(
        functools.partial(_mlstm_kernel, L=L),
        grid=(nb, nc),
        in_specs=in_specs,
        out_specs=[pl.BlockSpec((L, M_WIDTH), lambda b, c: (b * nc + c, 0)),
                   pl.BlockSpec((1, M_HEADS, M_HEAD_DIM, M_HEAD_DIM), lambda b, c: (b, 0, 0, 0)),
                   pl.BlockSpec((1, M_HEADS, M_HEAD_DIM), lambda b, c: (b, 0, 0)),
                   pl.BlockSpec((1, 1, LANES), lambda b, c: (b, 0, 0))],
        out_shape=[jax.ShapeDtypeStruct((nb * nc * L, M_WIDTH), F32),
                   jax.ShapeDtypeStruct((nb, M_HEADS, M_HEAD_DIM, M_HEAD_DIM), F32),
                   jax.ShapeDtypeStruct((nb, M_HEADS, M_HEAD_DIM), F32),
                   jax.ShapeDtypeStruct((nb, 1, LANES), F32)],
        scratch_shapes=[pltpu.VMEM((M_HEADS, M_HEAD_DIM, M_HEAD_DIM), F32),
                        pltpu.VMEM((M_HEADS, M_HEAD_DIM), F32),
                        pltpu.VMEM((1, LANES), F32),
                        pltpu.VMEM((SUBLANES, 2 * M_WIDTH), F32)],
        compiler_params=pltpu.CompilerParams(dimension_semantics=("arbitrary", "arbitrary")),
        name=name,
    )(*args)


def _outproj_kernel(xp_ref, xs_ref, attp_ref, atts_ref, hmp_ref, hms_ref, gate_ref, sh_ref, sc_ref, g_ref,
                    w_ref, wr_ref, br_ref, x2_ref, h2_ref, ri_ref, cnt_ref, xbuf, mixbuf):
    x = _load_x(xp_ref, xs_ref, xbuf)
    i = pl.program_id(0)

    @pl.when(i < P_TILES)
    def _():
        mixbuf[:, :A_WIDTH] = attp_ref[...].astype(BF16)
        mixbuf[:, A_WIDTH:] = hmp_ref[...].astype(BF16)

    @pl.when(i >= P_TILES)
    def _():
        mixbuf[:, :A_WIDTH] = atts_ref[...].astype(BF16)
        mixbuf[:, A_WIDTH:] = hms_ref[...].astype(BF16)

    mixed = jnp.dot(mixbuf[...], w_ref[...], preferred_element_type=F32)
    x2 = x + _per_chunk(gate_ref) * mixed
    x2_ref[...] = x2
    h2 = _rmsnorm(x2, g_ref[...]) * (1.0 + _per_chunk(sc_ref)) + _per_chunk(sh_ref)
    for s in range(ROW_SUB):
        h2_ref[pl.ds(s, TM, stride=ROW_SUB), :] = h2[:, s * LANES:(s + 1) * LANES]

    lg = jnp.dot(h2, wr_ref[...], preferred_element_type=F32, precision=HIGHEST) + br_ref[...]
    lane = lax.broadcasted_iota(jnp.int32, (TM, LANES), 1)
    lanef = lane.astype(F32)
    ninf = -jnp.inf

    def first_argmax(vals):
        mx = jnp.max(vals, axis=-1, keepdims=True)
        return mx, jnp.min(jnp.where(vals == mx, lanef, float(LANES)), axis=-1, keepdims=True)

    is_grp = lane < N_GROUPS
    gmax, grp = first_argmax(jnp.where(is_grp, lg, ninf))
    p_grp = 1.0 / jnp.sum(jnp.where(is_grp, jnp.exp(lg - gmax), 0.0), axis=-1, keepdims=True)
    base = N_GROUPS + grp * EXPERTS_PER_GROUP
    in_grp = (lanef >= base) & (lanef < base + EXPERTS_PER_GROUP)
    el = jnp.where(in_grp, lg, ninf)
    v1, i1 = first_argmax(el)
    v2, i2 = first_argmax(jnp.where(lanef == i1, ninf, el))
    e = jnp.exp(v2 - v1)
    g1 = p_grp / (1.0 + e)
    g2 = p_grp * e / (1.0 + e)
    e1 = i1 - N_GROUPS
    e2 = i2 - N_GROUPS
    oh1 = lanef == e1
    oh2 = lanef == e2
    oh = jnp.where(oh1 | oh2, 1.0, 0.0)
    earlier = (lax.broadcasted_iota(jnp.int32, (TM, TM), 0) > lax.broadcasted_iota(jnp.int32, (TM, TM), 1))
    before = jnp.dot(earlier.astype(BF16), oh.astype(BF16), preferred_element_type=F32)
    r1 = jnp.sum(jnp.where(oh1, before, 0.0), axis=-1, keepdims=True)
    r2 = jnp.sum(jnp.where(oh2, before, 0.0), axis=-1, keepdims=True)
    cnt_ref[...] = jnp.broadcast_to(jnp.sum(oh, axis=0, keepdims=True), (SUBLANES, LANES))
    cols = (e1, e2, g1, g2, r1, r2)
    ri = jnp.zeros((TM, LANES), F32)
    for k, cval in enumerate(cols):
        ri = jnp.where(lane == k, cval, ri)
    ri_ref[...] = ri


def _outproj(x_prompt, x_sample, att_p, att_s, hm_p, hm_s, modc, g_ffn, w_out_b, w_router, b_router):
    def tok(n):
        return pl.BlockSpec((TM, n), lambda i: (i, 0))

    def tok_p(n):
        return pl.BlockSpec((TM, n), lambda i: (jnp.minimum(i, P_TILES - 1), 0))

    def tok_s(n):
        return pl.BlockSpec((TM, n), lambda i: (jnp.maximum(i - P_TILES, 0), 0))

    def const(shape):
        return pl.BlockSpec(shape, lambda i: (0,) * len(shape))

    return pl.pallas_call(
        _outproj_kernel,
        grid=(N_TILES,),
        in_specs=[_xp_spec(), _xs_spec(), tok_p(A_WIDTH), tok_s(A_WIDTH), tok_p(M_WIDTH), tok_s(M_WIDTH),
                  _mod_spec(2), _mod_spec(3), _mod_spec(4),
                  const((1, D_MODEL)), const((D_MODEL, D_MODEL)), const((D_MODEL, LANES)), const((1, LANES))],
        out_specs=[tok(D_MODEL), pl.BlockSpec((TM * ROW_SUB, LANES), lambda i: (i, 0)), tok(LANES),
                   pl.BlockSpec((SUBLANES, LANES), lambda i: (i, 0))],
        out_shape=[jax.ShapeDtypeStruct((N_TOK, D_MODEL), F32),
                   jax.ShapeDtypeStruct((N_TOK * ROW_SUB, LANES), F32),
                   jax.ShapeDtypeStruct((N_TOK, LANES), F32),
                   jax.ShapeDtypeStruct((N_TILES * SUBLANES, LANES), F32)],
        scratch_shapes=[pltpu.VMEM((TM, D_MODEL), F32), pltpu.VMEM((TM, D_MODEL), BF16)],
        compiler_params=pltpu.CompilerParams(vmem_limit_bytes=VMEM_LIMIT),
        name="outproj",
    )(x_prompt, x_sample, att_p, att_s, hm_p, hm_s, modc, modc, modc, g_ffn, w_out_b, w_router, b_router)


def _row(ref, r):
    return ref.at[pl.ds(pl.multiple_of(r * ROW_SUB, ROW_SUB), ROW_SUB), :]


def _dispatch_kernel(dest_ref, zblk_ref, h_ref, xs_out, zbuf, sem, zsem):
    i = pl.program_id(0)
    blk_rows = MOE_R * ROW_SUB

    @pl.when(i == 0)
    def _():
        zbuf[...] = jnp.zeros_like(zbuf)

        def zero_copy(e):
            start = pl.multiple_of(jnp.maximum(zblk_ref[e], 0) * blk_rows, blk_rows)
            return pltpu.make_async_copy(zbuf, xs_out.at[pl.ds(start, blk_rows), :], zsem)

        def z_start(e, carry):
            @pl.when(zblk_ref[e] >= 0)
            def _():
                zero_copy(e).start()
            return carry

        def z_wait(e, carry):
            @pl.when(zblk_ref[e] >= 0)
            def _():
                zero_copy(e).wait()
            return carry

        lax.fori_loop(0, N_EXPERTS, z_start, 0)
        lax.fori_loop(0, N_EXPERTS, z_wait, 0)

        def tail_copy(b):
            return pltpu.make_async_copy(zbuf, xs_out.at[pl.ds(pl.multiple_of(b * blk_rows, blk_rows), blk_rows), :], zsem)

        def t_start(b, carry):
            tail_copy(b).start()
            return carry

        def t_wait(b, carry):
            tail_copy(b).wait()
            return carry

        lax.fori_loop(zblk_ref[N_EXPERTS], MOE_BLOCKS, t_start, 0)
        lax.fori_loop(zblk_ref[N_EXPERTS], MOE_BLOCKS, t_wait, 0)

    def copies(t):
        tok = i * TM + t
        return [pltpu.make_async_copy(_row(h_ref, t), _row(xs_out, dest_ref[2 * tok + k]), sem) for k in range(2)]

    def issue(t, carry):
        for k, cp in enumerate(copies(t)):
            cp.start(priority=k)
        return carry

    def wait(t, carry):
        for cp in copies(t):
            cp.wait()
        return carry

    lax.fori_loop(0, TM, issue, 0, unroll=8)
    lax.fori_loop(0, TM, wait, 0, unroll=8)


def _dispatch(dest, zblk, h2t):
    return pl.pallas_call(
        _dispatch_kernel,
        grid_spec=pltpu.PrefetchScalarGridSpec(
            num_scalar_prefetch=2,
            grid=(N_TILES,),
            in_specs=[pl.BlockSpec((TM * ROW_SUB, LANES), lambda i, d, z: (i, 0))],
            out_specs=pl.BlockSpec(memory_space=pl.ANY),
            scratch_shapes=[pltpu.VMEM((MOE_R * ROW_SUB, LANES), F32),
                            pltpu.SemaphoreType.DMA, pltpu.SemaphoreType.DMA]),
        out_shape=jax.ShapeDtypeStruct((MOE_ROWS * ROW_SUB, LANES), F32),
        compiler_params=pltpu.CompilerParams(dimension_semantics=("arbitrary",)),
        name="moe_dispatch",
    )(dest, zblk, h2t)


def _expert_kernel(be_ref, bv_ref, xs_ref, wg_ref, wu_ref, wd_ref, o_ref, wg_s, wu_s, wd_s):
    b = pl.program_id(0)
    prev = be_ref[jnp.maximum(b - 1, 0)]

    @pl.when((b == 0) | (be_ref[b] != prev))
    def _():
        wg_s[...] = wg_ref[0].astype(BF16)
        wu_s[...] = wu_ref[0].astype(BF16)
        wd_s[...] = wd_ref[0].astype(BF16)

    @pl.when(bv_ref[b] > 0)
    def _():
        x = jnp.concatenate([xs_ref[pl.ds(s, MOE_R, stride=ROW_SUB), :] for s in range(ROW_SUB)], axis=1)
        xb = x.astype(BF16)
        g = jnp.dot(xb, wg_s[...], preferred_element_type=F32)
        u = jnp.dot(xb, wu_s[...], preferred_element_type=F32)
        a = (g * _sigmoid(g) * u).astype(BF16)
        o = jnp.dot(a, wd_s[...], preferred_element_type=F32)
        for s in range(ROW_SUB):
            o_ref[pl.ds(s, MOE_R, stride=ROW_SUB), :] = o[:, s * LANES:(s + 1) * LANES]

    @pl.when(bv_ref[b] == 0)
    def _():
        o_ref[...] = jnp.zeros_like(o_ref)


def _experts(blk_e, blk_valid, xs, w_g, w_u, w_d):
    return pl.pallas_call(
        _expert_kernel,
        grid_spec=pltpu.PrefetchScalarGridSpec(
            num_scalar_prefetch=2,
            grid=(MOE_BLOCKS,),
            in_specs=[pl.BlockSpec((MOE_R * ROW_SUB, LANES), lambda b, be, bv: (b, 0)),
                      pl.BlockSpec((1, D_MODEL, D_EXPERT), lambda b, be, bv: (be[b], 0, 0)),
                      pl.BlockSpec((1, D_MODEL, D_EXPERT), lambda b, be, bv: (be[b], 0, 0)),
                      pl.BlockSpec((1, D_EXPERT, D_MODEL), lambda b, be, bv: (be[b], 0, 0))],
            out_specs=pl.BlockSpec((MOE_R * ROW_SUB, LANES), lambda b, be, bv: (b, 0)),
            scratch_shapes=[pltpu.VMEM((D_MODEL, D_EXPERT), BF16),
                            pltpu.VMEM((D_MODEL, D_EXPERT), BF16),
                            pltpu.VMEM((D_EXPERT, D_MODEL), BF16)]),
        out_shape=jax.ShapeDtypeStruct((MOE_ROWS * ROW_SUB, LANES), F32),
        compiler_params=pltpu.CompilerParams(dimension_semantics=("arbitrary",), vmem_limit_bytes=VMEM_LIMIT),
        name="moe_experts",
    )(blk_e, blk_valid, xs, w_g, w_u, w_d)


def _combine_kernel(dest_ref, o_hbm, x2_ref, ri_ref, gate_ref, gf_ref, yp_ref, ys_ref, obuf, sem):
    i = pl.program_id(0)

    def copies(t):
        tok = i * TM + t
        return [pltpu.make_async_copy(_row(o_hbm, dest_ref[2 * tok + k]), _row(obuf.at[k], t), sem)
                for k in range(2)]

    def issue(t, carry):
        for k, cp in enumerate(copies(t)):
            cp.start(priority=k)
        return carry

    def wait(t, carry):
        for cp in copies(t):
            cp.wait()
        return carry

    lax.fori_loop(0, TM, issue, 0, unroll=8)
    lax.fori_loop(0, TM, wait, 0, unroll=8)

    ri = ri_ref[...]
    g1 = ri[:, 2:3]
    g2 = ri[:, 3:4]
    moe = jnp.concatenate(
        [g1 * obuf[0, pl.ds(s, TM, stride=ROW_SUB), :] + g2 * obuf[1, pl.ds(s, TM, stride=ROW_SUB), :]
         for s in range(ROW_SUB)], axis=1)
    x3 = x2_ref[...] + _per_chunk(gate_ref) * moe
    y = _rmsnorm(x3, gf_ref[...])

    @pl.when(i < P_TILES)
    def _():
        yp_ref[0] = y

    @pl.when(i >= P_TILES)
    def _():
        ys_ref[...] = y.reshape(CH_PER_TILE, DEC_SEQ, D_MODEL)


def _combine(dest, o_rows, x2, rinfo, modc, g_final):
    xp_spec, xs_spec = _xp_spec(), _xs_spec()
    return pl.pallas_call(
        _combine_kernel,
        grid_spec=pltpu.PrefetchScalarGridSpec(
            num_scalar_prefetch=1,
            grid=(N_TILES,),
            in_specs=[pl.BlockSpec(memory_space=pl.ANY),
                      pl.BlockSpec((TM, D_MODEL), lambda i, d: (i, 0)),
                      pl.BlockSpec((TM, LANES), lambda i, d: (i, 0)),
                      pl.BlockSpec((CH_PER_TILE, 1, D_MODEL), lambda i, d: (i, 0, 5)),
                      pl.BlockSpec((1, D_MODEL), lambda i, d: (0, 0))],
            out_specs=[pl.BlockSpec(xp_spec.block_shape, lambda i, d: xp_spec.index_map(i)),
                       pl.BlockSpec(xs_spec.block_shape, lambda i, d: xs_spec.index_map(i))],
            scratch_shapes=[pltpu.VMEM((2, TM * ROW_SUB, LANES), F32), pltpu.SemaphoreType.DMA]),
        out_shape=[jax.ShapeDtypeStruct((BATCH, SEQ, D_MODEL), F32),
                   jax.ShapeDtypeStruct((DEC_BATCH, DEC_SEQ, D_MODEL), F32)],
        compiler_params=pltpu.CompilerParams(dimension_semantics=("arbitrary",), vmem_limit_bytes=VMEM_LIMIT),
        name="moe_combine",
    )(dest, o_rows, x2, rinfo, modc, g_final)


def _moe_plan(rinfo, cnt):
    tile_cnt = cnt.reshape(N_TILES, SUBLANES, LANES)[:, 0, :N_EXPERTS].astype(jnp.int32)
    counts = jnp.sum(tile_cnt, axis=0)
    padded = (counts + MOE_R - 1) // MOE_R * MOE_R
    pad_end = jnp.cumsum(padded)
    base = pad_end - padded
    tile_base = base[None, :] + jnp.cumsum(tile_cnt, axis=0) - tile_cnt
    eid = rinfo[:, 0:2].astype(jnp.int32).reshape(N_TILES, TM, 2, 1)
    rank = rinfo[:, 4:6].astype(jnp.int32).reshape(N_TILES, TM, 2)
    experts = jnp.arange(N_EXPERTS, dtype=jnp.int32)
    dest = jnp.sum(jnp.where(eid == experts, tile_base[:, None, None, :], 0), axis=-1) + rank
    blk_start = jnp.arange(MOE_BLOCKS, dtype=jnp.int32) * MOE_R
    blk_e = jnp.minimum(jnp.sum((pad_end[None, :] <= blk_start[:, None]).astype(jnp.int32), axis=1), N_EXPERTS - 1)
    blk_valid = jnp.clip(counts[blk_e] - (blk_start - base[blk_e]), 0, MOE_R).astype(jnp.int32)
    zblk = jnp.where(counts % MOE_R != 0, (base + counts) // MOE_R, -1)
    zblk = jnp.concatenate([zblk, pad_end[-1:] // MOE_R]).astype(jnp.int32)
    return dest.reshape(-1), blk_e, blk_valid, zblk


def kernel(x_prompt, x_sample, c_prompt, c_sample, cache_win_k, cache_win_v, state_conv, state_C, state_n, state_m, w_ada, b_ada, g_norm_mix, g_norm_ffn, w_in, attn_sinks, conv_w, conv_b, b_igate, b_fgate, g_mhnorm, w_out, w_router_group, b_router_group, w_router_expert, b_router_expert, w_exp_gate, w_exp_up, w_exp_down, g_final):
    l = 0
    mod = _modulation(jnp.concatenate([c_prompt, c_sample], axis=0), w_ada[l], b_ada[l])
    chunk_stream = np.concatenate([np.repeat(np.arange(BATCH), SEQ // CHUNK), BATCH + np.arange(DEC_BATCH)])
    modc = mod[chunk_stream].reshape(N_CHUNKS, 1, 6 * D_MODEL)

    wi = w_in[l]
    s_q, s_k, s_v, s_qkm, s_vm, s_ig, s_fg = 0, 512, 640, 768, 1792, 2304, 2308
    s_om = 2312
    w_in_r = jnp.concatenate(
        [wi[:, s_q:s_qkm], wi[:, s_qkm:s_vm], wi[:, s_vm:s_ig], wi[:, s_om:], wi[:, s_ig:s_om],
         jnp.zeros((D_MODEL, LANES - 2 * M_HEADS), F32)], axis=1).astype(BF16)
    q, kv, qkm, vm, om, gates = _inproj(x_prompt, x_sample, modc, g_norm_mix[l].reshape(1, -1), w_in_r)

    cache_kv = jnp.concatenate([cache_win_k[l].reshape(DEC_BATCH * WINDOW, A_KV_WIDTH),
                                cache_win_v[l].reshape(DEC_BATCH * WINDOW, A_KV_WIDTH)], axis=1)
    att_p, att_s = _attention(attn_sinks[l], q, kv, cache_kv)

    gbias = jnp.concatenate([b_igate[l], b_fgate[l], jnp.zeros((LANES - 2 * M_HEADS,), F32)]).reshape(1, LANES)
    common = (qkm, vm, om, gates, conv_w[l], conv_b[l].reshape(1, -1), gbias, g_mhnorm[l].reshape(1, -1))
    zeros_p = (jnp.zeros((BATCH, SUBLANES, 2 * M_WIDTH), F32),
               jnp.zeros((BATCH, M_HEADS, M_HEAD_DIM, M_HEAD_DIM), F32),
               jnp.zeros((BATCH, M_HEADS, M_HEAD_DIM), F32),
               jnp.zeros((BATCH, 1, LANES), F32))
    LP = 256
    hm_p, C_p, n_p, m_p = _mlstm(*common, *zeros_p, nb=BATCH, L=LP, nc=SEQ // LP, row_off=0, name="mlstm_prompt")
    carry_s = jnp.concatenate([jnp.zeros((DEC_BATCH, SUBLANES - (CONV_WIDTH - 1), 2 * M_WIDTH), F32),
                               state_conv[l]], axis=1)
    m0_s = jnp.pad(state_m[l], ((0, 0), (0, LANES - M_HEADS))).reshape(DEC_BATCH, 1, LANES)
    hm_s, C_s, n_s, m_s = _mlstm(*common, carry_s, state_C[l], state_n[l], m0_s,
                                 nb=DEC_BATCH, L=DEC_SEQ, nc=1, row_off=N_P, name="mlstm_sample")

    w_router = jnp.concatenate([w_router_group[l], w_router_expert[l],
                                jnp.zeros((D_MODEL, LANES - N_GROUPS - N_EXPERTS), F32)], axis=1)
    b_router = jnp.concatenate([b_router_group[l], b_router_expert[l],
                                jnp.zeros((LANES - N_GROUPS - N_EXPERTS,), F32)]).reshape(1, LANES)
    x2, h2t, rinfo, cnt = _outproj(x_prompt, x_sample, att_p, att_s, hm_p, hm_s, modc, g_norm_ffn[l].reshape(1, -1),
                                   w_out[l].astype(BF16), w_router, b_router)

    dest, blk_e, blk_valid, zblk = _moe_plan(rinfo, cnt)
    xs_rows = _dispatch(dest, zblk, h2t)
    o_rows = _experts(blk_e, blk_valid, xs_rows, w_exp_gate[l], w_exp_up[l], w_exp_down[l])
    y_prompt, y_sample = _combine(dest, o_rows, x2, rinfo, modc, g_final.reshape(1, -1))

    kv_p = kv[:N_P].reshape(BATCH, SEQ, 2 * A_KV_WIDTH)[:, -WINDOW:]
    win_k_p = kv_p[..., :A_KV_WIDTH].reshape(1, BATCH, WINDOW, A_KV_HEADS, A_HEAD_DIM)
    win_v_p = kv_p[..., A_KV_WIDTH:].reshape(1, BATCH, WINDOW, A_KV_HEADS, A_HEAD_DIM)
    kv_s = kv[N_P:].reshape(DEC_BATCH, DEC_SEQ, 2 * A_KV_WIDTH)
    win_k_s = jnp.concatenate([cache_win_k[l][:, DEC_SEQ:],
                               kv_s[..., :A_KV_WIDTH].reshape(DEC_BATCH, DEC_SEQ, A_KV_HEADS, A_HEAD_DIM)], axis=1)[None]
    win_v_s = jnp.concatenate([cache_win_v[l][:, DEC_SEQ:],
                               kv_s[..., A_KV_WIDTH:].reshape(DEC_BATCH, DEC_SEQ, A_KV_HEADS, A_HEAD_DIM)], axis=1)[None]
    conv_p = qkm[:N_P].reshape(BATCH, SEQ, 2 * M_WIDTH)[:, -(CONV_WIDTH - 1):][None]
    conv_s = qkm[N_P:].reshape(DEC_BATCH, DEC_SEQ, 2 * M_WIDTH)[:, -(CONV_WIDTH - 1):][None]
    return (y_prompt, y_sample,
            win_k_p, win_v_p, conv_p, C_p[None], n_p[None], m_p[:, 0, :M_HEADS][None],
            win_k_s, win_v_s, conv_s, C_s[None], n_s[None], m_s[:, 0, :M_HEADS][None])
```

```python
import functools

import numpy as np
import jax
import jax.numpy as jnp
from jax import lax
from jax.experimental import pallas as pl
from jax.experimental.pallas import tpu as pltpu

F32 = jnp.float32
BF16 = jnp.bfloat16
HIGHEST = lax.Precision.HIGHEST

LANES = 128
SUBLANES = 8

D_MODEL = 1024
BATCH = 8
SEQ = 2048
DEC_BATCH = 32
DEC_SEQ = 64
PAST_LEN = 4096
CHUNK = 64
A_WIDTH = 512
A_HEAD_DIM = 64
A_HEADS = 8
A_KV_HEADS = 2
A_GROUP = 4
A_KV_WIDTH = 128
WINDOW = 128
ROT_DIM = 16
ROPE_THETA = 500000.0
M_WIDTH = 512
M_HEADS = 4
M_HEAD_DIM = 128
CONV_WIDTH = 4
N_GROUPS = 4
EXPERTS_PER_GROUP = 8
N_EXPERTS = 32
D_EXPERT = 512
EPS = 1e-6

N_P = BATCH * SEQ
N_S = DEC_BATCH * DEC_SEQ
N_TOK = N_P + N_S
N_CHUNKS = N_TOK // CHUNK
TM = 512
N_TILES = N_TOK // TM
P_TILES = N_P // TM
CH_PER_TILE = TM // CHUNK
ROW_SUB = D_MODEL // LANES
N_ASSIGN = 2 * N_TOK
MOE_R = 256
MOE_BLOCKS = N_ASSIGN // MOE_R + N_EXPERTS
MOE_ROWS = MOE_BLOCKS * MOE_R
C_Q, C_K, C_V, C_QKM, C_VM, C_OM, C_G = 0, 512, 640, 768, 1792, 2304, 2816
D_IN_PAD = 2944
VMEM_LIMIT = 48 * 1024 * 1024


def _sigmoid(x):
    return 1.0 / (1.0 + jnp.exp(-x))


def _mod_kernel(c_ref, w_ref, b_ref, o_ref):
    c = c_ref[...]
    s = c * _sigmoid(c)
    o_ref[...] = jnp.dot(s, w_ref[...], preferred_element_type=F32, precision=HIGHEST) + b_ref[...]


def _modulation(c_all, w_ada, b_ada):
    n = c_all.shape[0]
    bn = 512
    return pl.pallas_call(
        _mod_kernel,
        grid=(6 * D_MODEL // bn,),
        in_specs=[pl.BlockSpec((n, D_MODEL), lambda j: (0, 0)),
                  pl.BlockSpec((D_MODEL, bn), lambda j: (0, j)),
                  pl.BlockSpec((1, bn), lambda j: (0, j))],
        out_specs=pl.BlockSpec((n, bn), lambda j: (0, j)),
        out_shape=jax.ShapeDtypeStruct((n, 6 * D_MODEL), F32),
        name="modulation",
    )(c_all, w_ada, b_ada.reshape(1, -1))


def _xp_spec():
    def idx(i):
        t = jnp.minimum(i, P_TILES - 1)
        return (t // (SEQ // TM), t % (SEQ // TM), 0)
    return pl.BlockSpec((1, TM, D_MODEL), idx)


def _xs_spec():
    return pl.BlockSpec((CH_PER_TILE, DEC_SEQ, D_MODEL), lambda i: (jnp.maximum(i - P_TILES, 0), 0, 0))


def _mod_spec(comp):
    return pl.BlockSpec((CH_PER_TILE, 1, D_MODEL), lambda i: (i, 0, comp))


def _load_x(xp_ref, xs_ref, xbuf):
    i = pl.program_id(0)

    @pl.when(i < P_TILES)
    def _():
        xbuf[...] = xp_ref[0]

    @pl.when(i >= P_TILES)
    def _():
        xbuf[...] = xs_ref[...].reshape(TM, D_MODEL)

    return xbuf[...]


def _per_chunk(m_ref):
    m = m_ref[...]
    return jnp.broadcast_to(m, (CH_PER_TILE, CHUNK, D_MODEL)).reshape(TM, D_MODEL)


def _rmsnorm(x, g):
    return x * lax.rsqrt(jnp.mean(x * x, axis=-1, keepdims=True) + EPS) * g


def _rope(x, cos, sa, sb):
    n = x.shape[1]
    rep = n // LANES
    if rep > 1:
        cos = jnp.concatenate([cos] * rep, axis=1)
        sa = jnp.concatenate([sa] * rep, axis=1)
        sb = jnp.concatenate([sb] * rep, axis=1)
    return x * cos + pltpu.roll(x, n - ROT_DIM // 2, 1) * sa + pltpu.roll(x, ROT_DIM // 2, 1) * sb


def _inproj_kernel(xp_ref, xs_ref, sh_ref, sc_ref, g_ref, w_ref, cos_ref, sa_ref, sb_ref,
                   q_ref, kv_ref, qkm_ref, vm_ref, om_ref, gt_ref, xbuf):
    x = _load_x(xp_ref, xs_ref, xbuf)
    h = _rmsnorm(x, g_ref[...]) * (1.0 + _per_chunk(sc_ref)) + _per_chunk(sh_ref)
    hb = h.astype(BF16)

    def proj(a, b):
        return jnp.dot(hb, w_ref[:, a:b], preferred_element_type=F32)

    cos, sa, sb = cos_ref[...], sa_ref[...], sb_ref[...]
    q_ref[...] = _rope(proj(C_Q, C_K), cos, sa, sb)
    kv_ref[:, :A_KV_WIDTH] = _rope(proj(C_K, C_V), cos, sa, sb)
    kv_ref[:, A_KV_WIDTH:] = proj(C_V, C_QKM)
    qkm_ref[...] = proj(C_QKM, C_VM)
    vm_ref[...] = proj(C_VM, C_OM)
    om_ref[...] = proj(C_OM, C_G)
    gt_ref[...] = proj(C_G, D_IN_PAD)


def _rope_tables():
    pos = np.concatenate([np.arange(SEQ), np.tile(PAST_LEN + np.arange(DEC_SEQ), CH_PER_TILE)]).astype(np.float64)
    inv_freq = ROPE_THETA ** (-np.arange(0, ROT_DIM, 2, dtype=np.float64) / ROT_DIM)
    lane = np.arange(LANES)
    hl = lane % A_HEAD_DIM
    ang = pos[:, None] * inv_freq[hl % (ROT_DIM // 2)][None, :]
    rot = (hl < ROT_DIM)[None, :]
    lo = (hl < ROT_DIM // 2)[None, :]
    cos = np.where(rot, np.cos(ang), 1.0)
    sa = np.where(lo, -np.sin(ang), 0.0)
    sb = np.where(rot & ~lo, np.sin(ang), 0.0)
    return [jnp.asarray(t, F32) for t in (cos, sa, sb)]


def _inproj(x_prompt, x_sample, modc, g_mix, w_in_r):
    cos, sa, sb = _rope_tables()
    tab_spec = pl.BlockSpec((TM, LANES), lambda i: (jnp.where(i < P_TILES, i % (SEQ // TM), SEQ // TM), 0))

    def out(n):
        return pl.BlockSpec((TM, n), lambda i: (i, 0)), jax.ShapeDtypeStruct((N_TOK, n), F32)

    outs = [out(A_WIDTH), out(2 * A_KV_WIDTH), out(2 * M_WIDTH), out(M_WIDTH), out(M_WIDTH), out(LANES)]
    return pl.pallas_call(
        _inproj_kernel,
        grid=(N_TILES,),
        in_specs=[_xp_spec(), _xs_spec(), _mod_spec(0), _mod_spec(1),
                  pl.BlockSpec((1, D_MODEL), lambda i: (0, 0)),
                  pl.BlockSpec((D_MODEL, D_IN_PAD), lambda i: (0, 0)),
                  tab_spec, tab_spec, tab_spec],
        out_specs=[o[0] for o in outs],
        out_shape=[o[1] for o in outs],
        scratch_shapes=[pltpu.VMEM((TM, D_MODEL), F32)],
        compiler_params=pltpu.CompilerParams(vmem_limit_bytes=VMEM_LIMIT),
        name="inproj",
    )(x_prompt, x_sample, modc, modc, g_mix, w_in_r, cos, sa, sb)


def _attn_chunk_kernel(sink_ref, q_ref, prev_ref, cur_ref, o_ref):
    q = q_ref[...] * (A_HEAD_DIM ** -0.5)
    kv = jnp.concatenate([prev_ref[...], cur_ref[...]], axis=0)
    for g in range(A_KV_HEADS):
        kg = kv[:, g * A_HEAD_DIM:(g + 1) * A_HEAD_DIM].astype(BF16)
        vg = kv[:, A_KV_WIDTH + g * A_HEAD_DIM:A_KV_WIDTH + (g + 1) * A_HEAD_DIM].astype(BF16)
        heads = [g * A_GROUP + i for i in range(A_GROUP)]
        qc = jnp.concatenate([q[:, h * A_HEAD_DIM:(h + 1) * A_HEAD_DIM] for h in heads], axis=0).astype(BF16)
        snk = jnp.concatenate([jnp.full((CHUNK, 1), sink_ref[h], F32) for h in heads], axis=0)
        s = lax.dot_general(qc, kg, (((1,), (1,)), ((), ())), preferred_element_type=F32)
        mx = jnp.maximum(jnp.max(s, axis=-1, keepdims=True), snk)
        p = jnp.exp(s - mx)
        den = jnp.sum(p, axis=-1, keepdims=True) + jnp.exp(snk - mx)
        o = jnp.dot(p.astype(BF16), vg, preferred_element_type=F32) / den
        for i, h in enumerate(heads):
            o_ref[:, h * A_HEAD_DIM:(h + 1) * A_HEAD_DIM] = o[i * CHUNK:(i + 1) * CHUNK]


ATT_TQ = 256
ATT_QB = 2 * CHUNK


def _attn_band_kernel(sink_ref, q_ref, prev_ref, cur_ref, o_ref, att_t):
    nk = ATT_QB + WINDOW
    q = (q_ref[...] * (A_HEAD_DIM ** -0.5)).astype(BF16)
    kv = jnp.concatenate([prev_ref[...], cur_ref[...]], axis=0)
    k2 = kv[:, :A_KV_WIDTH]
    k2r = pltpu.roll(k2, A_HEAD_DIM, 1)
    low = lax.broadcasted_iota(jnp.int32, k2.shape, 1) < A_HEAD_DIM
    k_placed = {(0, 0): jnp.where(low, k2, 0.0), (0, 1): jnp.where(low, 0.0, k2r),
                (1, 0): jnp.where(low, k2r, 0.0), (1, 1): jnp.where(low, 0.0, k2)}
    k_placed = {key: val.astype(BF16) for key, val in k_placed.items()}
    v_t = kv[:, A_KV_WIDTH:].T.astype(BF16)
    key_chunk = lax.broadcasted_iota(jnp.int32, (nk, ATT_QB), 0) // CHUNK
    q_chunk = lax.broadcasted_iota(jnp.int32, (nk, ATT_QB), 1) // CHUNK
    band = (key_chunk >= q_chunk) & (key_chunk <= q_chunk + WINDOW // CHUNK)
    has_history = pl.program_id(1) > 0
    for blk in range(ATT_TQ // ATT_QB):
        keys = slice(blk * ATT_QB, blk * ATT_QB + nk)
        qrows = slice(blk * ATT_QB, (blk + 1) * ATT_QB)
        valid = band & ((key_chunk >= WINDOW // CHUNK) | has_history) if blk == 0 else band
        for h in range(A_HEADS):
            g = h // A_GROUP
            s_t = lax.dot_general(k_placed[(g, h % 2)][keys], q[qrows, (h // 2) * LANES:(h // 2 + 1) * LANES],
                                  (((1,), (1,)), ((), ())), preferred_element_type=F32)
            s_t = jnp.where(valid, s_t, -jnp.inf)
            snk = sink_ref[h]
            mx = jnp.maximum(jnp.max(s_t, axis=0, keepdims=True), snk)
            p_t = jnp.exp(s_t - mx)
            den = jnp.sum(p_t, axis=0, keepdims=True) + jnp.exp(snk - mx)
            o_t = jnp.dot(v_t[:, keys], p_t.astype(BF16), preferred_element_type=F32)
            att_t[h * A_HEAD_DIM:(h + 1) * A_HEAD_DIM, qrows] = o_t[g * A_HEAD_DIM:(g + 1) * A_HEAD_DIM] / den
    o_ref[...] = att_t[...].T


def _attention(sinks, q, kv, cache_kv):
    smem = pl.BlockSpec(memory_space=pltpu.SMEM)
    tq = ATT_TQ
    nq = SEQ // tq
    att = pl.pallas_call(
        _attn_band_kernel,
        grid=(BATCH, nq),
        scratch_shapes=[pltpu.VMEM((A_WIDTH, ATT_TQ), F32)],
        in_specs=[smem,
                  pl.BlockSpec((tq, A_WIDTH), lambda b, j: (b * nq + j, 0)),
                  pl.BlockSpec((WINDOW, 2 * A_KV_WIDTH),
                               lambda b, j: (jnp.maximum((b * nq + j) * (tq // WINDOW) - 1, 0), 0)),
                  pl.BlockSpec((tq, 2 * A_KV_WIDTH), lambda b, j: (b * nq + j, 0))],
        out_specs=pl.BlockSpec((tq, A_WIDTH), lambda b, j: (b * nq + j, 0)),
        out_shape=jax.ShapeDtypeStruct((N_P, A_WIDTH), F32),
        name="attn_prompt",
    )(sinks, q, kv, kv)
    off = N_P // DEC_SEQ
    att_s = pl.pallas_call(
        _attn_chunk_kernel,
        grid=(DEC_BATCH,),
        in_specs=[smem,
                  pl.BlockSpec((DEC_SEQ, A_WIDTH), lambda b: (off + b, 0)),
                  pl.BlockSpec((WINDOW, 2 * A_KV_WIDTH), lambda b: (b, 0)),
                  pl.BlockSpec((DEC_SEQ, 2 * A_KV_WIDTH), lambda b: (off + b, 0))],
        out_specs=pl.BlockSpec((DEC_SEQ, A_WIDTH), lambda b: (b, 0)),
        out_shape=jax.ShapeDtypeStruct((N_S, A_WIDTH), F32),
        name="attn_sample",
    )(sinks, q, cache_kv, kv)
    return att, att_s


def _conv4(x, w, b):
    y = b + x * w[CONV_WIDTH - 1:CONV_WIDTH]
    for j in range(1, CONV_WIDTH):
        y = y + pltpu.roll(x, j, 0) * w[CONV_WIDTH - 1 - j:CONV_WIDTH - j]
    return y


def _mlstm_kernel(qkm_ref, vm_ref, om_ref, gt_ref, cw_ref, cb_ref, gb_ref, gmh_ref,
                  carry0_ref, c0_ref, n0_ref, m0_ref,
                  hm_ref, cout_ref, nout_ref, mout_ref, c_s, n_s, m_s, carry_s, *, L):
    c = pl.program_id(1)

    @pl.when(c == 0)
    def _():
        c_s[...] = c0_ref[0]
        n_s[...] = n0_ref[0]
        m_s[...] = m0_ref[0]
        carry_s[...] = carry0_ref[0]

    x = qkm_ref[...]
    w = cw_ref[...]
    b = cb_ref[...]
    y = _conv4(x, w, b)
    y8 = _conv4(jnp.concatenate([carry_s[...], x[:SUBLANES]], axis=0), w, b)
    y = jnp.concatenate([y8[SUBLANES:], y[SUBLANES:]], axis=0)
    carry_s[...] = x[L - SUBLANES:]
    a = y * _sigmoid(y)
    qa = a[:, :M_WIDTH] * (M_HEAD_DIM ** -0.5)
    ka = a[:, M_WIDTH:]
    v = vm_ref[...]
    om = om_ref[...]
    gmh = gmh_ref[...]

    z = gt_ref[...] + gb_ref[...]
    lane = lax.broadcasted_iota(jnp.int32, (L, LANES), 1)
    f_log = jnp.minimum(z, 0.0) - jnp.log1p(jnp.exp(-jnp.abs(z)))
    val = jnp.where(lane < M_HEADS, z, f_log)
    row = lax.broadcasted_iota(jnp.int32, (L, L), 0)
    col = lax.broadcasted_iota(jnp.int32, (L, L), 1)
    causal = row >= col
    cum = jnp.dot(causal.astype(F32), val, preferred_element_type=F32, precision=HIGHEST)
    G = jnp.where(lane < M_HEADS, val, cum)
    sel = (lax.broadcasted_iota(jnp.int32, (SUBLANES, LANES), 0)
           == lax.broadcasted_iota(jnp.int32, (SUBLANES, LANES), 1)).astype(F32)
    GT = lax.dot_general(sel, G, (((1,), (1,)), ((), ())), preferred_element_type=F32, precision=HIGHEST)

    m_row = m_s[...]
    lane1 = lax.broadcasted_iota(jnp.int32, (1, LANES), 1)
    outs = []
    for h in range(M_HEADS):
        a_col = G[:, M_HEADS + h:M_HEADS + h + 1]
        i_col = G[:, h:h + 1]
        i_row = GT[h:h + 1, :]
        b_row = GT[M_HEADS + h:M_HEADS + h + 1, :]
        m_prev = m_row[:, h:h + 1]
        dm = jnp.where(causal, a_col - b_row + i_row, -jnp.inf)
        inter = a_col + m_prev
        m_t = jnp.maximum(inter, jnp.max(dm, axis=-1, keepdims=True))
        w_intra = jnp.exp(dm - m_t)
        w_inter = jnp.exp(inter - m_t)
        sl = slice(h * M_HEAD_DIM, (h + 1) * M_HEAD_DIM)
        qh, kh, vh = qa[:, sl], ka[:, sl], v[:, sl]
        qb = qh.astype(BF16)
        scores = lax.dot_general(qb, kh.astype(BF16), (((1,), (1,)), ((), ())), preferred_element_type=F32) * w_intra
        ch = c_s[h]
        nh = n_s[h:h + 1, :]
        num = (jnp.dot(scores.astype(BF16), vh.astype(BF16), preferred_element_type=F32)
               + w_inter * lax.dot_general(qb, ch.astype(BF16), (((1,), (1,)), ((), ())), preferred_element_type=F32))
        den = jnp.sum(scores, axis=-1, keepdims=True) + w_inter * jnp.sum(qh * nh, axis=-1, keepdims=True)
        hv = num / jnp.maximum(jnp.abs(den), jnp.exp(-m_t))
        hn = hv * lax.rsqrt(jnp.mean(hv * hv, axis=-1, keepdims=True) + EPS)
        outs.append(hn * gmh[:, sl] * _sigmoid(om[:, sl]))
        m_new = m_t[L - 1:L, :]
        a_last = a_col[L - 1:L, :]
        w_end = jnp.exp(a_last - a_col + i_col - m_new)
        decay = jnp.exp(a_last + m_prev - m_new)
        kw = kh * w_end
        c_s[h] = decay * ch + lax.dot_general(vh.astype(BF16), kw.astype(BF16), (((0,), (0,)), ((), ())),
                                              preferred_element_type=F32)
        n_s[h:h + 1, :] = decay * nh + jnp.sum(kw, axis=0, keepdims=True)
        m_row = jnp.where(lane1 == h, m_new, m_row)
    m_s[...] = m_row
    hm_ref[...] = jnp.concatenate(outs, axis=1)

    @pl.when(c == pl.num_programs(1) - 1)
    def _():
        cout_ref[0] = c_s[...]
        nout_ref[0] = n_s[...]
        mout_ref[0] = m_s[...]


def _mlstm(qkm, vm, om, gates, conv_w, conv_b, gbias, g_mh, carry0, c0, n0, m0, *, nb, L, nc, row_off, name):
    off = row_off // L

    def tok(n):
        return pl.BlockSpec((L, n), lambda b, c: (off + b * nc + c, 0))

    def const(shape):
        return pl.BlockSpec(shape, lambda b, c: (0,) * len(shape))

    in_specs = [tok(2 * M_WIDTH), tok(M_WIDTH), tok(M_WIDTH), tok(LANES),
                const((CONV_WIDTH, 2 * M_WIDTH)), const((1, 2 * M_WIDTH)), const((1, LANES)), const((1, M_WIDTH)),
                pl.BlockSpec((1, SUBLANES, 2 * M_WIDTH), lambda b, c: (b, 0, 0)),
                pl.BlockSpec((1, M_HEADS, M_HEAD_DIM, M_HEAD_DIM), lambda b, c: (b, 0, 0, 0)),
                pl.BlockSpec((1, M_HEADS, M_HEAD_DIM), lambda b, c: (b, 0, 0)),
                pl.BlockSpec((1, 1, LANES), lambda b, c: (b, 0, 0))]
    args = [qkm, vm, om, gates, conv_w, conv_b, gbias, g_mh, carry0, c0, n0, m0]
    return pl.pallas_call(
        functools.partial(_mlstm_kernel, L=L),
        grid=(nb, nc),
        in_specs=in_specs,
        out_specs=[pl.BlockSpec((L, M_WIDTH), lambda b, c: (b * nc + c, 0)),
                   pl.BlockSpec((1, M_HEADS, M_HEAD_DIM, M_HEAD_DIM), lambda b, c: (b, 0, 0, 0)),
                   pl.BlockSpec((1, M_HEADS, M_HEAD_DIM), lambda b, c: (b, 0, 0)),
                   pl.BlockSpec((1, 1, LANES), lambda b, c: (b, 0, 0))],
        out_shape=[jax.ShapeDtypeStruct((nb * nc * L, M_WIDTH), F32),
                   jax.ShapeDtypeStruct((nb, M_HEADS, M_HEAD_DIM, M_HEAD_DIM), F32),
                   jax.ShapeDtypeStruct((nb, M_HEADS, M_HEAD_DIM), F32),
                   jax.ShapeDtypeStruct((nb, 1, LANES), F32)],
        scratch_shapes=[pltpu.VMEM((M_HEADS, M_HEAD_DIM, M_HEAD_DIM), F32),
                        pltpu.VMEM((M_HEADS, M_HEAD_DIM), F32),
                        pltpu.VMEM((1, LANES), F32),
                        pltpu.VMEM((SUBLANES, 2 * M_WIDTH), F32)],
        compiler_params=pltpu.CompilerParams(dimension_semantics=("arbitrary", "arbitrary")),
        name=name,
    )(*args)


def _outproj_kernel(xp_ref, xs_ref, attp_ref, atts_ref, hmp_ref, hms_ref, gate_ref, sh_ref, sc_ref, g_ref,
                    w_ref, wr_ref, br_ref, x2_ref, h2_ref, ri_ref, cnt_ref, xbuf, mixbuf):
    x = _load_x(xp_ref, xs_ref, xbuf)
    i = pl.program_id(0)

    @pl.when(i < P_TILES)
    def _():
        mixbuf[:, :A_WIDTH] = attp_ref[...].astype(BF16)
        mixbuf[:, A_WIDTH:] = hmp_ref[...].astype(BF16)

    @pl.when(i >= P_TILES)
    def _():
        mixbuf[:, :A_WIDTH] = atts_ref[...].astype(BF16)
        mixbuf[:, A_WIDTH:] = hms_ref[...].astype(BF16)

    mixed = jnp.dot(mixbuf[...], w_ref[...], preferred_element_type=F32)
    x2 = x + _per_chunk(gate_ref) * mixed
    x2_ref[...] = x2
    h2 = _rmsnorm(x2, g_ref[...]) * (1.0 + _per_chunk(sc_ref)) + _per_chunk(sh_ref)
    for s in range(ROW_SUB):
        h2_ref[pl.ds(s, TM, stride=ROW_SUB), :] = h2[:, s * LANES:(s + 1) * LANES]

    h_hi = h2.astype(BF16)
    h_lo = (h2 - h_hi.astype(F32)).astype(BF16)
    lg = (jnp.dot(h_hi, wr_ref[0], preferred_element_type=F32)
          + jnp.dot(h_lo, wr_ref[0], preferred_element_type=F32)
          + jnp.dot(h_hi, wr_ref[1], preferred_element_type=F32)) + br_ref[...]
    lane = lax.broadcasted_iota(jnp.int32, (TM, LANES), 1)
    lanef = lane.astype(F32)
    ninf = -jnp.inf

    def first_argmax(vals):
        mx = jnp.max(vals, axis=-1, keepdims=True)
        return mx, jnp.min(jnp.where(vals == mx, lanef, float(LANES)), axis=-1, keepdims=True)

    is_grp = lane < N_GROUPS
    gmax, grp = first_argmax(jnp.where(is_grp, lg, ninf))
    p_grp = 1.0 / jnp.sum(jnp.where(is_grp, jnp.exp(lg - gmax), 0.0), axis=-1, keepdims=True)
    base = N_GROUPS + grp * EXPERTS_PER_GROUP
    in_grp = (lanef >= base) & (lanef < base + EXPERTS_PER_GROUP)
    el = jnp.where(in_grp, lg, ninf)
    v1, i1 = first_argmax(el)
    v2, i2 = first_argmax(jnp.where(lanef == i1, ninf, el))
    e = jnp.exp(v2 - v1)
    g1 = p_grp / (1.0 + e)
    g2 = p_grp * e / (1.0 + e)
    e1 = i1 - N_GROUPS
    e2 = i2 - N_GROUPS
    oh1 = lanef == e1
    oh2 = lanef == e2
    oh = jnp.where(oh1 | oh2, 1.0, 0.0)
    earlier = (lax.broadcasted_iota(jnp.int32, (TM, TM), 0) > lax.broadcasted_iota(jnp.int32, (TM, TM), 1))
    before = jnp.dot(earlier.astype(BF16), oh.astype(BF16), preferred_element_type=F32)
    r1 = jnp.sum(jnp.where(oh1, before, 0.0), axis=-1, keepdims=True)
    r2 = jnp.sum(jnp.where(oh2, before, 0.0), axis=-1, keepdims=True)
    cnt_ref[...] = jnp.broadcast_to(jnp.sum(oh, axis=0, keepdims=True), (SUBLANES, LANES))
    cols = (e1, e2, g1, g2, r1, r2)
    ri = jnp.zeros((TM, LANES), F32)
    for k, cval in enumerate(cols):
        ri = jnp.where(lane == k, cval, ri)
    ri_ref[...] = ri


def _outproj(x_prompt, x_sample, att_p, att_s, hm_p, hm_s, modc, g_ffn, w_out_b, w_router, b_router):
    def tok(n):
        return pl.BlockSpec((TM, n), lambda i: (i, 0))

    def tok_p(n):
        return pl.BlockSpec((TM, n), lambda i: (jnp.minimum(i, P_TILES - 1), 0))

    def tok_s(n):
        return pl.BlockSpec((TM, n), lambda i: (jnp.maximum(i - P_TILES, 0), 0))

    def const(shape):
        return pl.BlockSpec(shape, lambda i: (0,) * len(shape))

    return pl.pallas_call(
        _outproj_kernel,
        grid=(N_TILES,),
        in_specs=[_xp_spec(), _xs_spec(), tok_p(A_WIDTH), tok_s(A_WIDTH), tok_p(M_WIDTH), tok_s(M_WIDTH),
                  _mod_spec(2), _mod_spec(3), _mod_spec(4),
                  const((1, D_MODEL)), const((D_MODEL, D_MODEL)), const((2, D_MODEL, LANES)), const((1, LANES))],
        out_specs=[tok(D_MODEL), pl.BlockSpec((TM * ROW_SUB, LANES), lambda i: (i, 0)), tok(LANES),
                   pl.BlockSpec((SUBLANES, LANES), lambda i: (i, 0))],
        out_shape=[jax.ShapeDtypeStruct((N_TOK, D_MODEL), F32),
                   jax.ShapeDtypeStruct((N_TOK * ROW_SUB, LANES), F32),
                   jax.ShapeDtypeStruct((N_TOK, LANES), F32),
                   jax.ShapeDtypeStruct((N_TILES * SUBLANES, LANES), F32)],
        scratch_shapes=[pltpu.VMEM((TM, D_MODEL), F32), pltpu.VMEM((TM, D_MODEL), BF16)],
        compiler_params=pltpu.CompilerParams(vmem_limit_bytes=VMEM_LIMIT),
        name="outproj",
    )(x_prompt, x_sample, att_p, att_s, hm_p, hm_s, modc, modc, modc, g_ffn, w_out_b, w_router, b_router)


def _row(ref, r):
    return ref.at[pl.ds(pl.multiple_of(r * ROW_SUB, ROW_SUB), ROW_SUB), :]


MOE_DUMP = 2 * MOE_R
MOE_SLOTS = N_ASSIGN + MOE_DUMP


def _invert_kernel(dest_ref, pid0_hbm, pid_ref, sem):
    cp = pltpu.make_async_copy(pid0_hbm, pid_ref, sem)
    cp.start()
    cp.wait()

    def body(t, carry):
        for k in range(2):
            pid_ref[dest_ref[2 * t + k]] = k * N_TOK + t
        return carry

    lax.fori_loop(0, N_TOK, body, 0, unroll=8)


def _invert(dest):
    pid0 = N_ASSIGN + jnp.arange(MOE_ROWS, dtype=jnp.int32) % MOE_DUMP
    return pl.pallas_call(
        _invert_kernel,
        in_specs=[pl.BlockSpec(memory_space=pltpu.SMEM), pl.BlockSpec(memory_space=pl.ANY)],
        out_specs=pl.BlockSpec(memory_space=pltpu.SMEM),
        out_shape=jax.ShapeDtypeStruct((MOE_ROWS,), jnp.int32),
        scratch_shapes=[pltpu.SemaphoreType.DMA],
        name="moe_invert",
    )(dest, pid0)


def _moe_kernel(be_ref, bv_ref, src_ref, dst_ref, h_hbm, wg_ref, wu_ref, wd_ref, o_hbm,
                xbuf, obuf, wg_s, wu_s, wd_s, gsem, ssem):
    b = pl.program_id(0)
    s = b % 2
    used = bv_ref[b] > 0
    prev_used = (b > 0) & (bv_ref[jnp.maximum(b - 1, 0)] > 0)
    blk_rows = MOE_R * ROW_SUB

    def tile_at(ref, start):
        return ref.at[pl.ds(pl.multiple_of(start, ROW_SUB), ROW_SUB), :]

    def gather(blk, slot, r):
        return pltpu.make_async_copy(tile_at(h_hbm, src_ref[blk * MOE_R + r]), _row(xbuf, slot * MOE_R + r),
                                     gsem.at[slot])

    def scatter(blk, slot, r):
        return pltpu.make_async_copy(_row(obuf, slot * MOE_R + r),
                                     tile_at(o_hbm, dst_ref[jnp.maximum(blk, 0) * MOE_R + r]), ssem)

    def wait_gathers(slot):
        start = pl.multiple_of(slot * blk_rows, blk_rows)
        pltpu.make_async_copy(h_hbm.at[pl.ds(0, blk_rows), :], xbuf.at[pl.ds(start, blk_rows), :], gsem.at[slot]).wait()

    def wait_scatters():
        pltpu.make_async_copy(obuf.at[pl.ds(0, blk_rows), :], o_hbm.at[pl.ds(0, blk_rows), :], ssem).wait()

    @pl.when(b == 0)
    def _():
        obuf[...] = jnp.zeros_like(obuf)
        spare = pltpu.make_async_copy(obuf, o_hbm.at[pl.ds(N_ASSIGN * ROW_SUB, MOE_DUMP * ROW_SUB), :], ssem)
        spare.start()
        spare.wait()
        for r in range(MOE_R):
            gather(0, 0, r).start(priority=r % 2)

    @pl.when(prev_used)
    def _():
        wait_scatters()

    @pl.when(used)
    def _():
        wait_gathers(s)
        prev = be_ref[jnp.maximum(b - 1, 0)]

        @pl.when((b == 0) | (be_ref[b] != prev))
        def _():
            wg_s[...] = wg_ref[0].astype(BF16)
            wu_s[...] = wu_ref[0].astype(BF16)
            wd_s[...] = wd_ref[0].astype(BF16)

        for r in range(MOE_R):
            gather(b + 1, 1 - s, r).start(priority=0)
        for r in range(MOE_R):
            scatter(b - 1, 1 - s, r).start(priority=1)
        x = jnp.concatenate([xbuf[pl.ds(s * blk_rows + j, MOE_R, stride=ROW_SUB), :] for j in range(ROW_SUB)], axis=1)
        xb = x.astype(BF16)
        g = jnp.dot(xb, wg_s[...], preferred_element_type=F32)
        u = jnp.dot(xb, wu_s[...], preferred_element_type=F32)
        a = (g * _sigmoid(g) * u).astype(BF16)
        o = jnp.dot(a, wd_s[...], preferred_element_type=F32)
        for j in range(ROW_SUB):
            obuf[pl.ds(s * blk_rows + j, MOE_R, stride=ROW_SUB), :] = o[:, j * LANES:(j + 1) * LANES]

    @pl.when(jnp.logical_not(used) & prev_used)
    def _():
        wait_gathers(s)
        for r in range(MOE_R):
            scatter(b - 1, 1 - s, r).start(priority=r % 2)
        wait_scatters()


def _experts(blk_e, blk_valid, pid, h2t, w_g, w_u, w_d):
    tok = jnp.minimum(jnp.where(pid >= N_TOK, pid - N_TOK, pid), N_TOK - 1)
    src = tok * ROW_SUB
    dst = pid * ROW_SUB
    return pl.pallas_call(
        _moe_kernel,
        grid_spec=pltpu.PrefetchScalarGridSpec(
            num_scalar_prefetch=4,
            grid=(MOE_BLOCKS,),
            in_specs=[pl.BlockSpec(memory_space=pl.ANY),
                      pl.BlockSpec((1, D_MODEL, D_EXPERT), lambda b, be, *_: (be[b], 0, 0)),
                      pl.BlockSpec((1, D_MODEL, D_EXPERT), lambda b, be, *_: (be[b], 0, 0)),
                      pl.BlockSpec((1, D_EXPERT, D_MODEL), lambda b, be, *_: (be[b], 0, 0))],
            out_specs=pl.BlockSpec(memory_space=pl.ANY),
            scratch_shapes=[pltpu.VMEM((2 * MOE_R * ROW_SUB, LANES), F32),
                            pltpu.VMEM((2 * MOE_R * ROW_SUB, LANES), F32),
                            pltpu.VMEM((D_MODEL, D_EXPERT), BF16),
                            pltpu.VMEM((D_MODEL, D_EXPERT), BF16),
                            pltpu.VMEM((D_EXPERT, D_MODEL), BF16),
                            pltpu.SemaphoreType.DMA((2,)),
                            pltpu.SemaphoreType.DMA]),
        out_shape=jax.ShapeDtypeStruct((MOE_SLOTS * ROW_SUB, LANES), F32),
        compiler_params=pltpu.CompilerParams(dimension_semantics=("arbitrary",), vmem_limit_bytes=VMEM_LIMIT),
        name="moe_experts",
    )(blk_e, blk_valid, src, dst, h2t, w_g, w_u, w_d)


def _combine_kernel(o1_ref, o2_ref, x2_ref, ri_ref, gate_ref, gf_ref, yp_ref, ys_ref):
    i = pl.program_id(0)
    ri = ri_ref[...]
    g1 = ri[:, 2:3]
    g2 = ri[:, 3:4]
    moe = jnp.concatenate(
        [g1 * o1_ref[pl.ds(s, TM, stride=ROW_SUB), :] + g2 * o2_ref[pl.ds(s, TM, stride=ROW_SUB), :]
         for s in range(ROW_SUB)], axis=1)
    x3 = x2_ref[...] + _per_chunk(gate_ref) * moe
    y = _rmsnorm(x3, gf_ref[...])

    @pl.when(i < P_TILES)
    def _():
        yp_ref[0] = y

    @pl.when(i >= P_TILES)
    def _():
        ys_ref[...] = y.reshape(CH_PER_TILE, DEC_SEQ, D_MODEL)


def _combine(o_rows, x2, rinfo, modc, g_final):
    return pl.pallas_call(
        _combine_kernel,
        grid=(N_TILES,),
        in_specs=[pl.BlockSpec((TM * ROW_SUB, LANES), lambda i: (i, 0)),
                  pl.BlockSpec((TM * ROW_SUB, LANES), lambda i: (N_TILES + i, 0)),
                  pl.BlockSpec((TM, D_MODEL), lambda i: (i, 0)),
                  pl.BlockSpec((TM, LANES), lambda i: (i, 0)),
                  _mod_spec(5),
                  pl.BlockSpec((1, D_MODEL), lambda i: (0, 0))],
        out_specs=[_xp_spec(), _xs_spec()],
        out_shape=[jax.ShapeDtypeStruct((BATCH, SEQ, D_MODEL), F32),
                   jax.ShapeDtypeStruct((DEC_BATCH, DEC_SEQ, D_MODEL), F32)],
        compiler_params=pltpu.CompilerParams(dimension_semantics=("arbitrary",), vmem_limit_bytes=VMEM_LIMIT),
        name="moe_combine",
    )(o_rows, o_rows, x2, rinfo, modc, g_final)


def _moe_plan(rinfo, cnt):
    tile_cnt = cnt.reshape(N_TILES, SUBLANES, LANES)[:, 0, :N_EXPERTS].astype(jnp.int32)
    counts = jnp.sum(tile_cnt, axis=0)
    padded = (counts + MOE_R - 1) // MOE_R * MOE_R
    pad_end = jnp.cumsum(padded)
    base = pad_end - padded
    tile_base = base[None, :] + jnp.cumsum(tile_cnt, axis=0) - tile_cnt
    eid = rinfo[:, 0:2].astype(jnp.int32).reshape(N_TILES, TM, 2, 1)
    rank = rinfo[:, 4:6].astype(jnp.int32).reshape(N_TILES, TM, 2)
    experts = jnp.arange(N_EXPERTS, dtype=jnp.int32)
    dest = jnp.sum(jnp.where(eid == experts, tile_base[:, None, None, :], 0), axis=-1) + rank
    blk_start = jnp.arange(MOE_BLOCKS, dtype=jnp.int32) * MOE_R
    blk_e = jnp.minimum(jnp.sum((pad_end[None, :] <= blk_start[:, None]).astype(jnp.int32), axis=1), N_EXPERTS - 1)
    blk_valid = jnp.clip(counts[blk_e] - (blk_start - base[blk_e]), 0, MOE_R).astype(jnp.int32)
    return dest.reshape(-1), blk_e, blk_valid


def kernel(x_prompt, x_sample, c_prompt, c_sample, cache_win_k, cache_win_v, state_conv, state_C, state_n, state_m, w_ada, b_ada, g_norm_mix, g_norm_ffn, w_in, attn_sinks, conv_w, conv_b, b_igate, b_fgate, g_mhnorm, w_out, w_router_group, b_router_group, w_router_expert, b_router_expert, w_exp_gate, w_exp_up, w_exp_down, g_final):
    l = 0
    mod = _modulation(jnp.concatenate([c_prompt, c_sample], axis=0), w_ada[l], b_ada[l])
    chunk_stream = np.concatenate([np.repeat(np.arange(BATCH), SEQ // CHUNK), BATCH + np.arange(DEC_BATCH)])
    modc = mod[chunk_stream].reshape(N_CHUNKS, 1, 6 * D_MODEL)

    wi = w_in[l]
    s_q, s_k, s_v, s_qkm, s_vm, s_ig, s_fg = 0, 512, 640, 768, 1792, 2304, 2308
    s_om = 2312
    w_in_r = jnp.concatenate(
        [wi[:, s_q:s_qkm], wi[:, s_qkm:s_vm], wi[:, s_vm:s_ig], wi[:, s_om:], wi[:, s_ig:s_om],
         jnp.zeros((D_MODEL, LANES - 2 * M_HEADS), F32)], axis=1).astype(BF16)
    q, kv, qkm, vm, om, gates = _inproj(x_prompt, x_sample, modc, g_norm_mix[l].reshape(1, -1), w_in_r)

    cache_kv = jnp.concatenate([cache_win_k[l].reshape(DEC_BATCH * WINDOW, A_KV_WIDTH),
                                cache_win_v[l].reshape(DEC_BATCH * WINDOW, A_KV_WIDTH)], axis=1)
    att_p, att_s = _attention(attn_sinks[l], q, kv, cache_kv)

    gbias = jnp.concatenate([b_igate[l], b_fgate[l], jnp.zeros((LANES - 2 * M_HEADS,), F32)]).reshape(1, LANES)
    common = (qkm, vm, om, gates, conv_w[l], conv_b[l].reshape(1, -1), gbias, g_mhnorm[l].reshape(1, -1))
    zeros_p = (jnp.zeros((BATCH, SUBLANES, 2 * M_WIDTH), F32),
               jnp.zeros((BATCH, M_HEADS, M_HEAD_DIM, M_HEAD_DIM), F32),
               jnp.zeros((BATCH, M_HEADS, M_HEAD_DIM), F32),
               jnp.zeros((BATCH, 1, LANES), F32))
    LP = 256
    hm_p, C_p, n_p, m_p = _mlstm(*common, *zeros_p, nb=BATCH, L=LP, nc=SEQ // LP, row_off=0, name="mlstm_prompt")
    carry_s = jnp.concatenate([jnp.zeros((DEC_BATCH, SUBLANES - (CONV_WIDTH - 1), 2 * M_WIDTH), F32),
                               state_conv[l]], axis=1)
    m0_s = jnp.pad(state_m[l], ((0, 0), (0, LANES - M_HEADS))).reshape(DEC_BATCH, 1, LANES)
    hm_s, C_s, n_s, m_s = _mlstm(*common, carry_s, state_C[l], state_n[l], m0_s,
                                 nb=DEC_BATCH, L=DEC_SEQ, nc=1, row_off=N_P, name="mlstm_sample")

    w_router = jnp.concatenate([w_router_group[l], w_router_expert[l],
                                jnp.zeros((D_MODEL, LANES - N_GROUPS - N_EXPERTS), F32)], axis=1)
    w_router_hi = w_router.astype(BF16)
    w_router = jnp.stack([w_router_hi, (w_router - w_router_hi.astype(F32)).astype(BF16)])
    b_router =jnp.concatenate([b_router_group[l], b_router_expert[l],
                                jnp.zeros((LANES - N_GROUPS - N_EXPERTS,), F32)]).reshape(1, LANES)
    x2, h2t, rinfo, cnt = _outproj(x_prompt, x_sample, att_p, att_s, hm_p, hm_s, modc, g_norm_ffn[l].reshape(1, -1),
                                   w_out[l].astype(BF16), w_router, b_router)

    dest, blk_e, blk_valid = _moe_plan(rinfo, cnt)
    o_rows = _experts(blk_e, blk_valid, _invert(dest), h2t, w_exp_gate[l], w_exp_up[l], w_exp_down[l])
    y_prompt, y_sample = _combine(o_rows, x2, rinfo, modc, g_final.reshape(1, -1))

    kv_p = kv[:N_P].reshape(BATCH, SEQ, 2 * A_KV_WIDTH)[:, -WINDOW:]
    win_k_p = kv_p[..., :A_KV_WIDTH].reshape(1, BATCH, WINDOW, A_KV_HEADS, A_HEAD_DIM)
    win_v_p = kv_p[..., A_KV_WIDTH:].reshape(1, BATCH, WINDOW, A_KV_HEADS, A_HEAD_DIM)
    kv_s = kv[N_P:].reshape(DEC_BATCH, DEC_SEQ, 2 * A_KV_WIDTH)
    win_k_s = jnp.concatenate([cache_win_k[l][:, DEC_SEQ:],
                               kv_s[..., :A_KV_WIDTH].reshape(DEC_BATCH, DEC_SEQ, A_KV_HEADS, A_HEAD_DIM)], axis=1)[None]
    win_v_s = jnp.concatenate([cache_win_v[l][:, DEC_SEQ:],
                               kv_s[..., A_KV_WIDTH:].reshape(DEC_BATCH, DEC_SEQ, A_KV_HEADS, A_HEAD_DIM)], axis=1)[None]
    conv_p = qkm[:N_P].reshape(BATCH, SEQ, 2 * M_WIDTH)[:, -(CONV_WIDTH - 1):][None]
    conv_s = qkm[N_P:].reshape(DEC_BATCH, DEC_SEQ, 2 * M_WIDTH)[:, -(CONV_WIDTH - 1):][None]
    return (y_prompt, y_sample,
            win_k_p, win_v_p, conv_p, C_p[None], n_p[None], m_p[:, 0, :M_HEADS][None],
            win_k_s, win_v_s, conv_s, C_s[None], n_s[None], m_s[:, 0, :M_HEADS][None])
```

```python
import functools

import numpy as np
import jax
import jax.numpy as jnp
from jax import lax
from jax.experimental import pallas as pl
from jax.experimental.pallas import tpu as pltpu

F32 = jnp.float32
BF16 = jnp.bfloat16
HIGHEST = lax.Precision.HIGHEST

LANES = 128
SUBLANES = 8

D_MODEL = 1024
BATCH = 8
SEQ = 2048
DEC_BATCH = 32
DEC_SEQ = 64
PAST_LEN = 4096
CHUNK = 64
A_WIDTH = 512
A_HEAD_DIM = 64
A_HEADS = 8
A_KV_HEADS = 2
A_GROUP = 4
A_KV_WIDTH = 128
WINDOW = 128
ROT_DIM = 16
ROPE_THETA = 500000.0
M_WIDTH = 512
M_HEADS = 4
M_HEAD_DIM = 128
CONV_WIDTH = 4
N_GROUPS = 4
EXPERTS_PER_GROUP = 8
N_EXPERTS = 32
D_EXPERT = 512
EPS = 1e-6

N_P = BATCH * SEQ
N_S = DEC_BATCH * DEC_SEQ
N_TOK = N_P + N_S
N_CHUNKS = N_TOK // CHUNK
TM = 512
N_TILES = N_TOK // TM
P_TILES = N_P // TM
CH_PER_TILE = TM // CHUNK
ROW_SUB = D_MODEL // LANES
N_ASSIGN = 2 * N_TOK
MOE_R = 256
MOE_BLOCKS = N_ASSIGN // MOE_R + N_EXPERTS
MOE_ROWS = MOE_BLOCKS * MOE_R
C_Q, C_K, C_V, C_QKM, C_VM, C_OM, C_G = 0, 512, 640, 768, 1792, 2304, 2816
D_IN_PAD = 2944
VMEM_LIMIT = 48 * 1024 * 1024


def _sigmoid(x):
    return 1.0 / (1.0 + jnp.exp(-x))


def _mod_kernel(c_ref, w_ref, b_ref, o_ref):
    c = c_ref[...]
    s = c * _sigmoid(c)
    o_ref[...] = jnp.dot(s, w_ref[...], preferred_element_type=F32, precision=HIGHEST) + b_ref[...]


def _modulation(c_all, w_ada, b_ada):
    n = c_all.shape[0]
    bn = 512
    return pl.pallas_call(
        _mod_kernel,
        grid=(6 * D_MODEL // bn,),
        in_specs=[pl.BlockSpec((n, D_MODEL), lambda j: (0, 0)),
                  pl.BlockSpec((D_MODEL, bn), lambda j: (0, j)),
                  pl.BlockSpec((1, bn), lambda j: (0, j))],
        out_specs=pl.BlockSpec((n, bn), lambda j: (0, j)),
        out_shape=jax.ShapeDtypeStruct((n, 6 * D_MODEL), F32),
        name="modulation",
    )(c_all, w_ada, b_ada.reshape(1, -1))


def _xp_spec():
    def idx(i):
        t = jnp.minimum(i, P_TILES - 1)
        return (t // (SEQ // TM), t % (SEQ // TM), 0)
    return pl.BlockSpec((1, TM, D_MODEL), idx)


def _xs_spec():
    return pl.BlockSpec((CH_PER_TILE, DEC_SEQ, D_MODEL), lambda i: (jnp.maximum(i - P_TILES, 0), 0, 0))


def _mod_spec(comp):
    return pl.BlockSpec((CH_PER_TILE, 1, D_MODEL), lambda i: (i, 0, comp))


def _load_x(xp_ref, xs_ref, xbuf):
    i = pl.program_id(0)

    @pl.when(i < P_TILES)
    def _():
        xbuf[...] = xp_ref[0]

    @pl.when(i >= P_TILES)
    def _():
        xbuf[...] = xs_ref[...].reshape(TM, D_MODEL)

    return xbuf[...]


def _per_chunk(m_ref):
    m = m_ref[...]
    return jnp.broadcast_to(m, (CH_PER_TILE, CHUNK, D_MODEL)).reshape(TM, D_MODEL)


def _rmsnorm(x, g):
    return x * lax.rsqrt(jnp.mean(x * x, axis=-1, keepdims=True) + EPS) * g


def _rope(x, cos, sa, sb):
    n = x.shape[1]
    rep = n // LANES
    if rep > 1:
        cos = jnp.concatenate([cos] * rep, axis=1)
        sa = jnp.concatenate([sa] * rep, axis=1)
        sb = jnp.concatenate([sb] * rep, axis=1)
    return x * cos + pltpu.roll(x, n - ROT_DIM // 2, 1) * sa + pltpu.roll(x, ROT_DIM // 2, 1) * sb


def _inproj_kernel(xp_ref, xs_ref, sh_ref, sc_ref, g_ref, w_ref, cos_ref, sa_ref, sb_ref,
                   q_ref, kv_ref, qkm_ref, vm_ref, om_ref, gt_ref, xbuf):
    x = _load_x(xp_ref, xs_ref, xbuf)
    h = _rmsnorm(x, g_ref[...]) * (1.0 + _per_chunk(sc_ref)) + _per_chunk(sh_ref)
    hb = h.astype(BF16)

    def proj(a, b):
        return jnp.dot(hb, w_ref[:, a:b], preferred_element_type=F32)

    cos, sa, sb = cos_ref[...], sa_ref[...], sb_ref[...]
    q_ref[...] = _rope(proj(C_Q, C_K), cos, sa, sb).astype(BF16)
    kv_ref[:, :A_KV_WIDTH] = _rope(proj(C_K, C_V), cos, sa, sb)
    kv_ref[:, A_KV_WIDTH:] = proj(C_V, C_QKM)
    qkm_ref[...] = proj(C_QKM, C_VM)
    vm_ref[...] = proj(C_VM, C_OM).astype(BF16)
    om_ref[...] = proj(C_OM, C_G)
    gt_ref[...] = proj(C_G, D_IN_PAD)


def _rope_tables():
    pos = np.concatenate([np.arange(SEQ), np.tile(PAST_LEN + np.arange(DEC_SEQ), CH_PER_TILE)]).astype(np.float64)
    inv_freq = ROPE_THETA ** (-np.arange(0, ROT_DIM, 2, dtype=np.float64) / ROT_DIM)
    lane = np.arange(LANES)
    hl = lane % A_HEAD_DIM
    ang = pos[:, None] * inv_freq[hl % (ROT_DIM // 2)][None, :]
    rot = (hl < ROT_DIM)[None, :]
    lo = (hl < ROT_DIM // 2)[None, :]
    cos = np.where(rot, np.cos(ang), 1.0)
    sa = np.where(lo, -np.sin(ang), 0.0)
    sb = np.where(rot & ~lo, np.sin(ang), 0.0)
    return [jnp.asarray(t, F32) for t in (cos, sa, sb)]


def _inproj(x_prompt, x_sample, modc, g_mix, w_in_r):
    cos, sa, sb = _rope_tables()
    tab_spec = pl.BlockSpec((TM, LANES), lambda i: (jnp.where(i < P_TILES, i % (SEQ // TM), SEQ // TM), 0))

    def out(n, dtype=F32):
        return pl.BlockSpec((TM, n), lambda i: (i, 0)), jax.ShapeDtypeStruct((N_TOK, n), dtype)

    outs = [out(A_WIDTH, BF16), out(2 * A_KV_WIDTH), out(2 * M_WIDTH), out(M_WIDTH, BF16), out(M_WIDTH), out(LANES)]
    return pl.pallas_call(
        _inproj_kernel,
        grid=(N_TILES,),
        in_specs=[_xp_spec(), _xs_spec(), _mod_spec(0), _mod_spec(1),
                  pl.BlockSpec((1, D_MODEL), lambda i: (0, 0)),
                  pl.BlockSpec((D_MODEL, D_IN_PAD), lambda i: (0, 0)),
                  tab_spec, tab_spec, tab_spec],
        out_specs=[o[0] for o in outs],
        out_shape=[o[1] for o in outs],
        scratch_shapes=[pltpu.VMEM((TM, D_MODEL), F32)],
        compiler_params=pltpu.CompilerParams(vmem_limit_bytes=VMEM_LIMIT),
        name="inproj",
    )(x_prompt, x_sample, modc, modc, g_mix, w_in_r, cos, sa, sb)


def _attn_chunk_kernel(sink_ref, q_ref, prev_ref, cur_ref, o_ref):
    q = q_ref[...] * (A_HEAD_DIM ** -0.5)
    kv = jnp.concatenate([prev_ref[...], cur_ref[...]], axis=0)
    for g in range(A_KV_HEADS):
        kg = kv[:, g * A_HEAD_DIM:(g + 1) * A_HEAD_DIM].astype(BF16)
        vg = kv[:, A_KV_WIDTH + g * A_HEAD_DIM:A_KV_WIDTH + (g + 1) * A_HEAD_DIM].astype(BF16)
        heads = [g * A_GROUP + i for i in range(A_GROUP)]
        qc = jnp.concatenate([q[:, h * A_HEAD_DIM:(h + 1) * A_HEAD_DIM] for h in heads], axis=0).astype(BF16)
        snk = jnp.concatenate([jnp.full((CHUNK, 1), sink_ref[h], F32) for h in heads], axis=0)
        s = lax.dot_general(qc, kg, (((1,), (1,)), ((), ())), preferred_element_type=F32)
        mx = jnp.maximum(jnp.max(s, axis=-1, keepdims=True), snk)
        p = jnp.exp(s - mx)
        den = jnp.sum(p, axis=-1, keepdims=True) + jnp.exp(snk - mx)
        o = jnp.dot(p.astype(BF16), vg, preferred_element_type=F32) / den
        for i, h in enumerate(heads):
            o_ref[:, h * A_HEAD_DIM:(h + 1) * A_HEAD_DIM] = o[i * CHUNK:(i + 1) * CHUNK]


ATT_TQ = 256
ATT_QB = 2 * CHUNK


def _attn_band_kernel(sink_ref, q_ref, prev_ref, cur_ref, o_ref, att_t):
    nk = ATT_QB + WINDOW
    q = (q_ref[...] * (A_HEAD_DIM ** -0.5)).astype(BF16)
    kv = jnp.concatenate([prev_ref[...], cur_ref[...]], axis=0)
    k2 = kv[:, :A_KV_WIDTH]
    k2r = pltpu.roll(k2, A_HEAD_DIM, 1)
    low = lax.broadcasted_iota(jnp.int32, k2.shape, 1) < A_HEAD_DIM
    k_placed = {(0, 0): jnp.where(low, k2, 0.0), (0, 1): jnp.where(low, 0.0, k2r),
                (1, 0): jnp.where(low, k2r, 0.0), (1, 1): jnp.where(low, 0.0, k2)}
    k_placed = {key: val.astype(BF16) for key, val in k_placed.items()}
    v_t = kv[:, A_KV_WIDTH:].T.astype(BF16)
    key_chunk = lax.broadcasted_iota(jnp.int32, (nk, ATT_QB), 0) // CHUNK
    q_chunk = lax.broadcasted_iota(jnp.int32, (nk, ATT_QB), 1) // CHUNK
    band = (key_chunk >= q_chunk) & (key_chunk <= q_chunk + WINDOW // CHUNK)
    has_history = pl.program_id(1) > 0
    for blk in range(ATT_TQ // ATT_QB):
        keys = slice(blk * ATT_QB, blk * ATT_QB + nk)
        qrows = slice(blk * ATT_QB, (blk + 1) * ATT_QB)
        valid = band & ((key_chunk >= WINDOW // CHUNK) | has_history) if blk == 0 else band
        for h in range(A_HEADS):
            g = h // A_GROUP
            s_t = lax.dot_general(k_placed[(g, h % 2)][keys], q[qrows, (h // 2) * LANES:(h // 2 + 1) * LANES],
                                  (((1,), (1,)), ((), ())), preferred_element_type=F32)
            s_t = jnp.where(valid, s_t, -jnp.inf)
            snk = sink_ref[h]
            mx = jnp.maximum(jnp.max(s_t, axis=0, keepdims=True), snk)
            p_t = jnp.exp(s_t - mx)
            den = jnp.sum(p_t, axis=0, keepdims=True) + jnp.exp(snk - mx)
            o_t = jnp.dot(v_t[:, keys], p_t.astype(BF16), preferred_element_type=F32)
            att_t[h * A_HEAD_DIM:(h + 1) * A_HEAD_DIM, qrows] = o_t[g * A_HEAD_DIM:(g + 1) * A_HEAD_DIM] / den
    o_ref[...] = att_t[...].T.astype(BF16)


def _attention(sinks, q, kv, cache_kv):
    smem = pl.BlockSpec(memory_space=pltpu.SMEM)
    tq = ATT_TQ
    nq = SEQ // tq
    att = pl.pallas_call(
        _attn_band_kernel,
        grid=(BATCH, nq),
        scratch_shapes=[pltpu.VMEM((A_WIDTH, ATT_TQ), F32)],
        in_specs=[smem,
                  pl.BlockSpec((tq, A_WIDTH), lambda b, j: (b * nq + j, 0)),
                  pl.BlockSpec((WINDOW, 2 * A_KV_WIDTH),
                               lambda b, j: (jnp.maximum((b * nq + j) * (tq // WINDOW) - 1, 0), 0)),
                  pl.BlockSpec((tq, 2 * A_KV_WIDTH), lambda b, j: (b * nq + j, 0))],
        out_specs=pl.BlockSpec((tq, A_WIDTH), lambda b, j: (b * nq + j, 0)),
        out_shape=jax.ShapeDtypeStruct((N_P, A_WIDTH), BF16),
        name="attn_prompt",
    )(sinks, q, kv, kv)
    off = N_P // DEC_SEQ
    att_s = pl.pallas_call(
        _attn_chunk_kernel,
        grid=(DEC_BATCH,),
        in_specs=[smem,
                  pl.BlockSpec((DEC_SEQ, A_WIDTH), lambda b: (off + b, 0)),
                  pl.BlockSpec((WINDOW, 2 * A_KV_WIDTH), lambda b: (b, 0)),
                  pl.BlockSpec((DEC_SEQ, 2 * A_KV_WIDTH), lambda b: (off + b, 0))],
        out_specs=pl.BlockSpec((DEC_SEQ, A_WIDTH), lambda b: (b, 0)),
        out_shape=jax.ShapeDtypeStruct((N_S, A_WIDTH), F32),
        name="attn_sample",
    )(sinks, q, cache_kv, kv)
    return att, att_s


def _conv4(x, w, b):
    y = b + x * w[CONV_WIDTH - 1:CONV_WIDTH]
    for j in range(1, CONV_WIDTH):
        y = y + pltpu.roll(x, j, 0) * w[CONV_WIDTH - 1 - j:CONV_WIDTH - j]
    return y


def _mlstm_kernel(qkm_ref, vm_ref, om_ref, gt_ref, cw_ref, cb_ref, gb_ref, gmh_ref,
                  carry0_ref, c0_ref, n0_ref, m0_ref,
                  hm_ref, cout_ref, nout_ref, mout_ref, c_s, n_s, m_s, carry_s, *, L):
    c = pl.program_id(1)

    @pl.when(c == 0)
    def _():
        c_s[...] = c0_ref[0]
        n_s[...] = n0_ref[0]
        m_s[...] = m0_ref[0]
        carry_s[...] = carry0_ref[0]

    x = qkm_ref[...]
    w = cw_ref[...]
    b = cb_ref[...]
    y = _conv4(x, w, b)
    y8 = _conv4(jnp.concatenate([carry_s[...], x[:SUBLANES]], axis=0), w, b)
    y = jnp.concatenate([y8[SUBLANES:], y[SUBLANES:]], axis=0)
    carry_s[...] = x[L - SUBLANES:]
    a = y * _sigmoid(y)
    qa = a[:, :M_WIDTH] * (M_HEAD_DIM ** -0.5)
    ka = a[:, M_WIDTH:]
    v = vm_ref[...]
    om = om_ref[...]
    gmh = gmh_ref[...]

    z = gt_ref[...] + gb_ref[...]
    lane = lax.broadcasted_iota(jnp.int32, (L, LANES), 1)
    f_log = jnp.minimum(z, 0.0) - jnp.log1p(jnp.exp(-jnp.abs(z)))
    val = jnp.where(lane < M_HEADS, z, f_log)
    row = lax.broadcasted_iota(jnp.int32, (L, L), 0)
    col = lax.broadcasted_iota(jnp.int32, (L, L), 1)
    causal = row >= col
    cum = jnp.dot(causal.astype(F32), val, preferred_element_type=F32, precision=HIGHEST)
    G = jnp.where(lane < M_HEADS, val, cum)
    sel = (lax.broadcasted_iota(jnp.int32, (SUBLANES, LANES), 0)
           == lax.broadcasted_iota(jnp.int32, (SUBLANES, LANES), 1)).astype(F32)
    GT = lax.dot_general(sel, G, (((1,), (1,)), ((), ())), preferred_element_type=F32, precision=HIGHEST)

    m_row = m_s[...]
    lane1 = lax.broadcasted_iota(jnp.int32, (1, LANES), 1)
    outs = []
    for h in range(M_HEADS):
        a_col = G[:, M_HEADS + h:M_HEADS + h + 1]
        i_col = G[:, h:h + 1]
        i_row = GT[h:h + 1, :]
        b_row = GT[M_HEADS + h:M_HEADS + h + 1, :]
        m_prev = m_row[:, h:h + 1]
        dm = jnp.where(causal, a_col - b_row + i_row, -jnp.inf)
        inter = a_col + m_prev
        m_t = jnp.maximum(inter, jnp.max(dm, axis=-1, keepdims=True))
        w_intra = jnp.exp(dm - m_t)
        w_inter = jnp.exp(inter - m_t)
        sl = slice(h * M_HEAD_DIM, (h + 1) * M_HEAD_DIM)
        qh, kh, vh = qa[:, sl], ka[:, sl], v[:, sl]
        qb = qh.astype(BF16)
        scores = lax.dot_general(qb, kh.astype(BF16), (((1,), (1,)), ((), ())), preferred_element_type=F32) * w_intra
        ch = c_s[h]
        nh = n_s[h:h + 1, :]
        num = (jnp.dot(scores.astype(BF16), vh.astype(BF16), preferred_element_type=F32)
               + w_inter * lax.dot_general(qb, ch.astype(BF16), (((1,), (1,)), ((), ())), preferred_element_type=F32))
        den = jnp.sum(scores, axis=-1, keepdims=True) + w_inter * jnp.sum(qh * nh, axis=-1, keepdims=True)
        hv = num / jnp.maximum(jnp.abs(den), jnp.exp(-m_t))
        hn = hv * lax.rsqrt(jnp.mean(hv * hv, axis=-1, keepdims=True) + EPS)
        outs.append(hn * gmh[:, sl] * _sigmoid(om[:, sl]))
        m_new = m_t[L - 1:L, :]
        a_last = a_col[L - 1:L, :]
        w_end = jnp.exp(a_last - a_col + i_col - m_new)
        decay = jnp.exp(a_last + m_prev - m_new)
        kw = kh * w_end
        c_s[h] = decay * ch + lax.dot_general(vh.astype(BF16), kw.astype(BF16), (((0,), (0,)), ((), ())),
                                              preferred_element_type=F32)
        n_s[h:h + 1, :] = decay * nh + jnp.sum(kw, axis=0, keepdims=True)
        m_row = jnp.where(lane1 == h, m_new, m_row)
    m_s[...] = m_row
    hm_ref[...] = jnp.concatenate(outs, axis=1).astype(BF16)

    @pl.when(c == pl.num_programs(1) - 1)
    def _():
        cout_ref[0] = c_s[...]
        nout_ref[0] = n_s[...]
        mout_ref[0] = m_s[...]


def _mlstm(qkm, vm, om, gates, conv_w, conv_b, gbias, g_mh, carry0, c0, n0, m0, *, nb, L, nc, row_off, name):
    off = row_off // L

    def tok(n):
        return pl.BlockSpec((L, n), lambda b, c: (off + b * nc + c, 0))

    def const(shape):
        return pl.BlockSpec(shape, lambda b, c: (0,) * len(shape))

    in_specs = [tok(2 * M_WIDTH), tok(M_WIDTH), tok(M_WIDTH), tok(LANES),
                const((CONV_WIDTH, 2 * M_WIDTH)), const((1, 2 * M_WIDTH)), const((1, LANES)), const((1, M_WIDTH)),
                pl.BlockSpec((1, SUBLANES, 2 * M_WIDTH), lambda b, c: (b, 0, 0)),
                pl.BlockSpec((1, M_HEADS, M_HEAD_DIM, M_HEAD_DIM), lambda b, c: (b, 0, 0, 0)),
                pl.BlockSpec((1, M_HEADS, M_HEAD_DIM), lambda b, c: (b, 0, 0)),
                pl.BlockSpec((1, 1, LANES), lambda b, c: (b, 0, 0))]
    args = [qkm, vm, om, gates, conv_w, conv_b, gbias, g_mh, carry0, c0, n0, m0]
    return pl.pallas_call(
        functools.partial(_mlstm_kernel, L=L),
        grid=(nb, nc),
        in_specs=in_specs,
        out_specs=[pl.BlockSpec((L, M_WIDTH), lambda b, c: (b * nc + c, 0)),
                   pl.BlockSpec((1, M_HEADS, M_HEAD_DIM, M_HEAD_DIM), lambda b, c: (b, 0, 0, 0)),
                   pl.BlockSpec((1, M_HEADS, M_HEAD_DIM), lambda b, c: (b, 0, 0)),
                   pl.BlockSpec((1, 1, LANES), lambda b, c: (b, 0, 0))],
        out_shape=[jax.ShapeDtypeStruct((nb * nc * L, M_WIDTH), BF16),
                   jax.ShapeDtypeStruct((nb, M_HEADS, M_HEAD_DIM, M_HEAD_DIM), F32),
                   jax.ShapeDtypeStruct((nb, M_HEADS, M_HEAD_DIM), F32),
                   jax.ShapeDtypeStruct((nb, 1, LANES), F32)],
        scratch_shapes=[pltpu.VMEM((M_HEADS, M_HEAD_DIM, M_HEAD_DIM), F32),
                        pltpu.VMEM((M_HEADS, M_HEAD_DIM), F32),
                        pltpu.VMEM((1, LANES), F32),
                        pltpu.VMEM((SUBLANES, 2 * M_WIDTH), F32)],
        compiler_params=pltpu.CompilerParams(dimension_semantics=("arbitrary", "arbitrary")),
        name=name,
    )(*args)


def _outproj_kernel(xp_ref, xs_ref, attp_ref, atts_ref, hmp_ref, hms_ref, gate_ref, sh_ref, sc_ref, g_ref,
                    w_ref, wr_ref, br_ref, x2_ref, h2_ref, ri_ref, rt_ref, cnt_ref, xbuf, mixbuf):
    x = _load_x(xp_ref, xs_ref, xbuf)
    i = pl.program_id(0)

    @pl.when(i < P_TILES)
    def _():
        mixbuf[:, :A_WIDTH] = attp_ref[...].astype(BF16)
        mixbuf[:, A_WIDTH:] = hmp_ref[...].astype(BF16)

    @pl.when(i >= P_TILES)
    def _():
        mixbuf[:, :A_WIDTH] = atts_ref[...].astype(BF16)
        mixbuf[:, A_WIDTH:] = hms_ref[...].astype(BF16)

    mixed = jnp.dot(mixbuf[...], w_ref[...], preferred_element_type=F32)
    x2 = x + _per_chunk(gate_ref) * mixed
    x2_ref[...] = x2
    h2 = _rmsnorm(x2, g_ref[...]) * (1.0 + _per_chunk(sc_ref)) + _per_chunk(sh_ref)
    for s in range(ROW_SUB):
        h2_ref[pl.ds(s, TM, stride=ROW_SUB), :] = h2[:, s * LANES:(s + 1) * LANES]

    h_hi = h2.astype(BF16)
    h_lo = (h2 - h_hi.astype(F32)).astype(BF16)
    lg = (jnp.dot(h_hi, wr_ref[0], preferred_element_type=F32)
          + jnp.dot(h_lo, wr_ref[0], preferred_element_type=F32)
          + jnp.dot(h_hi, wr_ref[1], preferred_element_type=F32)) + br_ref[...]
    lane = lax.broadcasted_iota(jnp.int32, (TM, LANES), 1)
    lanef = lane.astype(F32)
    ninf = -jnp.inf

    def first_argmax(vals):
        mx = jnp.max(vals, axis=-1, keepdims=True)
        return mx, jnp.min(jnp.where(vals == mx, lanef, float(LANES)), axis=-1, keepdims=True)

    is_grp = lane < N_GROUPS
    gmax, grp = first_argmax(jnp.where(is_grp, lg, ninf))
    p_grp = 1.0 / jnp.sum(jnp.where(is_grp, jnp.exp(lg - gmax), 0.0), axis=-1, keepdims=True)
    base = N_GROUPS + grp * EXPERTS_PER_GROUP
    in_grp = (lanef >= base) & (lanef < base + EXPERTS_PER_GROUP)
    el = jnp.where(in_grp, lg, ninf)
    v1, i1 = first_argmax(el)
    v2, i2 = first_argmax(jnp.where(lanef == i1, ninf, el))
    e = jnp.exp(v2 - v1)
    g1 = p_grp / (1.0 + e)
    g2 = p_grp * e / (1.0 + e)
    e1 = i1 - N_GROUPS
    e2 = i2 - N_GROUPS
    oh1 = lanef == e1
    oh2 = lanef == e2
    oh = jnp.where(oh1 | oh2, 1.0, 0.0)
    earlier = (lax.broadcasted_iota(jnp.int32, (TM, TM), 0) > lax.broadcasted_iota(jnp.int32, (TM, TM), 1))
    before = jnp.dot(earlier.astype(BF16), oh.astype(BF16), preferred_element_type=F32)
    r1 = jnp.sum(jnp.where(oh1, before, 0.0), axis=-1, keepdims=True)
    r2 = jnp.sum(jnp.where(oh2, before, 0.0), axis=-1, keepdims=True)
    cnt_ref[...] = jnp.broadcast_to(jnp.sum(oh, axis=0, keepdims=True), (SUBLANES, LANES))
    cols = (e1, e2, g1, g2, r1, r2)
    ri = jnp.zeros((TM, LANES), F32)
    for k, cval in enumerate(cols):
        ri = jnp.where(lane == k, cval, ri)
    ri_ref[...] = ri
    rt_ref[...] = ri.T[:SUBLANES]


def _outproj(x_prompt, x_sample, att_p, att_s, hm_p, hm_s, modc, g_ffn, w_out_b, w_router, b_router):
    def tok(n):
        return pl.BlockSpec((TM, n), lambda i: (i, 0))

    def tok_p(n):
        return pl.BlockSpec((TM, n), lambda i: (jnp.minimum(i, P_TILES - 1), 0))

    def tok_s(n):
        return pl.BlockSpec((TM, n), lambda i: (jnp.maximum(i - P_TILES, 0), 0))

    def const(shape):
        return pl.BlockSpec(shape, lambda i: (0,) * len(shape))

    return pl.pallas_call(
        _outproj_kernel,
        grid=(N_TILES,),
        in_specs=[_xp_spec(), _xs_spec(), tok_p(A_WIDTH), tok_s(A_WIDTH), tok_p(M_WIDTH), tok_s(M_WIDTH),
                  _mod_spec(2), _mod_spec(3), _mod_spec(4),
                  const((1, D_MODEL)), const((D_MODEL, D_MODEL)), const((2, D_MODEL, LANES)), const((1, LANES))],
        out_specs=[tok(D_MODEL), pl.BlockSpec((TM * ROW_SUB, LANES), lambda i: (i, 0)), tok(LANES),
                   pl.BlockSpec((SUBLANES, TM), lambda i: (0, i)),
                   pl.BlockSpec((SUBLANES, LANES), lambda i: (i, 0))],
        out_shape=[jax.ShapeDtypeStruct((N_TOK, D_MODEL), F32),
                   jax.ShapeDtypeStruct((N_TOK * ROW_SUB, LANES), F32),
                   jax.ShapeDtypeStruct((N_TOK, LANES), F32),
                   jax.ShapeDtypeStruct((SUBLANES, N_TOK), F32),
                   jax.ShapeDtypeStruct((N_TILES * SUBLANES, LANES), F32)],
        scratch_shapes=[pltpu.VMEM((TM, D_MODEL), F32), pltpu.VMEM((TM, D_MODEL), BF16)],
        compiler_params=pltpu.CompilerParams(vmem_limit_bytes=VMEM_LIMIT),
        name="outproj",
    )(x_prompt, x_sample, att_p, att_s, hm_p, hm_s, modc, modc, modc, g_ffn, w_out_b, w_router, b_router)


def _row(ref, r):
    return ref.at[pl.ds(pl.multiple_of(r * ROW_SUB, ROW_SUB), ROW_SUB), :]


MOE_DUMP = 2 * MOE_R
MOE_SLOTS = N_ASSIGN + MOE_DUMP


def _invert_kernel(dest_ref, pid0_hbm, pid_ref, sem):
    cp = pltpu.make_async_copy(pid0_hbm, pid_ref, sem)
    cp.start()
    cp.wait()

    def body(i, carry):
        pid_ref[dest_ref[i]] = i
        return carry

    lax.fori_loop(0, N_ASSIGN, body, 0, unroll=16)


def _invert(dest):
    pid0 = N_ASSIGN + jnp.arange(MOE_ROWS, dtype=jnp.int32) % MOE_DUMP
    return pl.pallas_call(
        _invert_kernel,
        in_specs=[pl.BlockSpec(memory_space=pltpu.SMEM), pl.BlockSpec(memory_space=pl.ANY)],
        out_specs=pl.BlockSpec(memory_space=pltpu.SMEM),
        out_shape=jax.ShapeDtypeStruct((MOE_ROWS,), jnp.int32),
        scratch_shapes=[pltpu.SemaphoreType.DMA],
        name="moe_invert",
    )(dest, pid0)


def _moe_kernel(be_ref, bv_ref, src_ref, dst_ref, h_hbm, wg_ref, wu_ref, wd_ref, o_hbm,
                xbuf, obuf, wg_s, wu_s, wd_s, gsem, ssem):
    b = pl.program_id(0)
    s = b % 2
    used = bv_ref[b] > 0
    prev_used = (b > 0) & (bv_ref[jnp.maximum(b - 1, 0)] > 0)
    blk_rows = MOE_R * ROW_SUB

    def tile_at(ref, start):
        return ref.at[pl.ds(pl.multiple_of(start, ROW_SUB), ROW_SUB), :]

    def gather(blk, slot, r):
        return pltpu.make_async_copy(tile_at(h_hbm, src_ref[blk * MOE_R + r]), _row(xbuf, slot * MOE_R + r),
                                     gsem.at[slot])

    def scatter(blk, slot, r):
        return pltpu.make_async_copy(_row(obuf, slot * MOE_R + r),
                                     tile_at(o_hbm, dst_ref[jnp.maximum(blk, 0) * MOE_R + r]), ssem)

    def wait_gathers(slot):
        start = pl.multiple_of(slot * blk_rows, blk_rows)
        pltpu.make_async_copy(h_hbm.at[pl.ds(0, blk_rows), :], xbuf.at[pl.ds(start, blk_rows), :], gsem.at[slot]).wait()

    def wait_scatters():
        pltpu.make_async_copy(obuf.at[pl.ds(0, blk_rows), :], o_hbm.at[pl.ds(0, blk_rows), :], ssem).wait()

    @pl.when(b == 0)
    def _():
        obuf[...] = jnp.zeros_like(obuf)
        spare = pltpu.make_async_copy(obuf, o_hbm.at[pl.ds(N_ASSIGN * ROW_SUB, MOE_DUMP * ROW_SUB), :], ssem)
        spare.start()
        spare.wait()
        for r in range(MOE_R):
            gather(0, 0, r).start(priority=r % 2)

    @pl.when(prev_used)
    def _():
        wait_scatters()

    @pl.when(used)
    def _():
        wait_gathers(s)
        prev = be_ref[jnp.maximum(b - 1, 0)]

        @pl.when((b == 0) | (be_ref[b] != prev))
        def _():
            wg_s[...] = wg_ref[0].astype(BF16)
            wu_s[...] = wu_ref[0].astype(BF16)
            wd_s[...] = wd_ref[0].astype(BF16)

        for r in range(MOE_R):
            gather(b + 1, 1 - s, r).start(priority=r % 2)
            scatter(b - 1, 1 - s, r).start(priority=1 - r % 2)
        x = jnp.concatenate([xbuf[pl.ds(s * blk_rows + j, MOE_R, stride=ROW_SUB), :] for j in range(ROW_SUB)], axis=1)
        xb = x.astype(BF16)
        g = jnp.dot(xb, wg_s[...], preferred_element_type=F32)
        u = jnp.dot(xb, wu_s[...], preferred_element_type=F32)
        a = (g * _sigmoid(g) * u).astype(BF16)
        o = jnp.dot(a, wd_s[...], preferred_element_type=F32)
        for j in range(ROW_SUB):
            obuf[pl.ds(s * blk_rows + j, MOE_R, stride=ROW_SUB), :] = o[:, j * LANES:(j + 1) * LANES]

    @pl.when(jnp.logical_not(used) & prev_used)
    def _():
        wait_gathers(s)
        for r in range(MOE_R):
            scatter(b - 1, 1 - s, r).start(priority=r % 2)
        wait_scatters()


def _experts(blk_e, blk_valid, pid, h2t, w_g, w_u, w_d):
    tok = jnp.minimum(jnp.where(pid >= N_TOK, pid - N_TOK, pid), N_TOK - 1)
    src = tok * ROW_SUB
    dst = pid * ROW_SUB
    return pl.pallas_call(
        _moe_kernel,
        grid_spec=pltpu.PrefetchScalarGridSpec(
            num_scalar_prefetch=4,
            grid=(MOE_BLOCKS,),
            in_specs=[pl.BlockSpec(memory_space=pl.ANY),
                      pl.BlockSpec((1, D_MODEL, D_EXPERT), lambda b, be, *_: (be[b], 0, 0)),
                      pl.BlockSpec((1, D_MODEL, D_EXPERT), lambda b, be, *_: (be[b], 0, 0)),
                      pl.BlockSpec((1, D_EXPERT, D_MODEL), lambda b, be, *_: (be[b], 0, 0))],
            out_specs=pl.BlockSpec(memory_space=pl.ANY),
            scratch_shapes=[pltpu.VMEM((2 * MOE_R * ROW_SUB, LANES), F32),
                            pltpu.VMEM((2 * MOE_R * ROW_SUB, LANES), F32),
                            pltpu.VMEM((D_MODEL, D_EXPERT), BF16),
                            pltpu.VMEM((D_MODEL, D_EXPERT), BF16),
                            pltpu.VMEM((D_EXPERT, D_MODEL), BF16),
                            pltpu.SemaphoreType.DMA((2,)),
                            pltpu.SemaphoreType.DMA]),
        out_shape=jax.ShapeDtypeStruct((MOE_SLOTS * ROW_SUB, LANES), F32),
        compiler_params=pltpu.CompilerParams(dimension_semantics=("arbitrary",), vmem_limit_bytes=VMEM_LIMIT),
        name="moe_experts",
    )(blk_e, blk_valid, src, dst, h2t, w_g, w_u, w_d)


def _combine_kernel(o1_ref, o2_ref, x2_ref, ri_ref, gate_ref, gf_ref, yp_ref, ys_ref):
    i = pl.program_id(0)
    ri = ri_ref[...]
    g1 = ri[:, 2:3]
    g2 = ri[:, 3:4]
    moe = jnp.concatenate(
        [g1 * o1_ref[pl.ds(s, TM, stride=ROW_SUB), :] + g2 * o2_ref[pl.ds(s, TM, stride=ROW_SUB), :]
         for s in range(ROW_SUB)], axis=1)
    x3 = x2_ref[...] + _per_chunk(gate_ref) * moe
    y = _rmsnorm(x3, gf_ref[...])

    @pl.when(i < P_TILES)
    def _():
        yp_ref[0] = y

    @pl.when(i >= P_TILES)
    def _():
        ys_ref[...] = y.reshape(CH_PER_TILE, DEC_SEQ, D_MODEL)


def _combine(o_rows, x2, rinfo, modc, g_final):
    return pl.pallas_call(
        _combine_kernel,
        grid=(N_TILES,),
        in_specs=[pl.BlockSpec((TM * ROW_SUB, LANES), lambda i: (i, 0)),
                  pl.BlockSpec((TM * ROW_SUB, LANES), lambda i: (N_TILES + i, 0)),
                  pl.BlockSpec((TM, D_MODEL), lambda i: (i, 0)),
                  pl.BlockSpec((TM, LANES), lambda i: (i, 0)),
                  _mod_spec(5),
                  pl.BlockSpec((1, D_MODEL), lambda i: (0, 0))],
        out_specs=[_xp_spec(), _xs_spec()],
        out_shape=[jax.ShapeDtypeStruct((BATCH, SEQ, D_MODEL), F32),
                   jax.ShapeDtypeStruct((DEC_BATCH, DEC_SEQ, D_MODEL), F32)],
        compiler_params=pltpu.CompilerParams(dimension_semantics=("arbitrary",), vmem_limit_bytes=VMEM_LIMIT),
        name="moe_combine",
    )(o_rows, o_rows, x2, rinfo, modc, g_final)


def _moe_plan(rt, cnt):
    experts = np.arange(N_EXPERTS)
    tile_cnt = cnt.reshape(N_TILES, SUBLANES, LANES)[:, 0, :N_EXPERTS]
    counts = jnp.sum(tile_cnt, axis=0)
    padded = jnp.ceil(counts / MOE_R) * MOE_R
    pad_end = jnp.sum(padded[:, None] * (experts[:, None] <= experts[None, :]), axis=0)
    base = pad_end - padded
    tiles = np.arange(N_TILES)
    earlier_tiles = (tiles[None, :, None] < tiles[:, None, None])
    tile_base = base[None, :] + jnp.sum(tile_cnt[None, :, :] * earlier_tiles, axis=1)
    eid = rt[0:2].reshape(1, 2, N_TILES, TM)
    rank = rt[4:6].reshape(2, N_TILES, TM)
    pick = eid == experts.astype(np.float32).reshape(N_EXPERTS, 1, 1, 1)
    dest = jnp.sum(jnp.where(pick, tile_base.T[:, None, :, None], 0.0), axis=0) + rank
    blk_start = (np.arange(MOE_BLOCKS) * MOE_R).astype(np.float32)
    blk_e = jnp.minimum(jnp.sum((blk_start[:, None] >= pad_end[None, :]).astype(F32), axis=1), N_EXPERTS - 1)
    mine = blk_e[:, None] == experts.astype(np.float32)[None, :]
    blk_fill = jnp.sum(jnp.where(mine, (counts + base)[None, :], 0.0), axis=1) - blk_start
    blk_valid = jnp.clip(blk_fill, 0, MOE_R)
    return dest.reshape(-1).astype(jnp.int32), blk_e.astype(jnp.int32), blk_valid.astype(jnp.int32)


def kernel(x_prompt, x_sample, c_prompt, c_sample, cache_win_k, cache_win_v, state_conv, state_C, state_n, state_m, w_ada, b_ada, g_norm_mix, g_norm_ffn, w_in, attn_sinks, conv_w, conv_b, b_igate, b_fgate, g_mhnorm, w_out, w_router_group, b_router_group, w_router_expert, b_router_expert, w_exp_gate, w_exp_up, w_exp_down, g_final):
    l = 0
    mod = _modulation(jnp.concatenate([c_prompt, c_sample], axis=0), w_ada[l], b_ada[l])
    chunk_stream = np.concatenate([np.repeat(np.arange(BATCH), SEQ // CHUNK), BATCH + np.arange(DEC_BATCH)])
    modc = mod[chunk_stream].reshape(N_CHUNKS, 1, 6 * D_MODEL)

    wi = w_in[l]
    s_q, s_k, s_v, s_qkm, s_vm, s_ig, s_fg = 0, 512, 640, 768, 1792, 2304, 2308
    s_om = 2312
    w_in_r = jnp.concatenate(
        [wi[:, s_q:s_qkm], wi[:, s_qkm:s_vm], wi[:, s_vm:s_ig], wi[:, s_om:], wi[:, s_ig:s_om],
         jnp.zeros((D_MODEL, LANES - 2 * M_HEADS), F32)], axis=1).astype(BF16)
    q, kv, qkm, vm, om, gates = _inproj(x_prompt, x_sample, modc, g_norm_mix[l].reshape(1, -1), w_in_r)

    cache_kv = jnp.concatenate([cache_win_k[l].reshape(DEC_BATCH * WINDOW, A_KV_WIDTH),
                                cache_win_v[l].reshape(DEC_BATCH * WINDOW, A_KV_WIDTH)], axis=1)
    att_p, att_s = _attention(attn_sinks[l], q, kv, cache_kv)

    gbias = jnp.concatenate([b_igate[l], b_fgate[l], jnp.zeros((LANES - 2 * M_HEADS,), F32)]).reshape(1, LANES)
    common = (qkm, vm, om, gates, conv_w[l], conv_b[l].reshape(1, -1), gbias, g_mhnorm[l].reshape(1, -1))
    zeros_p = (jnp.zeros((BATCH, SUBLANES, 2 * M_WIDTH), F32),
               jnp.zeros((BATCH, M_HEADS, M_HEAD_DIM, M_HEAD_DIM), F32),
               jnp.zeros((BATCH, M_HEADS, M_HEAD_DIM), F32),
               jnp.zeros((BATCH, 1, LANES), F32))
    LP = 256
    hm_p, C_p, n_p, m_p = _mlstm(*common, *zeros_p, nb=BATCH, L=LP, nc=SEQ // LP, row_off=0, name="mlstm_prompt")
    carry_s = jnp.concatenate([jnp.zeros((DEC_BATCH, SUBLANES - (CONV_WIDTH - 1), 2 * M_WIDTH), F32),
                               state_conv[l]], axis=1)
    m0_s = jnp.pad(state_m[l], ((0, 0), (0, LANES - M_HEADS))).reshape(DEC_BATCH, 1, LANES)
    hm_s, C_s, n_s, m_s = _mlstm(*common, carry_s, state_C[l], state_n[l], m0_s,
                                 nb=DEC_BATCH, L=DEC_SEQ, nc=1, row_off=N_P, name="mlstm_sample")

    w_router = jnp.concatenate([w_router_group[l], w_router_expert[l],
                                jnp.zeros((D_MODEL, LANES - N_GROUPS - N_EXPERTS), F32)], axis=1)
    w_router_hi = w_router.astype(BF16)
    w_router = jnp.stack([w_router_hi, (w_router - w_router_hi.astype(F32)).astype(BF16)])
    b_router =jnp.concatenate([b_router_group[l], b_router_expert[l],
                                jnp.zeros((LANES - N_GROUPS - N_EXPERTS,), F32)]).reshape(1, LANES)
    x2, h2t, rinfo, rt, cnt = _outproj(x_prompt, x_sample, att_p, att_s, hm_p, hm_s, modc,
                                       g_norm_ffn[l].reshape(1, -1), w_out[l].astype(BF16), w_router, b_router)

    dest, blk_e, blk_valid = _moe_plan(rt, cnt)
    o_rows = _experts(blk_e, blk_valid, _invert(dest), h2t, w_exp_gate[l], w_exp_up[l], w_exp_down[l])
    y_prompt, y_sample = _combine(o_rows, x2, rinfo, modc, g_final.reshape(1, -1))

    kv_p = kv.reshape(N_TOK // WINDOW, WINDOW, 2 * A_KV_WIDTH)[SEQ // WINDOW - 1:N_P // WINDOW:SEQ // WINDOW]
    win_k_p = kv_p[..., :A_KV_WIDTH].reshape(1, BATCH, WINDOW, A_KV_HEADS, A_HEAD_DIM)
    win_v_p = kv_p[..., A_KV_WIDTH:].reshape(1, BATCH, WINDOW, A_KV_HEADS, A_HEAD_DIM)
    kv_s = kv[N_P:].reshape(DEC_BATCH, DEC_SEQ, 2 * A_KV_WIDTH)
    win_k_s = jnp.concatenate([cache_win_k[l][:, DEC_SEQ:],
                               kv_s[..., :A_KV_WIDTH].reshape(DEC_BATCH, DEC_SEQ, A_KV_HEADS, A_HEAD_DIM)], axis=1)[None]
    win_v_s = jnp.concatenate([cache_win_v[l][:, DEC_SEQ:],
                               kv_s[..., A_KV_WIDTH:].reshape(DEC_BATCH, DEC_SEQ, A_KV_HEADS, A_HEAD_DIM)], axis=1)[None]
    qkm_c = qkm.reshape(N_CHUNKS, CHUNK, 2 * M_WIDTH)
    tail = slice(CHUNK - (CONV_WIDTH - 1), CHUNK)
    conv_p = qkm_c[SEQ // CHUNK - 1:N_P // CHUNK:SEQ // CHUNK, tail][None]
    conv_s = qkm_c[N_P // CHUNK:, tail][None]
    return (y_prompt, y_sample,
            win_k_p, win_v_p, conv_p, C_p[None], n_p[None], m_p[:, 0, :M_HEADS][None],
            win_k_s, win_v_s, conv_s, C_s[None], n_s[None], m_s[:, 0, :M_HEADS][None])
```

```python
import functools

import numpy as np
import jax
import jax.numpy as jnp
from jax import lax
from jax.experimental import pallas as pl
from jax.experimental.pallas import tpu as pltpu

F32 = jnp.float32
BF16 = jnp.bfloat16
HIGHEST = lax.Precision.HIGHEST

LANES = 128
SUBLANES = 8

D_MODEL = 1024
BATCH = 8
SEQ = 2048
DEC_BATCH = 32
DEC_SEQ = 64
PAST_LEN = 4096
CHUNK = 64
A_WIDTH = 512
A_HEAD_DIM = 64
A_HEADS = 8
A_KV_HEADS = 2
A_GROUP = 4
A_KV_WIDTH = 128
WINDOW = 128
ROT_DIM = 16
ROPE_THETA = 500000.0
M_WIDTH = 512
M_HEADS = 4
M_HEAD_DIM = 128
CONV_WIDTH = 4
N_GROUPS = 4
EXPERTS_PER_GROUP = 8
N_EXPERTS = 32
D_EXPERT = 512
EPS = 1e-6

N_P = BATCH * SEQ
N_S = DEC_BATCH * DEC_SEQ
N_TOK = N_P + N_S
N_CHUNKS = N_TOK // CHUNK
TM = 512
N_TILES = N_TOK // TM
P_TILES = N_P // TM
CH_PER_TILE = TM // CHUNK
ROW_SUB = D_MODEL // LANES
N_ASSIGN = 2 * N_TOK
MOE_R = 256
MOE_BLOCKS = N_ASSIGN // MOE_R + N_EXPERTS
MOE_ROWS = MOE_BLOCKS * MOE_R
C_Q, C_K, C_V, C_QKM, C_VM, C_OM, C_G = 0, 512, 640, 768, 1792, 2304, 2816
D_IN_PAD = 2944
VMEM_LIMIT = 48 * 1024 * 1024


def _sigmoid(x):
    return 1.0 / (1.0 + jnp.exp(-x))


def _mod_kernel(c_ref, w_ref, b_ref, o_ref):
    c = c_ref[...]
    s = c * _sigmoid(c)
    o_ref[...] = jnp.dot(s, w_ref[...], preferred_element_type=F32, precision=HIGHEST) + b_ref[...]


def _modulation(c_all, w_ada, b_ada):
    n = c_all.shape[0]
    bn = 512
    return pl.pallas_call(
        _mod_kernel,
        grid=(6 * D_MODEL // bn,),
        in_specs=[pl.BlockSpec((n, D_MODEL), lambda j: (0, 0)),
                  pl.BlockSpec((D_MODEL, bn), lambda j: (0, j)),
                  pl.BlockSpec((1, bn), lambda j: (0, j))],
        out_specs=pl.BlockSpec((n, bn), lambda j: (0, j)),
        out_shape=jax.ShapeDtypeStruct((n, 6 * D_MODEL), F32),
        name="modulation",
    )(c_all, w_ada, b_ada.reshape(1, -1))


def _xp_spec():
    def idx(i):
        t = jnp.minimum(i, P_TILES - 1)
        return (t // (SEQ // TM), t % (SEQ // TM), 0)
    return pl.BlockSpec((1, TM, D_MODEL), idx)


def _xs_spec():
    return pl.BlockSpec((CH_PER_TILE, DEC_SEQ, D_MODEL), lambda i: (jnp.maximum(i - P_TILES, 0), 0, 0))


def _mod_spec(comp):
    return pl.BlockSpec((CH_PER_TILE, 1, D_MODEL), lambda i: (i, 0, comp))


def _load_x(xp_ref, xs_ref, xbuf):
    i = pl.program_id(0)

    @pl.when(i < P_TILES)
    def _():
        xbuf[...] = xp_ref[0]

    @pl.when(i >= P_TILES)
    def _():
        xbuf[...] = xs_ref[...].reshape(TM, D_MODEL)

    return xbuf[...]


def _per_chunk(m_ref):
    m = m_ref[...]
    return jnp.broadcast_to(m, (CH_PER_TILE, CHUNK, D_MODEL)).reshape(TM, D_MODEL)


def _rmsnorm(x, g):
    return x * lax.rsqrt(jnp.mean(x * x, axis=-1, keepdims=True) + EPS) * g


def _rope(x, cos, sa, sb):
    n = x.shape[1]
    rep = n // LANES
    if rep > 1:
        cos = jnp.concatenate([cos] * rep, axis=1)
        sa = jnp.concatenate([sa] * rep, axis=1)
        sb = jnp.concatenate([sb] * rep, axis=1)
    return x * cos + pltpu.roll(x, n - ROT_DIM // 2, 1) * sa + pltpu.roll(x, ROT_DIM // 2, 1) * sb


def _inproj_kernel(xp_ref, xs_ref, sh_ref, sc_ref, g_ref, w_ref, cos_ref, sa_ref, sb_ref,
                   q_ref, kv_ref, qkm_ref, vm_ref, om_ref, gt_ref, xbuf):
    x = _load_x(xp_ref, xs_ref, xbuf)
    h = _rmsnorm(x, g_ref[...]) * (1.0 + _per_chunk(sc_ref)) + _per_chunk(sh_ref)
    hb = h.astype(BF16)

    def proj(a, b):
        return jnp.dot(hb, w_ref[:, a:b], preferred_element_type=F32)

    cos, sa, sb = cos_ref[...], sa_ref[...], sb_ref[...]
    q_ref[...] = _rope(proj(C_Q, C_K), cos, sa, sb).astype(BF16)
    kv_ref[:, :A_KV_WIDTH] = _rope(proj(C_K, C_V), cos, sa, sb)
    kv_ref[:, A_KV_WIDTH:] = proj(C_V, C_QKM)
    qkm_ref[...] = proj(C_QKM, C_VM)
    vm_ref[...] = proj(C_VM, C_OM).astype(BF16)
    om_ref[...] = proj(C_OM, C_G)
    gt_ref[...] = proj(C_G, D_IN_PAD)


def _rope_tables():
    pos = np.concatenate([np.arange(SEQ), np.tile(PAST_LEN + np.arange(DEC_SEQ), CH_PER_TILE)]).astype(np.float64)
    inv_freq = ROPE_THETA ** (-np.arange(0, ROT_DIM, 2, dtype=np.float64) / ROT_DIM)
    lane = np.arange(LANES)
    hl = lane % A_HEAD_DIM
    ang = pos[:, None] * inv_freq[hl % (ROT_DIM // 2)][None, :]
    rot = (hl < ROT_DIM)[None, :]
    lo = (hl < ROT_DIM // 2)[None, :]
    cos = np.where(rot, np.cos(ang), 1.0)
    sa = np.where(lo, -np.sin(ang), 0.0)
    sb = np.where(rot & ~lo, np.sin(ang), 0.0)
    return [jnp.asarray(t, F32) for t in (cos, sa, sb)]


def _inproj(x_prompt, x_sample, modc, g_mix, w_in_r):
    cos, sa, sb = _rope_tables()
    tab_spec = pl.BlockSpec((TM, LANES), lambda i: (jnp.where(i < P_TILES, i % (SEQ // TM), SEQ // TM), 0))

    def out(n, dtype=F32):
        return pl.BlockSpec((TM, n), lambda i: (i, 0)), jax.ShapeDtypeStruct((N_TOK, n), dtype)

    outs = [out(A_WIDTH, BF16), out(2 * A_KV_WIDTH), out(2 * M_WIDTH), out(M_WIDTH, BF16), out(M_WIDTH), out(LANES)]
    return pl.pallas_call(
        _inproj_kernel,
        grid=(N_TILES,),
        in_specs=[_xp_spec(), _xs_spec(), _mod_spec(0), _mod_spec(1),
                  pl.BlockSpec((1, D_MODEL), lambda i: (0, 0)),
                  pl.BlockSpec((D_MODEL, D_IN_PAD), lambda i: (0, 0)),
                  tab_spec, tab_spec, tab_spec],
        out_specs=[o[0] for o in outs],
        out_shape=[o[1] for o in outs],
        scratch_shapes=[pltpu.VMEM((TM, D_MODEL), F32)],
        compiler_params=pltpu.CompilerParams(vmem_limit_bytes=VMEM_LIMIT),
        name="inproj",
    )(x_prompt, x_sample, modc, modc, g_mix, w_in_r, cos, sa, sb)


def _attn_chunk_kernel(sink_ref, q_ref, prev_ref, cur_ref, o_ref):
    q = q_ref[...] * (A_HEAD_DIM ** -0.5)
    kv = jnp.concatenate([prev_ref[...], cur_ref[...]], axis=0)
    for g in range(A_KV_HEADS):
        kg = kv[:, g * A_HEAD_DIM:(g + 1) * A_HEAD_DIM].astype(BF16)
        vg = kv[:, A_KV_WIDTH + g * A_HEAD_DIM:A_KV_WIDTH + (g + 1) * A_HEAD_DIM].astype(BF16)
        heads = [g * A_GROUP + i for i in range(A_GROUP)]
        qc = jnp.concatenate([q[:, h * A_HEAD_DIM:(h + 1) * A_HEAD_DIM] for h in heads], axis=0).astype(BF16)
        snk = jnp.concatenate([jnp.full((CHUNK, 1), sink_ref[h], F32) for h in heads], axis=0)
        s = lax.dot_general(qc, kg, (((1,), (1,)), ((), ())), preferred_element_type=F32)
        mx = jnp.maximum(jnp.max(s, axis=-1, keepdims=True), snk)
        p = jnp.exp(s - mx)
        den = jnp.sum(p, axis=-1, keepdims=True) + jnp.exp(snk - mx)
        o = jnp.dot(p.astype(BF16), vg, preferred_element_type=F32) / den
        for i, h in enumerate(heads):
            o_ref[:, h * A_HEAD_DIM:(h + 1) * A_HEAD_DIM] = o[i * CHUNK:(i + 1) * CHUNK]


ATT_TQ = 256
ATT_QB = 2 * CHUNK


def _attn_band_kernel(sink_ref, q_ref, prev_ref, cur_ref, o_ref, att_t):
    nk = ATT_QB + WINDOW
    q = (q_ref[...] * (A_HEAD_DIM ** -0.5)).astype(BF16)
    kv = jnp.concatenate([prev_ref[...], cur_ref[...]], axis=0)
    k2 = kv[:, :A_KV_WIDTH]
    k2r = pltpu.roll(k2, A_HEAD_DIM, 1)
    low = lax.broadcasted_iota(jnp.int32, k2.shape, 1) < A_HEAD_DIM
    k_placed = {(0, 0): jnp.where(low, k2, 0.0), (0, 1): jnp.where(low, 0.0, k2r),
                (1, 0): jnp.where(low, k2r, 0.0), (1, 1): jnp.where(low, 0.0, k2)}
    k_placed = {key: val.astype(BF16) for key, val in k_placed.items()}
    v_t = kv[:, A_KV_WIDTH:].T.astype(BF16)
    key_chunk = lax.broadcasted_iota(jnp.int32, (nk, ATT_QB), 0) // CHUNK
    q_chunk = lax.broadcasted_iota(jnp.int32, (nk, ATT_QB), 1) // CHUNK
    band = (key_chunk >= q_chunk) & (key_chunk <= q_chunk + WINDOW // CHUNK)
    has_history = pl.program_id(1) > 0
    for blk in range(ATT_TQ // ATT_QB):
        keys = slice(blk * ATT_QB, blk * ATT_QB + nk)
        qrows = slice(blk * ATT_QB, (blk + 1) * ATT_QB)
        valid = band & ((key_chunk >= WINDOW // CHUNK) | has_history) if blk == 0 else band
        for h in range(A_HEADS):
            g = h // A_GROUP
            s_t = lax.dot_general(k_placed[(g, h % 2)][keys], q[qrows, (h // 2) * LANES:(h // 2 + 1) * LANES],
                                  (((1,), (1,)), ((), ())), preferred_element_type=F32)
            s_t = jnp.where(valid, s_t, -jnp.inf)
            snk = sink_ref[h]
            mx = jnp.maximum(jnp.max(s_t, axis=0, keepdims=True), snk)
            p_t = jnp.exp(s_t - mx)
            den = jnp.sum(p_t, axis=0, keepdims=True) + jnp.exp(snk - mx)
            o_t = jnp.dot(v_t[:, keys], p_t.astype(BF16), preferred_element_type=F32)
            att_t[h * A_HEAD_DIM:(h + 1) * A_HEAD_DIM, qrows] = o_t[g * A_HEAD_DIM:(g + 1) * A_HEAD_DIM] / den
    o_ref[...] = att_t[...].T.astype(BF16)


def _attention(sinks, q, kv, cache_kv):
    smem = pl.BlockSpec(memory_space=pltpu.SMEM)
    tq = ATT_TQ
    nq = SEQ // tq
    att = pl.pallas_call(
        _attn_band_kernel,
        grid=(BATCH, nq),
        scratch_shapes=[pltpu.VMEM((A_WIDTH, ATT_TQ), F32)],
        in_specs=[smem,
                  pl.BlockSpec((tq, A_WIDTH), lambda b, j: (b * nq + j, 0)),
                  pl.BlockSpec((WINDOW, 2 * A_KV_WIDTH),
                               lambda b, j: (jnp.maximum((b * nq + j) * (tq // WINDOW) - 1, 0), 0)),
                  pl.BlockSpec((tq, 2 * A_KV_WIDTH), lambda b, j: (b * nq + j, 0))],
        out_specs=pl.BlockSpec((tq, A_WIDTH), lambda b, j: (b * nq + j, 0)),
        out_shape=jax.ShapeDtypeStruct((N_P, A_WIDTH), BF16),
        name="attn_prompt",
    )(sinks, q, kv, kv)
    off = N_P // DEC_SEQ
    att_s = pl.pallas_call(
        _attn_chunk_kernel,
        grid=(DEC_BATCH,),
        in_specs=[smem,
                  pl.BlockSpec((DEC_SEQ, A_WIDTH), lambda b: (off + b, 0)),
                  pl.BlockSpec((WINDOW, 2 * A_KV_WIDTH), lambda b: (b, 0)),
                  pl.BlockSpec((DEC_SEQ, 2 * A_KV_WIDTH), lambda b: (off + b, 0))],
        out_specs=pl.BlockSpec((DEC_SEQ, A_WIDTH), lambda b: (b, 0)),
        out_shape=jax.ShapeDtypeStruct((N_S, A_WIDTH), F32),
        name="attn_sample",
    )(sinks, q, cache_kv, kv)
    return att, att_s


def _conv4(x, w, b):
    y = b + x * w[CONV_WIDTH - 1:CONV_WIDTH]
    for j in range(1, CONV_WIDTH):
        y = y + pltpu.roll(x, j, 0) * w[CONV_WIDTH - 1 - j:CONV_WIDTH - j]
    return y


def _mlstm_kernel(qkm_ref, vm_ref, om_ref, gt_ref, cw_ref, cb_ref, gb_ref, gmh_ref,
                  carry0_ref, c0_ref, n0_ref, m0_ref,
                  hm_ref, cout_ref, nout_ref, mout_ref, c_s, n_s, m_s, carry_s, *, L):
    c = pl.program_id(1)

    @pl.when(c == 0)
    def _():
        c_s[...] = c0_ref[0]
        n_s[...] = n0_ref[0]
        m_s[...] = m0_ref[0]
        carry_s[...] = carry0_ref[0]

    x = qkm_ref[...]
    w = cw_ref[...]
    b = cb_ref[...]
    y = _conv4(x, w, b)
    y8 = _conv4(jnp.concatenate([carry_s[...], x[:SUBLANES]], axis=0), w, b)
    y = jnp.concatenate([y8[SUBLANES:], y[SUBLANES:]], axis=0)
    carry_s[...] = x[L - SUBLANES:]
    a = y * _sigmoid(y)
    qa = a[:, :M_WIDTH] * (M_HEAD_DIM ** -0.5)
    ka = a[:, M_WIDTH:]
    v = vm_ref[...]
    om = om_ref[...]
    gmh = gmh_ref[...]

    z = gt_ref[...] + gb_ref[...]
    lane = lax.broadcasted_iota(jnp.int32, (L, LANES), 1)
    f_log = jnp.minimum(z, 0.0) - jnp.log1p(jnp.exp(-jnp.abs(z)))
    val = jnp.where(lane < M_HEADS, z, f_log)
    row = lax.broadcasted_iota(jnp.int32, (L, L), 0)
    col = lax.broadcasted_iota(jnp.int32, (L, L), 1)
    causal = row >= col
    cum = jnp.dot(causal.astype(F32), val, preferred_element_type=F32, precision=HIGHEST)
    G = jnp.where(lane < M_HEADS, val, cum)
    sel = (lax.broadcasted_iota(jnp.int32, (SUBLANES, LANES), 0)
           == lax.broadcasted_iota(jnp.int32, (SUBLANES, LANES), 1)).astype(F32)
    GT = lax.dot_general(sel, G, (((1,), (1,)), ((), ())), preferred_element_type=F32, precision=HIGHEST)

    m_row = m_s[...]
    lane1 = lax.broadcasted_iota(jnp.int32, (1, LANES), 1)
    outs = []
    for h in range(M_HEADS):
        a_col = G[:, M_HEADS + h:M_HEADS + h + 1]
        i_col = G[:, h:h + 1]
        i_row = GT[h:h + 1, :]
        b_row = GT[M_HEADS + h:M_HEADS + h + 1, :]
        m_prev = m_row[:, h:h + 1]
        dm = jnp.where(causal, a_col - b_row + i_row, -jnp.inf)
        inter = a_col + m_prev
        m_t = jnp.maximum(inter, jnp.max(dm, axis=-1, keepdims=True))
        w_intra = jnp.exp(dm - m_t)
        w_inter = jnp.exp(inter - m_t)
        sl = slice(h * M_HEAD_DIM, (h + 1) * M_HEAD_DIM)
        qh, kh, vh = qa[:, sl], ka[:, sl], v[:, sl]
        qb = qh.astype(BF16)
        scores = lax.dot_general(qb, kh.astype(BF16), (((1,), (1,)), ((), ())), preferred_element_type=F32) * w_intra
        ch = c_s[h]
        nh = n_s[h:h + 1, :]
        num = (jnp.dot(scores.astype(BF16), vh.astype(BF16), preferred_element_type=F32)
               + w_inter * lax.dot_general(qb, ch.astype(BF16), (((1,), (1,)), ((), ())), preferred_element_type=F32))
        den = jnp.sum(scores, axis=-1, keepdims=True) + w_inter * jnp.sum(qh * nh, axis=-1, keepdims=True)
        hv = num / jnp.maximum(jnp.abs(den), jnp.exp(-m_t))
        hn = hv * lax.rsqrt(jnp.mean(hv * hv, axis=-1, keepdims=True) + EPS)
        outs.append(hn * gmh[:, sl] * _sigmoid(om[:, sl]))
        m_new = m_t[L - 1:L, :]
        a_last = a_col[L - 1:L, :]
        w_end = jnp.exp(a_last - a_col + i_col - m_new)
        decay = jnp.exp(a_last + m_prev - m_new)
        kw = kh * w_end
        c_s[h] = decay * ch + lax.dot_general(vh.astype(BF16), kw.astype(BF16), (((0,), (0,)), ((), ())),
                                              preferred_element_type=F32)
        n_s[h:h + 1, :] = decay * nh + jnp.sum(kw, axis=0, keepdims=True)
        m_row = jnp.where(lane1 == h, m_new, m_row)
    m_s[...] = m_row
    hm_ref[...] = jnp.concatenate(outs, axis=1).astype(BF16)

    @pl.when(c == pl.num_programs(1) - 1)
    def _():
        cout_ref[0] = c_s[...]
        nout_ref[0] = n_s[...]
        mout_ref[0] = m_s[...]


def _mlstm(qkm, vm, om, gates, conv_w, conv_b, gbias, g_mh, carry0, c0, n0, m0, *, nb, L, nc, row_off, name):
    off = row_off // L

    def tok(n):
        return pl.BlockSpec((L, n), lambda b, c: (off + b * nc + c, 0))

    def const(shape):
        return pl.BlockSpec(shape, lambda b, c: (0,) * len(shape))

    in_specs = [tok(2 * M_WIDTH), tok(M_WIDTH), tok(M_WIDTH), tok(LANES),
                const((CONV_WIDTH, 2 * M_WIDTH)), const((1, 2 * M_WIDTH)), const((1, LANES)), const((1, M_WIDTH)),
                pl.BlockSpec((1, SUBLANES, 2 * M_WIDTH), lambda b, c: (b, 0, 0)),
                pl.BlockSpec((1, M_HEADS, M_HEAD_DIM, M_HEAD_DIM), lambda b, c: (b, 0, 0, 0)),
                pl.BlockSpec((1, M_HEADS, M_HEAD_DIM), lambda b, c: (b, 0, 0)),
                pl.BlockSpec((1, 1, LANES), lambda b, c: (b, 0, 0))]
    args = [qkm, vm, om, gates, conv_w, conv_b, gbias, g_mh, carry0, c0, n0, m0]
    return pl.pallas_call(
        functools.partial(_mlstm_kernel, L=L),
        grid=(nb, nc),
        in_specs=in_specs,
        out_specs=[pl.BlockSpec((L, M_WIDTH), lambda b, c: (b * nc + c, 0)),
                   pl.BlockSpec((1, M_HEADS, M_HEAD_DIM, M_HEAD_DIM), lambda b, c: (b, 0, 0, 0)),
                   pl.BlockSpec((1, M_HEADS, M_HEAD_DIM), lambda b, c: (b, 0, 0)),
                   pl.BlockSpec((1, 1, LANES), lambda b, c: (b, 0, 0))],
        out_shape=[jax.ShapeDtypeStruct((nb * nc * L, M_WIDTH), BF16),
                   jax.ShapeDtypeStruct((nb, M_HEADS, M_HEAD_DIM, M_HEAD_DIM), F32),
                   jax.ShapeDtypeStruct((nb, M_HEADS, M_HEAD_DIM), F32),
                   jax.ShapeDtypeStruct((nb, 1, LANES), F32)],
        scratch_shapes=[pltpu.VMEM((M_HEADS, M_HEAD_DIM, M_HEAD_DIM), F32),
                        pltpu.VMEM((M_HEADS, M_HEAD_DIM), F32),
                        pltpu.VMEM((1, LANES), F32),
                        pltpu.VMEM((SUBLANES, 2 * M_WIDTH), F32)],
        compiler_params=pltpu.CompilerParams(dimension_semantics=("arbitrary", "arbitrary")),
        name=name,
    )(*args)


def _outproj_kernel(xp_ref, xs_ref, attp_ref, atts_ref, hmp_ref, hms_ref, gate_ref, sh_ref, sc_ref, g_ref,
                    w_ref, wr_ref, br_ref, x2_ref, h2_ref, ri_ref, rt_ref, cnt_ref, xbuf, mixbuf):
    x = _load_x(xp_ref, xs_ref, xbuf)
    i = pl.program_id(0)

    @pl.when(i < P_TILES)
    def _():
        mixbuf[:, :A_WIDTH] = attp_ref[...].astype(BF16)
        mixbuf[:, A_WIDTH:] = hmp_ref[...].astype(BF16)

    @pl.when(i >= P_TILES)
    def _():
        mixbuf[:, :A_WIDTH] = atts_ref[...].astype(BF16)
        mixbuf[:, A_WIDTH:] = hms_ref[...].astype(BF16)

    mixed = jnp.dot(mixbuf[...], w_ref[...], preferred_element_type=F32)
    x2 = x + _per_chunk(gate_ref) * mixed
    x2_ref[...] = x2
    h2 = _rmsnorm(x2, g_ref[...]) * (1.0 + _per_chunk(sc_ref)) + _per_chunk(sh_ref)
    for s in range(ROW_SUB):
        h2_ref[pl.ds(s, TM, stride=ROW_SUB), :] = h2[:, s * LANES:(s + 1) * LANES]

    h_hi = h2.astype(BF16)
    h_lo = (h2 - h_hi.astype(F32)).astype(BF16)
    lg = (jnp.dot(h_hi, wr_ref[0], preferred_element_type=F32)
          + jnp.dot(h_lo, wr_ref[0], preferred_element_type=F32)
          + jnp.dot(h_hi, wr_ref[1], preferred_element_type=F32)) + br_ref[...]
    lane = lax.broadcasted_iota(jnp.int32, (TM, LANES), 1)
    lanef = lane.astype(F32)
    ninf = -jnp.inf

    def first_argmax(vals):
        mx = jnp.max(vals, axis=-1, keepdims=True)
        return mx, jnp.min(jnp.where(vals == mx, lanef, float(LANES)), axis=-1, keepdims=True)

    is_grp = lane < N_GROUPS
    gmax, grp = first_argmax(jnp.where(is_grp, lg, ninf))
    p_grp = 1.0 / jnp.sum(jnp.where(is_grp, jnp.exp(lg - gmax), 0.0), axis=-1, keepdims=True)
    base = N_GROUPS + grp * EXPERTS_PER_GROUP
    in_grp = (lanef >= base) & (lanef < base + EXPERTS_PER_GROUP)
    el = jnp.where(in_grp, lg, ninf)
    v1, i1 = first_argmax(el)
    v2, i2 = first_argmax(jnp.where(lanef == i1, ninf, el))
    e = jnp.exp(v2 - v1)
    g1 = p_grp / (1.0 + e)
    g2 = p_grp * e / (1.0 + e)
    e1 = i1 - N_GROUPS
    e2 = i2 - N_GROUPS
    oh1 = lanef == e1
    oh2 = lanef == e2
    oh = jnp.where(oh1 | oh2, 1.0, 0.0)
    earlier = (lax.broadcasted_iota(jnp.int32, (TM, TM), 0) > lax.broadcasted_iota(jnp.int32, (TM, TM), 1))
    before = jnp.dot(earlier.astype(BF16), oh.astype(BF16), preferred_element_type=F32)
    r1 = jnp.sum(jnp.where(oh1, before, 0.0), axis=-1, keepdims=True)
    r2 = jnp.sum(jnp.where(oh2, before, 0.0), axis=-1, keepdims=True)
    cnt_ref[...] = jnp.broadcast_to(jnp.sum(oh, axis=0, keepdims=True), (SUBLANES, LANES))
    cols = (e1, e2, g1, g2, r1, r2)
    ri = jnp.zeros((TM, LANES), F32)
    for k, cval in enumerate(cols):
        ri = jnp.where(lane == k, cval, ri)
    ri_ref[...] = ri
    rt_ref[...] = ri.T[:SUBLANES]


def _outproj(x_prompt, x_sample, att_p, att_s, hm_p, hm_s, modc, g_ffn, w_out_b, w_router, b_router):
    def tok(n):
        return pl.BlockSpec((TM, n), lambda i: (i, 0))

    def tok_p(n):
        return pl.BlockSpec((TM, n), lambda i: (jnp.minimum(i, P_TILES - 1), 0))

    def tok_s(n):
        return pl.BlockSpec((TM, n), lambda i: (jnp.maximum(i - P_TILES, 0), 0))

    def const(shape):
        return pl.BlockSpec(shape, lambda i: (0,) * len(shape))

    return pl.pallas_call(
        _outproj_kernel,
        grid=(N_TILES,),
        in_specs=[_xp_spec(), _xs_spec(), tok_p(A_WIDTH), tok_s(A_WIDTH), tok_p(M_WIDTH), tok_s(M_WIDTH),
                  _mod_spec(2), _mod_spec(3), _mod_spec(4),
                  const((1, D_MODEL)), const((D_MODEL, D_MODEL)), const((2, D_MODEL, LANES)), const((1, LANES))],
        out_specs=[tok(D_MODEL), pl.BlockSpec((TM * ROW_SUB, LANES), lambda i: (i, 0)), tok(LANES),
                   pl.BlockSpec((SUBLANES, TM), lambda i: (0, i)),
                   pl.BlockSpec((SUBLANES, LANES), lambda i: (i, 0))],
        out_shape=[jax.ShapeDtypeStruct((N_TOK, D_MODEL), F32),
                   jax.ShapeDtypeStruct((N_TOK * ROW_SUB, LANES), F32),
                   jax.ShapeDtypeStruct((N_TOK, LANES), F32),
                   jax.ShapeDtypeStruct((SUBLANES, N_TOK), F32),
                   jax.ShapeDtypeStruct((N_TILES * SUBLANES, LANES), F32)],
        scratch_shapes=[pltpu.VMEM((TM, D_MODEL), F32), pltpu.VMEM((TM, D_MODEL), BF16)],
        compiler_params=pltpu.CompilerParams(vmem_limit_bytes=VMEM_LIMIT),
        name="outproj",
    )(x_prompt, x_sample, att_p, att_s, hm_p, hm_s, modc, modc, modc, g_ffn, w_out_b, w_router, b_router)


def _row(ref, r):
    return ref.at[pl.ds(pl.multiple_of(r * ROW_SUB, ROW_SUB), ROW_SUB), :]


def _row_copies(i, t, dest_ref, make):
    tok = i * TM + t
    return [make(k, dest_ref[k * N_TOK + tok]) for k in range(2)]


def _dispatch_kernel(dest_ref, zblk_ref, h_ref, xs_out, zbuf, sem, zsem):
    i = pl.program_id(0)
    blk_rows = MOE_R * ROW_SUB

    @pl.when(i == 0)
    def _():
        zbuf[...] = jnp.zeros_like(zbuf)

        def zero_copy(e):
            start = pl.multiple_of(jnp.maximum(zblk_ref[e], 0) * blk_rows, blk_rows)
            return pltpu.make_async_copy(zbuf, xs_out.at[pl.ds(start, blk_rows), :], zsem)

        def z_start(e, carry):
            @pl.when(zblk_ref[e] >= 0)
            def _():
                zero_copy(e).start()
            return carry

        def z_wait(e, carry):
            @pl.when(zblk_ref[e] >= 0)
            def _():
                zero_copy(e).wait()
            return carry

        lax.fori_loop(0, N_EXPERTS, z_start, 0)
        lax.fori_loop(0, N_EXPERTS, z_wait, 0)

        def tail_copy(b):
            return pltpu.make_async_copy(zbuf, xs_out.at[pl.ds(pl.multiple_of(b * blk_rows, blk_rows), blk_rows), :], zsem)

        def t_start(b, carry):
            tail_copy(b).start()
            return carry

        def t_wait(b, carry):
            tail_copy(b).wait()
            return carry

        lax.fori_loop(zblk_ref[N_EXPERTS], MOE_BLOCKS, t_start, 0)
        lax.fori_loop(zblk_ref[N_EXPERTS], MOE_BLOCKS, t_wait, 0)

    def copies(t):
        return _row_copies(i, t, dest_ref, lambda k, d: pltpu.make_async_copy(_row(h_ref, t), _row(xs_out, d), sem))

    def issue(t, carry):
        for k, cp in enumerate(copies(t)):
            cp.start(priority=k)
        return carry

    def wait(t, carry):
        for cp in copies(t):
            cp.wait()
        return carry

    lax.fori_loop(0, TM, issue, 0, unroll=8)
    lax.fori_loop(0, TM, wait, 0, unroll=8)


def _dispatch(dest, zblk, h2t):
    return pl.pallas_call(
        _dispatch_kernel,
        grid_spec=pltpu.PrefetchScalarGridSpec(
            num_scalar_prefetch=2,
            grid=(N_TILES,),
            in_specs=[pl.BlockSpec((TM * ROW_SUB, LANES), lambda i, d, z: (i, 0))],
            out_specs=pl.BlockSpec(memory_space=pl.ANY),
            scratch_shapes=[pltpu.VMEM((MOE_R * ROW_SUB, LANES), F32),
                            pltpu.SemaphoreType.DMA, pltpu.SemaphoreType.DMA]),
        out_shape=jax.ShapeDtypeStruct((MOE_ROWS * ROW_SUB, LANES), F32),
        compiler_params=pltpu.CompilerParams(dimension_semantics=("arbitrary",)),
        name="moe_dispatch",
    )(dest, zblk, h2t)


def _expert_kernel(be_ref, bv_ref, xs_ref, wg_ref, wu_ref, wd_ref, o_ref, wg_s, wu_s, wd_s):
    b = pl.program_id(0)
    prev = be_ref[jnp.maximum(b - 1, 0)]

    @pl.when((b == 0) | (be_ref[b] != prev))
    def _():
        wg_s[...] = wg_ref[0].astype(BF16)
        wu_s[...] = wu_ref[0].astype(BF16)
        wd_s[...] = wd_ref[0].astype(BF16)

    @pl.when(bv_ref[b] > 0)
    def _():
        x = jnp.concatenate([xs_ref[pl.ds(s, MOE_R, stride=ROW_SUB), :] for s in range(ROW_SUB)], axis=1)
        xb = x.astype(BF16)
        g = jnp.dot(xb, wg_s[...], preferred_element_type=F32)
        u = jnp.dot(xb, wu_s[...], preferred_element_type=F32)
        a = (g * _sigmoid(g) * u).astype(BF16)
        o = jnp.dot(a, wd_s[...], preferred_element_type=F32)
        for s in range(ROW_SUB):
            o_ref[pl.ds(s, MOE_R, stride=ROW_SUB), :] = o[:, s * LANES:(s + 1) * LANES]

    @pl.when(bv_ref[b] == 0)
    def _():
        o_ref[...] = jnp.zeros_like(o_ref)


def _experts(blk_e, blk_valid, xs, w_g, w_u, w_d):
    def rows(b, be, bv):
        return (jnp.where(bv[b] > 0, b, MOE_BLOCKS - 1), 0)

    def weights(b, be, bv):
        return (be[b], 0, 0)

    return pl.pallas_call(
        _expert_kernel,
        grid_spec=pltpu.PrefetchScalarGridSpec(
            num_scalar_prefetch=2,
            grid=(MOE_BLOCKS,),
            in_specs=[pl.BlockSpec((MOE_R * ROW_SUB, LANES), rows),
                      pl.BlockSpec((1, D_MODEL, D_EXPERT), weights),
                      pl.BlockSpec((1, D_MODEL, D_EXPERT), weights),
                      pl.BlockSpec((1, D_EXPERT, D_MODEL), weights)],
            out_specs=pl.BlockSpec((MOE_R * ROW_SUB, LANES), lambda b, be, bv: (b, 0)),
            scratch_shapes=[pltpu.VMEM((D_MODEL, D_EXPERT), BF16),
                            pltpu.VMEM((D_MODEL, D_EXPERT), BF16),
                            pltpu.VMEM((D_EXPERT, D_MODEL), BF16)]),
        out_shape=jax.ShapeDtypeStruct((MOE_ROWS * ROW_SUB, LANES), F32),
        compiler_params=pltpu.CompilerParams(dimension_semantics=("arbitrary",), vmem_limit_bytes=VMEM_LIMIT),
        name="moe_experts",
    )(blk_e, blk_valid, xs, w_g, w_u, w_d)


def _combine_kernel(dest_ref, o_hbm, x2_ref, ri_ref, gate_ref, gf_ref, yp_ref, ys_ref, obuf, sem):
    i = pl.program_id(0)
    slot = i % 2
    tile_rows = TM * ROW_SUB

    def issue_tile(tile, slot_):
        def issue(t, carry):
            cps = _row_copies(tile, t, dest_ref, lambda k, d: pltpu.make_async_copy(
                _row(o_hbm, d), _row(obuf, (2 * slot_ + k) * TM + t), sem.at[slot_]))
            for k, cp in enumerate(cps):
                cp.start(priority=k)
            return carry

        lax.fori_loop(0, TM, issue, 0, unroll=8)

    @pl.when(i == 0)
    def _():
        issue_tile(0, 0)

    @pl.when(i + 1 < N_TILES)
    def _():
        issue_tile(i + 1, 1 - slot)

    for k in range(2):
        start = pl.multiple_of((2 * slot + k) * tile_rows, tile_rows)
        pltpu.make_async_copy(o_hbm.at[pl.ds(0, tile_rows), :], obuf.at[pl.ds(start, tile_rows), :],
                              sem.at[slot]).wait()

    ri = ri_ref[...]
    g1 = ri[:, 2:3]
    g2 = ri[:, 3:4]
    base = 2 * slot * tile_rows
    moe = jnp.concatenate(
        [g1 * obuf[pl.ds(base + s, TM, stride=ROW_SUB), :] + g2 * obuf[pl.ds(base + tile_rows + s, TM, stride=ROW_SUB), :]
         for s in range(ROW_SUB)], axis=1)
    x3 = x2_ref[...] + _per_chunk(gate_ref) * moe
    y = _rmsnorm(x3, gf_ref[...])

    @pl.when(i < P_TILES)
    def _():
        yp_ref[0] = y

    @pl.when(i >= P_TILES)
    def _():
        ys_ref[...] = y.reshape(CH_PER_TILE, DEC_SEQ, D_MODEL)


def _combine(dest, o_rows, x2, rinfo, modc, g_final):
    xp_spec, xs_spec = _xp_spec(), _xs_spec()
    return pl.pallas_call(
        _combine_kernel,
        grid_spec=pltpu.PrefetchScalarGridSpec(
            num_scalar_prefetch=1,
            grid=(N_TILES,),
            in_specs=[pl.BlockSpec(memory_space=pl.ANY),
                      pl.BlockSpec((TM, D_MODEL), lambda i, d: (i, 0)),
                      pl.BlockSpec((TM, LANES), lambda i, d: (i, 0)),
                      pl.BlockSpec((CH_PER_TILE, 1, D_MODEL), lambda i, d: (i, 0, 5)),
                      pl.BlockSpec((1, D_MODEL), lambda i, d: (0, 0))],
            out_specs=[pl.BlockSpec(xp_spec.block_shape, lambda i, d: xp_spec.index_map(i)),
                       pl.BlockSpec(xs_spec.block_shape, lambda i, d: xs_spec.index_map(i))],
            scratch_shapes=[pltpu.VMEM((4 * TM * ROW_SUB, LANES), F32), pltpu.SemaphoreType.DMA((2,))]),
        out_shape=[jax.ShapeDtypeStruct((BATCH, SEQ, D_MODEL), F32),
                   jax.ShapeDtypeStruct((DEC_BATCH, DEC_SEQ, D_MODEL), F32)],
        compiler_params=pltpu.CompilerParams(dimension_semantics=("arbitrary",), vmem_limit_bytes=VMEM_LIMIT),
        name="moe_combine",
    )(dest, o_rows, x2, rinfo, modc, g_final)


def _moe_plan(rt, cnt):
    experts = np.arange(N_EXPERTS)
    tile_cnt = cnt.reshape(N_TILES, SUBLANES, LANES)[:, 0, :N_EXPERTS]
    counts = jnp.sum(tile_cnt, axis=0)
    padded = jnp.ceil(counts / MOE_R) * MOE_R
    pad_end = jnp.sum(padded[:, None] * (experts[:, None] <= experts[None, :]), axis=0)
    base = pad_end - padded
    tiles = np.arange(N_TILES)
    earlier_tiles = (tiles[None, :, None] < tiles[:, None, None])
    tile_base = base[None, :] + jnp.sum(tile_cnt[None, :, :] * earlier_tiles, axis=1)
    eid = rt[0:2].reshape(1, 2, N_TILES, TM)
    rank = rt[4:6].reshape(2, N_TILES, TM)
    pick = eid == experts.astype(np.float32).reshape(N_EXPERTS, 1, 1, 1)
    dest = jnp.sum(jnp.where(pick, tile_base.T[:, None, :, None], 0.0), axis=0) + rank
    blk_start = (np.arange(MOE_BLOCKS) * MOE_R).astype(np.float32)
    blk_e = jnp.minimum(jnp.sum((blk_start[:, None] >= pad_end[None, :]).astype(F32), axis=1), N_EXPERTS - 1)
    mine = blk_e[:, None] == experts.astype(np.float32)[None, :]
    blk_fill = jnp.sum(jnp.where(mine, (counts + base)[None, :], 0.0), axis=1) - blk_start
    blk_valid = jnp.clip(blk_fill, 0, MOE_R)
    partial = jnp.where(counts != padded, jnp.floor((base + counts) / MOE_R), -1.0)
    zblk = jnp.concatenate([partial, pad_end[-1:] / MOE_R])
    return (dest.reshape(-1).astype(jnp.int32), blk_e.astype(jnp.int32), blk_valid.astype(jnp.int32),
            zblk.astype(jnp.int32))


def kernel(x_prompt, x_sample, c_prompt, c_sample, cache_win_k, cache_win_v, state_conv, state_C, state_n, state_m, w_ada, b_ada, g_norm_mix, g_norm_ffn, w_in, attn_sinks, conv_w, conv_b, b_igate, b_fgate, g_mhnorm, w_out, w_router_group, b_router_group, w_router_expert, b_router_expert, w_exp_gate, w_exp_up, w_exp_down, g_final):
    l = 0
    mod = _modulation(jnp.concatenate([c_prompt, c_sample], axis=0), w_ada[l], b_ada[l])
    chunk_stream = np.concatenate([np.repeat(np.arange(BATCH), SEQ // CHUNK), BATCH + np.arange(DEC_BATCH)])
    modc = mod[chunk_stream].reshape(N_CHUNKS, 1, 6 * D_MODEL)

    wi = w_in[l]
    s_q, s_k, s_v, s_qkm, s_vm, s_ig, s_fg = 0, 512, 640, 768, 1792, 2304, 2308
    s_om = 2312
    w_in_r = jnp.concatenate(
        [wi[:, s_q:s_qkm], wi[:, s_qkm:s_vm], wi[:, s_vm:s_ig], wi[:, s_om:], wi[:, s_ig:s_om],
         jnp.zeros((D_MODEL, LANES - 2 * M_HEADS), F32)], axis=1).astype(BF16)
    q, kv, qkm, vm, om, gates = _inproj(x_prompt, x_sample, modc, g_norm_mix[l].reshape(1, -1), w_in_r)

    cache_kv = jnp.concatenate([cache_win_k[l].reshape(DEC_BATCH * WINDOW, A_KV_WIDTH),
                                cache_win_v[l].reshape(DEC_BATCH * WINDOW, A_KV_WIDTH)], axis=1)
    att_p, att_s = _attention(attn_sinks[l], q, kv, cache_kv)

    gbias = jnp.concatenate([b_igate[l], b_fgate[l], jnp.zeros((LANES - 2 * M_HEADS,), F32)]).reshape(1, LANES)
    common = (qkm, vm, om, gates, conv_w[l], conv_b[l].reshape(1, -1), gbias, g_mhnorm[l].reshape(1, -1))
    zeros_p = (jnp.zeros((BATCH, SUBLANES, 2 * M_WIDTH), F32),
               jnp.zeros((BATCH, M_HEADS, M_HEAD_DIM, M_HEAD_DIM), F32),
               jnp.zeros((BATCH, M_HEADS, M_HEAD_DIM), F32),
               jnp.zeros((BATCH, 1, LANES), F32))
    LP = 256
    hm_p, C_p, n_p, m_p = _mlstm(*common, *zeros_p, nb=BATCH, L=LP, nc=SEQ // LP, row_off=0, name="mlstm_prompt")
    carry_s = jnp.concatenate([jnp.zeros((DEC_BATCH, SUBLANES - (CONV_WIDTH - 1), 2 * M_WIDTH), F32),
                               state_conv[l]], axis=1)
    m0_s = jnp.pad(state_m[l], ((0, 0), (0, LANES - M_HEADS))).reshape(DEC_BATCH, 1, LANES)
    hm_s, C_s, n_s, m_s = _mlstm(*common, carry_s, state_C[l], state_n[l], m0_s,
                                 nb=DEC_BATCH, L=DEC_SEQ, nc=1, row_off=N_P, name="mlstm_sample")

    w_router = jnp.concatenate([w_router_group[l], w_router_expert[l],
                                jnp.zeros((D_MODEL, LANES - N_GROUPS - N_EXPERTS), F32)], axis=1)
    w_router_hi = w_router.astype(BF16)
    w_router = jnp.stack([w_router_hi, (w_router - w_router_hi.astype(F32)).astype(BF16)])
    b_router =jnp.concatenate([b_router_group[l], b_router_expert[l],
                                jnp.zeros((LANES - N_GROUPS - N_EXPERTS,), F32)]).reshape(1, LANES)
    x2, h2t, rinfo, rt, cnt = _outproj(x_prompt, x_sample, att_p, att_s, hm_p, hm_s, modc,
                                       g_norm_ffn[l].reshape(1, -1), w_out[l].astype(BF16), w_router, b_router)

    dest, blk_e, blk_valid, zblk = _moe_plan(rt, cnt)
    xs_rows = _dispatch(dest, zblk, h2t)
    o_rows = _experts(blk_e, blk_valid, xs_rows, w_exp_gate[l], w_exp_up[l], w_exp_down[l])
    y_prompt, y_sample = _combine(dest, o_rows, x2, rinfo, modc, g_final.reshape(1, -1))

    kv_p = kv.reshape(N_TOK // WINDOW, WINDOW, 2 * A_KV_WIDTH)[SEQ // WINDOW - 1:N_P // WINDOW:SEQ // WINDOW]
    win_k_p = kv_p[..., :A_KV_WIDTH].reshape(1, BATCH, WINDOW, A_KV_HEADS, A_HEAD_DIM)
    win_v_p = kv_p[..., A_KV_WIDTH:].reshape(1, BATCH, WINDOW, A_KV_HEADS, A_HEAD_DIM)
    kv_s = kv[N_P:].reshape(DEC_BATCH, DEC_SEQ, 2 * A_KV_WIDTH)
    win_k_s = jnp.concatenate([cache_win_k[l][:, DEC_SEQ:],
                               kv_s[..., :A_KV_WIDTH].reshape(DEC_BATCH, DEC_SEQ, A_KV_HEADS, A_HEAD_DIM)], axis=1)[None]
    win_v_s = jnp.concatenate([cache_win_v[l][:, DEC_SEQ:],
                               kv_s[..., A_KV_WIDTH:].reshape(DEC_BATCH, DEC_SEQ, A_KV_HEADS, A_HEAD_DIM)], axis=1)[None]
    qkm_c = qkm.reshape(N_CHUNKS, CHUNK, 2 * M_WIDTH)
    tail = slice(CHUNK - (CONV_WIDTH - 1), CHUNK)
    conv_p = qkm_c[SEQ // CHUNK - 1:N_P // CHUNK:SEQ // CHUNK, tail][None]
    conv_s = qkm_c[N_P // CHUNK:, tail][None]
    return (y_prompt, y_sample,
            win_k_p, win_v_p, conv_p, C_p[None], n_p[None], m_p[:, 0, :M_HEADS][None],
            win_k_s, win_v_s, conv_s, C_s[None], n_s[None], m_s[:, 0, :M_HEADS][None])
```

```python
import functools

import numpy as np
import jax
import jax.numpy as jnp
from jax import lax
from jax.experimental import pallas as pl
from jax.experimental.pallas import tpu as pltpu

F32 = jnp.float32
BF16 = jnp.bfloat16
HIGHEST = lax.Precision.HIGHEST

LANES = 128
SUBLANES = 8

D_MODEL = 1024
BATCH = 8
SEQ = 2048
DEC_BATCH = 32
DEC_SEQ = 64
PAST_LEN = 4096
CHUNK = 64
A_WIDTH = 512
A_HEAD_DIM = 64
A_HEADS = 8
A_KV_HEADS = 2
A_GROUP = 4
A_KV_WIDTH = 128
WINDOW = 128
ROT_DIM = 16
ROPE_THETA = 500000.0
M_WIDTH = 512
M_HEADS = 4
M_HEAD_DIM = 128
CONV_WIDTH = 4
N_GROUPS = 4
EXPERTS_PER_GROUP = 8
N_EXPERTS = 32
D_EXPERT = 512
EPS = 1e-6

N_P = BATCH * SEQ
N_S = DEC_BATCH * DEC_SEQ
N_TOK = N_P + N_S
N_CHUNKS = N_TOK // CHUNK
TM = 512
N_TILES = N_TOK // TM
P_TILES = N_P // TM
CH_PER_TILE = TM // CHUNK
ROW_SUB = D_MODEL // LANES
N_ASSIGN = 2 * N_TOK
MOE_R = 256
MOE_BLOCKS = N_ASSIGN // MOE_R + N_EXPERTS
MOE_ROWS = MOE_BLOCKS * MOE_R
C_Q, C_K, C_V, C_QKM, C_VM, C_OM, C_G = 0, 512, 640, 768, 1792, 2304, 2816
D_IN_PAD = 2944
VMEM_LIMIT = 48 * 1024 * 1024


def _sigmoid(x):
    return 1.0 / (1.0 + jnp.exp(-x))


def _mod_kernel(c_ref, w_ref, b_ref, o_ref):
    c = c_ref[...]
    s = c * _sigmoid(c)
    o_ref[...] = jnp.dot(s, w_ref[...], preferred_element_type=F32, precision=HIGHEST) + b_ref[...]


def _modulation(c_all, w_ada, b_ada):
    n = c_all.shape[0]
    bn = 512
    return pl.pallas_call(
        _mod_kernel,
        grid=(6 * D_MODEL // bn,),
        in_specs=[pl.BlockSpec((n, D_MODEL), lambda j: (0, 0)),
                  pl.BlockSpec((D_MODEL, bn), lambda j: (0, j)),
                  pl.BlockSpec((1, bn), lambda j: (0, j))],
        out_specs=pl.BlockSpec((n, bn), lambda j: (0, j)),
        out_shape=jax.ShapeDtypeStruct((n, 6 * D_MODEL), F32),
        name="modulation",
    )(c_all, w_ada, b_ada.reshape(1, -1))


def _xp_spec():
    def idx(i):
        t = jnp.minimum(i, P_TILES - 1)
        return (t // (SEQ // TM), t % (SEQ // TM), 0)
    return pl.BlockSpec((1, TM, D_MODEL), idx)


def _xs_spec():
    return pl.BlockSpec((CH_PER_TILE, DEC_SEQ, D_MODEL), lambda i: (jnp.maximum(i - P_TILES, 0), 0, 0))


def _mod_spec(comp):
    return pl.BlockSpec((CH_PER_TILE, 1, D_MODEL), lambda i: (i, 0, comp))


def _load_x(xp_ref, xs_ref, xbuf):
    i = pl.program_id(0)

    @pl.when(i < P_TILES)
    def _():
        xbuf[...] = xp_ref[0]

    @pl.when(i >= P_TILES)
    def _():
        xbuf[...] = xs_ref[...].reshape(TM, D_MODEL)

    return xbuf[...]


def _per_chunk(m_ref):
    m = m_ref[...]
    return jnp.broadcast_to(m, (CH_PER_TILE, CHUNK, D_MODEL)).reshape(TM, D_MODEL)


def _rmsnorm(x, g):
    return x * lax.rsqrt(jnp.mean(x * x, axis=-1, keepdims=True) + EPS) * g


def _rope(x, cos, sa, sb):
    n = x.shape[1]
    rep = n // LANES
    if rep > 1:
        cos = jnp.concatenate([cos] * rep, axis=1)
        sa = jnp.concatenate([sa] * rep, axis=1)
        sb = jnp.concatenate([sb] * rep, axis=1)
    return x * cos + pltpu.roll(x, n - ROT_DIM // 2, 1) * sa + pltpu.roll(x, ROT_DIM // 2, 1) * sb


def _inproj_kernel(xp_ref, xs_ref, sh_ref, sc_ref, g_ref, w_ref, cos_ref, sa_ref, sb_ref,
                   q_ref, kv_ref, qkm_ref, vm_ref, om_ref, gt_ref, xbuf):
    x = _load_x(xp_ref, xs_ref, xbuf)
    h = _rmsnorm(x, g_ref[...]) * (1.0 + _per_chunk(sc_ref)) + _per_chunk(sh_ref)
    hb = h.astype(BF16)

    def proj(a, b):
        return jnp.dot(hb, w_ref[:, a:b], preferred_element_type=F32)

    cos, sa, sb = cos_ref[...], sa_ref[...], sb_ref[...]
    q_ref[...] = _rope(proj(C_Q, C_K), cos, sa, sb).astype(BF16)
    kv_ref[:, :A_KV_WIDTH] = _rope(proj(C_K, C_V), cos, sa, sb)
    kv_ref[:, A_KV_WIDTH:] = proj(C_V, C_QKM)
    qkm_ref[...] = proj(C_QKM, C_VM)
    vm_ref[...] = proj(C_VM, C_OM).astype(BF16)
    om_ref[...] = proj(C_OM, C_G)
    gt_ref[...] = proj(C_G, D_IN_PAD)


def _rope_tables():
    pos = np.concatenate([np.arange(SEQ), np.tile(PAST_LEN + np.arange(DEC_SEQ), CH_PER_TILE)]).astype(np.float64)
    inv_freq = ROPE_THETA ** (-np.arange(0, ROT_DIM, 2, dtype=np.float64) / ROT_DIM)
    lane = np.arange(LANES)
    hl = lane % A_HEAD_DIM
    ang = pos[:, None] * inv_freq[hl % (ROT_DIM // 2)][None, :]
    rot = (hl < ROT_DIM)[None, :]
    lo = (hl < ROT_DIM // 2)[None, :]
    cos = np.where(rot, np.cos(ang), 1.0)
    sa = np.where(lo, -np.sin(ang), 0.0)
    sb = np.where(rot & ~lo, np.sin(ang), 0.0)
    return [jnp.asarray(t, F32) for t in (cos, sa, sb)]


def _inproj(x_prompt, x_sample, modc, g_mix, w_in_r):
    cos, sa, sb = _rope_tables()
    tab_spec = pl.BlockSpec((TM, LANES), lambda i: (jnp.where(i < P_TILES, i % (SEQ // TM), SEQ // TM), 0))

    def out(n, dtype=F32):
        return pl.BlockSpec((TM, n), lambda i: (i, 0)), jax.ShapeDtypeStruct((N_TOK, n), dtype)

    outs = [out(A_WIDTH, BF16), out(2 * A_KV_WIDTH), out(2 * M_WIDTH), out(M_WIDTH, BF16), out(M_WIDTH), out(LANES)]
    return pl.pallas_call(
        _inproj_kernel,
        grid=(N_TILES,),
        in_specs=[_xp_spec(), _xs_spec(), _mod_spec(0), _mod_spec(1),
                  pl.BlockSpec((1, D_MODEL), lambda i: (0, 0)),
                  pl.BlockSpec((D_MODEL, D_IN_PAD), lambda i: (0, 0)),
                  tab_spec, tab_spec, tab_spec],
        out_specs=[o[0] for o in outs],
        out_shape=[o[1] for o in outs],
        scratch_shapes=[pltpu.VMEM((TM, D_MODEL), F32)],
        compiler_params=pltpu.CompilerParams(vmem_limit_bytes=VMEM_LIMIT),
        name="inproj",
    )(x_prompt, x_sample, modc, modc, g_mix, w_in_r, cos, sa, sb)


def _attn_chunk_kernel(sink_ref, q_ref, prev_ref, cur_ref, o_ref):
    q = q_ref[...] * (A_HEAD_DIM ** -0.5)
    kv = jnp.concatenate([prev_ref[...], cur_ref[...]], axis=0)
    for g in range(A_KV_HEADS):
        kg = kv[:, g * A_HEAD_DIM:(g + 1) * A_HEAD_DIM].astype(BF16)
        vg = kv[:, A_KV_WIDTH + g * A_HEAD_DIM:A_KV_WIDTH + (g + 1) * A_HEAD_DIM].astype(BF16)
        heads = [g * A_GROUP + i for i in range(A_GROUP)]
        qc = jnp.concatenate([q[:, h * A_HEAD_DIM:(h + 1) * A_HEAD_DIM] for h in heads], axis=0).astype(BF16)
        snk = jnp.concatenate([jnp.full((CHUNK, 1), sink_ref[h], F32) for h in heads], axis=0)
        s = lax.dot_general(qc, kg, (((1,), (1,)), ((), ())), preferred_element_type=F32)
        mx = jnp.maximum(jnp.max(s, axis=-1, keepdims=True), snk)
        p = jnp.exp(s - mx)
        den = jnp.sum(p, axis=-1, keepdims=True) + jnp.exp(snk - mx)
        o = jnp.dot(p.astype(BF16), vg, preferred_element_type=F32) / den
        for i, h in enumerate(heads):
            o_ref[:, h * A_HEAD_DIM:(h + 1) * A_HEAD_DIM] = o[i * CHUNK:(i + 1) * CHUNK]


ATT_TQ = 256
ATT_QB = 2 * CHUNK


def _attn_band_kernel(sink_ref, q_ref, prev_ref, cur_ref, o_ref, att_t):
    nk = ATT_QB + WINDOW
    q = (q_ref[...] * (A_HEAD_DIM ** -0.5)).astype(BF16)
    kv = jnp.concatenate([prev_ref[...], cur_ref[...]], axis=0)
    k2 = kv[:, :A_KV_WIDTH]
    k2r = pltpu.roll(k2, A_HEAD_DIM, 1)
    low = lax.broadcasted_iota(jnp.int32, k2.shape, 1) < A_HEAD_DIM
    k_placed = {(0, 0): jnp.where(low, k2, 0.0), (0, 1): jnp.where(low, 0.0, k2r),
                (1, 0): jnp.where(low, k2r, 0.0), (1, 1): jnp.where(low, 0.0, k2)}
    k_placed = {key: val.astype(BF16) for key, val in k_placed.items()}
    v_t = kv[:, A_KV_WIDTH:].T.astype(BF16)
    key_chunk = lax.broadcasted_iota(jnp.int32, (nk, ATT_QB), 0) // CHUNK
    q_chunk = lax.broadcasted_iota(jnp.int32, (nk, ATT_QB), 1) // CHUNK
    band = (key_chunk >= q_chunk) & (key_chunk <= q_chunk + WINDOW // CHUNK)
    has_history = pl.program_id(1) > 0
    for blk in range(ATT_TQ // ATT_QB):
        keys = slice(blk * ATT_QB, blk * ATT_QB + nk)
        qrows = slice(blk * ATT_QB, (blk + 1) * ATT_QB)
        valid = band & ((key_chunk >= WINDOW // CHUNK) | has_history) if blk == 0 else band
        for h in range(A_HEADS):
            g = h // A_GROUP
            s_t = lax.dot_general(k_placed[(g, h % 2)][keys], q[qrows, (h // 2) * LANES:(h // 2 + 1) * LANES],
                                  (((1,), (1,)), ((), ())), preferred_element_type=F32)
            s_t = jnp.where(valid, s_t, -jnp.inf)
            snk = sink_ref[h]
            mx = jnp.maximum(jnp.max(s_t, axis=0, keepdims=True), snk)
            p_t = jnp.exp(s_t - mx)
            den = jnp.sum(p_t, axis=0, keepdims=True) + jnp.exp(snk - mx)
            o_t = jnp.dot(v_t[:, keys], p_t.astype(BF16), preferred_element_type=F32)
            att_t[h * A_HEAD_DIM:(h + 1) * A_HEAD_DIM, qrows] = o_t[g * A_HEAD_DIM:(g + 1) * A_HEAD_DIM] / den
    o_ref[...] = att_t[...].T.astype(BF16)


def _attention(sinks, q, kv, cache_kv):
    smem = pl.BlockSpec(memory_space=pltpu.SMEM)
    tq = ATT_TQ
    nq = SEQ // tq
    att = pl.pallas_call(
        _attn_band_kernel,
        grid=(BATCH, nq),
        scratch_shapes=[pltpu.VMEM((A_WIDTH, ATT_TQ), F32)],
        in_specs=[smem,
                  pl.BlockSpec((tq, A_WIDTH), lambda b, j: (b * nq + j, 0)),
                  pl.BlockSpec((WINDOW, 2 * A_KV_WIDTH),
                               lambda b, j: (jnp.maximum((b * nq + j) * (tq // WINDOW) - 1, 0), 0)),
                  pl.BlockSpec((tq, 2 * A_KV_WIDTH), lambda b, j: (b * nq + j, 0))],
        out_specs=pl.BlockSpec((tq, A_WIDTH), lambda b, j: (b * nq + j, 0)),
        out_shape=jax.ShapeDtypeStruct((N_P, A_WIDTH), BF16),
        name="attn_prompt",
    )(sinks, q, kv, kv)
    off = N_P // DEC_SEQ
    att_s = pl.pallas_call(
        _attn_chunk_kernel,
        grid=(DEC_BATCH,),
        in_specs=[smem,
                  pl.BlockSpec((DEC_SEQ, A_WIDTH), lambda b: (off + b, 0)),
                  pl.BlockSpec((WINDOW, 2 * A_KV_WIDTH), lambda b: (b, 0)),
                  pl.BlockSpec((DEC_SEQ, 2 * A_KV_WIDTH), lambda b: (off + b, 0))],
        out_specs=pl.BlockSpec((DEC_SEQ, A_WIDTH), lambda b: (b, 0)),
        out_shape=jax.ShapeDtypeStruct((N_S, A_WIDTH), F32),
        name="attn_sample",
    )(sinks, q, cache_kv, kv)
    return att, att_s


def _conv4(x, w, b):
    y = b + x * w[CONV_WIDTH - 1:CONV_WIDTH]
    for j in range(1, CONV_WIDTH):
        y = y + pltpu.roll(x, j, 0) * w[CONV_WIDTH - 1 - j:CONV_WIDTH - j]
    return y


def _mlstm_kernel(qkm_ref, vm_ref, om_ref, gt_ref, cw_ref, cb_ref, gb_ref, gmh_ref,
                  carry0_ref, c0_ref, n0_ref, m0_ref,
                  hm_ref, cout_ref, nout_ref, mout_ref, c_s, n_s, m_s, carry_s, *, L, NS):
    c = pl.program_id(1)

    @pl.when(c == 0)
    def _():
        c_s[...] = c0_ref[...]
        n_s[...] = n0_ref[...]
        m_s[...] = m0_ref[...]
        carry_s[...] = carry0_ref[...]

    w = cw_ref[...]
    b = cb_ref[...]
    gmh = gmh_ref[...]
    lane = lax.broadcasted_iota(jnp.int32, (L, LANES), 1)
    causal = lax.broadcasted_iota(jnp.int32, (L, L), 0) >= lax.broadcasted_iota(jnp.int32, (L, L), 1)
    ones_causal = causal.astype(F32)
    sel = (lax.broadcasted_iota(jnp.int32, (SUBLANES, LANES), 0)
           == lax.broadcasted_iota(jnp.int32, (SUBLANES, LANES), 1)).astype(F32)
    lane1 = lax.broadcasted_iota(jnp.int32, (1, LANES), 1)
    for st in range(NS):
        _mlstm_stream(st, qkm_ref, vm_ref, om_ref, gt_ref, gb_ref, hm_ref, c_s, n_s, m_s, carry_s,
                      w, b, gmh, lane, causal, ones_causal, sel, lane1, L)

    @pl.when(c == pl.num_programs(1) - 1)
    def _():
        cout_ref[...] = c_s[...]
        nout_ref[...] = n_s[...]
        mout_ref[...] = m_s[...]


def _mlstm_stream(st, qkm_ref, vm_ref, om_ref, gt_ref, gb_ref, hm_ref, c_s, n_s, m_s, carry_s,
                  w, b, gmh, lane, causal, ones_causal, sel, lane1, L):
    x = qkm_ref[st]
    y = _conv4(x, w, b)
    y8 = _conv4(jnp.concatenate([carry_s[st], x[:SUBLANES]], axis=0), w, b)
    y = jnp.concatenate([y8[SUBLANES:], y[SUBLANES:]], axis=0)
    carry_s[st] = x[L - SUBLANES:]
    a = y * _sigmoid(y)
    qa = a[:, :M_WIDTH] * (M_HEAD_DIM ** -0.5)
    ka = a[:, M_WIDTH:]
    v = vm_ref[st]
    om = om_ref[st]

    z = gt_ref[st] + gb_ref[...]
    f_log = jnp.minimum(z, 0.0) - jnp.log1p(jnp.exp(-jnp.abs(z)))
    val = jnp.where(lane < M_HEADS, z, f_log)
    cum = jnp.dot(ones_causal, val, preferred_element_type=F32, precision=HIGHEST)
    G = jnp.where(lane < M_HEADS, val, cum)
    GT = lax.dot_general(sel, G, (((1,), (1,)), ((), ())), preferred_element_type=F32, precision=HIGHEST)

    m_row = m_s[st]
    outs = []
    for h in range(M_HEADS):
        a_col = G[:, M_HEADS + h:M_HEADS + h + 1]
        i_col = G[:, h:h + 1]
        i_row = GT[h:h + 1, :]
        b_row = GT[M_HEADS + h:M_HEADS + h + 1, :]
        m_prev = m_row[:, h:h + 1]
        dm = jnp.where(causal, a_col - b_row + i_row, -jnp.inf)
        inter = a_col + m_prev
        m_t = jnp.maximum(inter, jnp.max(dm, axis=-1, keepdims=True))
        w_intra = jnp.exp(dm - m_t)
        w_inter = jnp.exp(inter - m_t)
        sl = slice(h * M_HEAD_DIM, (h + 1) * M_HEAD_DIM)
        qh, kh, vh = qa[:, sl], ka[:, sl], v[:, sl]
        qb = qh.astype(BF16)
        scores = lax.dot_general(qb, kh.astype(BF16), (((1,), (1,)), ((), ())), preferred_element_type=F32) * w_intra
        ch = c_s[st, h]
        nh = n_s[st, h:h + 1, :]
        num = (jnp.dot(scores.astype(BF16), vh.astype(BF16), preferred_element_type=F32)
               + w_inter * lax.dot_general(qb, ch.astype(BF16), (((1,), (1,)), ((), ())), preferred_element_type=F32))
        den = jnp.sum(scores, axis=-1, keepdims=True) + w_inter * jnp.sum(qh * nh, axis=-1, keepdims=True)
        hv = num / jnp.maximum(jnp.abs(den), jnp.exp(-m_t))
        hn = hv * lax.rsqrt(jnp.mean(hv * hv, axis=-1, keepdims=True) + EPS)
        outs.append(hn * gmh[:, sl] * _sigmoid(om[:, sl]))
        m_new = m_t[L - 1:L, :]
        a_last = a_col[L - 1:L, :]
        w_end = jnp.exp(a_last - a_col + i_col - m_new)
        decay = jnp.exp(a_last + m_prev - m_new)
        kw = kh * w_end
        c_s[st, h] = decay * ch + lax.dot_general(vh.astype(BF16), kw.astype(BF16), (((0,), (0,)), ((), ())),
                                                  preferred_element_type=F32)
        n_s[st, h:h + 1, :] = decay * nh + jnp.sum(kw, axis=0, keepdims=True)
        m_row = jnp.where(lane1 == h, m_new, m_row)
    m_s[st] = m_row
    hm_ref[st] = jnp.concatenate(outs, axis=1).astype(BF16)


def _mlstm(qkm, vm, om, gates, conv_w, conv_b, gbias, g_mh, carry0, c0, n0, m0, *, nb, L, nc, ns, row_off, name):
    S = nc * L
    first = row_off // S // ns

    def tok(arr):
        n = arr.shape[-1]
        return arr.reshape(N_TOK // S, S, n), pl.BlockSpec((ns, L, n), lambda i, c: (first + i, c, 0))

    def const(shape):
        return pl.BlockSpec(shape, lambda i, c: (0,) * len(shape))

    def per_stream(shape):
        return pl.BlockSpec((ns,) + shape, lambda i, c: (i,) + (0,) * len(shape))

    toks = [tok(a) for a in (qkm, vm, om, gates)]
    state_specs = [per_stream((SUBLANES, 2 * M_WIDTH)), per_stream((M_HEADS, M_HEAD_DIM, M_HEAD_DIM)),
                   per_stream((M_HEADS, M_HEAD_DIM)), per_stream((1, LANES))]
    in_specs = ([t[1] for t in toks]
                + [const((CONV_WIDTH, 2 * M_WIDTH)), const((1, 2 * M_WIDTH)), const((1, LANES)), const((1, M_WIDTH))]
                + state_specs)
    args = [t[0] for t in toks] + [conv_w, conv_b, gbias, g_mh, carry0, c0, n0, m0]
    hm, c_out, n_out, m_out = pl.pallas_call(
        functools.partial(_mlstm_kernel, L=L, NS=ns),
        grid=(nb // ns, nc),
        in_specs=in_specs,
        out_specs=[pl.BlockSpec((ns, L, M_WIDTH), lambda i, c: (i, c, 0))] + state_specs[1:],
        out_shape=[jax.ShapeDtypeStruct((nb, S, M_WIDTH), BF16),
                   jax.ShapeDtypeStruct((nb, M_HEADS, M_HEAD_DIM, M_HEAD_DIM), F32),
                   jax.ShapeDtypeStruct((nb, M_HEADS, M_HEAD_DIM), F32),
                   jax.ShapeDtypeStruct((nb, 1, LANES), F32)],
        scratch_shapes=[pltpu.VMEM((ns, M_HEADS, M_HEAD_DIM, M_HEAD_DIM), F32),
                        pltpu.VMEM((ns, M_HEADS, M_HEAD_DIM), F32),
                        pltpu.VMEM((ns, 1, LANES), F32),
                        pltpu.VMEM((ns, SUBLANES, 2 * M_WIDTH), F32)],
        compiler_params=pltpu.CompilerParams(dimension_semantics=("arbitrary", "arbitrary"),
                                             vmem_limit_bytes=VMEM_LIMIT),
        name=name,
    )(*args)
    return hm.reshape(nb * S, M_WIDTH), c_out, n_out, m_out


def _outproj_kernel(xp_ref, xs_ref, attp_ref, atts_ref, hmp_ref, hms_ref, gate_ref, sh_ref, sc_ref, g_ref,
                    w_ref, wr_ref, br_ref, x2_ref, h2_ref, ri_ref, rt_ref, cnt_ref, xbuf, mixbuf):
    x = _load_x(xp_ref, xs_ref, xbuf)
    i = pl.program_id(0)

    @pl.when(i < P_TILES)
    def _():
        mixbuf[:, :A_WIDTH] = attp_ref[...].astype(BF16)
        mixbuf[:, A_WIDTH:] = hmp_ref[...].astype(BF16)

    @pl.when(i >= P_TILES)
    def _():
        mixbuf[:, :A_WIDTH] = atts_ref[...].astype(BF16)
        mixbuf[:, A_WIDTH:] = hms_ref[...].astype(BF16)

    mixed = jnp.dot(mixbuf[...], w_ref[...], preferred_element_type=F32)
    x2 = x + _per_chunk(gate_ref) * mixed
    x2_ref[...] = x2
    h2 = _rmsnorm(x2, g_ref[...]) * (1.0 + _per_chunk(sc_ref)) + _per_chunk(sh_ref)
    for s in range(ROW_SUB):
        h2_ref[pl.ds(s, TM, stride=ROW_SUB), :] = h2[:, s * LANES:(s + 1) * LANES]

    h_hi = h2.astype(BF16)
    h_lo = (h2 - h_hi.astype(F32)).astype(BF16)
    lg = (jnp.dot(h_hi, wr_ref[0], preferred_element_type=F32)
          + jnp.dot(h_lo, wr_ref[0], preferred_element_type=F32)
          + jnp.dot(h_hi, wr_ref[1], preferred_element_type=F32)) + br_ref[...]
    lane = lax.broadcasted_iota(jnp.int32, (TM, LANES), 1)
    lanef = lane.astype(F32)
    ninf = -jnp.inf

    def first_argmax(vals):
        mx = jnp.max(vals, axis=-1, keepdims=True)
        return mx, jnp.min(jnp.where(vals == mx, lanef, float(LANES)), axis=-1, keepdims=True)

    is_grp = lane < N_GROUPS
    gmax, grp = first_argmax(jnp.where(is_grp, lg, ninf))
    p_grp = 1.0 / jnp.sum(jnp.where(is_grp, jnp.exp(lg - gmax), 0.0), axis=-1, keepdims=True)
    base = N_GROUPS + grp * EXPERTS_PER_GROUP
    in_grp = (lanef >= base) & (lanef < base + EXPERTS_PER_GROUP)
    el = jnp.where(in_grp, lg, ninf)
    v1, i1 = first_argmax(el)
    v2, i2 = first_argmax(jnp.where(lanef == i1, ninf, el))
    e = jnp.exp(v2 - v1)
    g1 = p_grp / (1.0 + e)
    g2 = p_grp * e / (1.0 + e)
    e1 = i1 - N_GROUPS
    e2 = i2 - N_GROUPS
    oh1 = lanef == e1
    oh2 = lanef == e2
    oh = jnp.where(oh1 | oh2, 1.0, 0.0)
    earlier = (lax.broadcasted_iota(jnp.int32, (TM, TM), 0) > lax.broadcasted_iota(jnp.int32, (TM, TM), 1))
    before = jnp.dot(earlier.astype(BF16), oh.astype(BF16), preferred_element_type=F32)
    r1 = jnp.sum(jnp.where(oh1, before, 0.0), axis=-1, keepdims=True)
    r2 = jnp.sum(jnp.where(oh2, before, 0.0), axis=-1, keepdims=True)
    cnt_ref[...] = jnp.broadcast_to(jnp.sum(oh, axis=0, keepdims=True), (SUBLANES, LANES))
    cols = (e1, e2, g1, g2, r1, r2)
    ri = jnp.zeros((TM, LANES), F32)
    for k, cval in enumerate(cols):
        ri = jnp.where(lane == k, cval, ri)
    ri_ref[...] = ri
    rt_ref[...] = ri.T[:SUBLANES]


def _outproj(x_prompt, x_sample, att_p, att_s, hm_p, hm_s, modc, g_ffn, w_out_b, w_router, b_router):
    def tok(n):
        return pl.BlockSpec((TM, n), lambda i: (i, 0))

    def tok_p(n):
        return pl.BlockSpec((TM, n), lambda i: (jnp.minimum(i, P_TILES - 1), 0))

    def tok_s(n):
        return pl.BlockSpec((TM, n), lambda i: (jnp.maximum(i - P_TILES, 0), 0))

    def const(shape):
        return pl.BlockSpec(shape, lambda i: (0,) * len(shape))

    return pl.pallas_call(
        _outproj_kernel,
        grid=(N_TILES,),
        in_specs=[_xp_spec(), _xs_spec(), tok_p(A_WIDTH), tok_s(A_WIDTH), tok_p(M_WIDTH), tok_s(M_WIDTH),
                  _mod_spec(2), _mod_spec(3), _mod_spec(4),
                  const((1, D_MODEL)), const((D_MODEL, D_MODEL)), const((2, D_MODEL, LANES)), const((1, LANES))],
        out_specs=[tok(D_MODEL), pl.BlockSpec((TM * ROW_SUB, LANES), lambda i: (i, 0)), tok(LANES),
                   pl.BlockSpec((SUBLANES, TM), lambda i: (0, i)),
                   pl.BlockSpec((SUBLANES, LANES), lambda i: (i, 0))],
        out_shape=[jax.ShapeDtypeStruct((N_TOK, D_MODEL), F32),
                   jax.ShapeDtypeStruct((N_TOK * ROW_SUB, LANES), F32),
                   jax.ShapeDtypeStruct((N_TOK, LANES), F32),
                   jax.ShapeDtypeStruct((SUBLANES, N_TOK), F32),
                   jax.ShapeDtypeStruct((N_TILES * SUBLANES, LANES), F32)],
        scratch_shapes=[pltpu.VMEM((TM, D_MODEL), F32), pltpu.VMEM((TM, D_MODEL), BF16)],
        compiler_params=pltpu.CompilerParams(vmem_limit_bytes=VMEM_LIMIT),
        name="outproj",
    )(x_prompt, x_sample, att_p, att_s, hm_p, hm_s, modc, modc, modc, g_ffn, w_out_b, w_router, b_router)


def _row(ref, r):
    return ref.at[pl.ds(pl.multiple_of(r * ROW_SUB, ROW_SUB), ROW_SUB), :]


def _row_copies(i, t, dest_ref, make):
    tok = i * TM + t
    return [make(k, dest_ref[k * N_TOK + tok]) for k in range(2)]


def _dispatch_kernel(dest_ref, zblk_ref, h_ref, xs_out, zbuf, sem, zsem):
    i = pl.program_id(0)
    blk_rows = MOE_R * ROW_SUB

    @pl.when(i == 0)
    def _():
        zbuf[...] = jnp.zeros_like(zbuf)

        def zero_copy(e):
            start = pl.multiple_of(jnp.maximum(zblk_ref[e], 0) * blk_rows, blk_rows)
            return pltpu.make_async_copy(zbuf, xs_out.at[pl.ds(start, blk_rows), :], zsem)

        def z_start(e, carry):
            @pl.when(zblk_ref[e] >= 0)
            def _():
                zero_copy(e).start()
            return carry

        def z_wait(e, carry):
            @pl.when(zblk_ref[e] >= 0)
            def _():
                zero_copy(e).wait()
            return carry

        lax.fori_loop(0, N_EXPERTS, z_start, 0)
        lax.fori_loop(0, N_EXPERTS, z_wait, 0)

        def tail_copy(b):
            return pltpu.make_async_copy(zbuf, xs_out.at[pl.ds(pl.multiple_of(b * blk_rows, blk_rows), blk_rows), :], zsem)

        def t_start(b, carry):
            tail_copy(b).start()
            return carry

        def t_wait(b, carry):
            tail_copy(b).wait()
            return carry

        lax.fori_loop(zblk_ref[N_EXPERTS], MOE_BLOCKS, t_start, 0)
        lax.fori_loop(zblk_ref[N_EXPERTS], MOE_BLOCKS, t_wait, 0)

    def copies(t):
        return _row_copies(i, t, dest_ref, lambda k, d: pltpu.make_async_copy(_row(h_ref, t), _row(xs_out, d), sem))

    def issue(t, carry):
        for k, cp in enumerate(copies(t)):
            cp.start(priority=k)
        return carry

    def wait(t, carry):
        for cp in copies(t):
            cp.wait()
        return carry

    lax.fori_loop(0, TM, issue, 0, unroll=8)
    lax.fori_loop(0, TM, wait, 0, unroll=8)


def _dispatch(dest, zblk, h2t):
    return pl.pallas_call(
        _dispatch_kernel,
        grid_spec=pltpu.PrefetchScalarGridSpec(
            num_scalar_prefetch=2,
            grid=(N_TILES,),
            in_specs=[pl.BlockSpec((TM * ROW_SUB, LANES), lambda i, d, z: (i, 0))],
            out_specs=pl.BlockSpec(memory_space=pl.ANY),
            scratch_shapes=[pltpu.VMEM((MOE_R * ROW_SUB, LANES), F32),
                            pltpu.SemaphoreType.DMA, pltpu.SemaphoreType.DMA]),
        out_shape=jax.ShapeDtypeStruct((MOE_ROWS * ROW_SUB, LANES), F32),
        compiler_params=pltpu.CompilerParams(dimension_semantics=("arbitrary",)),
        name="moe_dispatch",
    )(dest, zblk, h2t)


def _expert_kernel(be_ref, bv_ref, first_ref, slot_ref, next_ref, xs_ref, wg_hbm, wu_hbm, wd_hbm, o_ref,
                   wg_f, wu_f, wd_f, wg_s, wu_s, wd_s, wsem):
    b = pl.program_id(0)

    def weight_copies(e, slot):
        return [pltpu.make_async_copy(src.at[e], dst.at[slot], wsem.at[slot])
                for src, dst in ((wg_hbm, wg_f), (wu_hbm, wu_f), (wd_hbm, wd_f))]

    @pl.when(b == 0)
    def _():
        for cp in weight_copies(be_ref[0], 0):
            cp.start()

    @pl.when(first_ref[b] == 1)
    def _():
        slot = slot_ref[b]
        for cp in weight_copies(be_ref[b], slot):
            cp.wait()
        wg_s[...] = wg_f[slot].astype(BF16)
        wu_s[...] = wu_f[slot].astype(BF16)
        wd_s[...] = wd_f[slot].astype(BF16)

        @pl.when(next_ref[b] >= 0)
        def _():
            for cp in weight_copies(next_ref[b], 1 - slot):
                cp.start()

    @pl.when(bv_ref[b] > 0)
    def _():
        x = jnp.concatenate([xs_ref[pl.ds(s, MOE_R, stride=ROW_SUB), :] for s in range(ROW_SUB)], axis=1)
        xb = x.astype(BF16)
        g = jnp.dot(xb, wg_s[...], preferred_element_type=F32)
        u = jnp.dot(xb, wu_s[...], preferred_element_type=F32)
        a = (g * _sigmoid(g) * u).astype(BF16)
        o = jnp.dot(a, wd_s[...], preferred_element_type=F32)
        for s in range(ROW_SUB):
            o_ref[pl.ds(s, MOE_R, stride=ROW_SUB), :] = o[:, s * LANES:(s + 1) * LANES]

    @pl.when(bv_ref[b] == 0)
    def _():
        o_ref[...] = jnp.zeros_like(o_ref)


def _experts(blk_e, blk_valid, xs, w_g, w_u, w_d):
    idx = np.arange(MOE_BLOCKS)
    used = blk_valid > 0
    first = used & ((idx == 0) | (blk_e != jnp.roll(blk_e, 1)))
    ordinal = jnp.sum(first[None, :] & (idx[None, :] <= idx[:, None]), axis=1) - 1
    later_first = jnp.where(first[None, :] & (idx[None, :] > idx[:, None]), idx[None, :], MOE_BLOCKS)
    nxt = jnp.min(later_first, axis=1)
    next_e = jnp.where(nxt < MOE_BLOCKS, jnp.sum(jnp.where(idx[None, :] == nxt[:, None], blk_e[None, :], 0), axis=1), -1)
    plan = [blk_e, blk_valid, first.astype(jnp.int32), (ordinal % 2).astype(jnp.int32), next_e.astype(jnp.int32)]

    def rows(b, be, bv, *_):
        return (jnp.where(bv[b] > 0, b, MOE_BLOCKS - 1), 0)

    hbm = pl.BlockSpec(memory_space=pl.ANY)
    return pl.pallas_call(
        _expert_kernel,
        grid_spec=pltpu.PrefetchScalarGridSpec(
            num_scalar_prefetch=len(plan),
            grid=(MOE_BLOCKS,),
            in_specs=[pl.BlockSpec((MOE_R * ROW_SUB, LANES), rows), hbm, hbm, hbm],
            out_specs=pl.BlockSpec((MOE_R * ROW_SUB, LANES), lambda b, *_: (b, 0)),
            scratch_shapes=[pltpu.VMEM((2, D_MODEL, D_EXPERT), F32),
                            pltpu.VMEM((2, D_MODEL, D_EXPERT), F32),
                            pltpu.VMEM((2, D_EXPERT, D_MODEL), F32),
                            pltpu.VMEM((D_MODEL, D_EXPERT), BF16),
                            pltpu.VMEM((D_MODEL, D_EXPERT), BF16),
                            pltpu.VMEM((D_EXPERT, D_MODEL), BF16),
                            pltpu.SemaphoreType.DMA((2,))]),
        out_shape=jax.ShapeDtypeStruct((MOE_ROWS * ROW_SUB, LANES), F32),
        compiler_params=pltpu.CompilerParams(dimension_semantics=("arbitrary",), vmem_limit_bytes=VMEM_LIMIT),
        name="moe_experts",
    )(*plan, xs, w_g, w_u, w_d)


def _combine_kernel(dest_ref, o_hbm, x2_ref, ri_ref, gate_ref, gf_ref, yp_ref, ys_ref, obuf, sem):
    i = pl.program_id(0)
    slot = i % 2
    tile_rows = TM * ROW_SUB

    def issue_tile(tile, slot_):
        def issue(t, carry):
            cps = _row_copies(tile, t, dest_ref, lambda k, d: pltpu.make_async_copy(
                _row(o_hbm, d), _row(obuf, (2 * slot_ + k) * TM + t), sem.at[slot_]))
            for k, cp in enumerate(cps):
                cp.start(priority=k)
            return carry

        lax.fori_loop(0, TM, issue, 0, unroll=8)

    @pl.when(i == 0)
    def _():
        issue_tile(0, 0)

    @pl.when(i + 1 < N_TILES)
    def _():
        issue_tile(i + 1, 1 - slot)

    for k in range(2):
        start = pl.multiple_of((2 * slot + k) * tile_rows, tile_rows)
        pltpu.make_async_copy(o_hbm.at[pl.ds(0, tile_rows), :], obuf.at[pl.ds(start, tile_rows), :],
                              sem.at[slot]).wait()

    ri = ri_ref[...]
    g1 = ri[:, 2:3]
    g2 = ri[:, 3:4]
    base = 2 * slot * tile_rows
    moe = jnp.concatenate(
        [g1 * obuf[pl.ds(base + s, TM, stride=ROW_SUB), :] + g2 * obuf[pl.ds(base + tile_rows + s, TM, stride=ROW_SUB), :]
         for s in range(ROW_SUB)], axis=1)
    x3 = x2_ref[...] + _per_chunk(gate_ref) * moe
    y = _rmsnorm(x3, gf_ref[...])

    @pl.when(i < P_TILES)
    def _():
        yp_ref[0] = y

    @pl.when(i >= P_TILES)
    def _():
        ys_ref[...] = y.reshape(CH_PER_TILE, DEC_SEQ, D_MODEL)


def _combine(dest, o_rows, x2, rinfo, modc, g_final):
    xp_spec, xs_spec = _xp_spec(), _xs_spec()
    return pl.pallas_call(
        _combine_kernel,
        grid_spec=pltpu.PrefetchScalarGridSpec(
            num_scalar_prefetch=1,
            grid=(N_TILES,),
            in_specs=[pl.BlockSpec(memory_space=pl.ANY),
                      pl.BlockSpec((TM, D_MODEL), lambda i, d: (i, 0)),
                      pl.BlockSpec((TM, LANES), lambda i, d: (i, 0)),
                      pl.BlockSpec((CH_PER_TILE, 1, D_MODEL), lambda i, d: (i, 0, 5)),
                      pl.BlockSpec((1, D_MODEL), lambda i, d: (0, 0))],
            out_specs=[pl.BlockSpec(xp_spec.block_shape, lambda i, d: xp_spec.index_map(i)),
                       pl.BlockSpec(xs_spec.block_shape, lambda i, d: xs_spec.index_map(i))],
            scratch_shapes=[pltpu.VMEM((4 * TM * ROW_SUB, LANES), F32), pltpu.SemaphoreType.DMA((2,))]),
        out_shape=[jax.ShapeDtypeStruct((BATCH, SEQ, D_MODEL), F32),
                   jax.ShapeDtypeStruct((DEC_BATCH, DEC_SEQ, D_MODEL), F32)],
        compiler_params=pltpu.CompilerParams(dimension_semantics=("arbitrary",), vmem_limit_bytes=VMEM_LIMIT),
        name="moe_combine",
    )(dest, o_rows, x2, rinfo, modc, g_final)


def _moe_plan(rt, cnt):
    experts = np.arange(N_EXPERTS)
    tile_cnt = cnt.reshape(N_TILES, SUBLANES, LANES)[:, 0, :N_EXPERTS]
    counts = jnp.sum(tile_cnt, axis=0)
    padded = jnp.ceil(counts / MOE_R) * MOE_R
    pad_end = jnp.sum(padded[:, None] * (experts[:, None] <= experts[None, :]), axis=0)
    base = pad_end - padded
    tiles = np.arange(N_TILES)
    earlier_tiles = (tiles[None, :, None] < tiles[:, None, None])
    tile_base = base[None, :] + jnp.sum(tile_cnt[None, :, :] * earlier_tiles, axis=1)
    eid = rt[0:2].reshape(1, 2, N_TILES, TM)
    rank = rt[4:6].reshape(2, N_TILES, TM)
    pick = eid == experts.astype(np.float32).reshape(N_EXPERTS, 1, 1, 1)
    dest = jnp.sum(jnp.where(pick, tile_base.T[:, None, :, None], 0.0), axis=0) + rank
    blk_start = (np.arange(MOE_BLOCKS) * MOE_R).astype(np.float32)
    blk_e = jnp.minimum(jnp.sum((blk_start[:, None] >= pad_end[None, :]).astype(F32), axis=1), N_EXPERTS - 1)
    mine = blk_e[:, None] == experts.astype(np.float32)[None, :]
    blk_fill = jnp.sum(jnp.where(mine, (counts + base)[None, :], 0.0), axis=1) - blk_start
    blk_valid = jnp.clip(blk_fill, 0, MOE_R)
    partial = jnp.where(counts != padded, jnp.floor((base + counts) / MOE_R), -1.0)
    zblk = jnp.concatenate([partial, pad_end[-1:] / MOE_R])
    return (dest.reshape(-1).astype(jnp.int32), blk_e.astype(jnp.int32), blk_valid.astype(jnp.int32),
            zblk.astype(jnp.int32))


def kernel(x_prompt, x_sample, c_prompt, c_sample, cache_win_k, cache_win_v, state_conv, state_C, state_n, state_m, w_ada, b_ada, g_norm_mix, g_norm_ffn, w_in, attn_sinks, conv_w, conv_b, b_igate, b_fgate, g_mhnorm, w_out, w_router_group, b_router_group, w_router_expert, b_router_expert, w_exp_gate, w_exp_up, w_exp_down, g_final):
    l = 0
    mod = _modulation(jnp.concatenate([c_prompt, c_sample], axis=0), w_ada[l], b_ada[l])
    chunk_stream = np.concatenate([np.repeat(np.arange(BATCH), SEQ // CHUNK), BATCH + np.arange(DEC_BATCH)])
    modc = mod[chunk_stream].reshape(N_CHUNKS, 1, 6 * D_MODEL)

    wi = w_in[l]
    s_q, s_k, s_v, s_qkm, s_vm, s_ig, s_fg = 0, 512, 640, 768, 1792, 2304, 2308
    s_om = 2312
    w_in_r = jnp.concatenate(
        [wi[:, s_q:s_qkm], wi[:, s_qkm:s_vm], wi[:, s_vm:s_ig], wi[:, s_om:], wi[:, s_ig:s_om],
         jnp.zeros((D_MODEL, LANES - 2 * M_HEADS), F32)], axis=1).astype(BF16)
    q, kv, qkm, vm, om, gates = _inproj(x_prompt, x_sample, modc, g_norm_mix[l].reshape(1, -1), w_in_r)

    cache_kv = jnp.concatenate([cache_win_k[l].reshape(DEC_BATCH * WINDOW, A_KV_WIDTH),
                                cache_win_v[l].reshape(DEC_BATCH * WINDOW, A_KV_WIDTH)], axis=1)
    att_p, att_s = _attention(attn_sinks[l], q, kv, cache_kv)

    gbias = jnp.concatenate([b_igate[l], b_fgate[l], jnp.zeros((LANES - 2 * M_HEADS,), F32)]).reshape(1, LANES)
    common = (qkm, vm, om, gates, conv_w[l], conv_b[l].reshape(1, -1), gbias, g_mhnorm[l].reshape(1, -1))
    zeros_p = (jnp.zeros((BATCH, SUBLANES, 2 * M_WIDTH), F32),
               jnp.zeros((BATCH, M_HEADS, M_HEAD_DIM, M_HEAD_DIM), F32),
               jnp.zeros((BATCH, M_HEADS, M_HEAD_DIM), F32),
               jnp.zeros((BATCH, 1, LANES), F32))
    LP = 256
    hm_p, C_p, n_p, m_p = _mlstm(*common, *zeros_p, nb=BATCH, L=LP, nc=SEQ // LP, ns=2, row_off=0,
                                 name="mlstm_prompt")
    carry_s = jnp.concatenate([jnp.zeros((DEC_BATCH, SUBLANES - (CONV_WIDTH - 1), 2 * M_WIDTH), F32),
                               state_conv[l]], axis=1)
    m0_s = jnp.pad(state_m[l], ((0, 0), (0, LANES - M_HEADS))).reshape(DEC_BATCH, 1, LANES)
    hm_s, C_s, n_s, m_s = _mlstm(*common, carry_s, state_C[l], state_n[l], m0_s,
                                 nb=DEC_BATCH, L=DEC_SEQ, nc=1, ns=4, row_off=N_P, name="mlstm_sample")

    w_router = jnp.concatenate([w_router_group[l], w_router_expert[l],
                                jnp.zeros((D_MODEL, LANES - N_GROUPS - N_EXPERTS), F32)], axis=1)
    w_router_hi = w_router.astype(BF16)
    w_router = jnp.stack([w_router_hi, (w_router - w_router_hi.astype(F32)).astype(BF16)])
    b_router =jnp.concatenate([b_router_group[l], b_router_expert[l],
                                jnp.zeros((LANES - N_GROUPS - N_EXPERTS,), F32)]).reshape(1, LANES)
    x2, h2t, rinfo, rt, cnt = _outproj(x_prompt, x_sample, att_p, att_s, hm_p, hm_s, modc,
                                       g_norm_ffn[l].reshape(1, -1), w_out[l].astype(BF16), w_router, b_router)

    dest, blk_e, blk_valid, zblk = _moe_plan(rt, cnt)
    xs_rows = _dispatch(dest, zblk, h2t)
    o_rows = _experts(blk_e, blk_valid, xs_rows, w_exp_gate[l], w_exp_up[l], w_exp_down[l])
    y_prompt, y_sample = _combine(dest, o_rows, x2, rinfo, modc, g_final.reshape(1, -1))

    kv_p = kv.reshape(N_TOK // WINDOW, WINDOW, 2 * A_KV_WIDTH)[SEQ // WINDOW - 1:N_P // WINDOW:SEQ // WINDOW]
    win_k_p = kv_p[..., :A_KV_WIDTH].reshape(1, BATCH, WINDOW, A_KV_HEADS, A_HEAD_DIM)
    win_v_p = kv_p[..., A_KV_WIDTH:].reshape(1, BATCH, WINDOW, A_KV_HEADS, A_HEAD_DIM)
    kv_s = kv[N_P:].reshape(DEC_BATCH, DEC_SEQ, 2 * A_KV_WIDTH)
    win_k_s = jnp.concatenate([cache_win_k[l][:, DEC_SEQ:],
                               kv_s[..., :A_KV_WIDTH].reshape(DEC_BATCH, DEC_SEQ, A_KV_HEADS, A_HEAD_DIM)], axis=1)[None]
    win_v_s = jnp.concatenate([cache_win_v[l][:, DEC_SEQ:],
                               kv_s[..., A_KV_WIDTH:].reshape(DEC_BATCH, DEC_SEQ, A_KV_HEADS, A_HEAD_DIM)], axis=1)[None]
    qkm_c = qkm.reshape(N_CHUNKS, CHUNK, 2 * M_WIDTH)
    tail = slice(CHUNK - (CONV_WIDTH - 1), CHUNK)
    conv_p = qkm_c[SEQ // CHUNK - 1:N_P // CHUNK:SEQ // CHUNK, tail][None]
    conv_s = qkm_c[N_P // CHUNK:, tail][None]
    return (y_prompt, y_sample,
            win_k_p, win_v_p, conv_p, C_p[None], n_p[None], m_p[:, 0, :M_HEADS][None],
            win_k_s, win_v_s, conv_s, C_s[None], n_s[None], m_s[:, 0, :M_HEADS][None])
```

```python
import functools

import numpy as np
import jax
import jax.numpy as jnp
from jax import lax
from jax.experimental import pallas as pl
from jax.experimental.pallas import tpu as pltpu

F32 = jnp.float32
BF16 = jnp.bfloat16
HIGHEST = lax.Precision.HIGHEST

LANES = 128
SUBLANES = 8

D_MODEL = 1024
BATCH = 8
SEQ = 2048
DEC_BATCH = 32
DEC_SEQ = 64
PAST_LEN = 4096
CHUNK = 64
A_WIDTH = 512
A_HEAD_DIM = 64
A_HEADS = 8
A_KV_HEADS = 2
A_GROUP = 4
A_KV_WIDTH = 128
WINDOW = 128
ROT_DIM = 16
ROPE_THETA = 500000.0
M_WIDTH = 512
M_HEADS = 4
M_HEAD_DIM = 128
CONV_WIDTH = 4
N_GROUPS = 4
EXPERTS_PER_GROUP = 8
N_EXPERTS = 32
D_EXPERT = 512
EPS = 1e-6

N_P = BATCH * SEQ
N_S = DEC_BATCH * DEC_SEQ
N_TOK = N_P + N_S
N_CHUNKS = N_TOK // CHUNK
TM = 512
N_TILES = N_TOK // TM
P_TILES = N_P // TM
CH_PER_TILE = TM // CHUNK
ROW_SUB = D_MODEL // LANES
N_ASSIGN = 2 * N_TOK
MOE_R = 512
MOE_BLOCKS = N_ASSIGN // MOE_R + N_EXPERTS
MOE_ROWS = MOE_BLOCKS * MOE_R
C_Q, C_K, C_V, C_QKM, C_VM, C_OM, C_G = 0, 512, 640, 768, 1792, 2304, 2816
D_IN_PAD = 2944
VMEM_LIMIT = 48 * 1024 * 1024


def _sigmoid(x):
    return 1.0 / (1.0 + jnp.exp(-x))


def _mod_kernel(c_ref, w_ref, b_ref, o_ref):
    c = c_ref[...]
    s = c * _sigmoid(c)
    o_ref[...] = jnp.dot(s, w_ref[...], preferred_element_type=F32, precision=HIGHEST) + b_ref[...]


def _modulation(c_all, w_ada, b_ada):
    n = c_all.shape[0]
    bn = 512
    return pl.pallas_call(
        _mod_kernel,
        grid=(6 * D_MODEL // bn,),
        in_specs=[pl.BlockSpec((n, D_MODEL), lambda j: (0, 0)),
                  pl.BlockSpec((D_MODEL, bn), lambda j: (0, j)),
                  pl.BlockSpec((1, bn), lambda j: (0, j))],
        out_specs=pl.BlockSpec((n, bn), lambda j: (0, j)),
        out_shape=jax.ShapeDtypeStruct((n, 6 * D_MODEL), F32),
        name="modulation",
    )(c_all, w_ada, b_ada.reshape(1, -1))


def _xp_spec():
    def idx(i):
        t = jnp.minimum(i, P_TILES - 1)
        return (t // (SEQ // TM), t % (SEQ // TM), 0)
    return pl.BlockSpec((1, TM, D_MODEL), idx)


def _xs_spec():
    return pl.BlockSpec((CH_PER_TILE, DEC_SEQ, D_MODEL), lambda i: (jnp.maximum(i - P_TILES, 0), 0, 0))


def _mod_spec(comp):
    return pl.BlockSpec((CH_PER_TILE, 1, D_MODEL), lambda i: (i, 0, comp))


def _load_x(xp_ref, xs_ref, xbuf):
    i = pl.program_id(0)

    @pl.when(i < P_TILES)
    def _():
        xbuf[...] = xp_ref[0]

    @pl.when(i >= P_TILES)
    def _():
        xbuf[...] = xs_ref[...].reshape(TM, D_MODEL)

    return xbuf[...]


def _per_chunk(m_ref):
    m = m_ref[...]
    return jnp.broadcast_to(m, (CH_PER_TILE, CHUNK, D_MODEL)).reshape(TM, D_MODEL)


def _rmsnorm(x, g):
    return x * lax.rsqrt(jnp.mean(x * x, axis=-1, keepdims=True) + EPS) * g


def _rope(x, cos, sa, sb):
    n = x.shape[1]
    rep = n // LANES
    if rep > 1:
        cos = jnp.concatenate([cos] * rep, axis=1)
        sa = jnp.concatenate([sa] * rep, axis=1)
        sb = jnp.concatenate([sb] * rep, axis=1)
    return x * cos + pltpu.roll(x, n - ROT_DIM // 2, 1) * sa + pltpu.roll(x, ROT_DIM // 2, 1) * sb


def _inproj_kernel(xp_ref, xs_ref, sh_ref, sc_ref, g_ref, w_ref, cos_ref, sa_ref, sb_ref,
                   q_ref, kv_ref, qkm_ref, vm_ref, om_ref, gt_ref, xbuf):
    x = _load_x(xp_ref, xs_ref, xbuf)
    h = _rmsnorm(x, g_ref[...]) * (1.0 + _per_chunk(sc_ref)) + _per_chunk(sh_ref)
    hb = h.astype(BF16)

    def proj(a, b):
        return jnp.dot(hb, w_ref[:, a:b], preferred_element_type=F32)

    cos, sa, sb = cos_ref[...], sa_ref[...], sb_ref[...]
    q_ref[...] = _rope(proj(C_Q, C_K), cos, sa, sb).astype(BF16)
    kv_ref[:, :A_KV_WIDTH] = _rope(proj(C_K, C_V), cos, sa, sb)
    kv_ref[:, A_KV_WIDTH:] = proj(C_V, C_QKM)
    qkm_ref[...] = proj(C_QKM, C_VM)
    vm_ref[...] = proj(C_VM, C_OM).astype(BF16)
    om_ref[...] = proj(C_OM, C_G)
    gt_ref[...] = proj(C_G, D_IN_PAD)


def _rope_tables():
    pos = np.concatenate([np.arange(SEQ), np.tile(PAST_LEN + np.arange(DEC_SEQ), CH_PER_TILE)]).astype(np.float64)
    inv_freq = ROPE_THETA ** (-np.arange(0, ROT_DIM, 2, dtype=np.float64) / ROT_DIM)
    lane = np.arange(LANES)
    hl = lane % A_HEAD_DIM
    ang = pos[:, None] * inv_freq[hl % (ROT_DIM // 2)][None, :]
    rot = (hl < ROT_DIM)[None, :]
    lo = (hl < ROT_DIM // 2)[None, :]
    cos = np.where(rot, np.cos(ang), 1.0)
    sa = np.where(lo, -np.sin(ang), 0.0)
    sb = np.where(rot & ~lo, np.sin(ang), 0.0)
    return [jnp.asarray(t, F32) for t in (cos, sa, sb)]


def _inproj(x_prompt, x_sample, modc, g_mix, w_in_r):
    cos, sa, sb = _rope_tables()
    tab_spec = pl.BlockSpec((TM, LANES), lambda i: (jnp.where(i < P_TILES, i % (SEQ // TM), SEQ // TM), 0))

    def out(n, dtype=F32):
        return pl.BlockSpec((TM, n), lambda i: (i, 0)), jax.ShapeDtypeStruct((N_TOK, n), dtype)

    outs = [out(A_WIDTH, BF16), out(2 * A_KV_WIDTH), out(2 * M_WIDTH), out(M_WIDTH, BF16), out(M_WIDTH), out(LANES)]
    return pl.pallas_call(
        _inproj_kernel,
        grid=(N_TILES,),
        in_specs=[_xp_spec(), _xs_spec(), _mod_spec(0), _mod_spec(1),
                  pl.BlockSpec((1, D_MODEL), lambda i: (0, 0)),
                  pl.BlockSpec((D_MODEL, D_IN_PAD), lambda i: (0, 0)),
                  tab_spec, tab_spec, tab_spec],
        out_specs=[o[0] for o in outs],
        out_shape=[o[1] for o in outs],
        scratch_shapes=[pltpu.VMEM((TM, D_MODEL), F32)],
        compiler_params=pltpu.CompilerParams(vmem_limit_bytes=VMEM_LIMIT),
        name="inproj",
    )(x_prompt, x_sample, modc, modc, g_mix, w_in_r, cos, sa, sb)


def _attn_chunk_kernel(sink_ref, q_ref, prev_ref, cur_ref, o_ref):
    q = q_ref[...] * (A_HEAD_DIM ** -0.5)
    kv = jnp.concatenate([prev_ref[...], cur_ref[...]], axis=0)
    for g in range(A_KV_HEADS):
        kg = kv[:, g * A_HEAD_DIM:(g + 1) * A_HEAD_DIM].astype(BF16)
        vg = kv[:, A_KV_WIDTH + g * A_HEAD_DIM:A_KV_WIDTH + (g + 1) * A_HEAD_DIM].astype(BF16)
        heads = [g * A_GROUP + i for i in range(A_GROUP)]
        qc = jnp.concatenate([q[:, h * A_HEAD_DIM:(h + 1) * A_HEAD_DIM] for h in heads], axis=0).astype(BF16)
        snk = jnp.concatenate([jnp.full((CHUNK, 1), sink_ref[h], F32) for h in heads], axis=0)
        s = lax.dot_general(qc, kg, (((1,), (1,)), ((), ())), preferred_element_type=F32)
        mx = jnp.maximum(jnp.max(s, axis=-1, keepdims=True), snk)
        p = jnp.exp(s - mx)
        den = jnp.sum(p, axis=-1, keepdims=True) + jnp.exp(snk - mx)
        o = jnp.dot(p.astype(BF16), vg, preferred_element_type=F32) / den
        for i, h in enumerate(heads):
            o_ref[:, h * A_HEAD_DIM:(h + 1) * A_HEAD_DIM] = o[i * CHUNK:(i + 1) * CHUNK]


ATT_TQ = 256
ATT_QB = 2 * CHUNK


def _attn_band_kernel(sink_ref, q_ref, prev_ref, cur_ref, o_ref, att_t):
    nk = ATT_QB + WINDOW
    q = (q_ref[...] * (A_HEAD_DIM ** -0.5)).astype(BF16)
    kv = jnp.concatenate([prev_ref[...], cur_ref[...]], axis=0)
    k2 = kv[:, :A_KV_WIDTH]
    k2r = pltpu.roll(k2, A_HEAD_DIM, 1)
    low = lax.broadcasted_iota(jnp.int32, k2.shape, 1) < A_HEAD_DIM
    k_placed = {(0, 0): jnp.where(low, k2, 0.0), (0, 1): jnp.where(low, 0.0, k2r),
                (1, 0): jnp.where(low, k2r, 0.0), (1, 1): jnp.where(low, 0.0, k2)}
    k_placed = {key: val.astype(BF16) for key, val in k_placed.items()}
    v_t = kv[:, A_KV_WIDTH:].T.astype(BF16)
    key_chunk = lax.broadcasted_iota(jnp.int32, (nk, ATT_QB), 0) // CHUNK
    q_chunk = lax.broadcasted_iota(jnp.int32, (nk, ATT_QB), 1) // CHUNK
    band = (key_chunk >= q_chunk) & (key_chunk <= q_chunk + WINDOW // CHUNK)
    has_history = pl.program_id(1) > 0
    for blk in range(ATT_TQ // ATT_QB):
        keys = slice(blk * ATT_QB, blk * ATT_QB + nk)
        qrows = slice(blk * ATT_QB, (blk + 1) * ATT_QB)
        valid = band & ((key_chunk >= WINDOW // CHUNK) | has_history) if blk == 0 else band
        for h in range(A_HEADS):
            g = h // A_GROUP
            s_t = lax.dot_general(k_placed[(g, h % 2)][keys], q[qrows, (h // 2) * LANES:(h // 2 + 1) * LANES],
                                  (((1,), (1,)), ((), ())), preferred_element_type=F32)
            s_t = jnp.where(valid, s_t, -jnp.inf)
            snk = sink_ref[h]
            mx = jnp.maximum(jnp.max(s_t, axis=0, keepdims=True), snk)
            p_t = jnp.exp(s_t - mx)
            den = jnp.sum(p_t, axis=0, keepdims=True) + jnp.exp(snk - mx)
            o_t = jnp.dot(v_t[:, keys], p_t.astype(BF16), preferred_element_type=F32)
            att_t[h * A_HEAD_DIM:(h + 1) * A_HEAD_DIM, qrows] = o_t[g * A_HEAD_DIM:(g + 1) * A_HEAD_DIM] / den
    o_ref[...] = att_t[...].T.astype(BF16)


def _attention(sinks, q, kv, cache_kv):
    smem = pl.BlockSpec(memory_space=pltpu.SMEM)
    tq = ATT_TQ
    nq = SEQ // tq
    att = pl.pallas_call(
        _attn_band_kernel,
        grid=(BATCH, nq),
        scratch_shapes=[pltpu.VMEM((A_WIDTH, ATT_TQ), F32)],
        in_specs=[smem,
                  pl.BlockSpec((tq, A_WIDTH), lambda b, j: (b * nq + j, 0)),
                  pl.BlockSpec((WINDOW, 2 * A_KV_WIDTH),
                               lambda b, j: (jnp.maximum((b * nq + j) * (tq // WINDOW) - 1, 0), 0)),
                  pl.BlockSpec((tq, 2 * A_KV_WIDTH), lambda b, j: (b * nq + j, 0))],
        out_specs=pl.BlockSpec((tq, A_WIDTH), lambda b, j: (b * nq + j, 0)),
        out_shape=jax.ShapeDtypeStruct((N_P, A_WIDTH), BF16),
        name="attn_prompt",
    )(sinks, q, kv, kv)
    off = N_P // DEC_SEQ
    att_s = pl.pallas_call(
        _attn_chunk_kernel,
        grid=(DEC_BATCH,),
        in_specs=[smem,
                  pl.BlockSpec((DEC_SEQ, A_WIDTH), lambda b: (off + b, 0)),
                  pl.BlockSpec((WINDOW, 2 * A_KV_WIDTH), lambda b: (b, 0)),
                  pl.BlockSpec((DEC_SEQ, 2 * A_KV_WIDTH), lambda b: (off + b, 0))],
        out_specs=pl.BlockSpec((DEC_SEQ, A_WIDTH), lambda b: (b, 0)),
        out_shape=jax.ShapeDtypeStruct((N_S, A_WIDTH), F32),
        name="attn_sample",
    )(sinks, q, cache_kv, kv)
    return att, att_s


def _conv4(x, w, b):
    y = b + x * w[CONV_WIDTH - 1:CONV_WIDTH]
    for j in range(1, CONV_WIDTH):
        y = y + pltpu.roll(x, j, 0) * w[CONV_WIDTH - 1 - j:CONV_WIDTH - j]
    return y


def _mlstm_kernel(qkm_ref, vm_ref, om_ref, gt_ref, cw_ref, cb_ref, gb_ref, gmh_ref,
                  carry0_ref, c0_ref, n0_ref, m0_ref,
                  hm_ref, cout_ref, nout_ref, mout_ref, c_s, n_s, m_s, carry_s, *, L, NS):
    c = pl.program_id(1)

    @pl.when(c == 0)
    def _():
        c_s[...] = c0_ref[...]
        n_s[...] = n0_ref[...]
        m_s[...] = m0_ref[...]
        carry_s[...] = carry0_ref[...]

    w = cw_ref[...]
    b = cb_ref[...]
    gmh = gmh_ref[...]
    lane = lax.broadcasted_iota(jnp.int32, (L, LANES), 1)
    causal = lax.broadcasted_iota(jnp.int32, (L, L), 0) >= lax.broadcasted_iota(jnp.int32, (L, L), 1)
    ones_causal = causal.astype(F32)
    sel = (lax.broadcasted_iota(jnp.int32, (SUBLANES, LANES), 0)
           == lax.broadcasted_iota(jnp.int32, (SUBLANES, LANES), 1)).astype(F32)
    lane1 = lax.broadcasted_iota(jnp.int32, (1, LANES), 1)
    for st in range(NS):
        _mlstm_stream(st, qkm_ref, vm_ref, om_ref, gt_ref, gb_ref, hm_ref, c_s, n_s, m_s, carry_s,
                      w, b, gmh, lane, causal, ones_causal, sel, lane1, L)

    @pl.when(c == pl.num_programs(1) - 1)
    def _():
        cout_ref[...] = c_s[...]
        nout_ref[...] = n_s[...]
        mout_ref[...] = m_s[...]


def _mlstm_stream(st, qkm_ref, vm_ref, om_ref, gt_ref, gb_ref, hm_ref, c_s, n_s, m_s, carry_s,
                  w, b, gmh, lane, causal, ones_causal, sel, lane1, L):
    x = qkm_ref[st]
    y = _conv4(x, w, b)
    y8 = _conv4(jnp.concatenate([carry_s[st], x[:SUBLANES]], axis=0), w, b)
    y = jnp.concatenate([y8[SUBLANES:], y[SUBLANES:]], axis=0)
    carry_s[st] = x[L - SUBLANES:]
    a = y * _sigmoid(y)
    qa = a[:, :M_WIDTH] * (M_HEAD_DIM ** -0.5)
    ka = a[:, M_WIDTH:]
    v = vm_ref[st]
    om = om_ref[st]

    z = gt_ref[st] + gb_ref[...]
    f_log = jnp.minimum(z, 0.0) - jnp.log1p(jnp.exp(-jnp.abs(z)))
    val = jnp.where(lane < M_HEADS, z, f_log)
    cum = jnp.dot(ones_causal, val, preferred_element_type=F32, precision=HIGHEST)
    G = jnp.where(lane < M_HEADS, val, cum)
    GT = lax.dot_general(sel, G, (((1,), (1,)), ((), ())), preferred_element_type=F32, precision=HIGHEST)

    m_row = m_s[st]
    outs = []
    for h in range(M_HEADS):
        a_col = G[:, M_HEADS + h:M_HEADS + h + 1]
        i_col = G[:, h:h + 1]
        i_row = GT[h:h + 1, :]
        b_row = GT[M_HEADS + h:M_HEADS + h + 1, :]
        m_prev = m_row[:, h:h + 1]
        dm = jnp.where(causal, a_col - b_row + i_row, -jnp.inf)
        inter = a_col + m_prev
        m_t = jnp.maximum(inter, jnp.max(dm, axis=-1, keepdims=True))
        w_intra = jnp.exp(dm - m_t)
        w_inter = jnp.exp(inter - m_t)
        sl = slice(h * M_HEAD_DIM, (h + 1) * M_HEAD_DIM)
        qh, kh, vh = qa[:, sl], ka[:, sl], v[:, sl]
        qb = qh.astype(BF16)
        scores = lax.dot_general(qb, kh.astype(BF16), (((1,), (1,)), ((), ())), preferred_element_type=F32) * w_intra
        ch = c_s[st, h]
        nh = n_s[st, h:h + 1, :]
        num = (jnp.dot(scores.astype(BF16), vh.astype(BF16), preferred_element_type=F32)
               + w_inter * lax.dot_general(qb, ch.astype(BF16), (((1,), (1,)), ((), ())), preferred_element_type=F32))
        den = jnp.sum(scores, axis=-1, keepdims=True) + w_inter * jnp.sum(qh * nh, axis=-1, keepdims=True)
        hv = num / jnp.maximum(jnp.abs(den), jnp.exp(-m_t))
        hn = hv * lax.rsqrt(jnp.mean(hv * hv, axis=-1, keepdims=True) + EPS)
        outs.append(hn * gmh[:, sl] * _sigmoid(om[:, sl]))
        m_new = m_t[L - 1:L, :]
        a_last = a_col[L - 1:L, :]
        w_end = jnp.exp(a_last - a_col + i_col - m_new)
        decay = jnp.exp(a_last + m_prev - m_new)
        kw = kh * w_end
        c_s[st, h] = decay * ch + lax.dot_general(vh.astype(BF16), kw.astype(BF16), (((0,), (0,)), ((), ())),
                                                  preferred_element_type=F32)
        n_s[st, h:h + 1, :] = decay * nh + jnp.sum(kw, axis=0, keepdims=True)
        m_row = jnp.where(lane1 == h, m_new, m_row)
    m_s[st] = m_row
    hm_ref[st] = jnp.concatenate(outs, axis=1).astype(BF16)


def _mlstm(qkm, vm, om, gates, conv_w, conv_b, gbias, g_mh, carry0, c0, n0, m0, *, nb, L, nc, ns, row_off, name):
    S = nc * L
    first = row_off // S // ns

    def tok(arr):
        n = arr.shape[-1]
        return arr.reshape(N_TOK // S, S, n), pl.BlockSpec((ns, L, n), lambda i, c: (first + i, c, 0))

    def const(shape):
        return pl.BlockSpec(shape, lambda i, c: (0,) * len(shape))

    def per_stream(shape):
        return pl.BlockSpec((ns,) + shape, lambda i, c: (i,) + (0,) * len(shape))

    toks = [tok(a) for a in (qkm, vm, om, gates)]
    state_specs = [per_stream((SUBLANES, 2 * M_WIDTH)), per_stream((M_HEADS, M_HEAD_DIM, M_HEAD_DIM)),
                   per_stream((M_HEADS, M_HEAD_DIM)), per_stream((1, LANES))]
    in_specs = ([t[1] for t in toks]
                + [const((CONV_WIDTH, 2 * M_WIDTH)), const((1, 2 * M_WIDTH)), const((1, LANES)), const((1, M_WIDTH))]
                + state_specs)
    args = [t[0] for t in toks] + [conv_w, conv_b, gbias, g_mh, carry0, c0, n0, m0]
    hm, c_out, n_out, m_out = pl.pallas_call(
        functools.partial(_mlstm_kernel, L=L, NS=ns),
        grid=(nb // ns, nc),
        in_specs=in_specs,
        out_specs=[pl.BlockSpec((ns, L, M_WIDTH), lambda i, c: (i, c, 0))] + state_specs[1:],
        out_shape=[jax.ShapeDtypeStruct((nb, S, M_WIDTH), BF16),
                   jax.ShapeDtypeStruct((nb, M_HEADS, M_HEAD_DIM, M_HEAD_DIM), F32),
                   jax.ShapeDtypeStruct((nb, M_HEADS, M_HEAD_DIM), F32),
                   jax.ShapeDtypeStruct((nb, 1, LANES), F32)],
        scratch_shapes=[pltpu.VMEM((ns, M_HEADS, M_HEAD_DIM, M_HEAD_DIM), F32),
                        pltpu.VMEM((ns, M_HEADS, M_HEAD_DIM), F32),
                        pltpu.VMEM((ns, 1, LANES), F32),
                        pltpu.VMEM((ns, SUBLANES, 2 * M_WIDTH), F32)],
        compiler_params=pltpu.CompilerParams(dimension_semantics=("arbitrary", "arbitrary"),
                                             vmem_limit_bytes=VMEM_LIMIT),
        name=name,
    )(*args)
    return hm.reshape(nb * S, M_WIDTH), c_out, n_out, m_out


def _outproj_kernel(xp_ref, xs_ref, attp_ref, atts_ref, hmp_ref, hms_ref, gate_ref, sh_ref, sc_ref, g_ref,
                    w_ref, wr_ref, br_ref, x2_ref, h2_ref, ri_ref, rt_ref, cnt_ref, xbuf, mixbuf):
    x = _load_x(xp_ref, xs_ref, xbuf)
    i = pl.program_id(0)

    @pl.when(i < P_TILES)
    def _():
        mixbuf[:, :A_WIDTH] = attp_ref[...].astype(BF16)
        mixbuf[:, A_WIDTH:] = hmp_ref[...].astype(BF16)

    @pl.when(i >= P_TILES)
    def _():
        mixbuf[:, :A_WIDTH] = atts_ref[...].astype(BF16)
        mixbuf[:, A_WIDTH:] = hms_ref[...].astype(BF16)

    mixed = jnp.dot(mixbuf[...], w_ref[...], preferred_element_type=F32)
    x2 = x + _per_chunk(gate_ref) * mixed
    x2_ref[...] = x2
    h2 = _rmsnorm(x2, g_ref[...]) * (1.0 + _per_chunk(sc_ref)) + _per_chunk(sh_ref)
    for s in range(ROW_SUB):
        h2_ref[pl.ds(s, TM, stride=ROW_SUB), :] = h2[:, s * LANES:(s + 1) * LANES]

    h_hi = h2.astype(BF16)
    h_lo = (h2 - h_hi.astype(F32)).astype(BF16)
    lg = (jnp.dot(h_hi, wr_ref[0], preferred_element_type=F32)
          + jnp.dot(h_lo, wr_ref[0], preferred_element_type=F32)
          + jnp.dot(h_hi, wr_ref[1], preferred_element_type=F32)) + br_ref[...]
    lane = lax.broadcasted_iota(jnp.int32, (TM, LANES), 1)
    lanef = lane.astype(F32)
    ninf = -jnp.inf

    def first_argmax(vals):
        mx = jnp.max(vals, axis=-1, keepdims=True)
        return mx, jnp.min(jnp.where(vals == mx, lanef, float(LANES)), axis=-1, keepdims=True)

    is_grp = lane < N_GROUPS
    gmax, grp = first_argmax(jnp.where(is_grp, lg, ninf))
    p_grp = 1.0 / jnp.sum(jnp.where(is_grp, jnp.exp(lg - gmax), 0.0), axis=-1, keepdims=True)
    base = N_GROUPS + grp * EXPERTS_PER_GROUP
    in_grp = (lanef >= base) & (lanef < base + EXPERTS_PER_GROUP)
    el = jnp.where(in_grp, lg, ninf)
    v1, i1 = first_argmax(el)
    v2, i2 = first_argmax(jnp.where(lanef == i1, ninf, el))
    e = jnp.exp(v2 - v1)
    g1 = p_grp / (1.0 + e)
    g2 = p_grp * e / (1.0 + e)
    e1 = i1 - N_GROUPS
    e2 = i2 - N_GROUPS
    oh1 = lanef == e1
    oh2 = lanef == e2
    oh = jnp.where(oh1 | oh2, 1.0, 0.0)
    earlier = (lax.broadcasted_iota(jnp.int32, (TM, TM), 0) > lax.broadcasted_iota(jnp.int32, (TM, TM), 1))
    before = jnp.dot(earlier.astype(BF16), oh.astype(BF16), preferred_element_type=F32)
    r1 = jnp.sum(jnp.where(oh1, before, 0.0), axis=-1, keepdims=True)
    r2 = jnp.sum(jnp.where(oh2, before, 0.0), axis=-1, keepdims=True)
    cnt_ref[...] = jnp.broadcast_to(jnp.sum(oh, axis=0, keepdims=True), (SUBLANES, LANES))
    cols = (e1, e2, g1, g2, r1, r2)
    ri = jnp.zeros((TM, LANES), F32)
    for k, cval in enumerate(cols):
        ri = jnp.where(lane == k, cval, ri)
    ri_ref[...] = ri
    rt_ref[...] = ri.T[:SUBLANES]


def _outproj(x_prompt, x_sample, att_p, att_s, hm_p, hm_s, modc, g_ffn, w_out_b, w_router, b_router):
    def tok(n):
        return pl.BlockSpec((TM, n), lambda i: (i, 0))

    def tok_p(n):
        return pl.BlockSpec((TM, n), lambda i: (jnp.minimum(i, P_TILES - 1), 0))

    def tok_s(n):
        return pl.BlockSpec((TM, n), lambda i: (jnp.maximum(i - P_TILES, 0), 0))

    def const(shape):
        return pl.BlockSpec(shape, lambda i: (0,) * len(shape))

    return pl.pallas_call(
        _outproj_kernel,
        grid=(N_TILES,),
        in_specs=[_xp_spec(), _xs_spec(), tok_p(A_WIDTH), tok_s(A_WIDTH), tok_p(M_WIDTH), tok_s(M_WIDTH),
                  _mod_spec(2), _mod_spec(3), _mod_spec(4),
                  const((1, D_MODEL)), const((D_MODEL, D_MODEL)), const((2, D_MODEL, LANES)), const((1, LANES))],
        out_specs=[tok(D_MODEL), pl.BlockSpec((TM * ROW_SUB, LANES), lambda i: (i, 0)), tok(LANES),
                   pl.BlockSpec((SUBLANES, TM), lambda i: (0, i)),
                   pl.BlockSpec((SUBLANES, LANES), lambda i: (i, 0))],
        out_shape=[jax.ShapeDtypeStruct((N_TOK, D_MODEL), F32),
                   jax.ShapeDtypeStruct((N_TOK * ROW_SUB, LANES), F32),
                   jax.ShapeDtypeStruct((N_TOK, LANES), F32),
                   jax.ShapeDtypeStruct((SUBLANES, N_TOK), F32),
                   jax.ShapeDtypeStruct((N_TILES * SUBLANES, LANES), F32)],
        scratch_shapes=[pltpu.VMEM((TM, D_MODEL), F32), pltpu.VMEM((TM, D_MODEL), BF16)],
        compiler_params=pltpu.CompilerParams(vmem_limit_bytes=VMEM_LIMIT),
        name="outproj",
    )(x_prompt, x_sample, att_p, att_s, hm_p, hm_s, modc, modc, modc, g_ffn, w_out_b, w_router, b_router)


def _row(ref, r):
    return ref.at[pl.ds(pl.multiple_of(r * ROW_SUB, ROW_SUB), ROW_SUB), :]


def _row_copies(i, t, dest_ref, make):
    tok = i * TM + t
    return [make(k, dest_ref[k * N_TOK + tok]) for k in range(2)]


def _dispatch_kernel(dest_ref, zblk_ref, h_ref, xs_out, zbuf, sem, zsem):
    i = pl.program_id(0)
    blk_rows = MOE_R * ROW_SUB

    @pl.when(i == 0)
    def _():
        zbuf[...] = jnp.zeros_like(zbuf)

        def zero_copy(e):
            start = pl.multiple_of(jnp.maximum(zblk_ref[e], 0) * blk_rows, blk_rows)
            return pltpu.make_async_copy(zbuf, xs_out.at[pl.ds(start, blk_rows), :], zsem)

        def z_start(e, carry):
            @pl.when(zblk_ref[e] >= 0)
            def _():
                zero_copy(e).start()
            return carry

        def z_wait(e, carry):
            @pl.when(zblk_ref[e] >= 0)
            def _():
                zero_copy(e).wait()
            return carry

        lax.fori_loop(0, N_EXPERTS, z_start, 0)
        lax.fori_loop(0, N_EXPERTS, z_wait, 0)

        def tail_copy(b):
            return pltpu.make_async_copy(zbuf, xs_out.at[pl.ds(pl.multiple_of(b * blk_rows, blk_rows), blk_rows), :], zsem)

        def t_start(b, carry):
            tail_copy(b).start()
            return carry

        def t_wait(b, carry):
            tail_copy(b).wait()
            return carry

        lax.fori_loop(zblk_ref[N_EXPERTS], MOE_BLOCKS, t_start, 0)
        lax.fori_loop(zblk_ref[N_EXPERTS], MOE_BLOCKS, t_wait, 0)

    def copies(t):
        return _row_copies(i, t, dest_ref, lambda k, d: pltpu.make_async_copy(_row(h_ref, t), _row(xs_out, d), sem))

    def issue(t, carry):
        for k, cp in enumerate(copies(t)):
            cp.start(priority=k)
        return carry

    def wait(t, carry):
        for cp in copies(t):
            cp.wait()
        return carry

    lax.fori_loop(0, TM, issue, 0, unroll=8)
    lax.fori_loop(0, TM, wait, 0, unroll=8)


def _dispatch(dest, zblk, h2t):
    return pl.pallas_call(
        _dispatch_kernel,
        grid_spec=pltpu.PrefetchScalarGridSpec(
            num_scalar_prefetch=2,
            grid=(N_TILES,),
            in_specs=[pl.BlockSpec((TM * ROW_SUB, LANES), lambda i, d, z: (i, 0))],
            out_specs=pl.BlockSpec(memory_space=pl.ANY),
            scratch_shapes=[pltpu.VMEM((MOE_R * ROW_SUB, LANES), F32),
                            pltpu.SemaphoreType.DMA, pltpu.SemaphoreType.DMA]),
        out_shape=jax.ShapeDtypeStruct((MOE_ROWS * ROW_SUB, LANES), F32),
        compiler_params=pltpu.CompilerParams(dimension_semantics=("arbitrary",)),
        name="moe_dispatch",
    )(dest, zblk, h2t)


def _expert_kernel(be_ref, bv_ref, first_ref, slot_ref, next_ref, xs_ref, wg_hbm, wu_hbm, wd_hbm, o_ref,
                   wg_f, wu_f, wd_f, wg_s, wu_s, wd_s, wsem):
    b = pl.program_id(0)

    def weight_copies(e, slot):
        return [pltpu.make_async_copy(src.at[e], dst.at[slot], wsem.at[slot])
                for src, dst in ((wg_hbm, wg_f), (wu_hbm, wu_f), (wd_hbm, wd_f))]

    @pl.when(b == 0)
    def _():
        for cp in weight_copies(be_ref[0], 0):
            cp.start()

    @pl.when(first_ref[b] == 1)
    def _():
        slot = slot_ref[b]
        for cp in weight_copies(be_ref[b], slot):
            cp.wait()
        wg_s[...] = wg_f[slot].astype(BF16)
        wu_s[...] = wu_f[slot].astype(BF16)
        wd_s[...] = wd_f[slot].astype(BF16)

        @pl.when(next_ref[b] >= 0)
        def _():
            for cp in weight_copies(next_ref[b], 1 - slot):
                cp.start()

    @pl.when(bv_ref[b] > 0)
    def _():
        x = jnp.concatenate([xs_ref[pl.ds(s, MOE_R, stride=ROW_SUB), :] for s in range(ROW_SUB)], axis=1)
        xb = x.astype(BF16)
        g = jnp.dot(xb, wg_s[...], preferred_element_type=F32)
        u = jnp.dot(xb, wu_s[...], preferred_element_type=F32)
        a = (g * _sigmoid(g) * u).astype(BF16)
        o = jnp.dot(a, wd_s[...], preferred_element_type=F32)
        for s in range(ROW_SUB):
            o_ref[pl.ds(s, MOE_R, stride=ROW_SUB), :] = o[:, s * LANES:(s + 1) * LANES]

    @pl.when(bv_ref[b] == 0)
    def _():
        o_ref[...] = jnp.zeros_like(o_ref)


def _experts(blk_e, blk_valid, xs, w_g, w_u, w_d):
    idx = np.arange(MOE_BLOCKS)
    used = blk_valid > 0
    first = used & ((idx == 0) | (blk_e != jnp.roll(blk_e, 1)))
    ordinal = jnp.sum(first[None, :] & (idx[None, :] <= idx[:, None]), axis=1) - 1
    later_first = jnp.where(first[None, :] & (idx[None, :] > idx[:, None]), idx[None, :], MOE_BLOCKS)
    nxt = jnp.min(later_first, axis=1)
    next_e = jnp.where(nxt < MOE_BLOCKS, jnp.sum(jnp.where(idx[None, :] == nxt[:, None], blk_e[None, :], 0), axis=1), -1)
    plan = [blk_e, blk_valid, first.astype(jnp.int32), (ordinal % 2).astype(jnp.int32), next_e.astype(jnp.int32)]

    def rows(b, be, bv, *_):
        return (jnp.where(bv[b] > 0, b, MOE_BLOCKS - 1), 0)

    hbm = pl.BlockSpec(memory_space=pl.ANY)
    return pl.pallas_call(
        _expert_kernel,
        grid_spec=pltpu.PrefetchScalarGridSpec(
            num_scalar_prefetch=len(plan),
            grid=(MOE_BLOCKS,),
            in_specs=[pl.BlockSpec((MOE_R * ROW_SUB, LANES), rows), hbm, hbm, hbm],
            out_specs=pl.BlockSpec((MOE_R * ROW_SUB, LANES), lambda b, *_: (b, 0)),
            scratch_shapes=[pltpu.VMEM((2, D_MODEL, D_EXPERT), F32),
                            pltpu.VMEM((2, D_MODEL, D_EXPERT), F32),
                            pltpu.VMEM((2, D_EXPERT, D_MODEL), F32),
                            pltpu.VMEM((D_MODEL, D_EXPERT), BF16),
                            pltpu.VMEM((D_MODEL, D_EXPERT), BF16),
                            pltpu.VMEM((D_EXPERT, D_MODEL), BF16),
                            pltpu.SemaphoreType.DMA((2,))]),
        out_shape=jax.ShapeDtypeStruct((MOE_ROWS * ROW_SUB, LANES), F32),
        compiler_params=pltpu.CompilerParams(dimension_semantics=("arbitrary",), vmem_limit_bytes=VMEM_LIMIT),
        name="moe_experts",
    )(*plan, xs, w_g, w_u, w_d)


def _combine_kernel(dest_ref, o_hbm, x2_ref, ri_ref, gate_ref, gf_ref, yp_ref, ys_ref, obuf, sem):
    i = pl.program_id(0)
    slot = i % 2
    tile_rows = TM * ROW_SUB

    def issue_tile(tile, slot_):
        def issue(t, carry):
            cps = _row_copies(tile, t, dest_ref, lambda k, d: pltpu.make_async_copy(
                _row(o_hbm, d), _row(obuf, (2 * slot_ + k) * TM + t), sem.at[slot_]))
            for k, cp in enumerate(cps):
                cp.start(priority=k)
            return carry

        lax.fori_loop(0, TM, issue, 0, unroll=8)

    @pl.when(i == 0)
    def _():
        issue_tile(0, 0)

    @pl.when(i + 1 < N_TILES)
    def _():
        issue_tile(i + 1, 1 - slot)

    for k in range(2):
        start = pl.multiple_of((2 * slot + k) * tile_rows, tile_rows)
        pltpu.make_async_copy(o_hbm.at[pl.ds(0, tile_rows), :], obuf.at[pl.ds(start, tile_rows), :],
                              sem.at[slot]).wait()

    ri = ri_ref[...]
    g1 = ri[:, 2:3]
    g2 = ri[:, 3:4]
    base = 2 * slot * tile_rows
    moe = jnp.concatenate(
        [g1 * obuf[pl.ds(base + s, TM, stride=ROW_SUB), :] + g2 * obuf[pl.ds(base + tile_rows + s, TM, stride=ROW_SUB), :]
         for s in range(ROW_SUB)], axis=1)
    x3 = x2_ref[...] + _per_chunk(gate_ref) * moe
    y = _rmsnorm(x3, gf_ref[...])

    @pl.when(i < P_TILES)
    def _():
        yp_ref[0] = y

    @pl.when(i >= P_TILES)
    def _():
        ys_ref[...] = y.reshape(CH_PER_TILE, DEC_SEQ, D_MODEL)


def _combine(dest, o_rows, x2, rinfo, modc, g_final):
    xp_spec, xs_spec = _xp_spec(), _xs_spec()
    return pl.pallas_call(
        _combine_kernel,
        grid_spec=pltpu.PrefetchScalarGridSpec(
            num_scalar_prefetch=1,
            grid=(N_TILES,),
            in_specs=[pl.BlockSpec(memory_space=pl.ANY),
                      pl.BlockSpec((TM, D_MODEL), lambda i, d: (i, 0)),
                      pl.BlockSpec((TM, LANES), lambda i, d: (i, 0)),
                      pl.BlockSpec((CH_PER_TILE, 1, D_MODEL), lambda i, d: (i, 0, 5)),
                      pl.BlockSpec((1, D_MODEL), lambda i, d: (0, 0))],
            out_specs=[pl.BlockSpec(xp_spec.block_shape, lambda i, d: xp_spec.index_map(i)),
                       pl.BlockSpec(xs_spec.block_shape, lambda i, d: xs_spec.index_map(i))],
            scratch_shapes=[pltpu.VMEM((4 * TM * ROW_SUB, LANES), F32), pltpu.SemaphoreType.DMA((2,))]),
        out_shape=[jax.ShapeDtypeStruct((BATCH, SEQ, D_MODEL), F32),
                   jax.ShapeDtypeStruct((DEC_BATCH, DEC_SEQ, D_MODEL), F32)],
        compiler_params=pltpu.CompilerParams(dimension_semantics=("arbitrary",), vmem_limit_bytes=VMEM_LIMIT),
        name="moe_combine",
    )(dest, o_rows, x2, rinfo, modc, g_final)


def _moe_plan(rt, cnt):
    experts = np.arange(N_EXPERTS)
    tile_cnt = cnt.reshape(N_TILES, SUBLANES, LANES)[:, 0, :N_EXPERTS]
    counts = jnp.sum(tile_cnt, axis=0)
    padded = jnp.ceil(counts / MOE_R) * MOE_R
    pad_end = jnp.sum(padded[:, None] * (experts[:, None] <= experts[None, :]), axis=0)
    base = pad_end - padded
    tiles = np.arange(N_TILES)
    earlier_tiles = (tiles[None, :, None] < tiles[:, None, None])
    tile_base = base[None, :] + jnp.sum(tile_cnt[None, :, :] * earlier_tiles, axis=1)
    eid = rt[0:2].reshape(1, 2, N_TILES, TM)
    rank = rt[4:6].reshape(2, N_TILES, TM)
    pick = eid == experts.astype(np.float32).reshape(N_EXPERTS, 1, 1, 1)
    dest = jnp.sum(jnp.where(pick, tile_base.T[:, None, :, None], 0.0), axis=0) + rank
    blk_start = (np.arange(MOE_BLOCKS) * MOE_R).astype(np.float32)
    blk_e = jnp.minimum(jnp.sum((blk_start[:, None] >= pad_end[None, :]).astype(F32), axis=1), N_EXPERTS - 1)
    mine = blk_e[:, None] == experts.astype(np.float32)[None, :]
    blk_fill = jnp.sum(jnp.where(mine, (counts + base)[None, :], 0.0), axis=1) - blk_start
    blk_valid = jnp.clip(blk_fill, 0, MOE_R)
    partial = jnp.where(counts != padded, jnp.floor((base + counts) / MOE_R), -1.0)
    zblk = jnp.concatenate([partial, pad_end[-1:] / MOE_R])
    return (dest.reshape(-1).astype(jnp.int32), blk_e.astype(jnp.int32), blk_valid.astype(jnp.int32),
            zblk.astype(jnp.int32))


def kernel(x_prompt, x_sample, c_prompt, c_sample, cache_win_k, cache_win_v, state_conv, state_C, state_n, state_m, w_ada, b_ada, g_norm_mix, g_norm_ffn, w_in, attn_sinks, conv_w, conv_b, b_igate, b_fgate, g_mhnorm, w_out, w_router_group, b_router_group, w_router_expert, b_router_expert, w_exp_gate, w_exp_up, w_exp_down, g_final):
    l = 0
    mod = _modulation(jnp.concatenate([c_prompt, c_sample], axis=0), w_ada[l], b_ada[l])
    chunk_stream = np.concatenate([np.repeat(np.arange(BATCH), SEQ // CHUNK), BATCH + np.arange(DEC_BATCH)])
    modc = mod[chunk_stream].reshape(N_CHUNKS, 1, 6 * D_MODEL)

    wi = w_in[l]
    s_q, s_k, s_v, s_qkm, s_vm, s_ig, s_fg = 0, 512, 640, 768, 1792, 2304, 2308
    s_om = 2312
    w_in_r = jnp.concatenate(
        [wi[:, s_q:s_qkm], wi[:, s_qkm:s_vm], wi[:, s_vm:s_ig], wi[:, s_om:], wi[:, s_ig:s_om],
         jnp.zeros((D_MODEL, LANES - 2 * M_HEADS), F32)], axis=1).astype(BF16)
    q, kv, qkm, vm, om, gates = _inproj(x_prompt, x_sample, modc, g_norm_mix[l].reshape(1, -1), w_in_r)

    cache_kv = jnp.concatenate([cache_win_k[l].reshape(DEC_BATCH * WINDOW, A_KV_WIDTH),
                                cache_win_v[l].reshape(DEC_BATCH * WINDOW, A_KV_WIDTH)], axis=1)
    att_p, att_s = _attention(attn_sinks[l], q, kv, cache_kv)

    gbias = jnp.concatenate([b_igate[l], b_fgate[l], jnp.zeros((LANES - 2 * M_HEADS,), F32)]).reshape(1, LANES)
    common = (qkm, vm, om, gates, conv_w[l], conv_b[l].reshape(1, -1), gbias, g_mhnorm[l].reshape(1, -1))
    zeros_p = (jnp.zeros((BATCH, SUBLANES, 2 * M_WIDTH), F32),
               jnp.zeros((BATCH, M_HEADS, M_HEAD_DIM, M_HEAD_DIM), F32),
               jnp.zeros((BATCH, M_HEADS, M_HEAD_DIM), F32),
               jnp.zeros((BATCH, 1, LANES), F32))
    LP = 256
    hm_p, C_p, n_p, m_p = _mlstm(*common, *zeros_p, nb=BATCH, L=LP, nc=SEQ // LP, ns=1, row_off=0,
                                 name="mlstm_prompt")
    carry_s = jnp.concatenate([jnp.zeros((DEC_BATCH, SUBLANES - (CONV_WIDTH - 1), 2 * M_WIDTH), F32),
                               state_conv[l]], axis=1)
    m0_s = jnp.pad(state_m[l], ((0, 0), (0, LANES - M_HEADS))).reshape(DEC_BATCH, 1, LANES)
    hm_s, C_s, n_s, m_s = _mlstm(*common, carry_s, state_C[l], state_n[l], m0_s,
                                 nb=DEC_BATCH, L=DEC_SEQ, nc=1, ns=4, row_off=N_P, name="mlstm_sample")

    w_router = jnp.concatenate([w_router_group[l], w_router_expert[l],
                                jnp.zeros((D_MODEL, LANES - N_GROUPS - N_EXPERTS), F32)], axis=1)
    w_router_hi = w_router.astype(BF16)
    w_router = jnp.stack([w_router_hi, (w_router - w_router_hi.astype(F32)).astype(BF16)])
    b_router =jnp.concatenate([b_router_group[l], b_router_expert[l],
                                jnp.zeros((LANES - N_GROUPS - N_EXPERTS,), F32)]).reshape(1, LANES)
    x2, h2t, rinfo, rt, cnt = _outproj(x_prompt, x_sample, att_p, att_s, hm_p, hm_s, modc,
                                       g_norm_ffn[l].reshape(1, -1), w_out[l].astype(BF16), w_router, b_router)

    dest, blk_e, blk_valid, zblk = _moe_plan(rt, cnt)
    xs_rows = _dispatch(dest, zblk, h2t)
    o_rows = _experts(blk_e, blk_valid, xs_rows, w_exp_gate[l], w_exp_up[l], w_exp_down[l])
    y_prompt, y_sample = _combine(dest, o_rows, x2, rinfo, modc, g_final.reshape(1, -1))

    kv_p = kv.reshape(N_TOK // WINDOW, WINDOW, 2 * A_KV_WIDTH)[SEQ // WINDOW - 1:N_P // WINDOW:SEQ // WINDOW]
    win_k_p = kv_p[..., :A_KV_WIDTH].reshape(1, BATCH, WINDOW, A_KV_HEADS, A_HEAD_DIM)
    win_v_p = kv_p[..., A_KV_WIDTH:].reshape(1, BATCH, WINDOW, A_KV_HEADS, A_HEAD_DIM)
    kv_s = kv[N_P:].reshape(DEC_BATCH, DEC_SEQ, 2 * A_KV_WIDTH)
    win_k_s = jnp.concatenate([cache_win_k[l][:, DEC_SEQ:],
                               kv_s[..., :A_KV_WIDTH].reshape(DEC_BATCH, DEC_SEQ, A_KV_HEADS, A_HEAD_DIM)], axis=1)[None]
    win_v_s = jnp.concatenate([cache_win_v[l][:, DEC_SEQ:],
                               kv_s[..., A_KV_WIDTH:].reshape(DEC_BATCH, DEC_SEQ, A_KV_HEADS, A_HEAD_DIM)], axis=1)[None]
    qkm_c = qkm.reshape(N_CHUNKS, CHUNK, 2 * M_WIDTH)
    tail = slice(CHUNK - (CONV_WIDTH - 1), CHUNK)
    conv_p = qkm_c[SEQ // CHUNK - 1:N_P // CHUNK:SEQ // CHUNK, tail][None]
    conv_s = qkm_c[N_P // CHUNK:, tail][None]
    return (y_prompt, y_sample,
            win_k_p, win_v_p, conv_p, C_p[None], n_p[None], m_p[:, 0, :M_HEADS][None],
            win_k_s, win_v_s, conv_s, C_s[None], n_s[None], m_s[:, 0, :M_HEADS][None])
```

```python
import functools

import numpy as np
import jax
import jax.numpy as jnp
from jax import lax
from jax.experimental import pallas as pl
from jax.experimental.pallas import tpu as pltpu

F32 = jnp.float32
BF16 = jnp.bfloat16
HIGHEST = lax.Precision.HIGHEST

LANES = 128
SUBLANES = 8

D_MODEL = 1024
BATCH = 8
SEQ = 2048
DEC_BATCH = 32
DEC_SEQ = 64
PAST_LEN = 4096
CHUNK = 64
A_WIDTH = 512
A_HEAD_DIM = 64
A_HEADS = 8
A_KV_HEADS = 2
A_GROUP = 4
A_KV_WIDTH = 128
WINDOW = 128
ROT_DIM = 16
ROPE_THETA = 500000.0
M_WIDTH = 512
M_HEADS = 4
M_HEAD_DIM = 128
CONV_WIDTH = 4
N_GROUPS = 4
EXPERTS_PER_GROUP = 8
N_EXPERTS = 32
D_EXPERT = 512
EPS = 1e-6

N_P = BATCH * SEQ
N_S = DEC_BATCH * DEC_SEQ
N_TOK = N_P + N_S
N_CHUNKS = N_TOK // CHUNK
TM = 512
N_TILES = N_TOK // TM
P_TILES = N_P // TM
CH_PER_TILE = TM // CHUNK
ROW_SUB = D_MODEL // LANES
N_ASSIGN = 2 * N_TOK
MOE_R = 512
MOE_BLOCKS = N_ASSIGN // MOE_R + N_EXPERTS
MOE_ROWS = MOE_BLOCKS * MOE_R
C_Q, C_K, C_V, C_QKM, C_VM, C_OM, C_G = 0, 512, 640, 768, 1792, 2304, 2816
D_IN_PAD = 2944
VMEM_LIMIT = 48 * 1024 * 1024


def _sigmoid(x):
    return 1.0 / (1.0 + jnp.exp(-x))


def _mod_kernel(c_ref, w_ref, b_ref, o_ref):
    c = c_ref[...]
    s = c * _sigmoid(c)
    o_ref[...] = jnp.dot(s, w_ref[...], preferred_element_type=F32, precision=HIGHEST) + b_ref[...]


def _modulation(c_all, w_ada, b_ada):
    n = c_all.shape[0]
    bn = 512
    return pl.pallas_call(
        _mod_kernel,
        grid=(6 * D_MODEL // bn,),
        in_specs=[pl.BlockSpec((n, D_MODEL), lambda j: (0, 0)),
                  pl.BlockSpec((D_MODEL, bn), lambda j: (0, j)),
                  pl.BlockSpec((1, bn), lambda j: (0, j))],
        out_specs=pl.BlockSpec((n, bn), lambda j: (0, j)),
        out_shape=jax.ShapeDtypeStruct((n, 6 * D_MODEL), F32),
        name="modulation",
    )(c_all, w_ada, b_ada.reshape(1, -1))


def _xp_spec():
    def idx(i):
        t = jnp.minimum(i, P_TILES - 1)
        return (t // (SEQ // TM), t % (SEQ // TM), 0)
    return pl.BlockSpec((1, TM, D_MODEL), idx)


def _xs_spec():
    return pl.BlockSpec((CH_PER_TILE, DEC_SEQ, D_MODEL), lambda i: (jnp.maximum(i - P_TILES, 0), 0, 0))


def _mod_spec(comp):
    return pl.BlockSpec((CH_PER_TILE, 1, D_MODEL), lambda i: (i, 0, comp))


def _load_x(xp_ref, xs_ref, xbuf):
    i = pl.program_id(0)

    @pl.when(i < P_TILES)
    def _():
        xbuf[...] = xp_ref[0]

    @pl.when(i >= P_TILES)
    def _():
        xbuf[...] = xs_ref[...].reshape(TM, D_MODEL)

    return xbuf[...]


def _per_chunk(m_ref):
    m = m_ref[...]
    return jnp.broadcast_to(m, (CH_PER_TILE, CHUNK, D_MODEL)).reshape(TM, D_MODEL)


def _rmsnorm(x, g):
    return x * lax.rsqrt(jnp.mean(x * x, axis=-1, keepdims=True) + EPS) * g


def _rope(x, cos, sa, sb):
    n = x.shape[1]
    rep = n // LANES
    if rep > 1:
        cos = jnp.concatenate([cos] * rep, axis=1)
        sa = jnp.concatenate([sa] * rep, axis=1)
        sb = jnp.concatenate([sb] * rep, axis=1)
    return x * cos + pltpu.roll(x, n - ROT_DIM // 2, 1) * sa + pltpu.roll(x, ROT_DIM // 2, 1) * sb


def _inproj_kernel(xp_ref, xs_ref, sh_ref, sc_ref, g_ref, w_ref, cos_ref, sa_ref, sb_ref,
                   q_ref, kv_ref, qkm_ref, vm_ref, om_ref, gt_ref, xbuf):
    x = _load_x(xp_ref, xs_ref, xbuf)
    h = _rmsnorm(x, g_ref[...]) * (1.0 + _per_chunk(sc_ref)) + _per_chunk(sh_ref)
    hb = h.astype(BF16)

    def proj(a, b):
        return jnp.dot(hb, w_ref[:, a:b], preferred_element_type=F32)

    cos, sa, sb = cos_ref[...], sa_ref[...], sb_ref[...]
    q_ref[...] = _rope(proj(C_Q, C_K), cos, sa, sb).astype(BF16)
    kv_ref[:, :A_KV_WIDTH] = _rope(proj(C_K, C_V), cos, sa, sb)
    kv_ref[:, A_KV_WIDTH:] = proj(C_V, C_QKM)
    qkm_ref[...] = proj(C_QKM, C_VM)
    vm_ref[...] = proj(C_VM, C_OM).astype(BF16)
    om_ref[...] = proj(C_OM, C_G)
    gt_ref[...] = proj(C_G, D_IN_PAD)


def _rope_tables():
    pos = np.concatenate([np.arange(SEQ), np.tile(PAST_LEN + np.arange(DEC_SEQ), CH_PER_TILE)]).astype(np.float64)
    inv_freq = ROPE_THETA ** (-np.arange(0, ROT_DIM, 2, dtype=np.float64) / ROT_DIM)
    lane = np.arange(LANES)
    hl = lane % A_HEAD_DIM
    ang = pos[:, None] * inv_freq[hl % (ROT_DIM // 2)][None, :]
    rot = (hl < ROT_DIM)[None, :]
    lo = (hl < ROT_DIM // 2)[None, :]
    cos = np.where(rot, np.cos(ang), 1.0)
    sa = np.where(lo, -np.sin(ang), 0.0)
    sb = np.where(rot & ~lo, np.sin(ang), 0.0)
    return [jnp.asarray(t, F32) for t in (cos, sa, sb)]


def _inproj(x_prompt, x_sample, modc, g_mix, w_in_r):
    cos, sa, sb = _rope_tables()
    tab_spec = pl.BlockSpec((TM, LANES), lambda i: (jnp.where(i < P_TILES, i % (SEQ // TM), SEQ // TM), 0))

    def out(n, dtype=F32):
        return pl.BlockSpec((TM, n), lambda i: (i, 0)), jax.ShapeDtypeStruct((N_TOK, n), dtype)

    outs = [out(A_WIDTH, BF16), out(2 * A_KV_WIDTH), out(2 * M_WIDTH), out(M_WIDTH, BF16), out(M_WIDTH), out(LANES)]
    return pl.pallas_call(
        _inproj_kernel,
        grid=(N_TILES,),
        in_specs=[_xp_spec(), _xs_spec(), _mod_spec(0), _mod_spec(1),
                  pl.BlockSpec((1, D_MODEL), lambda i: (0, 0)),
                  pl.BlockSpec((D_MODEL, D_IN_PAD), lambda i: (0, 0)),
                  tab_spec, tab_spec, tab_spec],
        out_specs=[o[0] for o in outs],
        out_shape=[o[1] for o in outs],
        scratch_shapes=[pltpu.VMEM((TM, D_MODEL), F32)],
        compiler_params=pltpu.CompilerParams(vmem_limit_bytes=VMEM_LIMIT),
        name="inproj",
    )(x_prompt, x_sample, modc, modc, g_mix, w_in_r, cos, sa, sb)


def _attn_chunk_kernel(sink_ref, q_ref, prev_ref, cur_ref, o_ref):
    q = q_ref[...] * (A_HEAD_DIM ** -0.5)
    kv = jnp.concatenate([prev_ref[...], cur_ref[...]], axis=0)
    for g in range(A_KV_HEADS):
        kg = kv[:, g * A_HEAD_DIM:(g + 1) * A_HEAD_DIM].astype(BF16)
        vg = kv[:, A_KV_WIDTH + g * A_HEAD_DIM:A_KV_WIDTH + (g + 1) * A_HEAD_DIM].astype(BF16)
        heads = [g * A_GROUP + i for i in range(A_GROUP)]
        qc = jnp.concatenate([q[:, h * A_HEAD_DIM:(h + 1) * A_HEAD_DIM] for h in heads], axis=0).astype(BF16)
        snk = jnp.concatenate([jnp.full((CHUNK, 1), sink_ref[h], F32) for h in heads], axis=0)
        s = lax.dot_general(qc, kg, (((1,), (1,)), ((), ())), preferred_element_type=F32)
        mx = jnp.maximum(jnp.max(s, axis=-1, keepdims=True), snk)
        p = jnp.exp(s - mx)
        den = jnp.sum(p, axis=-1, keepdims=True) + jnp.exp(snk - mx)
        o = jnp.dot(p.astype(BF16), vg, preferred_element_type=F32) / den
        for i, h in enumerate(heads):
            o_ref[:, h * A_HEAD_DIM:(h + 1) * A_HEAD_DIM] = o[i * CHUNK:(i + 1) * CHUNK]


ATT_TQ = 256
ATT_QB = 2 * CHUNK


def _attn_band_kernel(sink_ref, q_ref, prev_ref, cur_ref, o_ref, att_t):
    nk = ATT_QB + WINDOW
    q = (q_ref[...] * (A_HEAD_DIM ** -0.5)).astype(BF16)
    kv = jnp.concatenate([prev_ref[...], cur_ref[...]], axis=0)
    k2 = kv[:, :A_KV_WIDTH]
    k2r = pltpu.roll(k2, A_HEAD_DIM, 1)
    low = lax.broadcasted_iota(jnp.int32, k2.shape, 1) < A_HEAD_DIM
    k_placed = {(0, 0): jnp.where(low, k2, 0.0), (0, 1): jnp.where(low, 0.0, k2r),
                (1, 0): jnp.where(low, k2r, 0.0), (1, 1): jnp.where(low, 0.0, k2)}
    k_placed = {key: val.astype(BF16) for key, val in k_placed.items()}
    v_t = kv[:, A_KV_WIDTH:].T.astype(BF16)
    key_chunk = lax.broadcasted_iota(jnp.int32, (nk, ATT_QB), 0) // CHUNK
    q_chunk = lax.broadcasted_iota(jnp.int32, (nk, ATT_QB), 1) // CHUNK
    band = (key_chunk >= q_chunk) & (key_chunk <= q_chunk + WINDOW // CHUNK)
    has_history = pl.program_id(1) > 0
    for blk in range(ATT_TQ // ATT_QB):
        keys = slice(blk * ATT_QB, blk * ATT_QB + nk)
        qrows = slice(blk * ATT_QB, (blk + 1) * ATT_QB)
        valid = band & ((key_chunk >= WINDOW // CHUNK) | has_history) if blk == 0 else band
        for h in range(A_HEADS):
            g = h // A_GROUP
            s_t = lax.dot_general(k_placed[(g, h % 2)][keys], q[qrows, (h // 2) * LANES:(h // 2 + 1) * LANES],
                                  (((1,), (1,)), ((), ())), preferred_element_type=F32)
            s_t = jnp.where(valid, s_t, -jnp.inf)
            snk = sink_ref[h]
            mx = jnp.maximum(jnp.max(s_t, axis=0, keepdims=True), snk)
            p_t = jnp.exp(s_t - mx)
            den = jnp.sum(p_t, axis=0, keepdims=True) + jnp.exp(snk - mx)
            o_t = jnp.dot(v_t[:, keys], p_t.astype(BF16), preferred_element_type=F32)
            att_t[h * A_HEAD_DIM:(h + 1) * A_HEAD_DIM, qrows] = o_t[g * A_HEAD_DIM:(g + 1) * A_HEAD_DIM] / den
    o_ref[...] = att_t[...].T.astype(BF16)


def _attention(sinks, q, kv, cache_kv):
    smem = pl.BlockSpec(memory_space=pltpu.SMEM)
    tq = ATT_TQ
    nq = SEQ // tq
    att = pl.pallas_call(
        _attn_band_kernel,
        grid=(BATCH, nq),
        scratch_shapes=[pltpu.VMEM((A_WIDTH, ATT_TQ), F32)],
        in_specs=[smem,
                  pl.BlockSpec((tq, A_WIDTH), lambda b, j: (b * nq + j, 0)),
                  pl.BlockSpec((WINDOW, 2 * A_KV_WIDTH),
                               lambda b, j: (jnp.maximum((b * nq + j) * (tq // WINDOW) - 1, 0), 0)),
                  pl.BlockSpec((tq, 2 * A_KV_WIDTH), lambda b, j: (b * nq + j, 0))],
        out_specs=pl.BlockSpec((tq, A_WIDTH), lambda b, j: (b * nq + j, 0)),
        out_shape=jax.ShapeDtypeStruct((N_P, A_WIDTH), BF16),
        name="attn_prompt",
    )(sinks, q, kv, kv)
    off = N_P // DEC_SEQ
    att_s = pl.pallas_call(
        _attn_chunk_kernel,
        grid=(DEC_BATCH,),
        in_specs=[smem,
                  pl.BlockSpec((DEC_SEQ, A_WIDTH), lambda b: (off + b, 0)),
                  pl.BlockSpec((WINDOW, 2 * A_KV_WIDTH), lambda b: (b, 0)),
                  pl.BlockSpec((DEC_SEQ, 2 * A_KV_WIDTH), lambda b: (off + b, 0))],
        out_specs=pl.BlockSpec((DEC_SEQ, A_WIDTH), lambda b: (b, 0)),
        out_shape=jax.ShapeDtypeStruct((N_S, A_WIDTH), F32),
        name="attn_sample",
    )(sinks, q, cache_kv, kv)
    return att, att_s


def _conv4(x, w, b):
    y = b + x * w[CONV_WIDTH - 1:CONV_WIDTH]
    for j in range(1, CONV_WIDTH):
        y = y + pltpu.roll(x, j, 0) * w[CONV_WIDTH - 1 - j:CONV_WIDTH - j]
    return y


def _mlstm_kernel(qkm_ref, vm_ref, om_ref, gt_ref, cw_ref, cb_ref, gb_ref, gmh_ref,
                  carry0_ref, c0_ref, n0_ref, m0_ref,
                  hm_ref, cout_ref, nout_ref, mout_ref, c_s, n_s, m_s, carry_s, *, L, NS):
    c = pl.program_id(1)

    @pl.when(c == 0)
    def _():
        c_s[...] = c0_ref[...]
        n_s[...] = n0_ref[...]
        m_s[...] = m0_ref[...]
        carry_s[...] = carry0_ref[...]

    w = cw_ref[...]
    b = cb_ref[...]
    gmh = gmh_ref[...]
    lane = lax.broadcasted_iota(jnp.int32, (L, LANES), 1)
    causal = lax.broadcasted_iota(jnp.int32, (L, L), 0) >= lax.broadcasted_iota(jnp.int32, (L, L), 1)
    ones_causal = causal.astype(F32)
    sel = (lax.broadcasted_iota(jnp.int32, (SUBLANES, LANES), 0)
           == lax.broadcasted_iota(jnp.int32, (SUBLANES, LANES), 1)).astype(F32)
    lane1 = lax.broadcasted_iota(jnp.int32, (1, LANES), 1)
    for st in range(NS):
        _mlstm_stream(st, qkm_ref, vm_ref, om_ref, gt_ref, gb_ref, hm_ref, c_s, n_s, m_s, carry_s,
                      w, b, gmh, lane, causal, ones_causal, sel, lane1, L)

    @pl.when(c == pl.num_programs(1) - 1)
    def _():
        cout_ref[...] = c_s[...]
        nout_ref[...] = n_s[...]
        mout_ref[...] = m_s[...]


def _mlstm_stream(st, qkm_ref, vm_ref, om_ref, gt_ref, gb_ref, hm_ref, c_s, n_s, m_s, carry_s,
                  w, b, gmh, lane, causal, ones_causal, sel, lane1, L):
    x = qkm_ref[st]
    y = _conv4(x, w, b)
    y8 = _conv4(jnp.concatenate([carry_s[st], x[:SUBLANES]], axis=0), w, b)
    y = jnp.concatenate([y8[SUBLANES:], y[SUBLANES:]], axis=0)
    carry_s[st] = x[L - SUBLANES:]
    a = y * _sigmoid(y)
    qa = a[:, :M_WIDTH] * (M_HEAD_DIM ** -0.5)
    ka = a[:, M_WIDTH:]
    v = vm_ref[st]
    om = om_ref[st]

    z = gt_ref[st] + gb_ref[...]
    f_log = jnp.minimum(z, 0.0) - jnp.log1p(jnp.exp(-jnp.abs(z)))
    val = jnp.where(lane < M_HEADS, z, f_log)
    cum = jnp.dot(ones_causal, val, preferred_element_type=F32, precision=HIGHEST)
    G = jnp.where(lane < M_HEADS, val, cum)
    GT = lax.dot_general(sel, G, (((1,), (1,)), ((), ())), preferred_element_type=F32, precision=HIGHEST)

    m_row = m_s[st]
    outs = []
    for h in range(M_HEADS):
        a_col = G[:, M_HEADS + h:M_HEADS + h + 1]
        i_col = G[:, h:h + 1]
        i_row = GT[h:h + 1, :]
        b_row = GT[M_HEADS + h:M_HEADS + h + 1, :]
        m_prev = m_row[:, h:h + 1]
        dm = jnp.where(causal, a_col - b_row + i_row, -jnp.inf)
        inter = a_col + m_prev
        m_t = jnp.maximum(inter, jnp.max(dm, axis=-1, keepdims=True))
        w_intra = jnp.exp(dm - m_t)
        w_inter = jnp.exp(inter - m_t)
        sl = slice(h * M_HEAD_DIM, (h + 1) * M_HEAD_DIM)
        qh, kh, vh = qa[:, sl], ka[:, sl], v[:, sl]
        qb = qh.astype(BF16)
        scores = lax.dot_general(qb, kh.astype(BF16), (((1,), (1,)), ((), ())), preferred_element_type=F32) * w_intra
        ch = c_s[st, h]
        nh = n_s[st, h:h + 1, :]
        num = (jnp.dot(scores.astype(BF16), vh.astype(BF16), preferred_element_type=F32)
               + w_inter * lax.dot_general(qb, ch.astype(BF16), (((1,), (1,)), ((), ())), preferred_element_type=F32))
        den = jnp.sum(scores, axis=-1, keepdims=True) + w_inter * jnp.sum(qh * nh, axis=-1, keepdims=True)
        hv = num / jnp.maximum(jnp.abs(den), jnp.exp(-m_t))
        hn = hv * lax.rsqrt(jnp.mean(hv * hv, axis=-1, keepdims=True) + EPS)
        outs.append(hn * gmh[:, sl] * _sigmoid(om[:, sl]))
        m_new = m_t[L - 1:L, :]
        a_last = a_col[L - 1:L, :]
        w_end = jnp.exp(a_last - a_col + i_col - m_new)
        decay = jnp.exp(a_last + m_prev - m_new)
        kw = kh * w_end
        c_s[st, h] = decay * ch + lax.dot_general(vh.astype(BF16), kw.astype(BF16), (((0,), (0,)), ((), ())),
                                                  preferred_element_type=F32)
        n_s[st, h:h + 1, :] = decay * nh + jnp.sum(kw, axis=0, keepdims=True)
        m_row = jnp.where(lane1 == h, m_new, m_row)
    m_s[st] = m_row
    hm_ref[st] = jnp.concatenate(outs, axis=1).astype(BF16)


def _mlstm(qkm, vm, om, gates, conv_w, conv_b, gbias, g_mh, carry0, c0, n0, m0, *, nb, L, nc, ns, row_off, name):
    S = nc * L
    first = row_off // S // ns

    def tok(arr):
        n = arr.shape[-1]
        return arr.reshape(N_TOK // S, S, n), pl.BlockSpec((ns, L, n), lambda i, c: (first + i, c, 0))

    def const(shape):
        return pl.BlockSpec(shape, lambda i, c: (0,) * len(shape))

    def per_stream(shape):
        return pl.BlockSpec((ns,) + shape, lambda i, c: (i,) + (0,) * len(shape))

    toks = [tok(a) for a in (qkm, vm, om, gates)]
    state_specs = [per_stream((SUBLANES, 2 * M_WIDTH)), per_stream((M_HEADS, M_HEAD_DIM, M_HEAD_DIM)),
                   per_stream((M_HEADS, M_HEAD_DIM)), per_stream((1, LANES))]
    in_specs = ([t[1] for t in toks]
                + [const((CONV_WIDTH, 2 * M_WIDTH)), const((1, 2 * M_WIDTH)), const((1, LANES)), const((1, M_WIDTH))]
                + state_specs)
    args = [t[0] for t in toks] + [conv_w, conv_b, gbias, g_mh, carry0, c0, n0, m0]
    hm, c_out, n_out, m_out = pl.pallas_call(
        functools.partial(_mlstm_kernel, L=L, NS=ns),
        grid=(nb // ns, nc),
        in_specs=in_specs,
        out_specs=[pl.BlockSpec((ns, L, M_WIDTH), lambda i, c: (i, c, 0))] + state_specs[1:],
        out_shape=[jax.ShapeDtypeStruct((nb, S, M_WIDTH), BF16),
                   jax.ShapeDtypeStruct((nb, M_HEADS, M_HEAD_DIM, M_HEAD_DIM), F32),
                   jax.ShapeDtypeStruct((nb, M_HEADS, M_HEAD_DIM), F32),
                   jax.ShapeDtypeStruct((nb, 1, LANES), F32)],
        scratch_shapes=[pltpu.VMEM((ns, M_HEADS, M_HEAD_DIM, M_HEAD_DIM), F32),
                        pltpu.VMEM((ns, M_HEADS, M_HEAD_DIM), F32),
                        pltpu.VMEM((ns, 1, LANES), F32),
                        pltpu.VMEM((ns, SUBLANES, 2 * M_WIDTH), F32)],
        compiler_params=pltpu.CompilerParams(dimension_semantics=("arbitrary", "arbitrary"),
                                             vmem_limit_bytes=VMEM_LIMIT),
        name=name,
    )(*args)
    return hm.reshape(nb * S, M_WIDTH), c_out, n_out, m_out


def _outproj_kernel(xp_ref, xs_ref, attp_ref, atts_ref, hmp_ref, hms_ref, gate_ref, sh_ref, sc_ref, g_ref,
                    w_ref, wr_ref, br_ref, x2_ref, h2_ref, ri_ref, rt_ref, cnt_ref, xbuf, mixbuf):
    x = _load_x(xp_ref, xs_ref, xbuf)
    i = pl.program_id(0)

    @pl.when(i < P_TILES)
    def _():
        mixbuf[:, :A_WIDTH] = attp_ref[...].astype(BF16)
        mixbuf[:, A_WIDTH:] = hmp_ref[...].astype(BF16)

    @pl.when(i >= P_TILES)
    def _():
        mixbuf[:, :A_WIDTH] = atts_ref[...].astype(BF16)
        mixbuf[:, A_WIDTH:] = hms_ref[...].astype(BF16)

    mixed = jnp.dot(mixbuf[...], w_ref[...], preferred_element_type=F32)
    x2 = x + _per_chunk(gate_ref) * mixed
    x2_ref[...] = x2
    h2 = _rmsnorm(x2, g_ref[...]) * (1.0 + _per_chunk(sc_ref)) + _per_chunk(sh_ref)
    for s in range(ROW_SUB):
        h2_ref[pl.ds(s, TM, stride=ROW_SUB), :] = h2[:, s * LANES:(s + 1) * LANES]

    h_hi = h2.astype(BF16)
    h_lo = (h2 - h_hi.astype(F32)).astype(BF16)
    lg = (jnp.dot(h_hi, wr_ref[0], preferred_element_type=F32)
          + jnp.dot(h_lo, wr_ref[0], preferred_element_type=F32)
          + jnp.dot(h_hi, wr_ref[1], preferred_element_type=F32)) + br_ref[...]
    lane = lax.broadcasted_iota(jnp.int32, (TM, LANES), 1)
    lanef = lane.astype(F32)
    ninf = -jnp.inf

    def first_argmax(vals):
        mx = jnp.max(vals, axis=-1, keepdims=True)
        return mx, jnp.min(jnp.where(vals == mx, lanef, float(LANES)), axis=-1, keepdims=True)

    is_grp = lane < N_GROUPS
    gmax, grp = first_argmax(jnp.where(is_grp, lg, ninf))
    p_grp = 1.0 / jnp.sum(jnp.where(is_grp, jnp.exp(lg - gmax), 0.0), axis=-1, keepdims=True)
    base = N_GROUPS + grp * EXPERTS_PER_GROUP
    in_grp = (lanef >= base) & (lanef < base + EXPERTS_PER_GROUP)
    el = jnp.where(in_grp, lg, ninf)
    v1, i1 = first_argmax(el)
    v2, i2 = first_argmax(jnp.where(lanef == i1, ninf, el))
    e = jnp.exp(v2 - v1)
    g1 = p_grp / (1.0 + e)
    g2 = p_grp * e / (1.0 + e)
    e1 = i1 - N_GROUPS
    e2 = i2 - N_GROUPS
    oh1 = lanef == e1
    oh2 = lanef == e2
    oh = jnp.where(oh1 | oh2, 1.0, 0.0)
    earlier = (lax.broadcasted_iota(jnp.int32, (TM, TM), 0) > lax.broadcasted_iota(jnp.int32, (TM, TM), 1))
    before = jnp.dot(earlier.astype(BF16), oh.astype(BF16), preferred_element_type=F32)
    r1 = jnp.sum(jnp.where(oh1, before, 0.0), axis=-1, keepdims=True)
    r2 = jnp.sum(jnp.where(oh2, before, 0.0), axis=-1, keepdims=True)
    cnt_ref[...] = jnp.broadcast_to(jnp.sum(oh, axis=0, keepdims=True), (SUBLANES, LANES))
    cols = (e1, e2, g1, g2, r1, r2)
    ri = jnp.zeros((TM, LANES), F32)
    for k, cval in enumerate(cols):
        ri = jnp.where(lane == k, cval, ri)
    ri_ref[...] = ri
    rt_ref[...] = ri.T[:SUBLANES]


def _outproj(x_prompt, x_sample, att_p, att_s, hm_p, hm_s, modc, g_ffn, w_out_b, w_router, b_router):
    def tok(n):
        return pl.BlockSpec((TM, n), lambda i: (i, 0))

    def tok_p(n):
        return pl.BlockSpec((TM, n), lambda i: (jnp.minimum(i, P_TILES - 1), 0))

    def tok_s(n):
        return pl.BlockSpec((TM, n), lambda i: (jnp.maximum(i - P_TILES, 0), 0))

    def const(shape):
        return pl.BlockSpec(shape, lambda i: (0,) * len(shape))

    return pl.pallas_call(
        _outproj_kernel,
        grid=(N_TILES,),
        in_specs=[_xp_spec(), _xs_spec(), tok_p(A_WIDTH), tok_s(A_WIDTH), tok_p(M_WIDTH), tok_s(M_WIDTH),
                  _mod_spec(2), _mod_spec(3), _mod_spec(4),
                  const((1, D_MODEL)), const((D_MODEL, D_MODEL)), const((2, D_MODEL, LANES)), const((1, LANES))],
        out_specs=[tok(D_MODEL), pl.BlockSpec((TM * ROW_SUB, LANES), lambda i: (i, 0)), tok(LANES),
                   pl.BlockSpec((SUBLANES, TM), lambda i: (0, i)),
                   pl.BlockSpec((SUBLANES, LANES), lambda i: (i, 0))],
        out_shape=[jax.ShapeDtypeStruct((N_TOK, D_MODEL), F32),
                   jax.ShapeDtypeStruct((N_TOK * ROW_SUB, LANES), F32),
                   jax.ShapeDtypeStruct((N_TOK, LANES), F32),
                   jax.ShapeDtypeStruct((SUBLANES, N_TOK), F32),
                   jax.ShapeDtypeStruct((N_TILES * SUBLANES, LANES), F32)],
        scratch_shapes=[pltpu.VMEM((TM, D_MODEL), F32), pltpu.VMEM((TM, D_MODEL), BF16)],
        compiler_params=pltpu.CompilerParams(vmem_limit_bytes=VMEM_LIMIT),
        name="outproj",
    )(x_prompt, x_sample, att_p, att_s, hm_p, hm_s, modc, modc, modc, g_ffn, w_out_b, w_router, b_router)


def _row(ref, r):
    return ref.at[pl.ds(pl.multiple_of(r * ROW_SUB, ROW_SUB), ROW_SUB), :]


def _row_copies(i, t, dest_ref, make):
    tok = i * TM + t
    return [make(k, dest_ref[k * N_TOK + tok]) for k in range(2)]


def _dispatch_kernel(dest_ref, zblk_ref, h_ref, xs_out, zbuf, sem, zsem):
    i = pl.program_id(0)
    blk_rows = MOE_R * ROW_SUB

    @pl.when(i == 0)
    def _():
        zbuf[...] = jnp.zeros_like(zbuf)

        def zero_copy(e):
            start = pl.multiple_of(jnp.maximum(zblk_ref[e], 0) * blk_rows, blk_rows)
            return pltpu.make_async_copy(zbuf, xs_out.at[pl.ds(start, blk_rows), :], zsem)

        def z_start(e, carry):
            @pl.when(zblk_ref[e] >= 0)
            def _():
                zero_copy(e).start()
            return carry

        def z_wait(e, carry):
            @pl.when(zblk_ref[e] >= 0)
            def _():
                zero_copy(e).wait()
            return carry

        lax.fori_loop(0, N_EXPERTS, z_start, 0)
        lax.fori_loop(0, N_EXPERTS, z_wait, 0)

        def tail_copy(b):
            return pltpu.make_async_copy(zbuf, xs_out.at[pl.ds(pl.multiple_of(b * blk_rows, blk_rows), blk_rows), :], zsem)

        def t_start(b, carry):
            tail_copy(b).start()
            return carry

        def t_wait(b, carry):
            tail_copy(b).wait()
            return carry

        lax.fori_loop(zblk_ref[N_EXPERTS], MOE_BLOCKS, t_start, 0)
        lax.fori_loop(zblk_ref[N_EXPERTS], MOE_BLOCKS, t_wait, 0)

    def copies(t):
        return _row_copies(i, t, dest_ref, lambda k, d: pltpu.make_async_copy(_row(h_ref, t), _row(xs_out, d), sem))

    def issue(t, carry):
        for k, cp in enumerate(copies(t)):
            cp.start(priority=k)
        return carry

    def wait(t, carry):
        for cp in copies(t):
            cp.wait()
        return carry

    lax.fori_loop(0, TM, issue, 0, unroll=8)
    lax.fori_loop(0, TM, wait, 0, unroll=8)


def _dispatch(dest, zblk, h2t):
    return pl.pallas_call(
        _dispatch_kernel,
        grid_spec=pltpu.PrefetchScalarGridSpec(
            num_scalar_prefetch=2,
            grid=(N_TILES,),
            in_specs=[pl.BlockSpec((TM * ROW_SUB, LANES), lambda i, d, z: (i, 0))],
            out_specs=pl.BlockSpec(memory_space=pl.ANY),
            scratch_shapes=[pltpu.VMEM((MOE_R * ROW_SUB, LANES), F32),
                            pltpu.SemaphoreType.DMA, pltpu.SemaphoreType.DMA]),
        out_shape=jax.ShapeDtypeStruct((MOE_ROWS * ROW_SUB, LANES), F32),
        compiler_params=pltpu.CompilerParams(dimension_semantics=("arbitrary",)),
        name="moe_dispatch",
    )(dest, zblk, h2t)


def _expert_kernel(be_ref, bv_ref, first_ref, slot_ref, next_ref, xs_ref, wg_hbm, wu_hbm, wd_hbm, o_ref,
                   wg_f, wu_f, wd_f, wg_s, wu_s, wd_s, wsem):
    b = pl.program_id(0)

    def weight_copies(e, slot):
        return [pltpu.make_async_copy(src.at[e], dst.at[slot], wsem.at[slot])
                for src, dst in ((wg_hbm, wg_f), (wu_hbm, wu_f), (wd_hbm, wd_f))]

    @pl.when(b == 0)
    def _():
        for cp in weight_copies(be_ref[0], 0):
            cp.start()

    @pl.when(first_ref[b] == 1)
    def _():
        slot = slot_ref[b]
        for cp in weight_copies(be_ref[b], slot):
            cp.wait()
        wg_s[...] = wg_f[slot].astype(BF16)
        wu_s[...] = wu_f[slot].astype(BF16)
        wd_s[...] = wd_f[slot].astype(BF16)

        @pl.when(next_ref[b] >= 0)
        def _():
            for cp in weight_copies(next_ref[b], 1 - slot):
                cp.start()

    @pl.when(bv_ref[b] > 0)
    def _():
        x = jnp.concatenate([xs_ref[pl.ds(s, MOE_R, stride=ROW_SUB), :] for s in range(ROW_SUB)], axis=1)
        xb = x.astype(BF16)
        g = jnp.dot(xb, wg_s[...], preferred_element_type=F32)
        u = jnp.dot(xb, wu_s[...], preferred_element_type=F32)
        a = (g * _sigmoid(g) * u).astype(BF16)
        o = jnp.dot(a, wd_s[...], preferred_element_type=F32)
        for s in range(ROW_SUB):
            o_ref[pl.ds(s, MOE_R, stride=ROW_SUB), :] = o[:, s * LANES:(s + 1) * LANES]

    @pl.when(bv_ref[b] == 0)
    def _():
        o_ref[...] = jnp.zeros_like(o_ref)


def _experts(blk_e, blk_valid, xs, w_g, w_u, w_d):
    idx = np.arange(MOE_BLOCKS)
    used = blk_valid > 0
    first = used & ((idx == 0) | (blk_e != jnp.roll(blk_e, 1)))
    ordinal = jnp.sum(first[None, :] & (idx[None, :] <= idx[:, None]), axis=1) - 1
    later_first = jnp.where(first[None, :] & (idx[None, :] > idx[:, None]), idx[None, :], MOE_BLOCKS)
    nxt = jnp.min(later_first, axis=1)
    next_e = jnp.where(nxt < MOE_BLOCKS, jnp.sum(jnp.where(idx[None, :] == nxt[:, None], blk_e[None, :], 0), axis=1), -1)
    plan = [blk_e, blk_valid, first.astype(jnp.int32), (ordinal % 2).astype(jnp.int32), next_e.astype(jnp.int32)]

    def rows(b, be, bv, *_):
        return (jnp.where(bv[b] > 0, b, MOE_BLOCKS - 1), 0)

    hbm = pl.BlockSpec(memory_space=pl.ANY)
    return pl.pallas_call(
        _expert_kernel,
        grid_spec=pltpu.PrefetchScalarGridSpec(
            num_scalar_prefetch=len(plan),
            grid=(MOE_BLOCKS,),
            in_specs=[pl.BlockSpec((MOE_R * ROW_SUB, LANES), rows), hbm, hbm, hbm],
            out_specs=pl.BlockSpec((MOE_R * ROW_SUB, LANES), lambda b, *_: (b, 0)),
            scratch_shapes=[pltpu.VMEM((2, D_MODEL, D_EXPERT), F32),
                            pltpu.VMEM((2, D_MODEL, D_EXPERT), F32),
                            pltpu.VMEM((2, D_EXPERT, D_MODEL), F32),
                            pltpu.VMEM((D_MODEL, D_EXPERT), BF16),
                            pltpu.VMEM((D_MODEL, D_EXPERT), BF16),
                            pltpu.VMEM((D_EXPERT, D_MODEL), BF16),
                            pltpu.SemaphoreType.DMA((2,))]),
        out_shape=jax.ShapeDtypeStruct((MOE_ROWS * ROW_SUB, LANES), F32),
        compiler_params=pltpu.CompilerParams(dimension_semantics=("arbitrary",), vmem_limit_bytes=VMEM_LIMIT),
        name="moe_experts",
    )(*plan, xs, w_g, w_u, w_d)


PIECE = SUBLANES
SEG_ROWS = 2 * TM + N_EXPERTS * (PIECE - 1)


def _combine_kernel(tb_ref, tc_ref, pos_ref, o_hbm, x2_ref, ri_ref, gate_ref, gf_ref, yp_ref, ys_ref,
                    seg, pair, sem):
    i = pl.program_id(0)
    slot = i % 2
    piece_rows = PIECE * ROW_SUB

    def piece_copy(src_row, dst_row, slot_):
        return pltpu.make_async_copy(
            o_hbm.at[pl.ds(pl.multiple_of(src_row * ROW_SUB, ROW_SUB), piece_rows), :],
            seg.at[pl.ds(pl.multiple_of(dst_row * ROW_SUB, ROW_SUB), piece_rows), :], sem.at[slot_])

    def pieces_of(tile, e):
        return (tc_ref[tile * N_EXPERTS + e] + (PIECE - 1)) // PIECE

    def issue_tile(tile, slot_):
        off = slot_ * SEG_ROWS
        for e in range(N_EXPERTS):
            n = pieces_of(tile, e)
            src0 = tb_ref[tile * N_EXPERTS + e]

            def piece(j, carry, src0=src0, off=off, e=e):
                piece_copy(src0 + j * PIECE, off + j * PIECE, slot_).start(priority=e % 2)
                return carry

            lax.fori_loop(0, n, piece, 0)
            off = off + n * PIECE

    @pl.when(i == 0)
    def _():
        issue_tile(0, 0)

    @pl.when(i + 1 < N_TILES)
    def _():
        issue_tile(i + 1, 1 - slot)

    n_pieces = sum(pieces_of(i, e) for e in range(N_EXPERTS))

    def wait_piece(j, carry):
        piece_copy(0, 0, slot).wait()
        return carry

    lax.fori_loop(0, n_pieces, wait_piece, 0)

    def regroup(t, carry):
        for k in range(2):
            src = pl.multiple_of((slot * SEG_ROWS + pos_ref[k * N_TOK + i * TM + t]) * ROW_SUB, ROW_SUB)
            pair[pl.ds(pl.multiple_of((k * TM + t) * ROW_SUB, ROW_SUB), ROW_SUB), :] = seg[pl.ds(src, ROW_SUB), :]
        return carry

    lax.fori_loop(0, TM, regroup, 0, unroll=8)

    ri = ri_ref[...]
    g1 = ri[:, 2:3]
    g2 = ri[:, 3:4]
    tile_rows = TM * ROW_SUB
    moe = jnp.concatenate(
        [g1 * pair[pl.ds(s, TM, stride=ROW_SUB), :] + g2 * pair[pl.ds(tile_rows + s, TM, stride=ROW_SUB), :]
         for s in range(ROW_SUB)], axis=1)
    x3 = x2_ref[...] + _per_chunk(gate_ref) * moe
    y = _rmsnorm(x3, gf_ref[...])

    @pl.when(i < P_TILES)
    def _():
        yp_ref[0] = y

    @pl.when(i >= P_TILES)
    def _():
        ys_ref[...] = y.reshape(CH_PER_TILE, DEC_SEQ, D_MODEL)


def _combine(tile_base, tile_cnt, pos_local, o_rows, x2, rinfo, modc, g_final):
    xp_spec, xs_spec = _xp_spec(), _xs_spec()
    return pl.pallas_call(
        _combine_kernel,
        grid_spec=pltpu.PrefetchScalarGridSpec(
            num_scalar_prefetch=3,
            grid=(N_TILES,),
            in_specs=[pl.BlockSpec(memory_space=pl.ANY),
                      pl.BlockSpec((TM, D_MODEL), lambda i, *_: (i, 0)),
                      pl.BlockSpec((TM, LANES), lambda i, *_: (i, 0)),
                      pl.BlockSpec((CH_PER_TILE, 1, D_MODEL), lambda i, *_: (i, 0, 5)),
                      pl.BlockSpec((1, D_MODEL), lambda i, *_: (0, 0))],
            out_specs=[pl.BlockSpec(xp_spec.block_shape, lambda i, *_: xp_spec.index_map(i)),
                       pl.BlockSpec(xs_spec.block_shape, lambda i, *_: xs_spec.index_map(i))],
            scratch_shapes=[pltpu.VMEM((2 * SEG_ROWS * ROW_SUB, LANES), F32),
                            pltpu.VMEM((2 * TM * ROW_SUB, LANES), F32),
                            pltpu.SemaphoreType.DMA((2,))]),
        out_shape=[jax.ShapeDtypeStruct((BATCH, SEQ, D_MODEL), F32),
                   jax.ShapeDtypeStruct((DEC_BATCH, DEC_SEQ, D_MODEL), F32)],
        compiler_params=pltpu.CompilerParams(dimension_semantics=("arbitrary",), vmem_limit_bytes=VMEM_LIMIT),
        name="moe_combine",
    )(tile_base, tile_cnt, pos_local, o_rows, x2, rinfo, modc, g_final)


def _moe_plan(rt, cnt):
    experts = np.arange(N_EXPERTS)
    tile_cnt = cnt.reshape(N_TILES, SUBLANES, LANES)[:, 0, :N_EXPERTS]
    counts = jnp.sum(tile_cnt, axis=0)
    padded = jnp.ceil(counts / MOE_R) * MOE_R
    pad_end = jnp.sum(padded[:, None] * (experts[:, None] <= experts[None, :]), axis=0)
    base = pad_end - padded
    tiles = np.arange(N_TILES)
    earlier_tiles = (tiles[None, :, None] < tiles[:, None, None])
    tile_base = base[None, :] + jnp.sum(tile_cnt[None, :, :] * earlier_tiles, axis=1)
    eid = rt[0:2].reshape(1, 2, N_TILES, TM)
    rank = rt[4:6].reshape(2, N_TILES, TM)
    pick = eid == experts.astype(np.float32).reshape(N_EXPERTS, 1, 1, 1)
    dest = jnp.sum(jnp.where(pick, tile_base.T[:, None, :, None], 0.0), axis=0) + rank
    group_rows = jnp.ceil(tile_cnt / PIECE) * PIECE
    group_off = jnp.sum(group_rows[:, :, None] * (experts[:, None] < experts[None, :]), axis=1)
    pos_local = jnp.sum(jnp.where(pick, group_off.T[:, None, :, None], 0.0), axis=0) + rank
    blk_start = (np.arange(MOE_BLOCKS) * MOE_R).astype(np.float32)
    blk_e = jnp.minimum(jnp.sum((blk_start[:, None] >= pad_end[None, :]).astype(F32), axis=1), N_EXPERTS - 1)
    mine = blk_e[:, None] == experts.astype(np.float32)[None, :]
    blk_fill = jnp.sum(jnp.where(mine, (counts + base)[None, :], 0.0), axis=1) - blk_start
    blk_valid = jnp.clip(blk_fill, 0, MOE_R)
    partial = jnp.where(counts != padded, jnp.floor((base + counts) / MOE_R), -1.0)
    zblk = jnp.concatenate([partial, pad_end[-1:] / MOE_R])
    as_int = lambda a: a.reshape(-1).astype(jnp.int32)
    return (as_int(dest), as_int(blk_e), as_int(blk_valid), as_int(zblk),
            as_int(tile_base), as_int(tile_cnt), as_int(pos_local))


def kernel(x_prompt, x_sample, c_prompt, c_sample, cache_win_k, cache_win_v, state_conv, state_C, state_n, state_m, w_ada, b_ada, g_norm_mix, g_norm_ffn, w_in, attn_sinks, conv_w, conv_b, b_igate, b_fgate, g_mhnorm, w_out, w_router_group, b_router_group, w_router_expert, b_router_expert, w_exp_gate, w_exp_up, w_exp_down, g_final):
    l = 0
    mod = _modulation(jnp.concatenate([c_prompt, c_sample], axis=0), w_ada[l], b_ada[l])
    chunk_stream = np.concatenate([np.repeat(np.arange(BATCH), SEQ // CHUNK), BATCH + np.arange(DEC_BATCH)])
    modc = mod[chunk_stream].reshape(N_CHUNKS, 1, 6 * D_MODEL)

    wi = w_in[l]
    s_q, s_k, s_v, s_qkm, s_vm, s_ig, s_fg = 0, 512, 640, 768, 1792, 2304, 2308
    s_om = 2312
    w_in_r = jnp.concatenate(
        [wi[:, s_q:s_qkm], wi[:, s_qkm:s_vm], wi[:, s_vm:s_ig], wi[:, s_om:], wi[:, s_ig:s_om],
         jnp.zeros((D_MODEL, LANES - 2 * M_HEADS), F32)], axis=1).astype(BF16)
    q, kv, qkm, vm, om, gates = _inproj(x_prompt, x_sample, modc, g_norm_mix[l].reshape(1, -1), w_in_r)

    cache_kv = jnp.concatenate([cache_win_k[l].reshape(DEC_BATCH * WINDOW, A_KV_WIDTH),
                                cache_win_v[l].reshape(DEC_BATCH * WINDOW, A_KV_WIDTH)], axis=1)
    att_p, att_s = _attention(attn_sinks[l], q, kv, cache_kv)

    gbias = jnp.concatenate([b_igate[l], b_fgate[l], jnp.zeros((LANES - 2 * M_HEADS,), F32)]).reshape(1, LANES)
    common = (qkm, vm, om, gates, conv_w[l], conv_b[l].reshape(1, -1), gbias, g_mhnorm[l].reshape(1, -1))
    zeros_p = (jnp.zeros((BATCH, SUBLANES, 2 * M_WIDTH), F32),
               jnp.zeros((BATCH, M_HEADS, M_HEAD_DIM, M_HEAD_DIM), F32),
               jnp.zeros((BATCH, M_HEADS, M_HEAD_DIM), F32),
               jnp.zeros((BATCH, 1, LANES), F32))
    LP = 256
    hm_p, C_p, n_p, m_p = _mlstm(*common, *zeros_p, nb=BATCH, L=LP, nc=SEQ // LP, ns=1, row_off=0,
                                 name="mlstm_prompt")
    carry_s = jnp.concatenate([jnp.zeros((DEC_BATCH, SUBLANES - (CONV_WIDTH - 1), 2 * M_WIDTH), F32),
                               state_conv[l]], axis=1)
    m0_s = jnp.pad(state_m[l], ((0, 0), (0, LANES - M_HEADS))).reshape(DEC_BATCH, 1, LANES)
    hm_s, C_s, n_s, m_s = _mlstm(*common, carry_s, state_C[l], state_n[l], m0_s,
                                 nb=DEC_BATCH, L=DEC_SEQ, nc=1, ns=4, row_off=N_P, name="mlstm_sample")

    w_router = jnp.concatenate([w_router_group[l], w_router_expert[l],
                                jnp.zeros((D_MODEL, LANES - N_GROUPS - N_EXPERTS), F32)], axis=1)
    w_router_hi = w_router.astype(BF16)
    w_router = jnp.stack([w_router_hi, (w_router - w_router_hi.astype(F32)).astype(BF16)])
    b_router =jnp.concatenate([b_router_group[l], b_router_expert[l],
                                jnp.zeros((LANES - N_GROUPS - N_EXPERTS,), F32)]).reshape(1, LANES)
    x2, h2t, rinfo, rt, cnt = _outproj(x_prompt, x_sample, att_p, att_s, hm_p, hm_s, modc,
                                       g_norm_ffn[l].reshape(1, -1), w_out[l].astype(BF16), w_router, b_router)

    dest, blk_e, blk_valid, zblk, tile_base, tile_cnt, pos_local = _moe_plan(rt, cnt)
    xs_rows = _dispatch(dest, zblk, h2t)
    o_rows = _experts(blk_e, blk_valid, xs_rows, w_exp_gate[l], w_exp_up[l], w_exp_down[l])
    y_prompt, y_sample = _combine(tile_base, tile_cnt, pos_local, o_rows, x2, rinfo, modc, g_final.reshape(1, -1))

    kv_p = kv.reshape(N_TOK // WINDOW, WINDOW, 2 * A_KV_WIDTH)[SEQ // WINDOW - 1:N_P // WINDOW:SEQ // WINDOW]
    win_k_p = kv_p[..., :A_KV_WIDTH].reshape(1, BATCH, WINDOW, A_KV_HEADS, A_HEAD_DIM)
    win_v_p = kv_p[..., A_KV_WIDTH:].reshape(1, BATCH, WINDOW, A_KV_HEADS, A_HEAD_DIM)
    kv_s = kv[N_P:].reshape(DEC_BATCH, DEC_SEQ, 2 * A_KV_WIDTH)
    win_k_s = jnp.concatenate([cache_win_k[l][:, DEC_SEQ:],
                               kv_s[..., :A_KV_WIDTH].reshape(DEC_BATCH, DEC_SEQ, A_KV_HEADS, A_HEAD_DIM)], axis=1)[None]
    win_v_s = jnp.concatenate([cache_win_v[l][:, DEC_SEQ:],
                               kv_s[..., A_KV_WIDTH:].reshape(DEC_BATCH, DEC_SEQ, A_KV_HEADS, A_HEAD_DIM)], axis=1)[None]
    qkm_c = qkm.reshape(N_CHUNKS, CHUNK, 2 * M_WIDTH)
    tail = slice(CHUNK - (CONV_WIDTH - 1), CHUNK)
    conv_p = qkm_c[SEQ // CHUNK - 1:N_P // CHUNK:SEQ // CHUNK, tail][None]
    conv_s = qkm_c[N_P // CHUNK:, tail][None]
    return (y_prompt, y_sample,
            win_k_p, win_v_p, conv_p, C_p[None], n_p[None], m_p[:, 0, :M_HEADS][None],
            win_k_s, win_v_s, conv_s, C_s[None], n_s[None], m_s[:, 0, :M_HEADS][None])
```

```python
import functools

import numpy as np
import jax
import jax.numpy as jnp
from jax import lax
from jax.experimental import pallas as pl
from jax.experimental.pallas import tpu as pltpu

F32 = jnp.float32
BF16 = jnp.bfloat16
HIGHEST = lax.Precision.HIGHEST

LANES = 128
SUBLANES = 8

D_MODEL = 1024
BATCH = 8
SEQ = 2048
DEC_BATCH = 32
DEC_SEQ = 64
PAST_LEN = 4096
CHUNK = 64
A_WIDTH = 512
A_HEAD_DIM = 64
A_HEADS = 8
A_KV_HEADS = 2
A_GROUP = 4
A_KV_WIDTH = 128
WINDOW = 128
ROT_DIM = 16
ROPE_THETA = 500000.0
M_WIDTH = 512
M_HEADS = 4
M_HEAD_DIM = 128
CONV_WIDTH = 4
N_GROUPS = 4
EXPERTS_PER_GROUP = 8
N_EXPERTS = 32
D_EXPERT = 512
EPS = 1e-6

N_P = BATCH * SEQ
N_S = DEC_BATCH * DEC_SEQ
N_TOK = N_P + N_S
N_CHUNKS = N_TOK // CHUNK
TM = 512
N_TILES = N_TOK // TM
P_TILES = N_P // TM
CH_PER_TILE = TM // CHUNK
ROW_SUB = D_MODEL // LANES
N_ASSIGN = 2 * N_TOK
MOE_R = 512
MOE_BLOCKS = N_ASSIGN // MOE_R + N_EXPERTS
MOE_ROWS = MOE_BLOCKS * MOE_R
C_Q, C_K, C_V, C_QKM, C_VM, C_OM, C_G = 0, 512, 640, 768, 1792, 2304, 2816
D_IN_PAD = 2944
VMEM_LIMIT = 48 * 1024 * 1024


def _sigmoid(x):
    return 1.0 / (1.0 + jnp.exp(-x))


def _mod_kernel(c_ref, w_ref, b_ref, o_ref):
    c = c_ref[...]
    s = c * _sigmoid(c)
    o_ref[...] = jnp.dot(s, w_ref[...], preferred_element_type=F32, precision=HIGHEST) + b_ref[...]


def _modulation(c_all, w_ada, b_ada):
    n = c_all.shape[0]
    bn = 512
    return pl.pallas_call(
        _mod_kernel,
        grid=(6 * D_MODEL // bn,),
        in_specs=[pl.BlockSpec((n, D_MODEL), lambda j: (0, 0)),
                  pl.BlockSpec((D_MODEL, bn), lambda j: (0, j)),
                  pl.BlockSpec((1, bn), lambda j: (0, j))],
        out_specs=pl.BlockSpec((n, bn), lambda j: (0, j)),
        out_shape=jax.ShapeDtypeStruct((n, 6 * D_MODEL), F32),
        name="modulation",
    )(c_all, w_ada, b_ada.reshape(1, -1))


def _xp_spec():
    def idx(i):
        t = jnp.minimum(i, P_TILES - 1)
        return (t // (SEQ // TM), t % (SEQ // TM), 0)
    return pl.BlockSpec((1, TM, D_MODEL), idx)


def _xs_spec():
    return pl.BlockSpec((CH_PER_TILE, DEC_SEQ, D_MODEL), lambda i: (jnp.maximum(i - P_TILES, 0), 0, 0))


def _mod_spec(comp):
    return pl.BlockSpec((CH_PER_TILE, 1, D_MODEL), lambda i: (i, 0, comp))


def _load_x(xp_ref, xs_ref, xbuf):
    i = pl.program_id(0)

    @pl.when(i < P_TILES)
    def _():
        xbuf[...] = xp_ref[0]

    @pl.when(i >= P_TILES)
    def _():
        xbuf[...] = xs_ref[...].reshape(TM, D_MODEL)

    return xbuf[...]


def _per_chunk(m_ref):
    m = m_ref[...]
    return jnp.broadcast_to(m, (CH_PER_TILE, CHUNK, D_MODEL)).reshape(TM, D_MODEL)


def _rmsnorm(x, g):
    return x * lax.rsqrt(jnp.mean(x * x, axis=-1, keepdims=True) + EPS) * g


def _rope(x, cos, sa, sb):
    n = x.shape[1]
    rep = n // LANES
    if rep > 1:
        cos = jnp.concatenate([cos] * rep, axis=1)
        sa = jnp.concatenate([sa] * rep, axis=1)
        sb = jnp.concatenate([sb] * rep, axis=1)
    return x * cos + pltpu.roll(x, n - ROT_DIM // 2, 1) * sa + pltpu.roll(x, ROT_DIM // 2, 1) * sb


def _inproj_kernel(xp_ref, xs_ref, sh_ref, sc_ref, g_ref, w_ref, cos_ref, sa_ref, sb_ref,
                   q_ref, kv_ref, qkm_ref, vm_ref, om_ref, gt_ref, xbuf):
    x = _load_x(xp_ref, xs_ref, xbuf)
    h = _rmsnorm(x, g_ref[...]) * (1.0 + _per_chunk(sc_ref)) + _per_chunk(sh_ref)
    hb = h.astype(BF16)

    def proj(a, b):
        return jnp.dot(hb, w_ref[:, a:b], preferred_element_type=F32)

    cos, sa, sb = cos_ref[...], sa_ref[...], sb_ref[...]
    q_ref[...] = _rope(proj(C_Q, C_K), cos, sa, sb).astype(BF16)
    kv_ref[:, :A_KV_WIDTH] = _rope(proj(C_K, C_V), cos, sa, sb)
    kv_ref[:, A_KV_WIDTH:] = proj(C_V, C_QKM)
    qkm_ref[...] = proj(C_QKM, C_VM)
    vm_ref[...] = proj(C_VM, C_OM).astype(BF16)
    om_ref[...] = proj(C_OM, C_G)
    gt_ref[...] = proj(C_G, D_IN_PAD)


def _rope_tables():
    pos = np.concatenate([np.arange(SEQ), np.tile(PAST_LEN + np.arange(DEC_SEQ), CH_PER_TILE)]).astype(np.float64)
    inv_freq = ROPE_THETA ** (-np.arange(0, ROT_DIM, 2, dtype=np.float64) / ROT_DIM)
    lane = np.arange(LANES)
    hl = lane % A_HEAD_DIM
    ang = pos[:, None] * inv_freq[hl % (ROT_DIM // 2)][None, :]
    rot = (hl < ROT_DIM)[None, :]
    lo = (hl < ROT_DIM // 2)[None, :]
    cos = np.where(rot, np.cos(ang), 1.0)
    sa = np.where(lo, -np.sin(ang), 0.0)
    sb = np.where(rot & ~lo, np.sin(ang), 0.0)
    return [jnp.asarray(t, F32) for t in (cos, sa, sb)]


def _inproj(x_prompt, x_sample, modc, g_mix, w_in_r):
    cos, sa, sb = _rope_tables()
    tab_spec = pl.BlockSpec((TM, LANES), lambda i: (jnp.where(i < P_TILES, i % (SEQ // TM), SEQ // TM), 0))

    def out(n, dtype=F32):
        return pl.BlockSpec((TM, n), lambda i: (i, 0)), jax.ShapeDtypeStruct((N_TOK, n), dtype)

    outs = [out(A_WIDTH, BF16), out(2 * A_KV_WIDTH), out(2 * M_WIDTH), out(M_WIDTH, BF16), out(M_WIDTH), out(LANES)]
    return pl.pallas_call(
        _inproj_kernel,
        grid=(N_TILES,),
        in_specs=[_xp_spec(), _xs_spec(), _mod_spec(0), _mod_spec(1),
                  pl.BlockSpec((1, D_MODEL), lambda i: (0, 0)),
                  pl.BlockSpec((D_MODEL, D_IN_PAD), lambda i: (0, 0)),
                  tab_spec, tab_spec, tab_spec],
        out_specs=[o[0] for o in outs],
        out_shape=[o[1] for o in outs],
        scratch_shapes=[pltpu.VMEM((TM, D_MODEL), F32)],
        compiler_params=pltpu.CompilerParams(vmem_limit_bytes=VMEM_LIMIT),
        name="inproj",
    )(x_prompt, x_sample, modc, modc, g_mix, w_in_r, cos, sa, sb)


def _attn_chunk_kernel(sink_ref, q_ref, prev_ref, cur_ref, o_ref):
    q = q_ref[...] * (A_HEAD_DIM ** -0.5)
    kv = jnp.concatenate([prev_ref[...], cur_ref[...]], axis=0)
    for g in range(A_KV_HEADS):
        kg = kv[:, g * A_HEAD_DIM:(g + 1) * A_HEAD_DIM].astype(BF16)
        vg = kv[:, A_KV_WIDTH + g * A_HEAD_DIM:A_KV_WIDTH + (g + 1) * A_HEAD_DIM].astype(BF16)
        heads = [g * A_GROUP + i for i in range(A_GROUP)]
        qc = jnp.concatenate([q[:, h * A_HEAD_DIM:(h + 1) * A_HEAD_DIM] for h in heads], axis=0).astype(BF16)
        snk = jnp.concatenate([jnp.full((CHUNK, 1), sink_ref[h], F32) for h in heads], axis=0)
        s = lax.dot_general(qc, kg, (((1,), (1,)), ((), ())), preferred_element_type=F32)
        mx = jnp.maximum(jnp.max(s, axis=-1, keepdims=True), snk)
        p = jnp.exp(s - mx)
        den = jnp.sum(p, axis=-1, keepdims=True) + jnp.exp(snk - mx)
        o = jnp.dot(p.astype(BF16), vg, preferred_element_type=F32) / den
        for i, h in enumerate(heads):
            o_ref[:, h * A_HEAD_DIM:(h + 1) * A_HEAD_DIM] = o[i * CHUNK:(i + 1) * CHUNK]


ATT_TQ = 256
ATT_QB = 2 * CHUNK


def _attn_band_kernel(sink_ref, q_ref, prev_ref, cur_ref, o_ref, att_t):
    nk = ATT_QB + WINDOW
    q = (q_ref[...] * (A_HEAD_DIM ** -0.5)).astype(BF16)
    kv = jnp.concatenate([prev_ref[...], cur_ref[...]], axis=0)
    k2 = kv[:, :A_KV_WIDTH]
    k2r = pltpu.roll(k2, A_HEAD_DIM, 1)
    low = lax.broadcasted_iota(jnp.int32, k2.shape, 1) < A_HEAD_DIM
    k_placed = {(0, 0): jnp.where(low, k2, 0.0), (0, 1): jnp.where(low, 0.0, k2r),
                (1, 0): jnp.where(low, k2r, 0.0), (1, 1): jnp.where(low, 0.0, k2)}
    k_placed = {key: val.astype(BF16) for key, val in k_placed.items()}
    v_t = kv[:, A_KV_WIDTH:].T.astype(BF16)
    key_chunk = lax.broadcasted_iota(jnp.int32, (nk, ATT_QB), 0) // CHUNK
    q_chunk = lax.broadcasted_iota(jnp.int32, (nk, ATT_QB), 1) // CHUNK
    band = (key_chunk >= q_chunk) & (key_chunk <= q_chunk + WINDOW // CHUNK)
    has_history = pl.program_id(1) > 0
    for blk in range(ATT_TQ // ATT_QB):
        keys = slice(blk * ATT_QB, blk * ATT_QB + nk)
        qrows = slice(blk * ATT_QB, (blk + 1) * ATT_QB)
        valid = band & ((key_chunk >= WINDOW // CHUNK) | has_history) if blk == 0 else band
        for h in range(A_HEADS):
            g = h // A_GROUP
            s_t = lax.dot_general(k_placed[(g, h % 2)][keys], q[qrows, (h // 2) * LANES:(h // 2 + 1) * LANES],
                                  (((1,), (1,)), ((), ())), preferred_element_type=F32)
            s_t = jnp.where(valid, s_t, -jnp.inf)
            snk = sink_ref[h]
            mx = jnp.maximum(jnp.max(s_t, axis=0, keepdims=True), snk)
            p_t = jnp.exp(s_t - mx)
            den = jnp.sum(p_t, axis=0, keepdims=True) + jnp.exp(snk - mx)
            o_t = jnp.dot(v_t[:, keys], p_t.astype(BF16), preferred_element_type=F32)
            att_t[h * A_HEAD_DIM:(h + 1) * A_HEAD_DIM, qrows] = o_t[g * A_HEAD_DIM:(g + 1) * A_HEAD_DIM] / den
    o_ref[...] = att_t[...].T.astype(BF16)


def _attention(sinks, q, kv, cache_kv):
    smem = pl.BlockSpec(memory_space=pltpu.SMEM)
    tq = ATT_TQ
    nq = SEQ // tq
    att = pl.pallas_call(
        _attn_band_kernel,
        grid=(BATCH, nq),
        scratch_shapes=[pltpu.VMEM((A_WIDTH, ATT_TQ), F32)],
        in_specs=[smem,
                  pl.BlockSpec((tq, A_WIDTH), lambda b, j: (b * nq + j, 0)),
                  pl.BlockSpec((WINDOW, 2 * A_KV_WIDTH),
                               lambda b, j: (jnp.maximum((b * nq + j) * (tq // WINDOW) - 1, 0), 0)),
                  pl.BlockSpec((tq, 2 * A_KV_WIDTH), lambda b, j: (b * nq + j, 0))],
        out_specs=pl.BlockSpec((tq, A_WIDTH), lambda b, j: (b * nq + j, 0)),
        out_shape=jax.ShapeDtypeStruct((N_P, A_WIDTH), BF16),
        name="attn_prompt",
    )(sinks, q, kv, kv)
    off = N_P // DEC_SEQ
    att_s = pl.pallas_call(
        _attn_chunk_kernel,
        grid=(DEC_BATCH,),
        in_specs=[smem,
                  pl.BlockSpec((DEC_SEQ, A_WIDTH), lambda b: (off + b, 0)),
                  pl.BlockSpec((WINDOW, 2 * A_KV_WIDTH), lambda b: (b, 0)),
                  pl.BlockSpec((DEC_SEQ, 2 * A_KV_WIDTH), lambda b: (off + b, 0))],
        out_specs=pl.BlockSpec((DEC_SEQ, A_WIDTH), lambda b: (b, 0)),
        out_shape=jax.ShapeDtypeStruct((N_S, A_WIDTH), F32),
        name="attn_sample",
    )(sinks, q, cache_kv, kv)
    return att, att_s


def _conv4(x, w, b):
    x1 = pltpu.roll(x, 1, 0)
    near = b + x * w[3:4] + x1 * w[2:3]
    far = x * w[1:2] + x1 * w[0:1]
    return near + pltpu.roll(far, 2, 0)


def _mlstm_kernel(qkm_ref, vm_ref, om_ref, gt_ref, cw_ref, cb_ref, gb_ref, gmh_ref,
                  carry0_ref, c0_ref, n0_ref, m0_ref,
                  hm_ref, cout_ref, nout_ref, mout_ref, c_s, n_s, m_s, carry_s, *, L, NS):
    c = pl.program_id(1)

    @pl.when(c == 0)
    def _():
        c_s[...] = c0_ref[...]
        n_s[...] = n0_ref[...]
        m_s[...] = m0_ref[...]
        carry_s[...] = carry0_ref[...]

    w = cw_ref[...]
    b = cb_ref[...]
    gmh = gmh_ref[...]
    lane = lax.broadcasted_iota(jnp.int32, (L, LANES), 1)
    causal = lax.broadcasted_iota(jnp.int32, (L, L), 0) >= lax.broadcasted_iota(jnp.int32, (L, L), 1)
    ones_causal = causal.astype(F32)
    sel = (lax.broadcasted_iota(jnp.int32, (SUBLANES, LANES), 0)
           == lax.broadcasted_iota(jnp.int32, (SUBLANES, LANES), 1)).astype(F32)
    lane1 = lax.broadcasted_iota(jnp.int32, (1, LANES), 1)
    for st in range(NS):
        _mlstm_stream(st, qkm_ref, vm_ref, om_ref, gt_ref, gb_ref, hm_ref, c_s, n_s, m_s, carry_s,
                      w, b, gmh, lane, causal, ones_causal, sel, lane1, L)

    @pl.when(c == pl.num_programs(1) - 1)
    def _():
        cout_ref[...] = c_s[...]
        nout_ref[...] = n_s[...]
        mout_ref[...] = m_s[...]


def _mlstm_stream(st, qkm_ref, vm_ref, om_ref, gt_ref, gb_ref, hm_ref, c_s, n_s, m_s, carry_s,
                  w, b, gmh, lane, causal, ones_causal, sel, lane1, L):
    x = qkm_ref[st]
    y = _conv4(x, w, b)
    y8 = _conv4(jnp.concatenate([carry_s[st], x[:SUBLANES]], axis=0), w, b)
    y = jnp.concatenate([y8[SUBLANES:], y[SUBLANES:]], axis=0)
    carry_s[st] = x[L - SUBLANES:]
    a = y * _sigmoid(y)
    qa = a[:, :M_WIDTH] * (M_HEAD_DIM ** -0.5)
    ka = a[:, M_WIDTH:]
    v = vm_ref[st]
    om = om_ref[st]

    z = gt_ref[st] + gb_ref[...]
    f_log = jnp.minimum(z, 0.0) - jnp.log1p(jnp.exp(-jnp.abs(z)))
    val = jnp.where(lane < M_HEADS, z, f_log)
    cum = jnp.dot(ones_causal, val, preferred_element_type=F32, precision=HIGHEST)
    G = jnp.where(lane < M_HEADS, val, cum)
    GT = lax.dot_general(sel, G, (((1,), (1,)), ((), ())), preferred_element_type=F32, precision=HIGHEST)

    m_row = m_s[st]
    outs = []
    for h in range(M_HEADS):
        a_col = G[:, M_HEADS + h:M_HEADS + h + 1]
        i_col = G[:, h:h + 1]
        i_row = GT[h:h + 1, :]
        b_row = GT[M_HEADS + h:M_HEADS + h + 1, :]
        m_prev = m_row[:, h:h + 1]
        dm = jnp.where(causal, a_col - b_row + i_row, -jnp.inf)
        inter = a_col + m_prev
        m_t = jnp.maximum(inter, jnp.max(dm, axis=-1, keepdims=True))
        w_intra = jnp.exp(dm - m_t)
        w_inter = jnp.exp(inter - m_t)
        sl = slice(h * M_HEAD_DIM, (h + 1) * M_HEAD_DIM)
        qh, kh, vh = qa[:, sl], ka[:, sl], v[:, sl]
        qb = qh.astype(BF16)
        scores = lax.dot_general(qb, kh.astype(BF16), (((1,), (1,)), ((), ())), preferred_element_type=F32) * w_intra
        ch = c_s[st, h]
        nh = n_s[st, h:h + 1, :]
        num = (jnp.dot(scores.astype(BF16), vh.astype(BF16), preferred_element_type=F32)
               + w_inter * lax.dot_general(qb, ch.astype(BF16), (((1,), (1,)), ((), ())), preferred_element_type=F32))
        den = jnp.sum(scores, axis=-1, keepdims=True) + w_inter * jnp.sum(qh * nh, axis=-1, keepdims=True)
        hv = num / jnp.maximum(jnp.abs(den), jnp.exp(-m_t))
        hn = hv * lax.rsqrt(jnp.mean(hv * hv, axis=-1, keepdims=True) + EPS)
        outs.append(hn * gmh[:, sl] * _sigmoid(om[:, sl]))
        m_new = m_t[L - 1:L, :]
        a_last = a_col[L - 1:L, :]
        w_end = jnp.exp(a_last - a_col + i_col - m_new)
        decay = jnp.exp(a_last + m_prev - m_new)
        kw = kh * w_end
        c_s[st, h] = decay * ch + lax.dot_general(vh.astype(BF16), kw.astype(BF16), (((0,), (0,)), ((), ())),
                                                  preferred_element_type=F32)
        n_s[st, h:h + 1, :] = decay * nh + jnp.sum(kw, axis=0, keepdims=True)
        m_row = jnp.where(lane1 == h, m_new, m_row)
    m_s[st] = m_row
    hm_ref[st] = jnp.concatenate(outs, axis=1).astype(BF16)


def _mlstm(qkm, vm, om, gates, conv_w, conv_b, gbias, g_mh, carry0, c0, n0, m0, *, nb, L, nc, ns, row_off, name):
    S = nc * L
    first = row_off // S // ns

    def tok(arr):
        n = arr.shape[-1]
        return arr.reshape(N_TOK // S, S, n), pl.BlockSpec((ns, L, n), lambda i, c: (first + i, c, 0))

    def const(shape):
        return pl.BlockSpec(shape, lambda i, c: (0,) * len(shape))

    def per_stream(shape):
        return pl.BlockSpec((ns,) + shape, lambda i, c: (i,) + (0,) * len(shape))

    toks = [tok(a) for a in (qkm, vm, om, gates)]
    state_specs = [per_stream((SUBLANES, 2 * M_WIDTH)), per_stream((M_HEADS, M_HEAD_DIM, M_HEAD_DIM)),
                   per_stream((M_HEADS, M_HEAD_DIM)), per_stream((1, LANES))]
    in_specs = ([t[1] for t in toks]
                + [const((CONV_WIDTH, 2 * M_WIDTH)), const((1, 2 * M_WIDTH)), const((1, LANES)), const((1, M_WIDTH))]
                + state_specs)
    args = [t[0] for t in toks] + [conv_w, conv_b, gbias, g_mh, carry0, c0, n0, m0]
    hm, c_out, n_out, m_out = pl.pallas_call(
        functools.partial(_mlstm_kernel, L=L, NS=ns),
        grid=(nb // ns, nc),
        in_specs=in_specs,
        out_specs=[pl.BlockSpec((ns, L, M_WIDTH), lambda i, c: (i, c, 0))] + state_specs[1:],
        out_shape=[jax.ShapeDtypeStruct((nb, S, M_WIDTH), BF16),
                   jax.ShapeDtypeStruct((nb, M_HEADS, M_HEAD_DIM, M_HEAD_DIM), F32),
                   jax.ShapeDtypeStruct((nb, M_HEADS, M_HEAD_DIM), F32),
                   jax.ShapeDtypeStruct((nb, 1, LANES), F32)],
        scratch_shapes=[pltpu.VMEM((ns, M_HEADS, M_HEAD_DIM, M_HEAD_DIM), F32),
                        pltpu.VMEM((ns, M_HEADS, M_HEAD_DIM), F32),
                        pltpu.VMEM((ns, 1, LANES), F32),
                        pltpu.VMEM((ns, SUBLANES, 2 * M_WIDTH), F32)],
        compiler_params=pltpu.CompilerParams(dimension_semantics=("arbitrary", "arbitrary"),
                                             vmem_limit_bytes=VMEM_LIMIT),
        name=name,
    )(*args)
    return hm.reshape(nb * S, M_WIDTH), c_out, n_out, m_out


def _outproj_kernel(xp_ref, xs_ref, attp_ref, atts_ref, hmp_ref, hms_ref, gate_ref, sh_ref, sc_ref, g_ref,
                    w_ref, wr_ref, br_ref, x2_ref, h2_ref, ri_ref, rt_ref, cnt_ref, xbuf, mixbuf):
    x = _load_x(xp_ref, xs_ref, xbuf)
    i = pl.program_id(0)

    @pl.when(i < P_TILES)
    def _():
        mixbuf[:, :A_WIDTH] = attp_ref[...].astype(BF16)
        mixbuf[:, A_WIDTH:] = hmp_ref[...].astype(BF16)

    @pl.when(i >= P_TILES)
    def _():
        mixbuf[:, :A_WIDTH] = atts_ref[...].astype(BF16)
        mixbuf[:, A_WIDTH:] = hms_ref[...].astype(BF16)

    mixed = jnp.dot(mixbuf[...], w_ref[...], preferred_element_type=F32)
    x2 = x + _per_chunk(gate_ref) * mixed
    x2_ref[...] = x2
    h2 = _rmsnorm(x2, g_ref[...]) * (1.0 + _per_chunk(sc_ref)) + _per_chunk(sh_ref)
    for s in range(ROW_SUB):
        h2_ref[pl.ds(s, TM, stride=ROW_SUB), :] = h2[:, s * LANES:(s + 1) * LANES]

    h_hi = h2.astype(BF16)
    h_lo = (h2 - h_hi.astype(F32)).astype(BF16)
    lg = (jnp.dot(h_hi, wr_ref[0], preferred_element_type=F32)
          + jnp.dot(h_lo, wr_ref[0], preferred_element_type=F32)
          + jnp.dot(h_hi, wr_ref[1], preferred_element_type=F32)) + br_ref[...]
    lane = lax.broadcasted_iota(jnp.int32, (TM, LANES), 1)
    lanef = lane.astype(F32)
    ninf = -jnp.inf

    def first_argmax(vals):
        mx = jnp.max(vals, axis=-1, keepdims=True)
        return mx, jnp.min(jnp.where(vals == mx, lanef, float(LANES)), axis=-1, keepdims=True)

    is_grp = lane < N_GROUPS
    gmax, grp = first_argmax(jnp.where(is_grp, lg, ninf))
    p_grp = 1.0 / jnp.sum(jnp.where(is_grp, jnp.exp(lg - gmax), 0.0), axis=-1, keepdims=True)
    base = N_GROUPS + grp * EXPERTS_PER_GROUP
    in_grp = (lanef >= base) & (lanef < base + EXPERTS_PER_GROUP)
    el = jnp.where(in_grp, lg, ninf)
    v1, i1 = first_argmax(el)
    v2, i2 = first_argmax(jnp.where(lanef == i1, ninf, el))
    e = jnp.exp(v2 - v1)
    g1 = p_grp / (1.0 + e)
    g2 = p_grp * e / (1.0 + e)
    e1 = i1 - N_GROUPS
    e2 = i2 - N_GROUPS
    oh1 = lanef == e1
    oh2 = lanef == e2
    oh = jnp.where(oh1 | oh2, 1.0, 0.0)
    earlier = (lax.broadcasted_iota(jnp.int32, (TM, TM), 0) > lax.broadcasted_iota(jnp.int32, (TM, TM), 1))
    before = jnp.dot(earlier.astype(BF16), oh.astype(BF16), preferred_element_type=F32)
    r1 = jnp.sum(jnp.where(oh1, before, 0.0), axis=-1, keepdims=True)
    r2 = jnp.sum(jnp.where(oh2, before, 0.0), axis=-1, keepdims=True)
    cnt_ref[...] = jnp.broadcast_to(jnp.sum(oh, axis=0, keepdims=True), (SUBLANES, LANES))
    cols = (e1, e2, g1, g2, r1, r2)
    ri = jnp.zeros((TM, LANES), F32)
    for k, cval in enumerate(cols):
        ri = jnp.where(lane == k, cval, ri)
    ri_ref[...] = ri
    rt_ref[...] = ri.T[:SUBLANES]


def _outproj(x_prompt, x_sample, att_p, att_s, hm_p, hm_s, modc, g_ffn, w_out_b, w_router, b_router):
    def tok(n):
        return pl.BlockSpec((TM, n), lambda i: (i, 0))

    def tok_p(n):
        return pl.BlockSpec((TM, n), lambda i: (jnp.minimum(i, P_TILES - 1), 0))

    def tok_s(n):
        return pl.BlockSpec((TM, n), lambda i: (jnp.maximum(i - P_TILES, 0), 0))

    def const(shape):
        return pl.BlockSpec(shape, lambda i: (0,) * len(shape))

    return pl.pallas_call(
        _outproj_kernel,
        grid=(N_TILES,),
        in_specs=[_xp_spec(), _xs_spec(), tok_p(A_WIDTH), tok_s(A_WIDTH), tok_p(M_WIDTH), tok_s(M_WIDTH),
                  _mod_spec(2), _mod_spec(3), _mod_spec(4),
                  const((1, D_MODEL)), const((D_MODEL, D_MODEL)), const((2, D_MODEL, LANES)), const((1, LANES))],
        out_specs=[tok(D_MODEL), pl.BlockSpec((TM * ROW_SUB, LANES), lambda i: (i, 0)), tok(LANES),
                   pl.BlockSpec((SUBLANES, TM), lambda i: (0, i)),
                   pl.BlockSpec((SUBLANES, LANES), lambda i: (i, 0))],
        out_shape=[jax.ShapeDtypeStruct((N_TOK, D_MODEL), F32),
                   jax.ShapeDtypeStruct((N_TOK * ROW_SUB, LANES), F32),
                   jax.ShapeDtypeStruct((N_TOK, LANES), F32),
                   jax.ShapeDtypeStruct((SUBLANES, N_TOK), F32),
                   jax.ShapeDtypeStruct((N_TILES * SUBLANES, LANES), F32)],
        scratch_shapes=[pltpu.VMEM((TM, D_MODEL), F32), pltpu.VMEM((TM, D_MODEL), BF16)],
        compiler_params=pltpu.CompilerParams(vmem_limit_bytes=VMEM_LIMIT),
        name="outproj",
    )(x_prompt, x_sample, att_p, att_s, hm_p, hm_s, modc, modc, modc, g_ffn, w_out_b, w_router, b_router)


def _row(ref, r):
    return ref.at[pl.ds(pl.multiple_of(r * ROW_SUB, ROW_SUB), ROW_SUB), :]


def _row_copies(i, t, dest_ref, make):
    tok = i * TM + t
    return [make(k, dest_ref[k * N_TOK + tok]) for k in range(2)]


ZCHUNK = 64


def _dispatch_kernel(dest_ref, zlo_ref, zhi_ref, h_ref, xs_out, zbuf, sem, zsem):
    i = pl.program_id(0)
    chunk_rows = ZCHUNK * ROW_SUB

    @pl.when(i == 0)
    def _():
        zbuf[...] = jnp.zeros_like(zbuf)

        def chunk_copy(c):
            return pltpu.make_async_copy(zbuf, xs_out.at[pl.ds(pl.multiple_of(c * chunk_rows, chunk_rows), chunk_rows), :],
                                         zsem)

        def fill(rng, carry):
            def start(c, cc):
                chunk_copy(c).start()
                return cc

            def wait(c, cc):
                chunk_copy(c).wait()
                return cc

            lax.fori_loop(zlo_ref[rng], zhi_ref[rng], start, 0)
            lax.fori_loop(zlo_ref[rng], zhi_ref[rng], wait, 0)
            return carry

        lax.fori_loop(0, N_EXPERTS + 1, fill, 0)

    def copies(t):
        return _row_copies(i, t, dest_ref, lambda k, d: pltpu.make_async_copy(_row(h_ref, t), _row(xs_out, d), sem))

    def issue(t, carry):
        for k, cp in enumerate(copies(t)):
            cp.start(priority=k)
        return carry

    def wait(t, carry):
        for cp in copies(t):
            cp.wait()
        return carry

    lax.fori_loop(0, TM, issue, 0, unroll=8)
    lax.fori_loop(0, TM, wait, 0, unroll=8)


def _dispatch(dest, zlo, zhi, h2t):
    return pl.pallas_call(
        _dispatch_kernel,
        grid_spec=pltpu.PrefetchScalarGridSpec(
            num_scalar_prefetch=3,
            grid=(N_TILES,),
            in_specs=[pl.BlockSpec((TM * ROW_SUB, LANES), lambda i, *_: (i, 0))],
            out_specs=pl.BlockSpec(memory_space=pl.ANY),
            scratch_shapes=[pltpu.VMEM((ZCHUNK * ROW_SUB, LANES), F32),
                            pltpu.SemaphoreType.DMA, pltpu.SemaphoreType.DMA]),
        out_shape=jax.ShapeDtypeStruct((MOE_ROWS * ROW_SUB, LANES), F32),
        compiler_params=pltpu.CompilerParams(dimension_semantics=("arbitrary",)),
        name="moe_dispatch",
    )(dest, zlo, zhi, h2t)


def _expert_kernel(be_ref, bv_ref, first_ref, slot_ref, next_ref, xs_ref, wg_hbm, wu_hbm, wd_hbm, o_ref,
                   wg_f, wu_f, wd_f, wg_s, wu_s, wd_s, wsem):
    b = pl.program_id(0)

    def weight_copies(e, slot):
        return [pltpu.make_async_copy(src.at[e], dst.at[slot], wsem.at[slot])
                for src, dst in ((wg_hbm, wg_f), (wu_hbm, wu_f), (wd_hbm, wd_f))]

    @pl.when(b == 0)
    def _():
        for cp in weight_copies(be_ref[0], 0):
            cp.start()

    @pl.when(first_ref[b] == 1)
    def _():
        slot = slot_ref[b]
        for cp in weight_copies(be_ref[b], slot):
            cp.wait()
        wg_s[...] = wg_f[slot].astype(BF16)
        wu_s[...] = wu_f[slot].astype(BF16)
        wd_s[...] = wd_f[slot].astype(BF16)

        @pl.when(next_ref[b] >= 0)
        def _():
            for cp in weight_copies(next_ref[b], 1 - slot):
                cp.start()

    @pl.when(bv_ref[b] > 0)
    def _():
        x = jnp.concatenate([xs_ref[pl.ds(s, MOE_R, stride=ROW_SUB), :] for s in range(ROW_SUB)], axis=1)
        xb = x.astype(BF16)
        g = jnp.dot(xb, wg_s[...], preferred_element_type=F32)
        u = jnp.dot(xb, wu_s[...], preferred_element_type=F32)
        a = (g * _sigmoid(g) * u).astype(BF16)
        o = jnp.dot(a, wd_s[...], preferred_element_type=F32)
        for s in range(ROW_SUB):
            o_ref[pl.ds(s, MOE_R, stride=ROW_SUB), :] = o[:, s * LANES:(s + 1) * LANES]

    @pl.when(bv_ref[b] == 0)
    def _():
        o_ref[...] = jnp.zeros_like(o_ref)


def _experts(blk_e, blk_valid, xs, w_g, w_u, w_d):
    idx = np.arange(MOE_BLOCKS)
    used = blk_valid > 0
    first = used & ((idx == 0) | (blk_e != jnp.roll(blk_e, 1)))
    ordinal = jnp.sum(first[None, :] & (idx[None, :] <= idx[:, None]), axis=1) - 1
    later_first = jnp.where(first[None, :] & (idx[None, :] > idx[:, None]), idx[None, :], MOE_BLOCKS)
    nxt = jnp.min(later_first, axis=1)
    next_e = jnp.where(nxt < MOE_BLOCKS, jnp.sum(jnp.where(idx[None, :] == nxt[:, None], blk_e[None, :], 0), axis=1), -1)
    plan = [blk_e, blk_valid, first.astype(jnp.int32), (ordinal % 2).astype(jnp.int32), next_e.astype(jnp.int32)]

    def rows(b, be, bv, *_):
        return (jnp.where(bv[b] > 0, b, MOE_BLOCKS - 1), 0)

    hbm = pl.BlockSpec(memory_space=pl.ANY)
    return pl.pallas_call(
        _expert_kernel,
        grid_spec=pltpu.PrefetchScalarGridSpec(
            num_scalar_prefetch=len(plan),
            grid=(MOE_BLOCKS,),
            in_specs=[pl.BlockSpec((MOE_R * ROW_SUB, LANES), rows), hbm, hbm, hbm],
            out_specs=pl.BlockSpec((MOE_R * ROW_SUB, LANES), lambda b, *_: (b, 0)),
            scratch_shapes=[pltpu.VMEM((2, D_MODEL, D_EXPERT), F32),
                            pltpu.VMEM((2, D_MODEL, D_EXPERT), F32),
                            pltpu.VMEM((2, D_EXPERT, D_MODEL), F32),
                            pltpu.VMEM((D_MODEL, D_EXPERT), BF16),
                            pltpu.VMEM((D_MODEL, D_EXPERT), BF16),
                            pltpu.VMEM((D_EXPERT, D_MODEL), BF16),
                            pltpu.SemaphoreType.DMA((2,))]),
        out_shape=jax.ShapeDtypeStruct((MOE_ROWS * ROW_SUB, LANES), F32),
        compiler_params=pltpu.CompilerParams(dimension_semantics=("arbitrary",), vmem_limit_bytes=VMEM_LIMIT),
        name="moe_experts",
    )(*plan, xs, w_g, w_u, w_d)


def _combine_kernel(dest_ref, o_hbm, x2_ref, ri_ref, gate_ref, gf_ref, yp_ref, ys_ref, obuf, sem):
    i = pl.program_id(0)
    slot = i % 2
    tile_rows = TM * ROW_SUB

    def issue_tile(tile, slot_):
        def issue(t, carry):
            cps = _row_copies(tile, t, dest_ref, lambda k, d: pltpu.make_async_copy(
                _row(o_hbm, d), _row(obuf, (2 * slot_ + k) * TM + t), sem.at[slot_]))
            for k, cp in enumerate(cps):
                cp.start(priority=k)
            return carry

        lax.fori_loop(0, TM, issue, 0, unroll=8)

    @pl.when(i == 0)
    def _():
        issue_tile(0, 0)

    @pl.when(i + 1 < N_TILES)
    def _():
        issue_tile(i + 1, 1 - slot)

    for k in range(2):
        start = pl.multiple_of((2 * slot + k) * tile_rows, tile_rows)
        pltpu.make_async_copy(o_hbm.at[pl.ds(0, tile_rows), :], obuf.at[pl.ds(start, tile_rows), :],
                              sem.at[slot]).wait()

    ri = ri_ref[...]
    g1 = ri[:, 2:3]
    g2 = ri[:, 3:4]
    base = 2 * slot * tile_rows
    moe = jnp.concatenate(
        [g1 * obuf[pl.ds(base + s, TM, stride=ROW_SUB), :] + g2 * obuf[pl.ds(base + tile_rows + s, TM, stride=ROW_SUB), :]
         for s in range(ROW_SUB)], axis=1)
    x3 = x2_ref[...] + _per_chunk(gate_ref) * moe
    y = _rmsnorm(x3, gf_ref[...])

    @pl.when(i < P_TILES)
    def _():
        yp_ref[0] = y

    @pl.when(i >= P_TILES)
    def _():
        ys_ref[...] = y.reshape(CH_PER_TILE, DEC_SEQ, D_MODEL)


def _combine(dest, o_rows, x2, rinfo, modc, g_final):
    xp_spec, xs_spec = _xp_spec(), _xs_spec()
    return pl.pallas_call(
        _combine_kernel,
        grid_spec=pltpu.PrefetchScalarGridSpec(
            num_scalar_prefetch=1,
            grid=(N_TILES,),
            in_specs=[pl.BlockSpec(memory_space=pl.ANY),
                      pl.BlockSpec((TM, D_MODEL), lambda i, *_: (i, 0)),
                      pl.BlockSpec((TM, LANES), lambda i, *_: (i, 0)),
                      pl.BlockSpec((CH_PER_TILE, 1, D_MODEL), lambda i, *_: (i, 0, 5)),
                      pl.BlockSpec((1, D_MODEL), lambda i, *_: (0, 0))],
            out_specs=[pl.BlockSpec(xp_spec.block_shape, lambda i, *_: xp_spec.index_map(i)),
                       pl.BlockSpec(xs_spec.block_shape, lambda i, *_: xs_spec.index_map(i))],
            scratch_shapes=[pltpu.VMEM((4 * TM * ROW_SUB, LANES), F32), pltpu.SemaphoreType.DMA((2,))]),
        out_shape=[jax.ShapeDtypeStruct((BATCH, SEQ, D_MODEL), F32),
                   jax.ShapeDtypeStruct((DEC_BATCH, DEC_SEQ, D_MODEL), F32)],
        compiler_params=pltpu.CompilerParams(dimension_semantics=("arbitrary",), vmem_limit_bytes=VMEM_LIMIT),
        name="moe_combine",
    )(dest, o_rows, x2, rinfo, modc, g_final)


def _moe_plan(rt, cnt):
    experts = np.arange(N_EXPERTS)
    tile_cnt = cnt.reshape(N_TILES, SUBLANES, LANES)[:, 0, :N_EXPERTS]
    counts = jnp.sum(tile_cnt, axis=0)
    padded = jnp.ceil(counts / MOE_R) * MOE_R
    pad_end = jnp.sum(padded[:, None] * (experts[:, None] <= experts[None, :]), axis=0)
    base = pad_end - padded
    tiles = np.arange(N_TILES)
    earlier_tiles = (tiles[None, :, None] < tiles[:, None, None])
    tile_base = base[None, :] + jnp.sum(tile_cnt[None, :, :] * earlier_tiles, axis=1)
    eid = rt[0:2].reshape(1, 2, N_TILES, TM)
    rank = rt[4:6].reshape(2, N_TILES, TM)
    pick = eid == experts.astype(np.float32).reshape(N_EXPERTS, 1, 1, 1)
    dest = jnp.sum(jnp.where(pick, tile_base.T[:, None, :, None], 0.0), axis=0) + rank
    blk_start = (np.arange(MOE_BLOCKS) * MOE_R).astype(np.float32)
    blk_e = jnp.minimum(jnp.sum((blk_start[:, None] >= pad_end[None, :]).astype(F32), axis=1), N_EXPERTS - 1)
    mine = blk_e[:, None] == experts.astype(np.float32)[None, :]
    blk_fill = jnp.sum(jnp.where(mine, (counts + base)[None, :], 0.0), axis=1) - blk_start
    blk_valid = jnp.clip(blk_fill, 0, MOE_R)
    zlo = jnp.concatenate([jnp.floor((base + counts) / ZCHUNK), pad_end[-1:] / ZCHUNK])
    zhi = jnp.concatenate([pad_end / ZCHUNK, jnp.full((1,), MOE_ROWS // ZCHUNK, F32)])
    return (dest.reshape(-1).astype(jnp.int32), blk_e.astype(jnp.int32), blk_valid.astype(jnp.int32),
            zlo.astype(jnp.int32), zhi.astype(jnp.int32))


def kernel(x_prompt, x_sample, c_prompt, c_sample, cache_win_k, cache_win_v, state_conv, state_C, state_n, state_m, w_ada, b_ada, g_norm_mix, g_norm_ffn, w_in, attn_sinks, conv_w, conv_b, b_igate, b_fgate, g_mhnorm, w_out, w_router_group, b_router_group, w_router_expert, b_router_expert, w_exp_gate, w_exp_up, w_exp_down, g_final):
    l = 0
    mod = _modulation(jnp.concatenate([c_prompt, c_sample], axis=0), w_ada[l], b_ada[l])
    chunk_stream = np.concatenate([np.repeat(np.arange(BATCH), SEQ // CHUNK), BATCH + np.arange(DEC_BATCH)])
    modc = mod[chunk_stream].reshape(N_CHUNKS, 1, 6 * D_MODEL)

    wi = w_in[l]
    s_q, s_k, s_v, s_qkm, s_vm, s_ig, s_fg = 0, 512, 640, 768, 1792, 2304, 2308
    s_om = 2312
    w_in_r = jnp.concatenate(
        [wi[:, s_q:s_qkm], wi[:, s_qkm:s_vm], wi[:, s_vm:s_ig], wi[:, s_om:], wi[:, s_ig:s_om],
         jnp.zeros((D_MODEL, LANES - 2 * M_HEADS), F32)], axis=1).astype(BF16)
    q, kv, qkm, vm, om, gates = _inproj(x_prompt, x_sample, modc, g_norm_mix[l].reshape(1, -1), w_in_r)

    cache_kv = jnp.concatenate([cache_win_k[l].reshape(DEC_BATCH * WINDOW, A_KV_WIDTH),
                                cache_win_v[l].reshape(DEC_BATCH * WINDOW, A_KV_WIDTH)], axis=1)
    att_p, att_s = _attention(attn_sinks[l], q, kv, cache_kv)

    gbias = jnp.concatenate([b_igate[l], b_fgate[l], jnp.zeros((LANES - 2 * M_HEADS,), F32)]).reshape(1, LANES)
    common = (qkm, vm, om, gates, conv_w[l], conv_b[l].reshape(1, -1), gbias, g_mhnorm[l].reshape(1, -1))
    zeros_p = (jnp.zeros((BATCH, SUBLANES, 2 * M_WIDTH), F32),
               jnp.zeros((BATCH, M_HEADS, M_HEAD_DIM, M_HEAD_DIM), F32),
               jnp.zeros((BATCH, M_HEADS, M_HEAD_DIM), F32),
               jnp.zeros((BATCH, 1, LANES), F32))
    LP = 256
    hm_p, C_p, n_p, m_p = _mlstm(*common, *zeros_p, nb=BATCH, L=LP, nc=SEQ // LP, ns=1, row_off=0,
                                 name="mlstm_prompt")
    carry_s = jnp.concatenate([jnp.zeros((DEC_BATCH, SUBLANES - (CONV_WIDTH - 1), 2 * M_WIDTH), F32),
                               state_conv[l]], axis=1)
    m0_s = jnp.pad(state_m[l], ((0, 0), (0, LANES - M_HEADS))).reshape(DEC_BATCH, 1, LANES)
    hm_s, C_s, n_s, m_s = _mlstm(*common, carry_s, state_C[l], state_n[l], m0_s,
                                 nb=DEC_BATCH, L=DEC_SEQ, nc=1, ns=4, row_off=N_P, name="mlstm_sample")

    w_router = jnp.concatenate([w_router_group[l], w_router_expert[l],
                                jnp.zeros((D_MODEL, LANES - N_GROUPS - N_EXPERTS), F32)], axis=1)
    w_router_hi = w_router.astype(BF16)
    w_router = jnp.stack([w_router_hi, (w_router - w_router_hi.astype(F32)).astype(BF16)])
    b_router =jnp.concatenate([b_router_group[l], b_router_expert[l],
                                jnp.zeros((LANES - N_GROUPS - N_EXPERTS,), F32)]).reshape(1, LANES)
    x2, h2t, rinfo, rt, cnt = _outproj(x_prompt, x_sample, att_p, att_s, hm_p, hm_s, modc,
                                       g_norm_ffn[l].reshape(1, -1), w_out[l].astype(BF16), w_router, b_router)

    dest, blk_e, blk_valid, zlo, zhi = _moe_plan(rt, cnt)
    xs_rows = _dispatch(dest, zlo, zhi, h2t)
    o_rows = _experts(blk_e, blk_valid, xs_rows, w_exp_gate[l], w_exp_up[l], w_exp_down[l])
    y_prompt, y_sample = _combine(dest, o_rows, x2, rinfo, modc, g_final.reshape(1, -1))

    kv_p = kv.reshape(N_TOK // WINDOW, WINDOW, 2 * A_KV_WIDTH)[SEQ // WINDOW - 1:N_P // WINDOW:SEQ // WINDOW]
    win_k_p = kv_p[..., :A_KV_WIDTH].reshape(1, BATCH, WINDOW, A_KV_HEADS, A_HEAD_DIM)
    win_v_p = kv_p[..., A_KV_WIDTH:].reshape(1, BATCH, WINDOW, A_KV_HEADS, A_HEAD_DIM)
    kv_s = kv[N_P:].reshape(DEC_BATCH, DEC_SEQ, 2 * A_KV_WIDTH)
    win_k_s = jnp.concatenate([cache_win_k[l][:, DEC_SEQ:],
                               kv_s[..., :A_KV_WIDTH].reshape(DEC_BATCH, DEC_SEQ, A_KV_HEADS, A_HEAD_DIM)], axis=1)[None]
    win_v_s = jnp.concatenate([cache_win_v[l][:, DEC_SEQ:],
                               kv_s[..., A_KV_WIDTH:].reshape(DEC_BATCH, DEC_SEQ, A_KV_HEADS, A_HEAD_DIM)], axis=1)[None]
    qkm_c = qkm.reshape(N_CHUNKS, CHUNK, 2 * M_WIDTH)
    tail = slice(CHUNK - (CONV_WIDTH - 1), CHUNK)
    conv_p = qkm_c[SEQ // CHUNK - 1:N_P // CHUNK:SEQ // CHUNK, tail][None]
    conv_s = qkm_c[N_P // CHUNK:, tail][None]
    return (y_prompt, y_sample,
            win_k_p, win_v_p, conv_p, C_p[None], n_p[None], m_p[:, 0, :M_HEADS][None],
            win_k_s, win_v_s, conv_s, C_s[None], n_s[None], m_s[:, 0, :M_HEADS][None])
```

```python
import functools

import numpy as np
import jax
import jax.numpy as jnp
from jax import lax
from jax.experimental import pallas as pl
from jax.experimental.pallas import tpu as pltpu

F32 = jnp.float32
BF16 = jnp.bfloat16
HIGHEST = lax.Precision.HIGHEST

LANES = 128
SUBLANES = 8

D_MODEL = 1024
BATCH = 8
SEQ = 2048
DEC_BATCH = 32
DEC_SEQ = 64
PAST_LEN = 4096
CHUNK = 64
A_WIDTH = 512
A_HEAD_DIM = 64
A_HEADS = 8
A_KV_HEADS = 2
A_GROUP = 4
A_KV_WIDTH = 128
WINDOW = 128
ROT_DIM = 16
ROPE_THETA = 500000.0
M_WIDTH = 512
M_HEADS = 4
M_HEAD_DIM = 128
CONV_WIDTH = 4
N_GROUPS = 4
EXPERTS_PER_GROUP = 8
N_EXPERTS = 32
D_EXPERT = 512
EPS = 1e-6

N_P = BATCH * SEQ
N_S = DEC_BATCH * DEC_SEQ
N_TOK = N_P + N_S
N_CHUNKS = N_TOK // CHUNK
TM = 512
N_TILES = N_TOK // TM
P_TILES = N_P // TM
CH_PER_TILE = TM // CHUNK
ROW_SUB = D_MODEL // LANES
N_ASSIGN = 2 * N_TOK
MOE_R = 512
MOE_BLOCKS = N_ASSIGN // MOE_R + N_EXPERTS
MOE_ROWS = MOE_BLOCKS * MOE_R
C_Q, C_K, C_V, C_QKM, C_VM, C_OM, C_G = 0, 512, 640, 768, 1792, 2304, 2816
D_IN_PAD = 2944
VMEM_LIMIT = 48 * 1024 * 1024


def _sigmoid(x):
    return 1.0 / (1.0 + jnp.exp(-x))


def _mod_kernel(c_ref, w_ref, b_ref, o_ref):
    c = c_ref[...]
    s = c * _sigmoid(c)
    w = w_ref[...]
    s_hi, w_hi = s.astype(BF16), w.astype(BF16)
    s_lo, w_lo = (s - s_hi.astype(F32)).astype(BF16), (w - w_hi.astype(F32)).astype(BF16)
    o_ref[...] = (jnp.dot(s_hi, w_hi, preferred_element_type=F32) + jnp.dot(s_lo, w_hi, preferred_element_type=F32)
                  + jnp.dot(s_hi, w_lo, preferred_element_type=F32)) + b_ref[...]


def _modulation(c_all, w_ada, b_ada):
    n = c_all.shape[0]
    bn = 512
    return pl.pallas_call(
        _mod_kernel,
        grid=(6 * D_MODEL // bn,),
        in_specs=[pl.BlockSpec((n, D_MODEL), lambda j: (0, 0)),
                  pl.BlockSpec((D_MODEL, bn), lambda j: (0, j)),
                  pl.BlockSpec((1, bn), lambda j: (0, j))],
        out_specs=pl.BlockSpec((n, bn), lambda j: (0, j)),
        out_shape=jax.ShapeDtypeStruct((n, 6 * D_MODEL), F32),
        name="modulation",
    )(c_all, w_ada, b_ada.reshape(1, -1))


def _xp_spec():
    def idx(i):
        t = jnp.minimum(i, P_TILES - 1)
        return (t // (SEQ // TM), t % (SEQ // TM), 0)
    return pl.BlockSpec((1, TM, D_MODEL), idx)


def _xs_spec():
    return pl.BlockSpec((CH_PER_TILE, DEC_SEQ, D_MODEL), lambda i: (jnp.maximum(i - P_TILES, 0), 0, 0))


def _mod_spec(comp):
    return pl.BlockSpec((CH_PER_TILE, 1, D_MODEL), lambda i: (i, 0, comp))


def _load_x(xp_ref, xs_ref, xbuf):
    i = pl.program_id(0)

    @pl.when(i < P_TILES)
    def _():
        xbuf[...] = xp_ref[0]

    @pl.when(i >= P_TILES)
    def _():
        xbuf[...] = xs_ref[...].reshape(TM, D_MODEL)

    return xbuf[...]


def _per_chunk(m_ref):
    m = m_ref[...]
    return jnp.broadcast_to(m, (CH_PER_TILE, CHUNK, D_MODEL)).reshape(TM, D_MODEL)


def _rmsnorm(x, g):
    return x * lax.rsqrt(jnp.mean(x * x, axis=-1, keepdims=True) + EPS) * g


def _rope(x, cos, sa, sb):
    n = x.shape[1]
    rep = n // LANES
    if rep > 1:
        cos = jnp.concatenate([cos] * rep, axis=1)
        sa = jnp.concatenate([sa] * rep, axis=1)
        sb = jnp.concatenate([sb] * rep, axis=1)
    return x * cos + pltpu.roll(x, n - ROT_DIM // 2, 1) * sa + pltpu.roll(x, ROT_DIM // 2, 1) * sb


def _inproj_kernel(xp_ref, xs_ref, sh_ref, sc_ref, g_ref, w_ref, cos_ref, sa_ref, sb_ref,
                   q_ref, kv_ref, qkm_ref, vm_ref, om_ref, gt_ref, xbuf):
    x = _load_x(xp_ref, xs_ref, xbuf)
    h = _rmsnorm(x, g_ref[...]) * (1.0 + _per_chunk(sc_ref)) + _per_chunk(sh_ref)
    hb = h.astype(BF16)

    def proj(a, b):
        return jnp.dot(hb, w_ref[:, a:b], preferred_element_type=F32)

    cos, sa, sb = cos_ref[...], sa_ref[...], sb_ref[...]
    q_ref[...] = _rope(proj(C_Q, C_K), cos, sa, sb).astype(BF16)
    kv_ref[:, :A_KV_WIDTH] = _rope(proj(C_K, C_V), cos, sa, sb)
    kv_ref[:, A_KV_WIDTH:] = proj(C_V, C_QKM)
    qkm_ref[...] = proj(C_QKM, C_VM)
    vm_ref[...] = proj(C_VM, C_OM).astype(BF16)
    om_ref[...] = proj(C_OM, C_G)
    gt_ref[...] = proj(C_G, D_IN_PAD)


def _rope_tables():
    pos = np.concatenate([np.arange(SEQ), np.tile(PAST_LEN + np.arange(DEC_SEQ), CH_PER_TILE)]).astype(np.float64)
    inv_freq = ROPE_THETA ** (-np.arange(0, ROT_DIM, 2, dtype=np.float64) / ROT_DIM)
    lane = np.arange(LANES)
    hl = lane % A_HEAD_DIM
    ang = pos[:, None] * inv_freq[hl % (ROT_DIM // 2)][None, :]
    rot = (hl < ROT_DIM)[None, :]
    lo = (hl < ROT_DIM // 2)[None, :]
    cos = np.where(rot, np.cos(ang), 1.0)
    sa = np.where(lo, -np.sin(ang), 0.0)
    sb = np.where(rot & ~lo, np.sin(ang), 0.0)
    return [jnp.asarray(t, F32) for t in (cos, sa, sb)]


def _inproj(x_prompt, x_sample, modc, g_mix, w_in_r):
    cos, sa, sb = _rope_tables()
    tab_spec = pl.BlockSpec((TM, LANES), lambda i: (jnp.where(i < P_TILES, i % (SEQ // TM), SEQ // TM), 0))

    def out(n, dtype=F32):
        return pl.BlockSpec((TM, n), lambda i: (i, 0)), jax.ShapeDtypeStruct((N_TOK, n), dtype)

    outs = [out(A_WIDTH, BF16), out(2 * A_KV_WIDTH), out(2 * M_WIDTH), out(M_WIDTH, BF16), out(M_WIDTH), out(LANES)]
    return pl.pallas_call(
        _inproj_kernel,
        grid=(N_TILES,),
        in_specs=[_xp_spec(), _xs_spec(), _mod_spec(0), _mod_spec(1),
                  pl.BlockSpec((1, D_MODEL), lambda i: (0, 0)),
                  pl.BlockSpec((D_MODEL, D_IN_PAD), lambda i: (0, 0)),
                  tab_spec, tab_spec, tab_spec],
        out_specs=[o[0] for o in outs],
        out_shape=[o[1] for o in outs],
        scratch_shapes=[pltpu.VMEM((TM, D_MODEL), F32)],
        compiler_params=pltpu.CompilerParams(vmem_limit_bytes=VMEM_LIMIT),
        name="inproj",
    )(x_prompt, x_sample, modc, modc, g_mix, w_in_r, cos, sa, sb)


def _attn_chunk_kernel(sink_ref, q_ref, prev_ref, cur_ref, o_ref):
    q = q_ref[...] * (A_HEAD_DIM ** -0.5)
    kv = jnp.concatenate([prev_ref[...], cur_ref[...]], axis=0)
    for g in range(A_KV_HEADS):
        kg = kv[:, g * A_HEAD_DIM:(g + 1) * A_HEAD_DIM].astype(BF16)
        vg = kv[:, A_KV_WIDTH + g * A_HEAD_DIM:A_KV_WIDTH + (g + 1) * A_HEAD_DIM].astype(BF16)
        heads = [g * A_GROUP + i for i in range(A_GROUP)]
        qc = jnp.concatenate([q[:, h * A_HEAD_DIM:(h + 1) * A_HEAD_DIM] for h in heads], axis=0).astype(BF16)
        snk = jnp.concatenate([jnp.full((CHUNK, 1), sink_ref[h], F32) for h in heads], axis=0)
        s = lax.dot_general(qc, kg, (((1,), (1,)), ((), ())), preferred_element_type=F32)
        mx = jnp.maximum(jnp.max(s, axis=-1, keepdims=True), snk)
        p = jnp.exp(s - mx)
        den = jnp.sum(p, axis=-1, keepdims=True) + jnp.exp(snk - mx)
        o = jnp.dot(p.astype(BF16), vg, preferred_element_type=F32) / den
        for i, h in enumerate(heads):
            o_ref[:, h * A_HEAD_DIM:(h + 1) * A_HEAD_DIM] = o[i * CHUNK:(i + 1) * CHUNK]


ATT_TQ = 512
ATT_QB = 2 * CHUNK


def _attn_band_kernel(sink_ref, q_ref, prev_ref, cur_ref, o_ref, att_t):
    nk = ATT_QB + WINDOW
    q = (q_ref[...] * (A_HEAD_DIM ** -0.5)).astype(BF16)
    kv = jnp.concatenate([prev_ref[...], cur_ref[...]], axis=0)
    k2 = kv[:, :A_KV_WIDTH]
    k2r = pltpu.roll(k2, A_HEAD_DIM, 1)
    low = lax.broadcasted_iota(jnp.int32, k2.shape, 1) < A_HEAD_DIM
    k_placed = {(0, 0): jnp.where(low, k2, 0.0), (0, 1): jnp.where(low, 0.0, k2r),
                (1, 0): jnp.where(low, k2r, 0.0), (1, 1): jnp.where(low, 0.0, k2)}
    k_placed = {key: val.astype(BF16) for key, val in k_placed.items()}
    v_t = kv[:, A_KV_WIDTH:].T.astype(BF16)
    key_chunk = lax.broadcasted_iota(jnp.int32, (nk, ATT_QB), 0) // CHUNK
    q_chunk = lax.broadcasted_iota(jnp.int32, (nk, ATT_QB), 1) // CHUNK
    band = (key_chunk >= q_chunk) & (key_chunk <= q_chunk + WINDOW // CHUNK)
    has_history = pl.program_id(1) > 0
    for blk in range(ATT_TQ // ATT_QB):
        keys = slice(blk * ATT_QB, blk * ATT_QB + nk)
        qrows = slice(blk * ATT_QB, (blk + 1) * ATT_QB)
        valid = band & ((key_chunk >= WINDOW // CHUNK) | has_history) if blk == 0 else band
        for h in range(A_HEADS):
            g = h // A_GROUP
            s_t = lax.dot_general(k_placed[(g, h % 2)][keys], q[qrows, (h // 2) * LANES:(h // 2 + 1) * LANES],
                                  (((1,), (1,)), ((), ())), preferred_element_type=F32)
            s_t = jnp.where(valid, s_t, -jnp.inf)
            snk = sink_ref[h]
            mx = jnp.maximum(jnp.max(s_t, axis=0, keepdims=True), snk)
            p_t = jnp.exp(s_t - mx)
            den = jnp.sum(p_t, axis=0, keepdims=True) + jnp.exp(snk - mx)
            o_t = jnp.dot(v_t[:, keys], p_t.astype(BF16), preferred_element_type=F32)
            att_t[h * A_HEAD_DIM:(h + 1) * A_HEAD_DIM, qrows] = o_t[g * A_HEAD_DIM:(g + 1) * A_HEAD_DIM] / den
    o_ref[...] = att_t[...].T.astype(BF16)


def _attention(sinks, q, kv, cache_kv):
    smem = pl.BlockSpec(memory_space=pltpu.SMEM)
    tq = ATT_TQ
    nq = SEQ // tq
    att = pl.pallas_call(
        _attn_band_kernel,
        grid=(BATCH, nq),
        scratch_shapes=[pltpu.VMEM((A_WIDTH, ATT_TQ), F32)],
        in_specs=[smem,
                  pl.BlockSpec((tq, A_WIDTH), lambda b, j: (b * nq + j, 0)),
                  pl.BlockSpec((WINDOW, 2 * A_KV_WIDTH),
                               lambda b, j: (jnp.maximum((b * nq + j) * (tq // WINDOW) - 1, 0), 0)),
                  pl.BlockSpec((tq, 2 * A_KV_WIDTH), lambda b, j: (b * nq + j, 0))],
        out_specs=pl.BlockSpec((tq, A_WIDTH), lambda b, j: (b * nq + j, 0)),
        out_shape=jax.ShapeDtypeStruct((N_P, A_WIDTH), BF16),
        name="attn_prompt",
    )(sinks, q, kv, kv)
    off = N_P // DEC_SEQ
    att_s = pl.pallas_call(
        _attn_chunk_kernel,
        grid=(DEC_BATCH,),
        in_specs=[smem,
                  pl.BlockSpec((DEC_SEQ, A_WIDTH), lambda b: (off + b, 0)),
                  pl.BlockSpec((WINDOW, 2 * A_KV_WIDTH), lambda b: (b, 0)),
                  pl.BlockSpec((DEC_SEQ, 2 * A_KV_WIDTH), lambda b: (off + b, 0))],
        out_specs=pl.BlockSpec((DEC_SEQ, A_WIDTH), lambda b: (b, 0)),
        out_shape=jax.ShapeDtypeStruct((N_S, A_WIDTH), F32),
        name="attn_sample",
    )(sinks, q, cache_kv, kv)
    return att, att_s


def _conv4(x, w, b):
    x1 = pltpu.roll(x, 1, 0)
    near = b + x * w[3:4] + x1 * w[2:3]
    far = x * w[1:2] + x1 * w[0:1]
    return near + pltpu.roll(far, 2, 0)


def _mlstm_kernel(qkm_ref, vm_ref, om_ref, gt_ref, cw_ref, cb_ref, gb_ref, gmh_ref,
                  carry0_ref, c0_ref, n0_ref, m0_ref,
                  hm_ref, cout_ref, nout_ref, mout_ref, c_s, n_s, m_s, carry_s, *, L, NS):
    c = pl.program_id(1)

    @pl.when(c == 0)
    def _():
        c_s[...] = c0_ref[...]
        n_s[...] = n0_ref[...]
        m_s[...] = m0_ref[...]
        carry_s[...] = carry0_ref[...]

    w = cw_ref[...]
    b = cb_ref[...]
    gmh = gmh_ref[...]
    lane = lax.broadcasted_iota(jnp.int32, (L, LANES), 1)
    causal = lax.broadcasted_iota(jnp.int32, (L, L), 0) >= lax.broadcasted_iota(jnp.int32, (L, L), 1)
    ones_causal = causal.astype(F32)
    sel = (lax.broadcasted_iota(jnp.int32, (SUBLANES, LANES), 0)
           == lax.broadcasted_iota(jnp.int32, (SUBLANES, LANES), 1)).astype(F32)
    lane1 = lax.broadcasted_iota(jnp.int32, (1, LANES), 1)
    for st in range(NS):
        _mlstm_stream(st, qkm_ref, vm_ref, om_ref, gt_ref, gb_ref, hm_ref, c_s, n_s, m_s, carry_s,
                      w, b, gmh, lane, causal, ones_causal, sel, lane1, L)

    @pl.when(c == pl.num_programs(1) - 1)
    def _():
        cout_ref[...] = c_s[...]
        nout_ref[...] = n_s[...]
        mout_ref[...] = m_s[...]


def _mlstm_stream(st, qkm_ref, vm_ref, om_ref, gt_ref, gb_ref, hm_ref, c_s, n_s, m_s, carry_s,
                  w, b, gmh, lane, causal, ones_causal, sel, lane1, L):
    x = qkm_ref[st]
    y = _conv4(x, w, b)
    y8 = _conv4(jnp.concatenate([carry_s[st], x[:SUBLANES]], axis=0), w, b)
    y = jnp.concatenate([y8[SUBLANES:], y[SUBLANES:]], axis=0)
    carry_s[st] = x[L - SUBLANES:]
    a = y * _sigmoid(y)
    qa = a[:, :M_WIDTH] * (M_HEAD_DIM ** -0.5)
    ka = a[:, M_WIDTH:]
    v = vm_ref[st]
    om = om_ref[st]

    z = gt_ref[st] + gb_ref[...]
    f_log = jnp.minimum(z, 0.0) - jnp.log1p(jnp.exp(-jnp.abs(z)))
    val = jnp.where(lane < M_HEADS, z, f_log)
    cum = jnp.dot(ones_causal, val, preferred_element_type=F32, precision=HIGHEST)
    G = jnp.where(lane < M_HEADS, val, cum)
    GT = lax.dot_general(sel, G, (((1,), (1,)), ((), ())), preferred_element_type=F32, precision=HIGHEST)

    m_row = m_s[st]
    outs = []
    for h in range(M_HEADS):
        a_col = G[:, M_HEADS + h:M_HEADS + h + 1]
        i_col = G[:, h:h + 1]
        i_row = GT[h:h + 1, :]
        b_row = GT[M_HEADS + h:M_HEADS + h + 1, :]
        m_prev = m_row[:, h:h + 1]
        dm = jnp.where(causal, a_col - b_row + i_row, -jnp.inf)
        inter = a_col + m_prev
        m_t = jnp.maximum(inter, jnp.max(dm, axis=-1, keepdims=True))
        w_intra = jnp.exp(dm - m_t)
        w_inter = jnp.exp(inter - m_t)
        sl = slice(h * M_HEAD_DIM, (h + 1) * M_HEAD_DIM)
        qh, kh, vh = qa[:, sl], ka[:, sl], v[:, sl]
        qb = qh.astype(BF16)
        scores = lax.dot_general(qb, kh.astype(BF16), (((1,), (1,)), ((), ())), preferred_element_type=F32) * w_intra
        ch = c_s[st, h]
        nh = n_s[st, h:h + 1, :]
        num = (jnp.dot(scores.astype(BF16), vh.astype(BF16), preferred_element_type=F32)
               + w_inter * lax.dot_general(qb, ch.astype(BF16), (((1,), (1,)), ((), ())), preferred_element_type=F32))
        den = jnp.sum(scores, axis=-1, keepdims=True) + w_inter * jnp.sum(qh * nh, axis=-1, keepdims=True)
        hv = num / jnp.maximum(jnp.abs(den), jnp.exp(-m_t))
        hn = hv * lax.rsqrt(jnp.mean(hv * hv, axis=-1, keepdims=True) + EPS)
        outs.append(hn * gmh[:, sl] * _sigmoid(om[:, sl]))
        m_new = m_t[L - 1:L, :]
        a_last = a_col[L - 1:L, :]
        w_end = jnp.exp(a_last - a_col + i_col - m_new)
        decay = jnp.exp(a_last + m_prev - m_new)
        kw = kh * w_end
        c_s[st, h] = decay * ch + lax.dot_general(vh.astype(BF16), kw.astype(BF16), (((0,), (0,)), ((), ())),
                                                  preferred_element_type=F32)
        n_s[st, h:h + 1, :] = decay * nh + jnp.sum(kw, axis=0, keepdims=True)
        m_row = jnp.where(lane1 == h, m_new, m_row)
    m_s[st] = m_row
    hm_ref[st] = jnp.concatenate(outs, axis=1).astype(BF16)


def _mlstm(qkm, vm, om, gates, conv_w, conv_b, gbias, g_mh, carry0, c0, n0, m0, *, nb, L, nc, ns, row_off, name):
    S = nc * L
    first = row_off // S // ns

    def tok(arr):
        n = arr.shape[-1]
        return arr.reshape(N_TOK // S, S, n), pl.BlockSpec((ns, L, n), lambda i, c: (first + i, c, 0))

    def const(shape):
        return pl.BlockSpec(shape, lambda i, c: (0,) * len(shape))

    def per_stream(shape):
        return pl.BlockSpec((ns,) + shape, lambda i, c: (i,) + (0,) * len(shape))

    toks = [tok(a) for a in (qkm, vm, om, gates)]
    state_specs = [per_stream((SUBLANES, 2 * M_WIDTH)), per_stream((M_HEADS, M_HEAD_DIM, M_HEAD_DIM)),
                   per_stream((M_HEADS, M_HEAD_DIM)), per_stream((1, LANES))]
    in_specs = ([t[1] for t in toks]
                + [const((CONV_WIDTH, 2 * M_WIDTH)), const((1, 2 * M_WIDTH)), const((1, LANES)), const((1, M_WIDTH))]
                + state_specs)
    args = [t[0] for t in toks] + [conv_w, conv_b, gbias, g_mh, carry0, c0, n0, m0]
    hm, c_out, n_out, m_out = pl.pallas_call(
        functools.partial(_mlstm_kernel, L=L, NS=ns),
        grid=(nb // ns, nc),
        in_specs=in_specs,
        out_specs=[pl.BlockSpec((ns, L, M_WIDTH), lambda i, c: (i, c, 0))] + state_specs[1:],
        out_shape=[jax.ShapeDtypeStruct((nb, S, M_WIDTH), BF16),
                   jax.ShapeDtypeStruct((nb, M_HEADS, M_HEAD_DIM, M_HEAD_DIM), F32),
                   jax.ShapeDtypeStruct((nb, M_HEADS, M_HEAD_DIM), F32),
                   jax.ShapeDtypeStruct((nb, 1, LANES), F32)],
        scratch_shapes=[pltpu.VMEM((ns, M_HEADS, M_HEAD_DIM, M_HEAD_DIM), F32),
                        pltpu.VMEM((ns, M_HEADS, M_HEAD_DIM), F32),
                        pltpu.VMEM((ns, 1, LANES), F32),
                        pltpu.VMEM((ns, SUBLANES, 2 * M_WIDTH), F32)],
        compiler_params=pltpu.CompilerParams(dimension_semantics=("arbitrary", "arbitrary"),
                                             vmem_limit_bytes=VMEM_LIMIT),
        name=name,
    )(*args)
    return hm.reshape(nb * S, M_WIDTH), c_out, n_out, m_out


def _outproj_kernel(xp_ref, xs_ref, attp_ref, atts_ref, hmp_ref, hms_ref, gate_ref, sh_ref, sc_ref, g_ref,
                    w_ref, wr_ref, br_ref, x2_ref, h2_ref, ri_ref, rt_ref, cnt_ref, xbuf, mixbuf):
    x = _load_x(xp_ref, xs_ref, xbuf)
    i = pl.program_id(0)

    @pl.when(i < P_TILES)
    def _():
        mixbuf[:, :A_WIDTH] = attp_ref[...].astype(BF16)
        mixbuf[:, A_WIDTH:] = hmp_ref[...].astype(BF16)

    @pl.when(i >= P_TILES)
    def _():
        mixbuf[:, :A_WIDTH] = atts_ref[...].astype(BF16)
        mixbuf[:, A_WIDTH:] = hms_ref[...].astype(BF16)

    mixed = jnp.dot(mixbuf[...], w_ref[...], preferred_element_type=F32)
    x2 = x + _per_chunk(gate_ref) * mixed
    x2_ref[...] = x2
    h2 = _rmsnorm(x2, g_ref[...]) * (1.0 + _per_chunk(sc_ref)) + _per_chunk(sh_ref)
    for s in range(ROW_SUB):
        h2_ref[pl.ds(s, TM, stride=ROW_SUB), :] = h2[:, s * LANES:(s + 1) * LANES]

    h_hi = h2.astype(BF16)
    h_lo = (h2 - h_hi.astype(F32)).astype(BF16)
    lg = (jnp.dot(h_hi, wr_ref[0], preferred_element_type=F32)
          + jnp.dot(h_lo, wr_ref[0], preferred_element_type=F32)
          + jnp.dot(h_hi, wr_ref[1], preferred_element_type=F32)) + br_ref[...]
    lane = lax.broadcasted_iota(jnp.int32, (TM, LANES), 1)
    lanef = lane.astype(F32)
    ninf = -jnp.inf

    def first_argmax(vals):
        mx = jnp.max(vals, axis=-1, keepdims=True)
        return mx, jnp.min(jnp.where(vals == mx, lanef, float(LANES)), axis=-1, keepdims=True)

    is_grp = lane < N_GROUPS
    gmax, grp = first_argmax(jnp.where(is_grp, lg, ninf))
    p_grp = 1.0 / jnp.sum(jnp.where(is_grp, jnp.exp(lg - gmax), 0.0), axis=-1, keepdims=True)
    base = N_GROUPS + grp * EXPERTS_PER_GROUP
    in_grp = (lanef >= base) & (lanef < base + EXPERTS_PER_GROUP)
    el = jnp.where(in_grp, lg, ninf)
    v1, i1 = first_argmax(el)
    v2, i2 = first_argmax(jnp.where(lanef == i1, ninf, el))
    e = jnp.exp(v2 - v1)
    g1 = p_grp / (1.0 + e)
    g2 = p_grp * e / (1.0 + e)
    e1 = i1 - N_GROUPS
    e2 = i2 - N_GROUPS
    oh1 = lanef == e1
    oh2 = lanef == e2
    oh = jnp.where(oh1 | oh2, 1.0, 0.0)
    earlier = (lax.broadcasted_iota(jnp.int32, (TM, TM), 0) > lax.broadcasted_iota(jnp.int32, (TM, TM), 1))
    before = jnp.dot(earlier.astype(BF16), oh.astype(BF16), preferred_element_type=F32)
    r1 = jnp.sum(jnp.where(oh1, before, 0.0), axis=-1, keepdims=True)
    r2 = jnp.sum(jnp.where(oh2, before, 0.0), axis=-1, keepdims=True)
    cnt_ref[...] = jnp.broadcast_to(jnp.sum(oh, axis=0, keepdims=True), (SUBLANES, LANES))
    cols = (e1, e2, g1, g2, r1, r2)
    ri = jnp.zeros((TM, LANES), F32)
    for k, cval in enumerate(cols):
        ri = jnp.where(lane == k, cval, ri)
    ri_ref[...] = ri
    rt_ref[...] = ri.T[:SUBLANES]


def _outproj(x_prompt, x_sample, att_p, att_s, hm_p, hm_s, modc, g_ffn, w_out_b, w_router, b_router):
    def tok(n):
        return pl.BlockSpec((TM, n), lambda i: (i, 0))

    def tok_p(n):
        return pl.BlockSpec((TM, n), lambda i: (jnp.minimum(i, P_TILES - 1), 0))

    def tok_s(n):
        return pl.BlockSpec((TM, n), lambda i: (jnp.maximum(i - P_TILES, 0), 0))

    def const(shape):
        return pl.BlockSpec(shape, lambda i: (0,) * len(shape))

    return pl.pallas_call(
        _outproj_kernel,
        grid=(N_TILES,),
        in_specs=[_xp_spec(), _xs_spec(), tok_p(A_WIDTH), tok_s(A_WIDTH), tok_p(M_WIDTH), tok_s(M_WIDTH),
                  _mod_spec(2), _mod_spec(3), _mod_spec(4),
                  const((1, D_MODEL)), const((D_MODEL, D_MODEL)), const((2, D_MODEL, LANES)), const((1, LANES))],
        out_specs=[tok(D_MODEL), pl.BlockSpec((TM * ROW_SUB, LANES), lambda i: (i, 0)), tok(LANES),
                   pl.BlockSpec((SUBLANES, TM), lambda i: (0, i)),
                   pl.BlockSpec((SUBLANES, LANES), lambda i: (i, 0))],
        out_shape=[jax.ShapeDtypeStruct((N_TOK, D_MODEL), F32),
                   jax.ShapeDtypeStruct((N_TOK * ROW_SUB, LANES), F32),
                   jax.ShapeDtypeStruct((N_TOK, LANES), F32),
                   jax.ShapeDtypeStruct((SUBLANES, N_TOK), F32),
                   jax.ShapeDtypeStruct((N_TILES * SUBLANES, LANES), F32)],
        scratch_shapes=[pltpu.VMEM((TM, D_MODEL), F32), pltpu.VMEM((TM, D_MODEL), BF16)],
        compiler_params=pltpu.CompilerParams(vmem_limit_bytes=VMEM_LIMIT),
        name="outproj",
    )(x_prompt, x_sample, att_p, att_s, hm_p, hm_s, modc, modc, modc, g_ffn, w_out_b, w_router, b_router)


def _row(ref, r):
    return ref.at[pl.ds(pl.multiple_of(r * ROW_SUB, ROW_SUB), ROW_SUB), :]


def _row_copies(i, t, dest_ref, make):
    tok = i * TM + t
    return [make(k, dest_ref[k * N_TOK + tok]) for k in range(2)]


ZCHUNK = 64


def _dispatch_kernel(dest_ref, zlo_ref, zhi_ref, h_ref, xs_out, zbuf, sem, zsem):
    i = pl.program_id(0)
    chunk_rows = ZCHUNK * ROW_SUB

    @pl.when(i == 0)
    def _():
        zbuf[...] = jnp.zeros_like(zbuf)

        def chunk_copy(c):
            return pltpu.make_async_copy(zbuf, xs_out.at[pl.ds(pl.multiple_of(c * chunk_rows, chunk_rows), chunk_rows), :],
                                         zsem)

        def start_range(rng, carry):
            def start(c, cc):
                chunk_copy(c).start()
                return cc

            lax.fori_loop(zlo_ref[rng], zhi_ref[rng], start, 0)
            return carry

        def wait_range(rng, carry):
            def wait(c, cc):
                chunk_copy(c).wait()
                return cc

            lax.fori_loop(zlo_ref[rng], zhi_ref[rng], wait, 0)
            return carry

        lax.fori_loop(0, N_EXPERTS + 1, start_range, 0)
        lax.fori_loop(0, N_EXPERTS + 1, wait_range, 0)

    def copies(t):
        return _row_copies(i, t, dest_ref, lambda k, d: pltpu.make_async_copy(_row(h_ref, t), _row(xs_out, d), sem))

    def issue(t, carry):
        for k, cp in enumerate(copies(t)):
            cp.start(priority=k)
        return carry

    def wait(t, carry):
        for cp in copies(t):
            cp.wait()
        return carry

    lax.fori_loop(0, TM, issue, 0, unroll=8)
    lax.fori_loop(0, TM, wait, 0, unroll=8)


def _dispatch(dest, zlo, zhi, h2t):
    return pl.pallas_call(
        _dispatch_kernel,
        grid_spec=pltpu.PrefetchScalarGridSpec(
            num_scalar_prefetch=3,
            grid=(N_TILES,),
            in_specs=[pl.BlockSpec((TM * ROW_SUB, LANES), lambda i, *_: (i, 0))],
            out_specs=pl.BlockSpec(memory_space=pl.ANY),
            scratch_shapes=[pltpu.VMEM((ZCHUNK * ROW_SUB, LANES), F32),
                            pltpu.SemaphoreType.DMA, pltpu.SemaphoreType.DMA]),
        out_shape=jax.ShapeDtypeStruct((MOE_ROWS * ROW_SUB, LANES), F32),
        compiler_params=pltpu.CompilerParams(dimension_semantics=("arbitrary",)),
        name="moe_dispatch",
    )(dest, zlo, zhi, h2t)


def _expert_kernel(be_ref, bv_ref, first_ref, slot_ref, next_ref, xs_ref, wg_hbm, wu_hbm, wd_hbm, o_ref,
                   wg_f, wu_f, wd_f, wg_s, wu_s, wd_s, wsem):
    b = pl.program_id(0)

    def weight_copies(e, slot):
        return [pltpu.make_async_copy(src.at[e], dst.at[slot], wsem.at[slot])
                for src, dst in ((wg_hbm, wg_f), (wu_hbm, wu_f), (wd_hbm, wd_f))]

    @pl.when(b == 0)
    def _():
        for cp in weight_copies(be_ref[0], 0):
            cp.start()

    @pl.when(first_ref[b] == 1)
    def _():
        slot = slot_ref[b]
        for cp in weight_copies(be_ref[b], slot):
            cp.wait()
        wg_s[...] = wg_f[slot].astype(BF16)
        wu_s[...] = wu_f[slot].astype(BF16)
        wd_s[...] = wd_f[slot].astype(BF16)

        @pl.when(next_ref[b] >= 0)
        def _():
            for cp in weight_copies(next_ref[b], 1 - slot):
                cp.start()

    @pl.when(bv_ref[b] > 0)
    def _():
        x = jnp.concatenate([xs_ref[pl.ds(s, MOE_R, stride=ROW_SUB), :] for s in range(ROW_SUB)], axis=1)
        xb = x.astype(BF16)
        g = jnp.dot(xb, wg_s[...], preferred_element_type=F32)
        u = jnp.dot(xb, wu_s[...], preferred_element_type=F32)
        a = (g * _sigmoid(g) * u).astype(BF16)
        o = jnp.dot(a, wd_s[...], preferred_element_type=F32)
        for s in range(ROW_SUB):
            o_ref[pl.ds(s, MOE_R, stride=ROW_SUB), :] = o[:, s * LANES:(s + 1) * LANES]

    @pl.when(bv_ref[b] == 0)
    def _():
        o_ref[...] = jnp.zeros_like(o_ref)


def _experts(blk_e, blk_valid, xs, w_g, w_u, w_d):
    idx = np.arange(MOE_BLOCKS)
    used = blk_valid > 0
    first = used & ((idx == 0) | (blk_e != jnp.roll(blk_e, 1)))
    ordinal = jnp.sum(first[None, :] & (idx[None, :] <= idx[:, None]), axis=1) - 1
    later_first = jnp.where(first[None, :] & (idx[None, :] > idx[:, None]), idx[None, :], MOE_BLOCKS)
    nxt = jnp.min(later_first, axis=1)
    next_e = jnp.where(nxt < MOE_BLOCKS, jnp.sum(jnp.where(idx[None, :] == nxt[:, None], blk_e[None, :], 0), axis=1), -1)
    plan = [blk_e, blk_valid, first.astype(jnp.int32), (ordinal % 2).astype(jnp.int32), next_e.astype(jnp.int32)]

    def rows(b, be, bv, *_):
        return (jnp.where(bv[b] > 0, b, MOE_BLOCKS - 1), 0)

    hbm = pl.BlockSpec(memory_space=pl.ANY)
    return pl.pallas_call(
        _expert_kernel,
        grid_spec=pltpu.PrefetchScalarGridSpec(
            num_scalar_prefetch=len(plan),
            grid=(MOE_BLOCKS,),
            in_specs=[pl.BlockSpec((MOE_R * ROW_SUB, LANES), rows), hbm, hbm, hbm],
            out_specs=pl.BlockSpec((MOE_R * ROW_SUB, LANES), lambda b, *_: (b, 0)),
            scratch_shapes=[pltpu.VMEM((2, D_MODEL, D_EXPERT), F32),
                            pltpu.VMEM((2, D_MODEL, D_EXPERT), F32),
                            pltpu.VMEM((2, D_EXPERT, D_MODEL), F32),
                            pltpu.VMEM((D_MODEL, D_EXPERT), BF16),
                            pltpu.VMEM((D_MODEL, D_EXPERT), BF16),
                            pltpu.VMEM((D_EXPERT, D_MODEL), BF16),
                            pltpu.SemaphoreType.DMA((2,))]),
        out_shape=jax.ShapeDtypeStruct((MOE_ROWS * ROW_SUB, LANES), F32),
        compiler_params=pltpu.CompilerParams(dimension_semantics=("arbitrary",), vmem_limit_bytes=VMEM_LIMIT),
        name="moe_experts",
    )(*plan, xs, w_g, w_u, w_d)


def _combine_kernel(dest_ref, o_hbm, x2_ref, ri_ref, gate_ref, gf_ref, yp_ref, ys_ref, obuf, sem):
    i = pl.program_id(0)
    slot = i % 2
    tile_rows = TM * ROW_SUB

    def issue_tile(tile, slot_):
        def issue(t, carry):
            cps = _row_copies(tile, t, dest_ref, lambda k, d: pltpu.make_async_copy(
                _row(o_hbm, d), _row(obuf, (2 * slot_ + k) * TM + t), sem.at[slot_]))
            for k, cp in enumerate(cps):
                cp.start(priority=k)
            return carry

        lax.fori_loop(0, TM, issue, 0, unroll=8)

    @pl.when(i == 0)
    def _():
        issue_tile(0, 0)

    @pl.when(i + 1 < N_TILES)
    def _():
        issue_tile(i + 1, 1 - slot)

    for k in range(2):
        start = pl.multiple_of((2 * slot + k) * tile_rows, tile_rows)
        pltpu.make_async_copy(o_hbm.at[pl.ds(0, tile_rows), :], obuf.at[pl.ds(start, tile_rows), :],
                              sem.at[slot]).wait()

    ri = ri_ref[...]
    g1 = ri[:, 2:3]
    g2 = ri[:, 3:4]
    base = 2 * slot * tile_rows
    moe = jnp.concatenate(
        [g1 * obuf[pl.ds(base + s, TM, stride=ROW_SUB), :] + g2 * obuf[pl.ds(base + tile_rows + s, TM, stride=ROW_SUB), :]
         for s in range(ROW_SUB)], axis=1)
    x3 = x2_ref[...] + _per_chunk(gate_ref) * moe
    y = _rmsnorm(x3, gf_ref[...])

    @pl.when(i < P_TILES)
    def _():
        yp_ref[0] = y

    @pl.when(i >= P_TILES)
    def _():
        ys_ref[...] = y.reshape(CH_PER_TILE, DEC_SEQ, D_MODEL)


def _combine(dest, o_rows, x2, rinfo, modc, g_final):
    xp_spec, xs_spec = _xp_spec(), _xs_spec()
    return pl.pallas_call(
        _combine_kernel,
        grid_spec=pltpu.PrefetchScalarGridSpec(
            num_scalar_prefetch=1,
            grid=(N_TILES,),
            in_specs=[pl.BlockSpec(memory_space=pl.ANY),
                      pl.BlockSpec((TM, D_MODEL), lambda i, *_: (i, 0)),
                      pl.BlockSpec((TM, LANES), lambda i, *_: (i, 0)),
                      pl.BlockSpec((CH_PER_TILE, 1, D_MODEL), lambda i, *_: (i, 0, 5)),
                      pl.BlockSpec((1, D_MODEL), lambda i, *_: (0, 0))],
            out_specs=[pl.BlockSpec(xp_spec.block_shape, lambda i, *_: xp_spec.index_map(i)),
                       pl.BlockSpec(xs_spec.block_shape, lambda i, *_: xs_spec.index_map(i))],
            scratch_shapes=[pltpu.VMEM((4 * TM * ROW_SUB, LANES), F32), pltpu.SemaphoreType.DMA((2,))]),
        out_shape=[jax.ShapeDtypeStruct((BATCH, SEQ, D_MODEL), F32),
                   jax.ShapeDtypeStruct((DEC_BATCH, DEC_SEQ, D_MODEL), F32)],
        compiler_params=pltpu.CompilerParams(dimension_semantics=("arbitrary",), vmem_limit_bytes=VMEM_LIMIT),
        name="moe_combine",
    )(dest, o_rows, x2, rinfo, modc, g_final)


def _moe_plan(rt, cnt):
    experts = np.arange(N_EXPERTS)
    tile_cnt = cnt.reshape(N_TILES, SUBLANES, LANES)[:, 0, :N_EXPERTS]
    counts = jnp.sum(tile_cnt, axis=0)
    padded = jnp.ceil(counts / MOE_R) * MOE_R
    pad_end = jnp.sum(padded[:, None] * (experts[:, None] <= experts[None, :]), axis=0)
    base = pad_end - padded
    tiles = np.arange(N_TILES)
    earlier_tiles = (tiles[None, :, None] < tiles[:, None, None])
    tile_base = base[None, :] + jnp.sum(tile_cnt[None, :, :] * earlier_tiles, axis=1)
    eid = rt[0:2].reshape(1, 2, N_TILES, TM)
    rank = rt[4:6].reshape(2, N_TILES, TM)
    pick = eid == experts.astype(np.float32).reshape(N_EXPERTS, 1, 1, 1)
    dest = jnp.sum(jnp.where(pick, tile_base.T[:, None, :, None], 0.0), axis=0) + rank
    blk_start = (np.arange(MOE_BLOCKS) * MOE_R).astype(np.float32)
    blk_e = jnp.minimum(jnp.sum((blk_start[:, None] >= pad_end[None, :]).astype(F32), axis=1), N_EXPERTS - 1)
    mine = blk_e[:, None] == experts.astype(np.float32)[None, :]
    blk_fill = jnp.sum(jnp.where(mine, (counts + base)[None, :], 0.0), axis=1) - blk_start
    blk_valid = jnp.clip(blk_fill, 0, MOE_R)
    zlo = jnp.concatenate([jnp.floor((base + counts) / ZCHUNK), pad_end[-1:] / ZCHUNK])
    zhi = jnp.concatenate([pad_end / ZCHUNK, jnp.full((1,), MOE_ROWS // ZCHUNK, F32)])
    return (dest.reshape(-1).astype(jnp.int32), blk_e.astype(jnp.int32), blk_valid.astype(jnp.int32),
            zlo.astype(jnp.int32), zhi.astype(jnp.int32))


def kernel(x_prompt, x_sample, c_prompt, c_sample, cache_win_k, cache_win_v, state_conv, state_C, state_n, state_m, w_ada, b_ada, g_norm_mix, g_norm_ffn, w_in, attn_sinks, conv_w, conv_b, b_igate, b_fgate, g_mhnorm, w_out, w_router_group, b_router_group, w_router_expert, b_router_expert, w_exp_gate, w_exp_up, w_exp_down, g_final):
    l = 0
    n_streams = BATCH + DEC_BATCH
    pad_streams = -n_streams % (2 * SUBLANES)
    mod = _modulation(jnp.concatenate([c_prompt, c_sample, jnp.zeros((pad_streams, D_MODEL), F32)], axis=0),
                      w_ada[l], b_ada[l])
    chunk_stream = np.concatenate([np.repeat(np.arange(BATCH), SEQ // CHUNK), BATCH + np.arange(DEC_BATCH)])
    modc = mod[chunk_stream].reshape(N_CHUNKS, 1, 6 * D_MODEL)

    wi = w_in[l]
    s_q, s_k, s_v, s_qkm, s_vm, s_ig, s_fg = 0, 512, 640, 768, 1792, 2304, 2308
    s_om = 2312
    w_in_r = jnp.concatenate(
        [wi[:, s_q:s_qkm], wi[:, s_qkm:s_vm], wi[:, s_vm:s_ig], wi[:, s_om:], wi[:, s_ig:s_om],
         jnp.zeros((D_MODEL, LANES - 2 * M_HEADS), F32)], axis=1).astype(BF16)
    q, kv, qkm, vm, om, gates = _inproj(x_prompt, x_sample, modc, g_norm_mix[l].reshape(1, -1), w_in_r)

    cache_kv = jnp.concatenate([cache_win_k[l].reshape(DEC_BATCH * WINDOW, A_KV_WIDTH),
                                cache_win_v[l].reshape(DEC_BATCH * WINDOW, A_KV_WIDTH)], axis=1)
    att_p, att_s = _attention(attn_sinks[l], q, kv, cache_kv)

    gbias = jnp.concatenate([b_igate[l], b_fgate[l], jnp.zeros((LANES - 2 * M_HEADS,), F32)]).reshape(1, LANES)
    common = (qkm, vm, om, gates, conv_w[l], conv_b[l].reshape(1, -1), gbias, g_mhnorm[l].reshape(1, -1))
    zeros_p = (jnp.zeros((BATCH, SUBLANES, 2 * M_WIDTH), F32),
               jnp.zeros((BATCH, M_HEADS, M_HEAD_DIM, M_HEAD_DIM), F32),
               jnp.zeros((BATCH, M_HEADS, M_HEAD_DIM), F32),
               jnp.zeros((BATCH, 1, LANES), F32))
    LP = 256
    hm_p, C_p, n_p, m_p = _mlstm(*common, *zeros_p, nb=BATCH, L=LP, nc=SEQ // LP, ns=1, row_off=0,
                                 name="mlstm_prompt")
    carry_s = jnp.concatenate([jnp.zeros((DEC_BATCH, SUBLANES - (CONV_WIDTH - 1), 2 * M_WIDTH), F32),
                               state_conv[l]], axis=1)
    m0_s = jnp.pad(state_m[l], ((0, 0), (0, LANES - M_HEADS))).reshape(DEC_BATCH, 1, LANES)
    hm_s, C_s, n_s, m_s = _mlstm(*common, carry_s, state_C[l], state_n[l], m0_s,
                                 nb=DEC_BATCH, L=DEC_SEQ, nc=1, ns=4, row_off=N_P, name="mlstm_sample")

    w_router = jnp.concatenate([w_router_group[l], w_router_expert[l],
                                jnp.zeros((D_MODEL, LANES - N_GROUPS - N_EXPERTS), F32)], axis=1)
    w_router_hi = w_router.astype(BF16)
    w_router = jnp.stack([w_router_hi, (w_router - w_router_hi.astype(F32)).astype(BF16)])
    b_router =jnp.concatenate([b_router_group[l], b_router_expert[l],
                                jnp.zeros((LANES - N_GROUPS - N_EXPERTS,), F32)]).reshape(1, LANES)
    x2, h2t, rinfo, rt, cnt = _outproj(x_prompt, x_sample, att_p, att_s, hm_p, hm_s, modc,
                                       g_norm_ffn[l].reshape(1, -1), w_out[l].astype(BF16), w_router, b_router)

    dest, blk_e, blk_valid, zlo, zhi = _moe_plan(rt, cnt)
    xs_rows = _dispatch(dest, zlo, zhi, h2t)
    o_rows = _experts(blk_e, blk_valid, xs_rows, w_exp_gate[l], w_exp_up[l], w_exp_down[l])
    y_prompt, y_sample = _combine(dest, o_rows, x2, rinfo, modc, g_final.reshape(1, -1))

    kv_p = kv.reshape(N_TOK // WINDOW, WINDOW, 2 * A_KV_WIDTH)[SEQ // WINDOW - 1:N_P // WINDOW:SEQ // WINDOW]
    win_k_p = kv_p[..., :A_KV_WIDTH].reshape(1, BATCH, WINDOW, A_KV_HEADS, A_HEAD_DIM)
    win_v_p = kv_p[..., A_KV_WIDTH:].reshape(1, BATCH, WINDOW, A_KV_HEADS, A_HEAD_DIM)
    kv_s = kv[N_P:].reshape(DEC_BATCH, DEC_SEQ, 2 * A_KV_WIDTH)
    win_k_s = jnp.concatenate([cache_win_k[l][:, DEC_SEQ:],
                               kv_s[..., :A_KV_WIDTH].reshape(DEC_BATCH, DEC_SEQ, A_KV_HEADS, A_HEAD_DIM)], axis=1)[None]
    win_v_s = jnp.concatenate([cache_win_v[l][:, DEC_SEQ:],
                               kv_s[..., A_KV_WIDTH:].reshape(DEC_BATCH, DEC_SEQ, A_KV_HEADS, A_HEAD_DIM)], axis=1)[None]
    qkm_c = qkm.reshape(N_CHUNKS, CHUNK, 2 * M_WIDTH)
    tail = slice(CHUNK - (CONV_WIDTH - 1), CHUNK)
    conv_p = qkm_c[SEQ // CHUNK - 1:N_P // CHUNK:SEQ // CHUNK, tail][None]
    conv_s = qkm_c[N_P // CHUNK:, tail][None]
    return (y_prompt, y_sample,
            win_k_p, win_v_p, conv_p, C_p[None], n_p[None], m_p[:, 0, :M_HEADS][None],
            win_k_s, win_v_s, conv_s, C_s[None], n_s[None], m_s[:, 0, :M_HEADS][None])
```

```python
import functools

import numpy as np
import jax
import jax.numpy as jnp
from jax import lax
from jax.experimental import pallas as pl
from jax.experimental.pallas import tpu as pltpu

F32 = jnp.float32
BF16 = jnp.bfloat16
HIGHEST = lax.Precision.HIGHEST

LANES = 128
SUBLANES = 8

D_MODEL = 1024
BATCH = 8
SEQ = 2048
DEC_BATCH = 32
DEC_SEQ = 64
PAST_LEN = 4096
CHUNK = 64
A_WIDTH = 512
A_HEAD_DIM = 64
A_HEADS = 8
A_KV_HEADS = 2
A_GROUP = 4
A_KV_WIDTH = 128
WINDOW = 128
ROT_DIM = 16
ROPE_THETA = 500000.0
M_WIDTH = 512
M_HEADS = 4
M_HEAD_DIM = 128
CONV_WIDTH = 4
N_GROUPS = 4
EXPERTS_PER_GROUP = 8
N_EXPERTS = 32
D_EXPERT = 512
EPS = 1e-6

N_P = BATCH * SEQ
N_S = DEC_BATCH * DEC_SEQ
N_TOK = N_P + N_S
N_CHUNKS = N_TOK // CHUNK
TM = 512
N_TILES = N_TOK // TM
P_TILES = N_P // TM
CH_PER_TILE = TM // CHUNK
ROW_SUB = D_MODEL // LANES
N_ASSIGN = 2 * N_TOK
MOE_R = 512
MOE_BLOCKS = N_ASSIGN // MOE_R + N_EXPERTS
MOE_ROWS = MOE_BLOCKS * MOE_R
C_Q, C_K, C_V, C_QKM, C_VM, C_OM, C_G = 0, 512, 640, 768, 1792, 2304, 2816
D_IN_PAD = 2944
VMEM_LIMIT = 48 * 1024 * 1024


def _sigmoid(x):
    return 1.0 / (1.0 + jnp.exp(-x))


def _mod_kernel(c_ref, w_ref, b_ref, o_ref):
    c = c_ref[...]
    s = c * _sigmoid(c)
    w = w_ref[...]
    s_hi, w_hi = s.astype(BF16), w.astype(BF16)
    s_lo, w_lo = (s - s_hi.astype(F32)).astype(BF16), (w - w_hi.astype(F32)).astype(BF16)
    o_ref[...] = (jnp.dot(s_hi, w_hi, preferred_element_type=F32) + jnp.dot(s_lo, w_hi, preferred_element_type=F32)
                  + jnp.dot(s_hi, w_lo, preferred_element_type=F32)) + b_ref[...]


def _modulation(c_all, w_ada, b_ada):
    n = c_all.shape[0]
    bn = 512
    return pl.pallas_call(
        _mod_kernel,
        grid=(6 * D_MODEL // bn,),
        in_specs=[pl.BlockSpec((n, D_MODEL), lambda j: (0, 0)),
                  pl.BlockSpec((D_MODEL, bn), lambda j: (0, j)),
                  pl.BlockSpec((1, bn), lambda j: (0, j))],
        out_specs=pl.BlockSpec((n, bn), lambda j: (0, j)),
        out_shape=jax.ShapeDtypeStruct((n, 6 * D_MODEL), F32),
        name="modulation",
    )(c_all, w_ada, b_ada.reshape(1, -1))


def _xp_spec():
    def idx(i):
        t = jnp.minimum(i, P_TILES - 1)
        return (t // (SEQ // TM), t % (SEQ // TM), 0)
    return pl.BlockSpec((1, TM, D_MODEL), idx)


def _xs_spec():
    return pl.BlockSpec((CH_PER_TILE, DEC_SEQ, D_MODEL), lambda i: (jnp.maximum(i - P_TILES, 0), 0, 0))


def _mod_spec(comp):
    return pl.BlockSpec((CH_PER_TILE, D_MODEL), lambda i, *_: (i, comp))


def _load_x(xp_ref, xs_ref, xbuf):
    i = pl.program_id(0)

    @pl.when(i < P_TILES)
    def _():
        xbuf[...] = xp_ref[0]

    @pl.when(i >= P_TILES)
    def _():
        xbuf[...] = xs_ref[...].reshape(TM, D_MODEL)

    return xbuf[...]


def _per_chunk(m_ref):
    m = m_ref[...]
    return jnp.concatenate([jnp.broadcast_to(m[c:c + 1], (CHUNK, D_MODEL)) for c in range(CH_PER_TILE)], axis=0)


def _rmsnorm(x, g):
    return x * lax.rsqrt(jnp.mean(x * x, axis=-1, keepdims=True) + EPS) * g


def _rope(x, cos, sa, sb):
    n = x.shape[1]
    rep = n // LANES
    if rep > 1:
        cos = jnp.concatenate([cos] * rep, axis=1)
        sa = jnp.concatenate([sa] * rep, axis=1)
        sb = jnp.concatenate([sb] * rep, axis=1)
    return x * cos + pltpu.roll(x, n - ROT_DIM // 2, 1) * sa + pltpu.roll(x, ROT_DIM // 2, 1) * sb


def _inproj_kernel(xp_ref, xs_ref, sh_ref, sc_ref, g_ref, w_ref, cos_ref, sa_ref, sb_ref,
                   q_ref, kv_ref, qkm_ref, vm_ref, om_ref, gt_ref, xbuf):
    x = _load_x(xp_ref, xs_ref, xbuf)
    h = _rmsnorm(x, g_ref[...]) * (1.0 + _per_chunk(sc_ref)) + _per_chunk(sh_ref)
    hb = h.astype(BF16)

    def proj(a, b):
        return jnp.dot(hb, w_ref[:, a:b], preferred_element_type=F32)

    cos, sa, sb = cos_ref[...], sa_ref[...], sb_ref[...]
    q_ref[...] = _rope(proj(C_Q, C_K), cos, sa, sb).astype(BF16)
    kv_ref[:, :A_KV_WIDTH] = _rope(proj(C_K, C_V), cos, sa, sb)
    kv_ref[:, A_KV_WIDTH:] = proj(C_V, C_QKM)
    qkm_ref[...] = proj(C_QKM, C_VM)
    vm_ref[...] = proj(C_VM, C_OM).astype(BF16)
    om_ref[...] = proj(C_OM, C_G)
    gt_ref[...] = proj(C_G, D_IN_PAD)


def _rope_tables():
    pos = np.concatenate([np.arange(SEQ), np.tile(PAST_LEN + np.arange(DEC_SEQ), CH_PER_TILE)]).astype(np.float64)
    inv_freq = ROPE_THETA ** (-np.arange(0, ROT_DIM, 2, dtype=np.float64) / ROT_DIM)
    lane = np.arange(LANES)
    hl = lane % A_HEAD_DIM
    ang = pos[:, None] * inv_freq[hl % (ROT_DIM // 2)][None, :]
    rot = (hl < ROT_DIM)[None, :]
    lo = (hl < ROT_DIM // 2)[None, :]
    cos = np.where(rot, np.cos(ang), 1.0)
    sa = np.where(lo, -np.sin(ang), 0.0)
    sb = np.where(rot & ~lo, np.sin(ang), 0.0)
    return [jnp.asarray(t, F32) for t in (cos, sa, sb)]


def _inproj(x_prompt, x_sample, modc, g_mix, w_in_r):
    cos, sa, sb = _rope_tables()
    tab_spec = pl.BlockSpec((TM, LANES), lambda i: (jnp.where(i < P_TILES, i % (SEQ // TM), SEQ // TM), 0))

    def out(n, dtype=F32):
        return pl.BlockSpec((TM, n), lambda i: (i, 0)), jax.ShapeDtypeStruct((N_TOK, n), dtype)

    outs = [out(A_WIDTH, BF16), out(2 * A_KV_WIDTH), out(2 * M_WIDTH), out(M_WIDTH, BF16), out(M_WIDTH), out(LANES)]
    return pl.pallas_call(
        _inproj_kernel,
        grid=(N_TILES,),
        in_specs=[_xp_spec(), _xs_spec(), _mod_spec(0), _mod_spec(1),
                  pl.BlockSpec((1, D_MODEL), lambda i: (0, 0)),
                  pl.BlockSpec((D_MODEL, D_IN_PAD), lambda i: (0, 0)),
                  tab_spec, tab_spec, tab_spec],
        out_specs=[o[0] for o in outs],
        out_shape=[o[1] for o in outs],
        scratch_shapes=[pltpu.VMEM((TM, D_MODEL), F32)],
        compiler_params=pltpu.CompilerParams(vmem_limit_bytes=VMEM_LIMIT),
        name="inproj",
    )(x_prompt, x_sample, modc, modc, g_mix, w_in_r, cos, sa, sb)


ATT_NS = 1


def _attn_chunk_kernel(sink_ref, q_ref, prev_ref, cur_ref, o_ref):
    for st in range(ATT_NS):
        rows = slice(st * CHUNK, (st + 1) * CHUNK)
        _attn_chunk(sink_ref, q_ref[rows, :], prev_ref[st * WINDOW:(st + 1) * WINDOW, :], cur_ref[rows, :],
                    o_ref, rows)


def _attn_chunk(sink_ref, q, prev, cur, o_ref, rows):
    q = q * (A_HEAD_DIM ** -0.5)
    kv = jnp.concatenate([prev, cur], axis=0)
    for g in range(A_KV_HEADS):
        kg = kv[:, g * A_HEAD_DIM:(g + 1) * A_HEAD_DIM].astype(BF16)
        vg = kv[:, A_KV_WIDTH + g * A_HEAD_DIM:A_KV_WIDTH + (g + 1) * A_HEAD_DIM].astype(BF16)
        heads = [g * A_GROUP + i for i in range(A_GROUP)]
        qc = jnp.concatenate([q[:, h * A_HEAD_DIM:(h + 1) * A_HEAD_DIM] for h in heads], axis=0).astype(BF16)
        snk = jnp.concatenate([jnp.full((CHUNK, 1), sink_ref[h], F32) for h in heads], axis=0)
        s = lax.dot_general(qc, kg, (((1,), (1,)), ((), ())), preferred_element_type=F32)
        mx = jnp.maximum(jnp.max(s, axis=-1, keepdims=True), snk)
        p = jnp.exp(s - mx)
        den = jnp.sum(p, axis=-1, keepdims=True) + jnp.exp(snk - mx)
        o = jnp.dot(p.astype(BF16), vg, preferred_element_type=F32) / den
        for i, h in enumerate(heads):
            o_ref[rows, h * A_HEAD_DIM:(h + 1) * A_HEAD_DIM] = o[i * CHUNK:(i + 1) * CHUNK]


ATT_TQ = 512
ATT_QB = 2 * CHUNK


def _attn_band_kernel(sink_ref, q_ref, prev_ref, cur_ref, o_ref, att_t):
    nk = ATT_QB + WINDOW
    q = (q_ref[...] * (A_HEAD_DIM ** -0.5)).astype(BF16)
    kv = jnp.concatenate([prev_ref[...], cur_ref[...]], axis=0)
    k2 = kv[:, :A_KV_WIDTH]
    k2r = pltpu.roll(k2, A_HEAD_DIM, 1)
    low = lax.broadcasted_iota(jnp.int32, k2.shape, 1) < A_HEAD_DIM
    k_placed = {(0, 0): jnp.where(low, k2, 0.0), (0, 1): jnp.where(low, 0.0, k2r),
                (1, 0): jnp.where(low, k2r, 0.0), (1, 1): jnp.where(low, 0.0, k2)}
    k_placed = {key: val.astype(BF16) for key, val in k_placed.items()}
    v_t = kv[:, A_KV_WIDTH:].T.astype(BF16)
    key_chunk = lax.broadcasted_iota(jnp.int32, (nk, ATT_QB), 0) // CHUNK
    q_chunk = lax.broadcasted_iota(jnp.int32, (nk, ATT_QB), 1) // CHUNK
    band = (key_chunk >= q_chunk) & (key_chunk <= q_chunk + WINDOW // CHUNK)
    has_history = pl.program_id(1) > 0
    for blk in range(ATT_TQ // ATT_QB):
        keys = slice(blk * ATT_QB, blk * ATT_QB + nk)
        qrows = slice(blk * ATT_QB, (blk + 1) * ATT_QB)
        valid = band & ((key_chunk >= WINDOW // CHUNK) | has_history) if blk == 0 else band
        for h in range(A_HEADS):
            g = h // A_GROUP
            s_t = lax.dot_general(k_placed[(g, h % 2)][keys], q[qrows, (h // 2) * LANES:(h // 2 + 1) * LANES],
                                  (((1,), (1,)), ((), ())), preferred_element_type=F32)
            s_t = jnp.where(valid, s_t, -jnp.inf)
            snk = sink_ref[h]
            mx = jnp.maximum(jnp.max(s_t, axis=0, keepdims=True), snk)
            p_t = jnp.exp(s_t - mx)
            den = jnp.sum(p_t, axis=0, keepdims=True) + jnp.exp(snk - mx)
            o_t = jnp.dot(v_t[:, keys], p_t.astype(BF16), preferred_element_type=F32)
            att_t[h * A_HEAD_DIM:(h + 1) * A_HEAD_DIM, qrows] = o_t[g * A_HEAD_DIM:(g + 1) * A_HEAD_DIM] / den
    o_ref[...] = att_t[...].T.astype(BF16)


def _attention(sinks, q, kv, cache_kv):
    smem = pl.BlockSpec(memory_space=pltpu.SMEM)
    tq = ATT_TQ
    nq = SEQ // tq
    att = pl.pallas_call(
        _attn_band_kernel,
        grid=(BATCH, nq),
        scratch_shapes=[pltpu.VMEM((A_WIDTH, ATT_TQ), F32)],
        in_specs=[smem,
                  pl.BlockSpec((tq, A_WIDTH), lambda b, j: (b * nq + j, 0)),
                  pl.BlockSpec((WINDOW, 2 * A_KV_WIDTH),
                               lambda b, j: (jnp.maximum((b * nq + j) * (tq // WINDOW) - 1, 0), 0)),
                  pl.BlockSpec((tq, 2 * A_KV_WIDTH), lambda b, j: (b * nq + j, 0))],
        out_specs=pl.BlockSpec((tq, A_WIDTH), lambda b, j: (b * nq + j, 0)),
        out_shape=jax.ShapeDtypeStruct((N_P, A_WIDTH), BF16),
        name="attn_prompt",
    )(sinks, q, kv, kv)
    rows = ATT_NS * DEC_SEQ
    off = N_P // rows
    att_s = pl.pallas_call(
        _attn_chunk_kernel,
        grid=(DEC_BATCH // ATT_NS,),
        in_specs=[smem,
                  pl.BlockSpec((rows, A_WIDTH), lambda b: (off + b, 0)),
                  pl.BlockSpec((ATT_NS * WINDOW, 2 * A_KV_WIDTH), lambda b: (b, 0)),
                  pl.BlockSpec((rows, 2 * A_KV_WIDTH), lambda b: (off + b, 0))],
        out_specs=pl.BlockSpec((rows, A_WIDTH), lambda b: (b, 0)),
        out_shape=jax.ShapeDtypeStruct((N_S, A_WIDTH), F32),
        name="attn_sample",
    )(sinks, q, cache_kv, kv)
    return att, att_s


def _conv4(x, w, b):
    x1 = pltpu.roll(x, 1, 0)
    near = b + x * w[3:4] + x1 * w[2:3]
    far = x * w[1:2] + x1 * w[0:1]
    return near + pltpu.roll(far, 2, 0)


def _mlstm_kernel(qkm_ref, vm_ref, om_ref, gt_ref, cw_ref, cb_ref, gb_ref, gmh_ref,
                  carry0_ref, c0_ref, n0_ref, m0_ref,
                  hm_ref, cout_ref, nout_ref, mout_ref, c_s, n_s, m_s, carry_s, *, L, NS):
    c = pl.program_id(1)

    @pl.when(c == 0)
    def _():
        c_s[...] = c0_ref[...]
        n_s[...] = n0_ref[...]
        m_s[...] = m0_ref[...]
        carry_s[...] = carry0_ref[...]

    w = cw_ref[...]
    b = cb_ref[...]
    gmh = gmh_ref[...]
    lane = lax.broadcasted_iota(jnp.int32, (L, LANES), 1)
    causal = lax.broadcasted_iota(jnp.int32, (L, L), 0) >= lax.broadcasted_iota(jnp.int32, (L, L), 1)
    ones_causal = causal.astype(F32)
    sel = (lax.broadcasted_iota(jnp.int32, (SUBLANES, LANES), 0)
           == lax.broadcasted_iota(jnp.int32, (SUBLANES, LANES), 1)).astype(F32)
    lane1 = lax.broadcasted_iota(jnp.int32, (1, LANES), 1)
    for st in range(NS):
        _mlstm_stream(st, qkm_ref, vm_ref, om_ref, gt_ref, gb_ref, hm_ref, c_s, n_s, m_s, carry_s,
                      w, b, gmh, lane, causal, ones_causal, sel, lane1, L)

    @pl.when(c == pl.num_programs(1) - 1)
    def _():
        cout_ref[...] = c_s[...]
        nout_ref[...] = n_s[...]
        mout_ref[...] = m_s[...]


def _mlstm_stream(st, qkm_ref, vm_ref, om_ref, gt_ref, gb_ref, hm_ref, c_s, n_s, m_s, carry_s,
                  w, b, gmh, lane, causal, ones_causal, sel, lane1, L):
    x = qkm_ref[st]
    y = _conv4(x, w, b)
    y8 = _conv4(jnp.concatenate([carry_s[st], x[:SUBLANES]], axis=0), w, b)
    y = jnp.concatenate([y8[SUBLANES:], y[SUBLANES:]], axis=0)
    carry_s[st] = x[L - SUBLANES:]
    a = y * _sigmoid(y)
    qa = a[:, :M_WIDTH] * (M_HEAD_DIM ** -0.5)
    ka = a[:, M_WIDTH:]
    v = vm_ref[st]
    om = om_ref[st]

    z = gt_ref[st] + gb_ref[...]
    f_log = jnp.minimum(z, 0.0) - jnp.log1p(jnp.exp(-jnp.abs(z)))
    val = jnp.where(lane < M_HEADS, z, f_log)
    cum = jnp.dot(ones_causal, val, preferred_element_type=F32, precision=HIGHEST)
    G = jnp.where(lane < M_HEADS, val, cum)
    GT = lax.dot_general(sel, G, (((1,), (1,)), ((), ())), preferred_element_type=F32, precision=HIGHEST)

    m_row = m_s[st]
    outs = []
    for h in range(M_HEADS):
        a_col = G[:, M_HEADS + h:M_HEADS + h + 1]
        i_col = G[:, h:h + 1]
        i_row = GT[h:h + 1, :]
        b_row = GT[M_HEADS + h:M_HEADS + h + 1, :]
        m_prev = m_row[:, h:h + 1]
        dm = jnp.where(causal, a_col - b_row + i_row, -jnp.inf)
        inter = a_col + m_prev
        m_t = jnp.maximum(inter, jnp.max(dm, axis=-1, keepdims=True))
        w_intra = jnp.exp(dm - m_t)
        w_inter = jnp.exp(inter - m_t)
        sl = slice(h * M_HEAD_DIM, (h + 1) * M_HEAD_DIM)
        qh, kh, vh = qa[:, sl], ka[:, sl], v[:, sl]
        qb = qh.astype(BF16)
        scores = lax.dot_general(qb, kh.astype(BF16), (((1,), (1,)), ((), ())), preferred_element_type=F32) * w_intra
        ch = c_s[st, h]
        nh = n_s[st, h:h + 1, :]
        num = (jnp.dot(scores.astype(BF16), vh.astype(BF16), preferred_element_type=F32)
               + w_inter * lax.dot_general(qb, ch.astype(BF16), (((1,), (1,)), ((), ())), preferred_element_type=F32))
        den = jnp.sum(scores, axis=-1, keepdims=True) + w_inter * jnp.sum(qh * nh, axis=-1, keepdims=True)
        hv = num / jnp.maximum(jnp.abs(den), jnp.exp(-m_t))
        hn = hv * lax.rsqrt(jnp.mean(hv * hv, axis=-1, keepdims=True) + EPS)
        outs.append(hn * gmh[:, sl] * _sigmoid(om[:, sl]))
        m_new = m_t[L - 1:L, :]
        a_last = a_col[L - 1:L, :]
        w_end = jnp.exp(a_last - a_col + i_col - m_new)
        decay = jnp.exp(a_last + m_prev - m_new)
        kw = kh * w_end
        c_s[st, h] = decay * ch + lax.dot_general(vh.astype(BF16), kw.astype(BF16), (((0,), (0,)), ((), ())),
                                                  preferred_element_type=F32)
        n_s[st, h:h + 1, :] = decay * nh + jnp.sum(kw, axis=0, keepdims=True)
        m_row = jnp.where(lane1 == h, m_new, m_row)
    m_s[st] = m_row
    hm_ref[st] = jnp.concatenate(outs, axis=1).astype(BF16)


def _mlstm(qkm, vm, om, gates, conv_w, conv_b, gbias, g_mh, carry0, c0, n0, m0, *, nb, L, nc, ns, row_off, name):
    S = nc * L
    first = row_off // S // ns

    def tok(arr):
        n = arr.shape[-1]
        return arr.reshape(N_TOK // S, S, n), pl.BlockSpec((ns, L, n), lambda i, c: (first + i, c, 0))

    def const(shape):
        return pl.BlockSpec(shape, lambda i, c: (0,) * len(shape))

    def per_stream(shape):
        return pl.BlockSpec((ns,) + shape, lambda i, c: (i,) + (0,) * len(shape))

    toks = [tok(a) for a in (qkm, vm, om, gates)]
    state_specs = [per_stream((SUBLANES, 2 * M_WIDTH)), per_stream((M_HEADS, M_HEAD_DIM, M_HEAD_DIM)),
                   per_stream((M_HEADS, M_HEAD_DIM)), per_stream((1, LANES))]
    in_specs = ([t[1] for t in toks]
                + [const((CONV_WIDTH, 2 * M_WIDTH)), const((1, 2 * M_WIDTH)), const((1, LANES)), const((1, M_WIDTH))]
                + state_specs)
    args = [t[0] for t in toks] + [conv_w, conv_b, gbias, g_mh, carry0, c0, n0, m0]
    hm, c_out, n_out, m_out = pl.pallas_call(
        functools.partial(_mlstm_kernel, L=L, NS=ns),
        grid=(nb // ns, nc),
        in_specs=in_specs,
        out_specs=[pl.BlockSpec((ns, L, M_WIDTH), lambda i, c: (i, c, 0))] + state_specs[1:],
        out_shape=[jax.ShapeDtypeStruct((nb, S, M_WIDTH), BF16),
                   jax.ShapeDtypeStruct((nb, M_HEADS, M_HEAD_DIM, M_HEAD_DIM), F32),
                   jax.ShapeDtypeStruct((nb, M_HEADS, M_HEAD_DIM), F32),
                   jax.ShapeDtypeStruct((nb, 1, LANES), F32)],
        scratch_shapes=[pltpu.VMEM((ns, M_HEADS, M_HEAD_DIM, M_HEAD_DIM), F32),
                        pltpu.VMEM((ns, M_HEADS, M_HEAD_DIM), F32),
                        pltpu.VMEM((ns, 1, LANES), F32),
                        pltpu.VMEM((ns, SUBLANES, 2 * M_WIDTH), F32)],
        compiler_params=pltpu.CompilerParams(dimension_semantics=("arbitrary", "arbitrary"),
                                             vmem_limit_bytes=VMEM_LIMIT),
        name=name,
    )(*args)
    return hm.reshape(nb * S, M_WIDTH), c_out, n_out, m_out


def _outproj_kernel(xp_ref, xs_ref, attp_ref, atts_ref, hmp_ref, hms_ref, gate_ref, sh_ref, sc_ref, g_ref,
                    w_ref, wr_ref, br_ref, x2_ref, h2_ref, ri_ref, rt_ref, cnt_ref, xbuf, mixbuf):
    x = _load_x(xp_ref, xs_ref, xbuf)
    i = pl.program_id(0)

    @pl.when(i < P_TILES)
    def _():
        mixbuf[:, :A_WIDTH] = attp_ref[...].astype(BF16)
        mixbuf[:, A_WIDTH:] = hmp_ref[...].astype(BF16)

    @pl.when(i >= P_TILES)
    def _():
        mixbuf[:, :A_WIDTH] = atts_ref[...].astype(BF16)
        mixbuf[:, A_WIDTH:] = hms_ref[...].astype(BF16)

    mixed = jnp.dot(mixbuf[...], w_ref[...], preferred_element_type=F32)
    x2 = x + _per_chunk(gate_ref) * mixed
    x2_ref[...] = x2
    h2 = _rmsnorm(x2, g_ref[...]) * (1.0 + _per_chunk(sc_ref)) + _per_chunk(sh_ref)
    for s in range(ROW_SUB):
        h2_ref[pl.ds(s, TM, stride=ROW_SUB), :] = h2[:, s * LANES:(s + 1) * LANES]

    h_hi = h2.astype(BF16)
    h_lo = (h2 - h_hi.astype(F32)).astype(BF16)
    hi_w = jnp.dot(h_hi, wr_ref[...], preferred_element_type=F32)
    lg = (hi_w[:, :LANES] + hi_w[:, LANES:]
          + jnp.dot(h_lo, wr_ref[:, :LANES], preferred_element_type=F32)) + br_ref[...]
    lane = lax.broadcasted_iota(jnp.int32, (TM, LANES), 1)
    lanef = lane.astype(F32)
    ninf = -jnp.inf

    def first_argmax(vals):
        mx = jnp.max(vals, axis=-1, keepdims=True)
        return mx, jnp.min(jnp.where(vals == mx, lanef, float(LANES)), axis=-1, keepdims=True)

    is_grp = lane < N_GROUPS
    gmax, grp = first_argmax(jnp.where(is_grp, lg, ninf))
    p_grp = 1.0 / jnp.sum(jnp.where(is_grp, jnp.exp(lg - gmax), 0.0), axis=-1, keepdims=True)
    base = N_GROUPS + grp * EXPERTS_PER_GROUP
    in_grp = (lanef >= base) & (lanef < base + EXPERTS_PER_GROUP)
    el = jnp.where(in_grp, lg, ninf)
    v1, i1 = first_argmax(el)
    v2, i2 = first_argmax(jnp.where(lanef == i1, ninf, el))
    e = jnp.exp(v2 - v1)
    g1 = p_grp / (1.0 + e)
    g2 = p_grp * e / (1.0 + e)
    e1 = i1 - N_GROUPS
    e2 = i2 - N_GROUPS
    oh1 = lanef == e1
    oh2 = lanef == e2
    oh = jnp.where(oh1 | oh2, 1.0, 0.0)
    earlier = (lax.broadcasted_iota(jnp.int32, (TM, TM), 0) > lax.broadcasted_iota(jnp.int32, (TM, TM), 1))
    before = jnp.dot(earlier.astype(BF16), oh.astype(BF16), preferred_element_type=F32)
    r1 = jnp.sum(jnp.where(oh1, before, 0.0), axis=-1, keepdims=True)
    r2 = jnp.sum(jnp.where(oh2, before, 0.0), axis=-1, keepdims=True)
    cnt_ref[...] = jnp.broadcast_to(jnp.sum(oh, axis=0, keepdims=True), (SUBLANES, LANES))
    cols = (e1, e2, g1, g2, r1, r2)
    ri = jnp.zeros((TM, LANES), F32)
    for k, cval in enumerate(cols):
        ri = jnp.where(lane == k, cval, ri)
    ri_ref[...] = ri
    rt_ref[...] = ri.T[:SUBLANES]


def _outproj(x_prompt, x_sample, att_p, att_s, hm_p, hm_s, modc, g_ffn, w_out_b, w_router, b_router):
    def tok(n):
        return pl.BlockSpec((TM, n), lambda i: (i, 0))

    def tok_p(n):
        return pl.BlockSpec((TM, n), lambda i: (jnp.minimum(i, P_TILES - 1), 0))

    def tok_s(n):
        return pl.BlockSpec((TM, n), lambda i: (jnp.maximum(i - P_TILES, 0), 0))

    def const(shape):
        return pl.BlockSpec(shape, lambda i: (0,) * len(shape))

    return pl.pallas_call(
        _outproj_kernel,
        grid=(N_TILES,),
        in_specs=[_xp_spec(), _xs_spec(), tok_p(A_WIDTH), tok_s(A_WIDTH), tok_p(M_WIDTH), tok_s(M_WIDTH),
                  _mod_spec(2), _mod_spec(3), _mod_spec(4),
                  const((1, D_MODEL)), const((D_MODEL, D_MODEL)), const((D_MODEL, 2 * LANES)), const((1, LANES))],
        out_specs=[tok(D_MODEL), pl.BlockSpec((TM * ROW_SUB, LANES), lambda i: (i, 0)), tok(LANES),
                   pl.BlockSpec((SUBLANES, TM), lambda i: (0, i)),
                   pl.BlockSpec((SUBLANES, LANES), lambda i: (i, 0))],
        out_shape=[jax.ShapeDtypeStruct((N_TOK, D_MODEL), F32),
                   jax.ShapeDtypeStruct((N_TOK * ROW_SUB, LANES), F32),
                   jax.ShapeDtypeStruct((N_TOK, LANES), F32),
                   jax.ShapeDtypeStruct((SUBLANES, N_TOK), F32),
                   jax.ShapeDtypeStruct((N_TILES * SUBLANES, LANES), F32)],
        scratch_shapes=[pltpu.VMEM((TM, D_MODEL), F32), pltpu.VMEM((TM, D_MODEL), BF16)],
        compiler_params=pltpu.CompilerParams(vmem_limit_bytes=VMEM_LIMIT),
        name="outproj",
    )(x_prompt, x_sample, att_p, att_s, hm_p, hm_s, modc, modc, modc, g_ffn, w_out_b, w_router, b_router)


def _row(ref, r):
    return ref.at[pl.ds(pl.multiple_of(r * ROW_SUB, ROW_SUB), ROW_SUB), :]


def _row_copies(i, t, dest_ref, make):
    tok = i * TM + t
    return [make(k, dest_ref[k * N_TOK + tok]) for k in range(2)]


ZCHUNK = 64


def _dispatch_kernel(dest_ref, zlo_ref, zhi_ref, h_ref, xs_out, zbuf, sem, zsem):
    i = pl.program_id(0)
    chunk_rows = ZCHUNK * ROW_SUB

    @pl.when(i == 0)
    def _():
        zbuf[...] = jnp.zeros_like(zbuf)

        def chunk_copy(c):
            return pltpu.make_async_copy(zbuf, xs_out.at[pl.ds(pl.multiple_of(c * chunk_rows, chunk_rows), chunk_rows), :],
                                         zsem)

        def start_range(rng, carry):
            def start(c, cc):
                chunk_copy(c).start()
                return cc

            lax.fori_loop(zlo_ref[rng], zhi_ref[rng], start, 0)
            return carry

        def wait_range(rng, carry):
            def wait(c, cc):
                chunk_copy(c).wait()
                return cc

            lax.fori_loop(zlo_ref[rng], zhi_ref[rng], wait, 0)
            return carry

        lax.fori_loop(0, N_EXPERTS + 1, start_range, 0)
        lax.fori_loop(0, N_EXPERTS + 1, wait_range, 0)

    def copies(t):
        return _row_copies(i, t, dest_ref, lambda k, d: pltpu.make_async_copy(_row(h_ref, t), _row(xs_out, d), sem))

    def issue(t, carry):
        for k, cp in enumerate(copies(t)):
            cp.start(priority=k)
        return carry

    def wait(t, carry):
        for cp in copies(t):
            cp.wait()
        return carry

    lax.fori_loop(0, TM, issue, 0, unroll=8)
    lax.fori_loop(0, TM, wait, 0, unroll=8)


def _dispatch(dest, zlo, zhi, h2t):
    return pl.pallas_call(
        _dispatch_kernel,
        grid_spec=pltpu.PrefetchScalarGridSpec(
            num_scalar_prefetch=3,
            grid=(N_TILES,),
            in_specs=[pl.BlockSpec((TM * ROW_SUB, LANES), lambda i, *_: (i, 0))],
            out_specs=pl.BlockSpec(memory_space=pl.ANY),
            scratch_shapes=[pltpu.VMEM((ZCHUNK * ROW_SUB, LANES), F32),
                            pltpu.SemaphoreType.DMA, pltpu.SemaphoreType.DMA]),
        out_shape=jax.ShapeDtypeStruct((MOE_ROWS * ROW_SUB, LANES), F32),
        compiler_params=pltpu.CompilerParams(dimension_semantics=("arbitrary",)),
        name="moe_dispatch",
    )(dest, zlo, zhi, h2t)


def _expert_kernel(be_ref, bv_ref, first_ref, slot_ref, next_ref, xs_ref, wg_hbm, wu_hbm, wd_hbm, o_ref,
                   wg_f, wu_f, wd_f, wg_s, wu_s, wd_s, wsem):
    b = pl.program_id(0)

    def weight_copies(e, slot):
        return [pltpu.make_async_copy(src.at[e], dst.at[slot], wsem.at[slot])
                for src, dst in ((wg_hbm, wg_f), (wu_hbm, wu_f), (wd_hbm, wd_f))]

    @pl.when(b == 0)
    def _():
        for cp in weight_copies(be_ref[0], 0):
            cp.start()

    @pl.when(first_ref[b] == 1)
    def _():
        slot = slot_ref[b]
        for cp in weight_copies(be_ref[b], slot):
            cp.wait()
        wg_s[...] = wg_f[slot].astype(BF16)
        wu_s[...] = wu_f[slot].astype(BF16)
        wd_s[...] = wd_f[slot].astype(BF16)

        @pl.when(next_ref[b] >= 0)
        def _():
            for cp in weight_copies(next_ref[b], 1 - slot):
                cp.start()

    @pl.when(bv_ref[b] > 0)
    def _():
        x = jnp.concatenate([xs_ref[pl.ds(s, MOE_R, stride=ROW_SUB), :] for s in range(ROW_SUB)], axis=1)
        xb = x.astype(BF16)
        g = jnp.dot(xb, wg_s[...], preferred_element_type=F32)
        u = jnp.dot(xb, wu_s[...], preferred_element_type=F32)
        a = (g * _sigmoid(g) * u).astype(BF16)
        o = jnp.dot(a, wd_s[...], preferred_element_type=F32)
        for s in range(ROW_SUB):
            o_ref[pl.ds(s, MOE_R, stride=ROW_SUB), :] = o[:, s * LANES:(s + 1) * LANES]

    @pl.when(bv_ref[b] == 0)
    def _():
        o_ref[...] = jnp.zeros_like(o_ref)


def _experts(blk_e, blk_valid, xs, w_g, w_u, w_d):
    idx = np.arange(MOE_BLOCKS)
    used = blk_valid > 0
    first = used & ((idx == 0) | (blk_e != jnp.roll(blk_e, 1)))
    ordinal = jnp.sum(first[None, :] & (idx[None, :] <= idx[:, None]), axis=1) - 1
    later_first = jnp.where(first[None, :] & (idx[None, :] > idx[:, None]), idx[None, :], MOE_BLOCKS)
    nxt = jnp.min(later_first, axis=1)
    next_e = jnp.where(nxt < MOE_BLOCKS, jnp.sum(jnp.where(idx[None, :] == nxt[:, None], blk_e[None, :], 0), axis=1), -1)
    plan = [blk_e, blk_valid, first.astype(jnp.int32), (ordinal % 2).astype(jnp.int32), next_e.astype(jnp.int32)]

    def rows(b, be, bv, *_):
        return (jnp.where(bv[b] > 0, b, MOE_BLOCKS - 1), 0)

    hbm = pl.BlockSpec(memory_space=pl.ANY)
    return pl.pallas_call(
        _expert_kernel,
        grid_spec=pltpu.PrefetchScalarGridSpec(
            num_scalar_prefetch=len(plan),
            grid=(MOE_BLOCKS,),
            in_specs=[pl.BlockSpec((MOE_R * ROW_SUB, LANES), rows), hbm, hbm, hbm],
            out_specs=pl.BlockSpec((MOE_R * ROW_SUB, LANES), lambda b, *_: (b, 0)),
            scratch_shapes=[pltpu.VMEM((2, D_MODEL, D_EXPERT), F32),
                            pltpu.VMEM((2, D_MODEL, D_EXPERT), F32),
                            pltpu.VMEM((2, D_EXPERT, D_MODEL), F32),
                            pltpu.VMEM((D_MODEL, D_EXPERT), BF16),
                            pltpu.VMEM((D_MODEL, D_EXPERT), BF16),
                            pltpu.VMEM((D_EXPERT, D_MODEL), BF16),
                            pltpu.SemaphoreType.DMA((2,))]),
        out_shape=jax.ShapeDtypeStruct((MOE_ROWS * ROW_SUB, LANES), F32),
        compiler_params=pltpu.CompilerParams(dimension_semantics=("arbitrary",), vmem_limit_bytes=VMEM_LIMIT),
        name="moe_experts",
    )(*plan, xs, w_g, w_u, w_d)


def _combine_kernel(dest_ref, o_hbm, x2_ref, ri_ref, gate_ref, gf_ref, yp_ref, ys_ref, obuf, sem):
    i = pl.program_id(0)
    slot = i % 2
    tile_rows = TM * ROW_SUB

    def issue_tile(tile, slot_):
        def issue(t, carry):
            cps = _row_copies(tile, t, dest_ref, lambda k, d: pltpu.make_async_copy(
                _row(o_hbm, d), _row(obuf, (2 * slot_ + k) * TM + t), sem.at[slot_]))
            for k, cp in enumerate(cps):
                cp.start(priority=k)
            return carry

        lax.fori_loop(0, TM, issue, 0, unroll=8)

    @pl.when(i == 0)
    def _():
        issue_tile(0, 0)

    @pl.when(i + 1 < N_TILES)
    def _():
        issue_tile(i + 1, 1 - slot)

    for k in range(2):
        start = pl.multiple_of((2 * slot + k) * tile_rows, tile_rows)
        pltpu.make_async_copy(o_hbm.at[pl.ds(0, tile_rows), :], obuf.at[pl.ds(start, tile_rows), :],
                              sem.at[slot]).wait()

    ri = ri_ref[...]
    g1 = ri[:, 2:3]
    g2 = ri[:, 3:4]
    base = 2 * slot * tile_rows
    moe = jnp.concatenate(
        [g1 * obuf[pl.ds(base + s, TM, stride=ROW_SUB), :] + g2 * obuf[pl.ds(base + tile_rows + s, TM, stride=ROW_SUB), :]
         for s in range(ROW_SUB)], axis=1)
    x3 = x2_ref[...] + _per_chunk(gate_ref) * moe
    y = _rmsnorm(x3, gf_ref[...])

    @pl.when(i < P_TILES)
    def _():
        yp_ref[0] = y

    @pl.when(i >= P_TILES)
    def _():
        ys_ref[...] = y.reshape(CH_PER_TILE, DEC_SEQ, D_MODEL)


def _combine(dest, o_rows, x2, rinfo, modc, g_final):
    xp_spec, xs_spec = _xp_spec(), _xs_spec()
    return pl.pallas_call(
        _combine_kernel,
        grid_spec=pltpu.PrefetchScalarGridSpec(
            num_scalar_prefetch=1,
            grid=(N_TILES,),
            in_specs=[pl.BlockSpec(memory_space=pl.ANY),
                      pl.BlockSpec((TM, D_MODEL), lambda i, *_: (i, 0)),
                      pl.BlockSpec((TM, LANES), lambda i, *_: (i, 0)),
                      _mod_spec(5),
                      pl.BlockSpec((1, D_MODEL), lambda i, *_: (0, 0))],
            out_specs=[pl.BlockSpec(xp_spec.block_shape, lambda i, *_: xp_spec.index_map(i)),
                       pl.BlockSpec(xs_spec.block_shape, lambda i, *_: xs_spec.index_map(i))],
            scratch_shapes=[pltpu.VMEM((4 * TM * ROW_SUB, LANES), F32), pltpu.SemaphoreType.DMA((2,))]),
        out_shape=[jax.ShapeDtypeStruct((BATCH, SEQ, D_MODEL), F32),
                   jax.ShapeDtypeStruct((DEC_BATCH, DEC_SEQ, D_MODEL), F32)],
        compiler_params=pltpu.CompilerParams(dimension_semantics=("arbitrary",), vmem_limit_bytes=VMEM_LIMIT),
        name="moe_combine",
    )(dest, o_rows, x2, rinfo, modc, g_final)


def _moe_plan(rt, cnt):
    experts = np.arange(N_EXPERTS)
    tile_cnt = cnt.reshape(N_TILES, SUBLANES, LANES)[:, 0, :N_EXPERTS]
    counts = jnp.sum(tile_cnt, axis=0)
    padded = jnp.ceil(counts / MOE_R) * MOE_R
    pad_end = jnp.sum(padded[:, None] * (experts[:, None] <= experts[None, :]), axis=0)
    base = pad_end - padded
    tiles = np.arange(N_TILES)
    earlier_tiles = (tiles[None, :, None] < tiles[:, None, None])
    tile_base = base[None, :] + jnp.sum(tile_cnt[None, :, :] * earlier_tiles, axis=1)
    eid = rt[0:2].reshape(1, 2, N_TILES, TM)
    rank = rt[4:6].reshape(2, N_TILES, TM)
    pick = eid == experts.astype(np.float32).reshape(N_EXPERTS, 1, 1, 1)
    dest = jnp.sum(jnp.where(pick, tile_base.T[:, None, :, None], 0.0), axis=0) + rank
    blk_start = (np.arange(MOE_BLOCKS) * MOE_R).astype(np.float32)
    blk_e = jnp.minimum(jnp.sum((blk_start[:, None] >= pad_end[None, :]).astype(F32), axis=1), N_EXPERTS - 1)
    mine = blk_e[:, None] == experts.astype(np.float32)[None, :]
    blk_fill = jnp.sum(jnp.where(mine, (counts + base)[None, :], 0.0), axis=1) - blk_start
    blk_valid = jnp.clip(blk_fill, 0, MOE_R)
    zlo = jnp.concatenate([jnp.floor((base + counts) / ZCHUNK), pad_end[-1:] / ZCHUNK])
    zhi = jnp.concatenate([pad_end / ZCHUNK, jnp.full((1,), MOE_ROWS // ZCHUNK, F32)])
    return (dest.reshape(-1).astype(jnp.int32), blk_e.astype(jnp.int32), blk_valid.astype(jnp.int32),
            zlo.astype(jnp.int32), zhi.astype(jnp.int32))


def kernel(x_prompt, x_sample, c_prompt, c_sample, cache_win_k, cache_win_v, state_conv, state_C, state_n, state_m, w_ada, b_ada, g_norm_mix, g_norm_ffn, w_in, attn_sinks, conv_w, conv_b, b_igate, b_fgate, g_mhnorm, w_out, w_router_group, b_router_group, w_router_expert, b_router_expert, w_exp_gate, w_exp_up, w_exp_down, g_final):
    l = 0
    n_streams = BATCH + DEC_BATCH
    pad_streams = -n_streams % (2 * SUBLANES)
    mod = _modulation(jnp.concatenate([c_prompt, c_sample, jnp.zeros((pad_streams, D_MODEL), F32)], axis=0),
                      w_ada[l], b_ada[l])
    chunk_stream = np.concatenate([np.repeat(np.arange(BATCH), SEQ // CHUNK), BATCH + np.arange(DEC_BATCH)])
    modc = mod[chunk_stream]

    wi = w_in[l]
    s_q, s_k, s_v, s_qkm, s_vm, s_ig, s_fg = 0, 512, 640, 768, 1792, 2304, 2308
    s_om = 2312
    w_in_r = jnp.concatenate(
        [wi[:, s_q:s_qkm], wi[:, s_qkm:s_vm], wi[:, s_vm:s_ig], wi[:, s_om:], wi[:, s_ig:s_om],
         jnp.zeros((D_MODEL, LANES - 2 * M_HEADS), F32)], axis=1).astype(BF16)
    q, kv, qkm, vm, om, gates = _inproj(x_prompt, x_sample, modc, g_norm_mix[l].reshape(1, -1), w_in_r)

    cache_kv = jnp.concatenate([cache_win_k[l].reshape(DEC_BATCH * WINDOW, A_KV_WIDTH),
                                cache_win_v[l].reshape(DEC_BATCH * WINDOW, A_KV_WIDTH)], axis=1)
    att_p, att_s = _attention(attn_sinks[l], q, kv, cache_kv)

    gbias = jnp.concatenate([b_igate[l], b_fgate[l], jnp.zeros((LANES - 2 * M_HEADS,), F32)]).reshape(1, LANES)
    common = (qkm, vm, om, gates, conv_w[l], conv_b[l].reshape(1, -1), gbias, g_mhnorm[l].reshape(1, -1))
    zeros_p = (jnp.zeros((BATCH, SUBLANES, 2 * M_WIDTH), F32),
               jnp.zeros((BATCH, M_HEADS, M_HEAD_DIM, M_HEAD_DIM), F32),
               jnp.zeros((BATCH, M_HEADS, M_HEAD_DIM), F32),
               jnp.zeros((BATCH, 1, LANES), F32))
    LP = 256
    hm_p, C_p, n_p, m_p = _mlstm(*common, *zeros_p, nb=BATCH, L=LP, nc=SEQ // LP, ns=1, row_off=0,
                                 name="mlstm_prompt")
    carry_s = jnp.concatenate([jnp.zeros((DEC_BATCH, SUBLANES - (CONV_WIDTH - 1), 2 * M_WIDTH), F32),
                               state_conv[l]], axis=1)
    m0_s = jnp.pad(state_m[l], ((0, 0), (0, LANES - M_HEADS))).reshape(DEC_BATCH, 1, LANES)
    hm_s, C_s, n_s, m_s = _mlstm(*common, carry_s, state_C[l], state_n[l], m0_s,
                                 nb=DEC_BATCH, L=DEC_SEQ, nc=1, ns=4, row_off=N_P, name="mlstm_sample")

    w_router = jnp.concatenate([w_router_group[l], w_router_expert[l],
                                jnp.zeros((D_MODEL, LANES - N_GROUPS - N_EXPERTS), F32)], axis=1)
    w_router_hi = w_router.astype(BF16)
    w_router = jnp.concatenate([w_router_hi, (w_router - w_router_hi.astype(F32)).astype(BF16)], axis=1)
    b_router =jnp.concatenate([b_router_group[l], b_router_expert[l],
                                jnp.zeros((LANES - N_GROUPS - N_EXPERTS,), F32)]).reshape(1, LANES)
    x2, h2t, rinfo, rt, cnt = _outproj(x_prompt, x_sample, att_p, att_s, hm_p, hm_s, modc,
                                       g_norm_ffn[l].reshape(1, -1), w_out[l].astype(BF16), w_router, b_router)

    dest, blk_e, blk_valid, zlo, zhi = _moe_plan(rt, cnt)
    xs_rows = _dispatch(dest, zlo, zhi, h2t)
    o_rows = _experts(blk_e, blk_valid, xs_rows, w_exp_gate[l], w_exp_up[l], w_exp_down[l])
    y_prompt, y_sample = _combine(dest, o_rows, x2, rinfo, modc, g_final.reshape(1, -1))

    kv_p = kv.reshape(N_TOK // WINDOW, WINDOW, 2 * A_KV_WIDTH)[SEQ // WINDOW - 1:N_P // WINDOW:SEQ // WINDOW]
    win_k_p = kv_p[..., :A_KV_WIDTH].reshape(1, BATCH, WINDOW, A_KV_HEADS, A_HEAD_DIM)
    win_v_p = kv_p[..., A_KV_WIDTH:].reshape(1, BATCH, WINDOW, A_KV_HEADS, A_HEAD_DIM)
    kv_s = kv[N_P:].reshape(DEC_BATCH, DEC_SEQ, 2 * A_KV_WIDTH)
    win_k_s = jnp.concatenate([cache_win_k[l][:, DEC_SEQ:],
                               kv_s[..., :A_KV_WIDTH].reshape(DEC_BATCH, DEC_SEQ, A_KV_HEADS, A_HEAD_DIM)], axis=1)[None]
    win_v_s = jnp.concatenate([cache_win_v[l][:, DEC_SEQ:],
                               kv_s[..., A_KV_WIDTH:].reshape(DEC_BATCH, DEC_SEQ, A_KV_HEADS, A_HEAD_DIM)], axis=1)[None]
    qkm_c = qkm.reshape(N_CHUNKS, CHUNK, 2 * M_WIDTH)
    tail = slice(CHUNK - (CONV_WIDTH - 1), CHUNK)
    conv_p = qkm_c[SEQ // CHUNK - 1:N_P // CHUNK:SEQ // CHUNK, tail][None]
    conv_s = qkm_c[N_P // CHUNK:, tail][None]
    return (y_prompt, y_sample,
            win_k_p, win_v_p, conv_p, C_p[None], n_p[None], m_p[:, 0, :M_HEADS][None],
            win_k_s, win_v_s, conv_s, C_s[None], n_s[None], m_s[:, 0, :M_HEADS][None])
```

```python
import functools

import numpy as np
import jax
import jax.numpy as jnp
from jax import lax
from jax.experimental import pallas as pl
from jax.experimental.pallas import tpu as pltpu

F32 = jnp.float32
BF16 = jnp.bfloat16
HIGHEST = lax.Precision.HIGHEST

LANES = 128
SUBLANES = 8

D_MODEL = 1024
BATCH = 8
SEQ = 2048
DEC_BATCH = 32
DEC_SEQ = 64
PAST_LEN = 4096
CHUNK = 64
A_WIDTH = 512
A_HEAD_DIM = 64
A_HEADS = 8
A_KV_HEADS = 2
A_GROUP = 4
A_KV_WIDTH = 128
WINDOW = 128
ROT_DIM = 16
ROPE_THETA = 500000.0
M_WIDTH = 512
M_HEADS = 4
M_HEAD_DIM = 128
CONV_WIDTH = 4
N_GROUPS = 4
EXPERTS_PER_GROUP = 8
N_EXPERTS = 32
D_EXPERT = 512
EPS = 1e-6

N_P = BATCH * SEQ
N_S = DEC_BATCH * DEC_SEQ
N_TOK = N_P + N_S
N_CHUNKS = N_TOK // CHUNK
TM = 512
N_TILES = N_TOK // TM
P_TILES = N_P // TM
CH_PER_TILE = TM // CHUNK
ROW_SUB = D_MODEL // LANES
N_ASSIGN = 2 * N_TOK
MOE_R = 512
MOE_BLOCKS = N_ASSIGN // MOE_R + N_EXPERTS
MOE_ROWS = MOE_BLOCKS * MOE_R
C_Q, C_K, C_V, C_QKM, C_VM, C_OM, C_G = 0, 512, 640, 768, 1792, 2304, 2816
D_IN_PAD = 2944
VMEM_LIMIT = 48 * 1024 * 1024


def _sigmoid(x):
    return 1.0 / (1.0 + jnp.exp(-x))


def _mod_kernel(c_ref, w_ref, b_ref, o_ref):
    c = c_ref[...]
    s = c * _sigmoid(c)
    w = w_ref[...]
    s_hi, w_hi = s.astype(BF16), w.astype(BF16)
    s_lo, w_lo = (s - s_hi.astype(F32)).astype(BF16), (w - w_hi.astype(F32)).astype(BF16)
    o_ref[...] = (jnp.dot(s_hi, w_hi, preferred_element_type=F32) + jnp.dot(s_lo, w_hi, preferred_element_type=F32)
                  + jnp.dot(s_hi, w_lo, preferred_element_type=F32)) + b_ref[...]


def _modulation(c_all, w_ada, b_ada):
    n = c_all.shape[0]
    bn = 512
    return pl.pallas_call(
        _mod_kernel,
        grid=(6 * D_MODEL // bn,),
        in_specs=[pl.BlockSpec((n, D_MODEL), lambda j: (0, 0)),
                  pl.BlockSpec((D_MODEL, bn), lambda j: (0, j)),
                  pl.BlockSpec((1, bn), lambda j: (0, j))],
        out_specs=pl.BlockSpec((n, bn), lambda j: (0, j)),
        out_shape=jax.ShapeDtypeStruct((n, 6 * D_MODEL), F32),
        name="modulation",
    )(c_all, w_ada, b_ada.reshape(1, -1))


def _xp_spec():
    def idx(i):
        t = jnp.minimum(i, P_TILES - 1)
        return (t // (SEQ // TM), t % (SEQ // TM), 0)
    return pl.BlockSpec((1, TM, D_MODEL), idx)


def _xs_spec():
    return pl.BlockSpec((CH_PER_TILE, DEC_SEQ, D_MODEL), lambda i: (jnp.maximum(i - P_TILES, 0), 0, 0))


def _mod_spec(comp):
    return pl.BlockSpec((CH_PER_TILE, D_MODEL), lambda i, *_: (i, comp))


def _load_x(xp_ref, xs_ref):
    return jnp.where(pl.program_id(0) < P_TILES, xp_ref[0], xs_ref[...].reshape(TM, D_MODEL))


def _per_chunk(m_ref):
    m = m_ref[...]
    return jnp.concatenate([jnp.broadcast_to(m[c:c + 1], (CHUNK, D_MODEL)) for c in range(CH_PER_TILE)], axis=0)


def _rmsnorm(x, g):
    return x * lax.rsqrt(jnp.mean(x * x, axis=-1, keepdims=True) + EPS) * g


def _rope(x, cos, sa, sb):
    n = x.shape[1]
    rep = n // LANES
    if rep > 1:
        cos = jnp.concatenate([cos] * rep, axis=1)
        sa = jnp.concatenate([sa] * rep, axis=1)
        sb = jnp.concatenate([sb] * rep, axis=1)
    return x * cos + pltpu.roll(x, n - ROT_DIM // 2, 1) * sa + pltpu.roll(x, ROT_DIM // 2, 1) * sb


def _inproj_kernel(xp_ref, xs_ref, sh_ref, sc_ref, g_ref, w_ref, cos_ref, sa_ref, sb_ref,
                   q_ref, kv_ref, qkm_ref, vm_ref, om_ref, gt_ref):
    x = _load_x(xp_ref, xs_ref)
    h = _rmsnorm(x, g_ref[...]) * (1.0 + _per_chunk(sc_ref)) + _per_chunk(sh_ref)
    hb = h.astype(BF16)

    def proj(a, b):
        return jnp.dot(hb, w_ref[:, a:b], preferred_element_type=F32)

    cos, sa, sb = cos_ref[...], sa_ref[...], sb_ref[...]
    q_ref[...] = _rope(proj(C_Q, C_K), cos, sa, sb).astype(BF16)
    kv_ref[:, :A_KV_WIDTH] = _rope(proj(C_K, C_V), cos, sa, sb)
    kv_ref[:, A_KV_WIDTH:] = proj(C_V, C_QKM)
    qkm_ref[...] = proj(C_QKM, C_VM)
    vm_ref[...] = proj(C_VM, C_OM).astype(BF16)
    om_ref[...] = proj(C_OM, C_G)
    gt_ref[...] = proj(C_G, D_IN_PAD)


def _rope_tables():
    pos = np.concatenate([np.arange(SEQ), np.tile(PAST_LEN + np.arange(DEC_SEQ), CH_PER_TILE)]).astype(np.float64)
    inv_freq = ROPE_THETA ** (-np.arange(0, ROT_DIM, 2, dtype=np.float64) / ROT_DIM)
    lane = np.arange(LANES)
    hl = lane % A_HEAD_DIM
    ang = pos[:, None] * inv_freq[hl % (ROT_DIM // 2)][None, :]
    rot = (hl < ROT_DIM)[None, :]
    lo = (hl < ROT_DIM // 2)[None, :]
    cos = np.where(rot, np.cos(ang), 1.0)
    sa = np.where(lo, -np.sin(ang), 0.0)
    sb = np.where(rot & ~lo, np.sin(ang), 0.0)
    return [jnp.asarray(t, F32) for t in (cos, sa, sb)]


def _inproj(x_prompt, x_sample, modc, g_mix, w_in_r):
    cos, sa, sb = _rope_tables()
    tab_spec = pl.BlockSpec((TM, LANES), lambda i: (jnp.where(i < P_TILES, i % (SEQ // TM), SEQ // TM), 0))

    def out(n, dtype=F32):
        return pl.BlockSpec((TM, n), lambda i: (i, 0)), jax.ShapeDtypeStruct((N_TOK, n), dtype)

    outs = [out(A_WIDTH, BF16), out(2 * A_KV_WIDTH), out(2 * M_WIDTH), out(M_WIDTH, BF16), out(M_WIDTH), out(LANES)]
    return pl.pallas_call(
        _inproj_kernel,
        grid=(N_TILES,),
        in_specs=[_xp_spec(), _xs_spec(), _mod_spec(0), _mod_spec(1),
                  pl.BlockSpec((1, D_MODEL), lambda i: (0, 0)),
                  pl.BlockSpec((D_MODEL, D_IN_PAD), lambda i: (0, 0)),
                  tab_spec, tab_spec, tab_spec],
        out_specs=[o[0] for o in outs],
        out_shape=[o[1] for o in outs],
        compiler_params=pltpu.CompilerParams(vmem_limit_bytes=VMEM_LIMIT),
        name="inproj",
    )(x_prompt, x_sample, modc, modc, g_mix, w_in_r, cos, sa, sb)


ATT_NS = 1


def _attn_chunk_kernel(sink_ref, q_ref, prev_ref, cur_ref, o_ref):
    for st in range(ATT_NS):
        rows = slice(st * CHUNK, (st + 1) * CHUNK)
        _attn_chunk(sink_ref, q_ref[rows, :], prev_ref[st * WINDOW:(st + 1) * WINDOW, :], cur_ref[rows, :],
                    o_ref, rows)


def _attn_chunk(sink_ref, q, prev, cur, o_ref, rows):
    q = q * (A_HEAD_DIM ** -0.5)
    kv = jnp.concatenate([prev, cur], axis=0)
    for g in range(A_KV_HEADS):
        kg = kv[:, g * A_HEAD_DIM:(g + 1) * A_HEAD_DIM].astype(BF16)
        vg = kv[:, A_KV_WIDTH + g * A_HEAD_DIM:A_KV_WIDTH + (g + 1) * A_HEAD_DIM].astype(BF16)
        heads = [g * A_GROUP + i for i in range(A_GROUP)]
        qc = jnp.concatenate([q[:, h * A_HEAD_DIM:(h + 1) * A_HEAD_DIM] for h in heads], axis=0).astype(BF16)
        snk = jnp.concatenate([jnp.full((CHUNK, 1), sink_ref[h], F32) for h in heads], axis=0)
        s = lax.dot_general(qc, kg, (((1,), (1,)), ((), ())), preferred_element_type=F32)
        mx = jnp.maximum(jnp.max(s, axis=-1, keepdims=True), snk)
        p = jnp.exp(s - mx)
        den = jnp.sum(p, axis=-1, keepdims=True) + jnp.exp(snk - mx)
        o = jnp.dot(p.astype(BF16), vg, preferred_element_type=F32) / den
        for i, h in enumerate(heads):
            o_ref[rows, h * A_HEAD_DIM:(h + 1) * A_HEAD_DIM] = o[i * CHUNK:(i + 1) * CHUNK]


ATT_TQ = 512
ATT_QB = 2 * CHUNK


def _attn_band_kernel(sink_ref, q_ref, prev_ref, cur_ref, o_ref, att_t):
    nk = ATT_QB + WINDOW
    q = (q_ref[...] * (A_HEAD_DIM ** -0.5)).astype(BF16)
    kv = jnp.concatenate([prev_ref[...], cur_ref[...]], axis=0)
    k2 = kv[:, :A_KV_WIDTH]
    k2r = pltpu.roll(k2, A_HEAD_DIM, 1)
    low = lax.broadcasted_iota(jnp.int32, k2.shape, 1) < A_HEAD_DIM
    k_placed = {(0, 0): jnp.where(low, k2, 0.0), (0, 1): jnp.where(low, 0.0, k2r),
                (1, 0): jnp.where(low, k2r, 0.0), (1, 1): jnp.where(low, 0.0, k2)}
    k_placed = {key: val.astype(BF16) for key, val in k_placed.items()}
    v_t = kv[:, A_KV_WIDTH:].T.astype(BF16)
    key_chunk = lax.broadcasted_iota(jnp.int32, (nk, ATT_QB), 0) // CHUNK
    q_chunk = lax.broadcasted_iota(jnp.int32, (nk, ATT_QB), 1) // CHUNK
    band = (key_chunk >= q_chunk) & (key_chunk <= q_chunk + WINDOW // CHUNK)
    has_history = pl.program_id(1) > 0
    for blk in range(ATT_TQ // ATT_QB):
        keys = slice(blk * ATT_QB, blk * ATT_QB + nk)
        qrows = slice(blk * ATT_QB, (blk + 1) * ATT_QB)
        valid = band & ((key_chunk >= WINDOW // CHUNK) | has_history) if blk == 0 else band
        for h in range(A_HEADS):
            g = h // A_GROUP
            s_t = lax.dot_general(k_placed[(g, h % 2)][keys], q[qrows, (h // 2) * LANES:(h // 2 + 1) * LANES],
                                  (((1,), (1,)), ((), ())), preferred_element_type=F32)
            s_t = jnp.where(valid, s_t, -jnp.inf)
            snk = sink_ref[h]
            mx = jnp.maximum(jnp.max(s_t, axis=0, keepdims=True), snk)
            p_t = jnp.exp(s_t - mx)
            den = jnp.sum(p_t, axis=0, keepdims=True) + jnp.exp(snk - mx)
            o_t = jnp.dot(v_t[:, keys], p_t.astype(BF16), preferred_element_type=F32)
            att_t[h * A_HEAD_DIM:(h + 1) * A_HEAD_DIM, qrows] = o_t[g * A_HEAD_DIM:(g + 1) * A_HEAD_DIM] / den
    o_ref[...] = att_t[...].T.astype(BF16)


def _attention(sinks, q, kv, cache_kv):
    smem = pl.BlockSpec(memory_space=pltpu.SMEM)
    tq = ATT_TQ
    nq = SEQ // tq
    att = pl.pallas_call(
        _attn_band_kernel,
        grid=(BATCH, nq),
        scratch_shapes=[pltpu.VMEM((A_WIDTH, ATT_TQ), F32)],
        in_specs=[smem,
                  pl.BlockSpec((tq, A_WIDTH), lambda b, j: (b * nq + j, 0)),
                  pl.BlockSpec((WINDOW, 2 * A_KV_WIDTH),
                               lambda b, j: (jnp.maximum((b * nq + j) * (tq // WINDOW) - 1, 0), 0)),
                  pl.BlockSpec((tq, 2 * A_KV_WIDTH), lambda b, j: (b * nq + j, 0))],
        out_specs=pl.BlockSpec((tq, A_WIDTH), lambda b, j: (b * nq + j, 0)),
        out_shape=jax.ShapeDtypeStruct((N_P, A_WIDTH), BF16),
        name="attn_prompt",
    )(sinks, q, kv, kv)
    rows = ATT_NS * DEC_SEQ
    off = N_P // rows
    att_s = pl.pallas_call(
        _attn_chunk_kernel,
        grid=(DEC_BATCH // ATT_NS,),
        in_specs=[smem,
                  pl.BlockSpec((rows, A_WIDTH), lambda b: (off + b, 0)),
                  pl.BlockSpec((ATT_NS * WINDOW, 2 * A_KV_WIDTH), lambda b: (b, 0)),
                  pl.BlockSpec((rows, 2 * A_KV_WIDTH), lambda b: (off + b, 0))],
        out_specs=pl.BlockSpec((rows, A_WIDTH), lambda b: (b, 0)),
        out_shape=jax.ShapeDtypeStruct((N_S, A_WIDTH), F32),
        name="attn_sample",
    )(sinks, q, cache_kv, kv)
    return att, att_s


def _conv4(x, w, b):
    x1 = pltpu.roll(x, 1, 0)
    near = b + x * w[3:4] + x1 * w[2:3]
    far = x * w[1:2] + x1 * w[0:1]
    return near + pltpu.roll(far, 2, 0)


def _mlstm_kernel(qkm_ref, vm_ref, om_ref, gt_ref, cw_ref, cb_ref, gb_ref, gmh_ref,
                  carry0_ref, c0_ref, n0_ref, m0_ref,
                  hm_ref, cout_ref, nout_ref, mout_ref, c_s, n_s, m_s, carry_s, *, L, NS):
    c = pl.program_id(1)

    @pl.when(c == 0)
    def _():
        c_s[...] = c0_ref[...]
        n_s[...] = n0_ref[...]
        m_s[...] = m0_ref[...]
        carry_s[...] = carry0_ref[...]

    w = cw_ref[...]
    b = cb_ref[...]
    gmh = gmh_ref[...]
    lane = lax.broadcasted_iota(jnp.int32, (L, LANES), 1)
    causal = lax.broadcasted_iota(jnp.int32, (L, L), 0) >= lax.broadcasted_iota(jnp.int32, (L, L), 1)
    ones_causal = causal.astype(F32)
    sel = (lax.broadcasted_iota(jnp.int32, (SUBLANES, LANES), 0)
           == lax.broadcasted_iota(jnp.int32, (SUBLANES, LANES), 1)).astype(F32)
    lane1 = lax.broadcasted_iota(jnp.int32, (1, LANES), 1)
    for st in range(NS):
        _mlstm_stream(st, qkm_ref, vm_ref, om_ref, gt_ref, gb_ref, hm_ref, c_s, n_s, m_s, carry_s,
                      w, b, gmh, lane, causal, ones_causal, sel, lane1, L)

    @pl.when(c == pl.num_programs(1) - 1)
    def _():
        cout_ref[...] = c_s[...]
        nout_ref[...] = n_s[...]
        mout_ref[...] = m_s[...]


def _mlstm_stream(st, qkm_ref, vm_ref, om_ref, gt_ref, gb_ref, hm_ref, c_s, n_s, m_s, carry_s,
                  w, b, gmh, lane, causal, ones_causal, sel, lane1, L):
    x = qkm_ref[st]
    y = _conv4(x, w, b)
    y8 = _conv4(jnp.concatenate([carry_s[st], x[:SUBLANES]], axis=0), w, b)
    y = jnp.concatenate([y8[SUBLANES:], y[SUBLANES:]], axis=0)
    carry_s[st] = x[L - SUBLANES:]
    a = y * _sigmoid(y)
    qa = a[:, :M_WIDTH] * (M_HEAD_DIM ** -0.5)
    ka = a[:, M_WIDTH:]
    v = vm_ref[st]
    om = om_ref[st]

    z = gt_ref[st] + gb_ref[...]
    f_log = jnp.minimum(z, 0.0) - jnp.log1p(jnp.exp(-jnp.abs(z)))
    val = jnp.where(lane < M_HEADS, z, f_log)
    cum = jnp.dot(ones_causal, val, preferred_element_type=F32, precision=HIGHEST)
    G = jnp.where(lane < M_HEADS, val, cum)
    GT = lax.dot_general(sel, G, (((1,), (1,)), ((), ())), preferred_element_type=F32, precision=HIGHEST)

    m_row = m_s[st]
    outs = []
    for h in range(M_HEADS):
        a_col = G[:, M_HEADS + h:M_HEADS + h + 1]
        i_col = G[:, h:h + 1]
        i_row = GT[h:h + 1, :]
        b_row = GT[M_HEADS + h:M_HEADS + h + 1, :]
        m_prev = m_row[:, h:h + 1]
        dm = jnp.where(causal, a_col - b_row + i_row, -jnp.inf)
        inter = a_col + m_prev
        m_t = jnp.maximum(inter, jnp.max(dm, axis=-1, keepdims=True))
        w_intra = jnp.exp(dm - m_t)
        w_inter = jnp.exp(inter - m_t)
        sl = slice(h * M_HEAD_DIM, (h + 1) * M_HEAD_DIM)
        qh, kh, vh = qa[:, sl], ka[:, sl], v[:, sl]
        qb = qh.astype(BF16)
        scores = lax.dot_general(qb, kh.astype(BF16), (((1,), (1,)), ((), ())), preferred_element_type=F32) * w_intra
        ch = c_s[st, h]
        nh = n_s[st, h:h + 1, :]
        num = (jnp.dot(scores.astype(BF16), vh.astype(BF16), preferred_element_type=F32)
               + w_inter * lax.dot_general(qb, ch.astype(BF16), (((1,), (1,)), ((), ())), preferred_element_type=F32))
        den = jnp.sum(scores, axis=-1, keepdims=True) + w_inter * jnp.sum(qh * nh, axis=-1, keepdims=True)
        hv = num / jnp.maximum(jnp.abs(den), jnp.exp(-m_t))
        hn = hv * lax.rsqrt(jnp.mean(hv * hv, axis=-1, keepdims=True) + EPS)
        outs.append(hn * gmh[:, sl] * _sigmoid(om[:, sl]))
        m_new = m_t[L - 1:L, :]
        a_last = a_col[L - 1:L, :]
        w_end = jnp.exp(a_last - a_col + i_col - m_new)
        decay = jnp.exp(a_last + m_prev - m_new)
        kw = kh * w_end
        c_s[st, h] = decay * ch + lax.dot_general(vh.astype(BF16), kw.astype(BF16), (((0,), (0,)), ((), ())),
                                                  preferred_element_type=F32)
        n_s[st, h:h + 1, :] = decay * nh + jnp.sum(kw, axis=0, keepdims=True)
        m_row = jnp.where(lane1 == h, m_new, m_row)
    m_s[st] = m_row
    hm_ref[st] = jnp.concatenate(outs, axis=1).astype(BF16)


def _mlstm(qkm, vm, om, gates, conv_w, conv_b, gbias, g_mh, carry0, c0, n0, m0, *, nb, L, nc, ns, row_off, name):
    S = nc * L
    first = row_off // S // ns

    def tok(arr):
        n = arr.shape[-1]
        return arr.reshape(N_TOK // S, S, n), pl.BlockSpec((ns, L, n), lambda i, c: (first + i, c, 0))

    def const(shape):
        return pl.BlockSpec(shape, lambda i, c: (0,) * len(shape))

    def per_stream(shape):
        return pl.BlockSpec((ns,) + shape, lambda i, c: (i,) + (0,) * len(shape))

    toks = [tok(a) for a in (qkm, vm, om, gates)]
    state_specs = [per_stream((SUBLANES, 2 * M_WIDTH)), per_stream((M_HEADS, M_HEAD_DIM, M_HEAD_DIM)),
                   per_stream((M_HEADS, M_HEAD_DIM)), per_stream((1, LANES))]
    in_specs = ([t[1] for t in toks]
                + [const((CONV_WIDTH, 2 * M_WIDTH)), const((1, 2 * M_WIDTH)), const((1, LANES)), const((1, M_WIDTH))]
                + state_specs)
    args = [t[0] for t in toks] + [conv_w, conv_b, gbias, g_mh, carry0, c0, n0, m0]
    hm, c_out, n_out, m_out = pl.pallas_call(
        functools.partial(_mlstm_kernel, L=L, NS=ns),
        grid=(nb // ns, nc),
        in_specs=in_specs,
        out_specs=[pl.BlockSpec((ns, L, M_WIDTH), lambda i, c: (i, c, 0))] + state_specs[1:],
        out_shape=[jax.ShapeDtypeStruct((nb, S, M_WIDTH), BF16),
                   jax.ShapeDtypeStruct((nb, M_HEADS, M_HEAD_DIM, M_HEAD_DIM), F32),
                   jax.ShapeDtypeStruct((nb, M_HEADS, M_HEAD_DIM), F32),
                   jax.ShapeDtypeStruct((nb, 1, LANES), F32)],
        scratch_shapes=[pltpu.VMEM((ns, M_HEADS, M_HEAD_DIM, M_HEAD_DIM), F32),
                        pltpu.VMEM((ns, M_HEADS, M_HEAD_DIM), F32),
                        pltpu.VMEM((ns, 1, LANES), F32),
                        pltpu.VMEM((ns, SUBLANES, 2 * M_WIDTH), F32)],
        compiler_params=pltpu.CompilerParams(dimension_semantics=("arbitrary", "arbitrary"),
                                             vmem_limit_bytes=VMEM_LIMIT),
        name=name,
    )(*args)
    return hm.reshape(nb * S, M_WIDTH), c_out, n_out, m_out


def _outproj_kernel(xp_ref, xs_ref, attp_ref, atts_ref, hmp_ref, hms_ref, gate_ref, sh_ref, sc_ref, g_ref,
                    w_ref, wr_ref, br_ref, x2_ref, h2_ref, ri_ref, rt_ref, cnt_ref):
    x = _load_x(xp_ref, xs_ref)
    is_prompt = pl.program_id(0) < P_TILES
    att = jnp.where(is_prompt, attp_ref[...], atts_ref[...].astype(BF16))
    hm = jnp.where(is_prompt, hmp_ref[...], hms_ref[...])
    mixed = (jnp.dot(att, w_ref[:A_WIDTH, :], preferred_element_type=F32)
             + jnp.dot(hm, w_ref[A_WIDTH:, :], preferred_element_type=F32))
    x2 = x + _per_chunk(gate_ref) * mixed
    x2_ref[...] = x2
    h2 = _rmsnorm(x2, g_ref[...]) * (1.0 + _per_chunk(sc_ref)) + _per_chunk(sh_ref)
    for s in range(ROW_SUB):
        h2_ref[pl.ds(s, TM, stride=ROW_SUB), :] = h2[:, s * LANES:(s + 1) * LANES]

    h_hi = h2.astype(BF16)
    h_lo = (h2 - h_hi.astype(F32)).astype(BF16)
    hi_w = jnp.dot(h_hi, wr_ref[...], preferred_element_type=F32)
    lg = (hi_w[:, :LANES] + hi_w[:, LANES:]
          + jnp.dot(h_lo, wr_ref[:, :LANES], preferred_element_type=F32)) + br_ref[...]
    lane = lax.broadcasted_iota(jnp.int32, (TM, LANES), 1)
    lanef = lane.astype(F32)
    ninf = -jnp.inf

    def first_argmax(vals):
        mx = jnp.max(vals, axis=-1, keepdims=True)
        return mx, jnp.min(jnp.where(vals == mx, lanef, float(LANES)), axis=-1, keepdims=True)

    is_grp = lane < N_GROUPS
    gmax, grp = first_argmax(jnp.where(is_grp, lg, ninf))
    p_grp = 1.0 / jnp.sum(jnp.where(is_grp, jnp.exp(lg - gmax), 0.0), axis=-1, keepdims=True)
    base = N_GROUPS + grp * EXPERTS_PER_GROUP
    in_grp = (lanef >= base) & (lanef < base + EXPERTS_PER_GROUP)
    el = jnp.where(in_grp, lg, ninf)
    v1, i1 = first_argmax(el)
    v2, i2 = first_argmax(jnp.where(lanef == i1, ninf, el))
    e = jnp.exp(v2 - v1)
    g1 = p_grp / (1.0 + e)
    g2 = p_grp * e / (1.0 + e)
    e1 = i1 - N_GROUPS
    e2 = i2 - N_GROUPS
    oh1 = lanef == e1
    oh2 = lanef == e2
    oh = jnp.where(oh1 | oh2, 1.0, 0.0)
    earlier = (lax.broadcasted_iota(jnp.int32, (TM, TM), 0) > lax.broadcasted_iota(jnp.int32, (TM, TM), 1))
    before = jnp.dot(earlier.astype(BF16), oh.astype(BF16), preferred_element_type=F32)
    r1 = jnp.sum(jnp.where(oh1, before, 0.0), axis=-1, keepdims=True)
    r2 = jnp.sum(jnp.where(oh2, before, 0.0), axis=-1, keepdims=True)
    cnt_ref[...] = jnp.broadcast_to(jnp.sum(oh, axis=0, keepdims=True), (SUBLANES, LANES))
    cols = (e1, e2, g1, g2, r1, r2)
    ri = jnp.zeros((TM, LANES), F32)
    for k, cval in enumerate(cols):
        ri = jnp.where(lane == k, cval, ri)
    ri_ref[...] = ri
    rt_ref[...] = ri.T[:SUBLANES]


def _outproj(x_prompt, x_sample, att_p, att_s, hm_p, hm_s, modc, g_ffn, w_out_b, w_router, b_router):
    def tok(n):
        return pl.BlockSpec((TM, n), lambda i: (i, 0))

    def tok_p(n):
        return pl.BlockSpec((TM, n), lambda i: (jnp.minimum(i, P_TILES - 1), 0))

    def tok_s(n):
        return pl.BlockSpec((TM, n), lambda i: (jnp.maximum(i - P_TILES, 0), 0))

    def const(shape):
        return pl.BlockSpec(shape, lambda i: (0,) * len(shape))

    return pl.pallas_call(
        _outproj_kernel,
        grid=(N_TILES,),
        in_specs=[_xp_spec(), _xs_spec(), tok_p(A_WIDTH), tok_s(A_WIDTH), tok_p(M_WIDTH), tok_s(M_WIDTH),
                  _mod_spec(2), _mod_spec(3), _mod_spec(4),
                  const((1, D_MODEL)), const((D_MODEL, D_MODEL)), const((D_MODEL, 2 * LANES)), const((1, LANES))],
        out_specs=[tok(D_MODEL), pl.BlockSpec((TM * ROW_SUB, LANES), lambda i: (i, 0)), tok(LANES),
                   pl.BlockSpec((SUBLANES, TM), lambda i: (0, i)),
                   pl.BlockSpec((SUBLANES, LANES), lambda i: (i, 0))],
        out_shape=[jax.ShapeDtypeStruct((N_TOK, D_MODEL), F32),
                   jax.ShapeDtypeStruct((N_TOK * ROW_SUB, LANES), F32),
                   jax.ShapeDtypeStruct((N_TOK, LANES), F32),
                   jax.ShapeDtypeStruct((SUBLANES, N_TOK), F32),
                   jax.ShapeDtypeStruct((N_TILES * SUBLANES, LANES), F32)],
        compiler_params=pltpu.CompilerParams(vmem_limit_bytes=VMEM_LIMIT),
        name="outproj",
    )(x_prompt, x_sample, att_p, att_s, hm_p, hm_s, modc, modc, modc, g_ffn, w_out_b, w_router, b_router)


def _row(ref, r):
    return ref.at[pl.ds(pl.multiple_of(r * ROW_SUB, ROW_SUB), ROW_SUB), :]


def _row_copies(i, t, dest_ref, make):
    tok = i * TM + t
    return [make(k, dest_ref[k * N_TOK + tok]) for k in range(2)]


ZCHUNK = 64


def _dispatch_kernel(dest_ref, zlo_ref, zhi_ref, h_ref, xs_out, zbuf, sem, zsem):
    i = pl.program_id(0)
    chunk_rows = ZCHUNK * ROW_SUB

    @pl.when(i == 0)
    def _():
        zbuf[...] = jnp.zeros_like(zbuf)

        def chunk_copy(c):
            return pltpu.make_async_copy(zbuf, xs_out.at[pl.ds(pl.multiple_of(c * chunk_rows, chunk_rows), chunk_rows), :],
                                         zsem)

        def start_range(rng, carry):
            def start(c, cc):
                chunk_copy(c).start()
                return cc

            lax.fori_loop(zlo_ref[rng], zhi_ref[rng], start, 0)
            return carry

        def wait_range(rng, carry):
            def wait(c, cc):
                chunk_copy(c).wait()
                return cc

            lax.fori_loop(zlo_ref[rng], zhi_ref[rng], wait, 0)
            return carry

        lax.fori_loop(0, N_EXPERTS + 1, start_range, 0)
        lax.fori_loop(0, N_EXPERTS + 1, wait_range, 0)

    def copies(t):
        return _row_copies(i, t, dest_ref, lambda k, d: pltpu.make_async_copy(_row(h_ref, t), _row(xs_out, d), sem))

    def issue(t, carry):
        for k, cp in enumerate(copies(t)):
            cp.start(priority=k)
        return carry

    def wait(t, carry):
        for cp in copies(t):
            cp.wait()
        return carry

    lax.fori_loop(0, TM, issue, 0, unroll=8)
    lax.fori_loop(0, TM, wait, 0, unroll=8)


def _dispatch(dest, zlo, zhi, h2t):
    return pl.pallas_call(
        _dispatch_kernel,
        grid_spec=pltpu.PrefetchScalarGridSpec(
            num_scalar_prefetch=3,
            grid=(N_TILES,),
            in_specs=[pl.BlockSpec((TM * ROW_SUB, LANES), lambda i, *_: (i, 0))],
            out_specs=pl.BlockSpec(memory_space=pl.ANY),
            scratch_shapes=[pltpu.VMEM((ZCHUNK * ROW_SUB, LANES), F32),
                            pltpu.SemaphoreType.DMA, pltpu.SemaphoreType.DMA]),
        out_shape=jax.ShapeDtypeStruct((MOE_ROWS * ROW_SUB, LANES), F32),
        compiler_params=pltpu.CompilerParams(dimension_semantics=("arbitrary",)),
        name="moe_dispatch",
    )(dest, zlo, zhi, h2t)


def _expert_kernel(be_ref, bv_ref, first_ref, slot_ref, next_ref, xs_ref, wg_hbm, wu_hbm, wd_hbm, o_ref,
                   wg_f, wu_f, wd_f, wg_s, wu_s, wd_s, wsem):
    b = pl.program_id(0)

    def weight_copies(e, slot):
        return [pltpu.make_async_copy(src.at[e], dst.at[slot], wsem.at[slot])
                for src, dst in ((wg_hbm, wg_f), (wu_hbm, wu_f), (wd_hbm, wd_f))]

    @pl.when(b == 0)
    def _():
        for cp in weight_copies(be_ref[0], 0):
            cp.start()

    @pl.when(first_ref[b] == 1)
    def _():
        slot = slot_ref[b]
        for cp in weight_copies(be_ref[b], slot):
            cp.wait()
        wg_s[...] = wg_f[slot].astype(BF16)
        wu_s[...] = wu_f[slot].astype(BF16)
        wd_s[...] = wd_f[slot].astype(BF16)

        @pl.when(next_ref[b] >= 0)
        def _():
            for cp in weight_copies(next_ref[b], 1 - slot):
                cp.start()

    @pl.when(bv_ref[b] > 0)
    def _():
        x = jnp.concatenate([xs_ref[pl.ds(s, MOE_R, stride=ROW_SUB), :] for s in range(ROW_SUB)], axis=1)
        xb = x.astype(BF16)
        g = jnp.dot(xb, wg_s[...], preferred_element_type=F32)
        u = jnp.dot(xb, wu_s[...], preferred_element_type=F32)
        a = (g * _sigmoid(g) * u).astype(BF16)
        o = jnp.dot(a, wd_s[...], preferred_element_type=F32)
        for s in range(ROW_SUB):
            o_ref[pl.ds(s, MOE_R, stride=ROW_SUB), :] = o[:, s * LANES:(s + 1) * LANES]

    @pl.when(bv_ref[b] == 0)
    def _():
        o_ref[...] = jnp.zeros_like(o_ref)


def _experts(blk_e, blk_valid, xs, w_g, w_u, w_d):
    idx = np.arange(MOE_BLOCKS)
    used = blk_valid > 0
    first = used & ((idx == 0) | (blk_e != jnp.roll(blk_e, 1)))
    ordinal = jnp.sum(first[None, :] & (idx[None, :] <= idx[:, None]), axis=1) - 1
    later_first = jnp.where(first[None, :] & (idx[None, :] > idx[:, None]), idx[None, :], MOE_BLOCKS)
    nxt = jnp.min(later_first, axis=1)
    next_e = jnp.where(nxt < MOE_BLOCKS, jnp.sum(jnp.where(idx[None, :] == nxt[:, None], blk_e[None, :], 0), axis=1), -1)
    plan = [blk_e, blk_valid, first.astype(jnp.int32), (ordinal % 2).astype(jnp.int32), next_e.astype(jnp.int32)]

    def rows(b, be, bv, *_):
        return (jnp.where(bv[b] > 0, b, MOE_BLOCKS - 1), 0)

    hbm = pl.BlockSpec(memory_space=pl.ANY)
    return pl.pallas_call(
        _expert_kernel,
        grid_spec=pltpu.PrefetchScalarGridSpec(
            num_scalar_prefetch=len(plan),
            grid=(MOE_BLOCKS,),
            in_specs=[pl.BlockSpec((MOE_R * ROW_SUB, LANES), rows), hbm, hbm, hbm],
            out_specs=pl.BlockSpec((MOE_R * ROW_SUB, LANES), lambda b, *_: (b, 0)),
            scratch_shapes=[pltpu.VMEM((2, D_MODEL, D_EXPERT), F32),
                            pltpu.VMEM((2, D_MODEL, D_EXPERT), F32),
                            pltpu.VMEM((2, D_EXPERT, D_MODEL), F32),
                            pltpu.VMEM((D_MODEL, D_EXPERT), BF16),
                            pltpu.VMEM((D_MODEL, D_EXPERT), BF16),
                            pltpu.VMEM((D_EXPERT, D_MODEL), BF16),
                            pltpu.SemaphoreType.DMA((2,))]),
        out_shape=jax.ShapeDtypeStruct((MOE_ROWS * ROW_SUB, LANES), F32),
        compiler_params=pltpu.CompilerParams(dimension_semantics=("arbitrary",), vmem_limit_bytes=VMEM_LIMIT),
        name="moe_experts",
    )(*plan, xs, w_g, w_u, w_d)


def _combine_kernel(dest_ref, o_hbm, x2_ref, ri_ref, gate_ref, gf_ref, yp_ref, ys_ref, obuf, sem):
    i = pl.program_id(0)
    slot = i % 2
    tile_rows = TM * ROW_SUB

    def issue_tile(tile, slot_):
        def issue(t, carry):
            cps = _row_copies(tile, t, dest_ref, lambda k, d: pltpu.make_async_copy(
                _row(o_hbm, d), _row(obuf, (2 * slot_ + k) * TM + t), sem.at[slot_]))
            for k, cp in enumerate(cps):
                cp.start(priority=k)
            return carry

        lax.fori_loop(0, TM, issue, 0, unroll=8)

    @pl.when(i == 0)
    def _():
        issue_tile(0, 0)

    @pl.when(i + 1 < N_TILES)
    def _():
        issue_tile(i + 1, 1 - slot)

    for k in range(2):
        start = pl.multiple_of((2 * slot + k) * tile_rows, tile_rows)
        pltpu.make_async_copy(o_hbm.at[pl.ds(0, tile_rows), :], obuf.at[pl.ds(start, tile_rows), :],
                              sem.at[slot]).wait()

    ri = ri_ref[...]
    g1 = ri[:, 2:3]
    g2 = ri[:, 3:4]
    base = 2 * slot * tile_rows
    moe = jnp.concatenate(
        [g1 * obuf[pl.ds(base + s, TM, stride=ROW_SUB), :] + g2 * obuf[pl.ds(base + tile_rows + s, TM, stride=ROW_SUB), :]
         for s in range(ROW_SUB)], axis=1)
    x3 = x2_ref[...] + _per_chunk(gate_ref) * moe
    y = _rmsnorm(x3, gf_ref[...])

    @pl.when(i < P_TILES)
    def _():
        yp_ref[0] = y

    @pl.when(i >= P_TILES)
    def _():
        ys_ref[...] = y.reshape(CH_PER_TILE, DEC_SEQ, D_MODEL)


def _combine(dest, o_rows, x2, rinfo, modc, g_final):
    xp_spec, xs_spec = _xp_spec(), _xs_spec()
    return pl.pallas_call(
        _combine_kernel,
        grid_spec=pltpu.PrefetchScalarGridSpec(
            num_scalar_prefetch=1,
            grid=(N_TILES,),
            in_specs=[pl.BlockSpec(memory_space=pl.ANY),
                      pl.BlockSpec((TM, D_MODEL), lambda i, *_: (i, 0)),
                      pl.BlockSpec((TM, LANES), lambda i, *_: (i, 0)),
                      _mod_spec(5),
                      pl.BlockSpec((1, D_MODEL), lambda i, *_: (0, 0))],
            out_specs=[pl.BlockSpec(xp_spec.block_shape, lambda i, *_: xp_spec.index_map(i)),
                       pl.BlockSpec(xs_spec.block_shape, lambda i, *_: xs_spec.index_map(i))],
            scratch_shapes=[pltpu.VMEM((4 * TM * ROW_SUB, LANES), F32), pltpu.SemaphoreType.DMA((2,))]),
        out_shape=[jax.ShapeDtypeStruct((BATCH, SEQ, D_MODEL), F32),
                   jax.ShapeDtypeStruct((DEC_BATCH, DEC_SEQ, D_MODEL), F32)],
        compiler_params=pltpu.CompilerParams(dimension_semantics=("arbitrary",), vmem_limit_bytes=VMEM_LIMIT),
        name="moe_combine",
    )(dest, o_rows, x2, rinfo, modc, g_final)


def _moe_plan(rt, cnt):
    experts = np.arange(N_EXPERTS)
    tile_cnt = cnt.reshape(N_TILES, SUBLANES, LANES)[:, 0, :N_EXPERTS]
    counts = jnp.sum(tile_cnt, axis=0)
    padded = jnp.ceil(counts / MOE_R) * MOE_R
    pad_end = jnp.sum(padded[:, None] * (experts[:, None] <= experts[None, :]), axis=0)
    base = pad_end - padded
    tiles = np.arange(N_TILES)
    earlier_tiles = (tiles[None, :, None] < tiles[:, None, None])
    tile_base = base[None, :] + jnp.sum(tile_cnt[None, :, :] * earlier_tiles, axis=1)
    eid = rt[0:2].reshape(1, 2, N_TILES, TM)
    rank = rt[4:6].reshape(2, N_TILES, TM)
    pick = eid == experts.astype(np.float32).reshape(N_EXPERTS, 1, 1, 1)
    dest = jnp.sum(jnp.where(pick, tile_base.T[:, None, :, None], 0.0), axis=0) + rank
    blk_start = (np.arange(MOE_BLOCKS) * MOE_R).astype(np.float32)
    blk_e = jnp.minimum(jnp.sum((blk_start[:, None] >= pad_end[None, :]).astype(F32), axis=1), N_EXPERTS - 1)
    mine = blk_e[:, None] == experts.astype(np.float32)[None, :]
    blk_fill = jnp.sum(jnp.where(mine, (counts + base)[None, :], 0.0), axis=1) - blk_start
    blk_valid = jnp.clip(blk_fill, 0, MOE_R)
    zlo = jnp.concatenate([jnp.floor((base + counts) / ZCHUNK), pad_end[-1:] / ZCHUNK])
    zhi = jnp.concatenate([pad_end / ZCHUNK, jnp.full((1,), MOE_ROWS // ZCHUNK, F32)])
    return (dest.reshape(-1).astype(jnp.int32), blk_e.astype(jnp.int32), blk_valid.astype(jnp.int32),
            zlo.astype(jnp.int32), zhi.astype(jnp.int32))


def kernel(x_prompt, x_sample, c_prompt, c_sample, cache_win_k, cache_win_v, state_conv, state_C, state_n, state_m, w_ada, b_ada, g_norm_mix, g_norm_ffn, w_in, attn_sinks, conv_w, conv_b, b_igate, b_fgate, g_mhnorm, w_out, w_router_group, b_router_group, w_router_expert, b_router_expert, w_exp_gate, w_exp_up, w_exp_down, g_final):
    l = 0
    n_streams = BATCH + DEC_BATCH
    pad_streams = -n_streams % (2 * SUBLANES)
    mod = _modulation(jnp.concatenate([c_prompt, c_sample, jnp.zeros((pad_streams, D_MODEL), F32)], axis=0),
                      w_ada[l], b_ada[l])
    chunk_stream = np.concatenate([np.repeat(np.arange(BATCH), SEQ // CHUNK), BATCH + np.arange(DEC_BATCH)])
    modc = mod[chunk_stream]

    wi = w_in[l]
    s_q, s_k, s_v, s_qkm, s_vm, s_ig, s_fg = 0, 512, 640, 768, 1792, 2304, 2308
    s_om = 2312
    w_in_r = jnp.concatenate(
        [wi[:, s_q:s_qkm], wi[:, s_qkm:s_vm], wi[:, s_vm:s_ig], wi[:, s_om:], wi[:, s_ig:s_om],
         jnp.zeros((D_MODEL, LANES - 2 * M_HEADS), F32)], axis=1).astype(BF16)
    q, kv, qkm, vm, om, gates = _inproj(x_prompt, x_sample, modc, g_norm_mix[l].reshape(1, -1), w_in_r)

    cache_kv = jnp.concatenate([cache_win_k[l].reshape(DEC_BATCH * WINDOW, A_KV_WIDTH),
                                cache_win_v[l].reshape(DEC_BATCH * WINDOW, A_KV_WIDTH)], axis=1)
    att_p, att_s = _attention(attn_sinks[l], q, kv, cache_kv)

    gbias = jnp.concatenate([b_igate[l], b_fgate[l], jnp.zeros((LANES - 2 * M_HEADS,), F32)]).reshape(1, LANES)
    common = (qkm, vm, om, gates, conv_w[l], conv_b[l].reshape(1, -1), gbias, g_mhnorm[l].reshape(1, -1))
    zeros_p = (jnp.zeros((BATCH, SUBLANES, 2 * M_WIDTH), F32),
               jnp.zeros((BATCH, M_HEADS, M_HEAD_DIM, M_HEAD_DIM), F32),
               jnp.zeros((BATCH, M_HEADS, M_HEAD_DIM), F32),
               jnp.zeros((BATCH, 1, LANES), F32))
    LP = 256
    hm_p, C_p, n_p, m_p = _mlstm(*common, *zeros_p, nb=BATCH, L=LP, nc=SEQ // LP, ns=1, row_off=0,
                                 name="mlstm_prompt")
    carry_s = jnp.concatenate([jnp.zeros((DEC_BATCH, SUBLANES - (CONV_WIDTH - 1), 2 * M_WIDTH), F32),
                               state_conv[l]], axis=1)
    m0_s = jnp.pad(state_m[l], ((0, 0), (0, LANES - M_HEADS))).reshape(DEC_BATCH, 1, LANES)
    hm_s, C_s, n_s, m_s = _mlstm(*common, carry_s, state_C[l], state_n[l], m0_s,
                                 nb=DEC_BATCH, L=DEC_SEQ, nc=1, ns=4, row_off=N_P, name="mlstm_sample")

    w_router = jnp.concatenate([w_router_group[l], w_router_expert[l],
                                jnp.zeros((D_MODEL, LANES - N_GROUPS - N_EXPERTS), F32)], axis=1)
    w_router_hi = w_router.astype(BF16)
    w_router = jnp.concatenate([w_router_hi, (w_router - w_router_hi.astype(F32)).astype(BF16)], axis=1)
    b_router =jnp.concatenate([b_router_group[l], b_router_expert[l],
                                jnp.zeros((LANES - N_GROUPS - N_EXPERTS,), F32)]).reshape(1, LANES)
    x2, h2t, rinfo, rt, cnt = _outproj(x_prompt, x_sample, att_p, att_s, hm_p, hm_s, modc,
                                       g_norm_ffn[l].reshape(1, -1), w_out[l].astype(BF16), w_router, b_router)

    dest, blk_e, blk_valid, zlo, zhi = _moe_plan(rt, cnt)
    xs_rows = _dispatch(dest, zlo, zhi, h2t)
    o_rows = _experts(blk_e, blk_valid, xs_rows, w_exp_gate[l], w_exp_up[l], w_exp_down[l])
    y_prompt, y_sample = _combine(dest, o_rows, x2, rinfo, modc, g_final.reshape(1, -1))

    kv_p = kv.reshape(N_TOK // WINDOW, WINDOW, 2 * A_KV_WIDTH)[SEQ // WINDOW - 1:N_P // WINDOW:SEQ // WINDOW]
    win_k_p = kv_p[..., :A_KV_WIDTH].reshape(1, BATCH, WINDOW, A_KV_HEADS, A_HEAD_DIM)
    win_v_p = kv_p[..., A_KV_WIDTH:].reshape(1, BATCH, WINDOW, A_KV_HEADS, A_HEAD_DIM)
    kv_s = kv[N_P:].reshape(DEC_BATCH, DEC_SEQ, 2 * A_KV_WIDTH)
    win_k_s = jnp.concatenate([cache_win_k[l][:, DEC_SEQ:],
                               kv_s[..., :A_KV_WIDTH].reshape(DEC_BATCH, DEC_SEQ, A_KV_HEADS, A_HEAD_DIM)], axis=1)[None]
    win_v_s = jnp.concatenate([cache_win_v[l][:, DEC_SEQ:],
                               kv_s[..., A_KV_WIDTH:].reshape(DEC_BATCH, DEC_SEQ, A_KV_HEADS, A_HEAD_DIM)], axis=1)[None]
    qkm_c = qkm.reshape(N_CHUNKS, CHUNK, 2 * M_WIDTH)
    tail = slice(CHUNK - (CONV_WIDTH - 1), CHUNK)
    conv_p = qkm_c[SEQ // CHUNK - 1:N_P // CHUNK:SEQ // CHUNK, tail][None]
    conv_s = qkm_c[N_P // CHUNK:, tail][None]
    return (y_prompt, y_sample,
            win_k_p, win_v_p, conv_p, C_p[None], n_p[None], m_p[:, 0, :M_HEADS][None],
            win_k_s, win_v_s, conv_s, C_s[None], n_s[None], m_s[:, 0, :M_HEADS][None])
```

```python
import functools

import numpy as np
import jax
import jax.numpy as jnp
from jax import lax
from jax.experimental import pallas as pl
from jax.experimental.pallas import tpu as pltpu

F32 = jnp.float32
BF16 = jnp.bfloat16
HIGHEST = lax.Precision.HIGHEST

LANES = 128
SUBLANES = 8

D_MODEL = 1024
BATCH = 8
SEQ = 2048
DEC_BATCH = 32
DEC_SEQ = 64
PAST_LEN = 4096
CHUNK = 64
A_WIDTH = 512
A_HEAD_DIM = 64
A_HEADS = 8
A_KV_HEADS = 2
A_GROUP = 4
A_KV_WIDTH = 128
WINDOW = 128
ROT_DIM = 16
ROPE_THETA = 500000.0
M_WIDTH = 512
M_HEADS = 4
M_HEAD_DIM = 128
CONV_WIDTH = 4
N_GROUPS = 4
EXPERTS_PER_GROUP = 8
N_EXPERTS = 32
D_EXPERT = 512
EPS = 1e-6

N_P = BATCH * SEQ
N_S = DEC_BATCH * DEC_SEQ
N_TOK = N_P + N_S
N_CHUNKS = N_TOK // CHUNK
TM = 512
N_TILES = N_TOK // TM
P_TILES = N_P // TM
CH_PER_TILE = TM // CHUNK
ROW_SUB = D_MODEL // LANES
N_ASSIGN = 2 * N_TOK
MOE_R = 512
MOE_BLOCKS = N_ASSIGN // MOE_R + N_EXPERTS
MOE_ROWS = MOE_BLOCKS * MOE_R
C_Q, C_K, C_V, C_QKM, C_VM, C_OM, C_G = 0, 512, 640, 768, 1792, 2304, 2816
D_IN_PAD = 2944
VMEM_LIMIT = 48 * 1024 * 1024


def _sigmoid(x):
    return 1.0 / (1.0 + jnp.exp(-x))


def _mod_kernel(c_ref, w_ref, b_ref, o_ref):
    c = c_ref[...]
    s = c * _sigmoid(c)
    w = w_ref[...]
    s_hi, w_hi = s.astype(BF16), w.astype(BF16)
    s_lo, w_lo = (s - s_hi.astype(F32)).astype(BF16), (w - w_hi.astype(F32)).astype(BF16)
    o_ref[...] = (jnp.dot(s_hi, w_hi, preferred_element_type=F32) + jnp.dot(s_lo, w_hi, preferred_element_type=F32)
                  + jnp.dot(s_hi, w_lo, preferred_element_type=F32)) + b_ref[...]


def _modulation(c_all, w_ada, b_ada):
    n = c_all.shape[0]
    bn = 512
    return pl.pallas_call(
        _mod_kernel,
        grid=(6 * D_MODEL // bn,),
        in_specs=[pl.BlockSpec((n, D_MODEL), lambda j: (0, 0)),
                  pl.BlockSpec((D_MODEL, bn), lambda j: (0, j)),
                  pl.BlockSpec((1, bn), lambda j: (0, j))],
        out_specs=pl.BlockSpec((n, bn), lambda j: (0, j)),
        out_shape=jax.ShapeDtypeStruct((n, 6 * D_MODEL), F32),
        name="modulation",
    )(c_all, w_ada, b_ada.reshape(1, -1))


def _xp_spec():
    def idx(i):
        t = jnp.minimum(i, P_TILES - 1)
        return (t // (SEQ // TM), t % (SEQ // TM), 0)
    return pl.BlockSpec((1, TM, D_MODEL), idx)


def _xs_spec():
    return pl.BlockSpec((CH_PER_TILE, DEC_SEQ, D_MODEL), lambda i: (jnp.maximum(i - P_TILES, 0), 0, 0))


def _mod_spec(comp):
    return pl.BlockSpec((CH_PER_TILE, D_MODEL), lambda i, *_: (i, comp))


def _load_x(xp_ref, xs_ref):
    return jnp.where(pl.program_id(0) < P_TILES, xp_ref[0], xs_ref[...].reshape(TM, D_MODEL))


def _per_chunk(m_ref):
    m = m_ref[...]
    return jnp.concatenate([jnp.broadcast_to(m[c:c + 1], (CHUNK, D_MODEL)) for c in range(CH_PER_TILE)], axis=0)


def _rmsnorm(x, g):
    return x * lax.rsqrt(jnp.mean(x * x, axis=-1, keepdims=True) + EPS) * g


def _rope(x, cos, sa, sb):
    n = x.shape[1]
    rep = n // LANES
    if rep > 1:
        cos = jnp.concatenate([cos] * rep, axis=1)
        sa = jnp.concatenate([sa] * rep, axis=1)
        sb = jnp.concatenate([sb] * rep, axis=1)
    return x * cos + pltpu.roll(x, n - ROT_DIM // 2, 1) * sa + pltpu.roll(x, ROT_DIM // 2, 1) * sb


def _inproj_kernel(xp_ref, xs_ref, sh_ref, sc_ref, g_ref, w_ref, cos_ref, sa_ref, sb_ref,
                   q_ref, kv_ref, qkm_ref, vm_ref, om_ref, gt_ref):
    x = _load_x(xp_ref, xs_ref)
    h = _rmsnorm(x, g_ref[...]) * (1.0 + _per_chunk(sc_ref)) + _per_chunk(sh_ref)
    hb = h.astype(BF16)

    def proj(a, b):
        return jnp.dot(hb, w_ref[:, a:b], preferred_element_type=F32)

    cos, sa, sb = cos_ref[...], sa_ref[...], sb_ref[...]
    q_ref[...] = _rope(proj(C_Q, C_K), cos, sa, sb).astype(BF16)
    kv_ref[:, :A_KV_WIDTH] = _rope(proj(C_K, C_V), cos, sa, sb)
    kv_ref[:, A_KV_WIDTH:] = proj(C_V, C_QKM)
    qkm_ref[...] = proj(C_QKM, C_VM)
    vm_ref[...] = proj(C_VM, C_OM).astype(BF16)
    om_ref[...] = proj(C_OM, C_G)
    gt_ref[...] = proj(C_G, D_IN_PAD)


def _rope_tables():
    pos = np.concatenate([np.arange(SEQ), np.tile(PAST_LEN + np.arange(DEC_SEQ), CH_PER_TILE)]).astype(np.float64)
    inv_freq = ROPE_THETA ** (-np.arange(0, ROT_DIM, 2, dtype=np.float64) / ROT_DIM)
    lane = np.arange(LANES)
    hl = lane % A_HEAD_DIM
    ang = pos[:, None] * inv_freq[hl % (ROT_DIM // 2)][None, :]
    rot = (hl < ROT_DIM)[None, :]
    lo = (hl < ROT_DIM // 2)[None, :]
    cos = np.where(rot, np.cos(ang), 1.0)
    sa = np.where(lo, -np.sin(ang), 0.0)
    sb = np.where(rot & ~lo, np.sin(ang), 0.0)
    return [jnp.asarray(t, F32) for t in (cos, sa, sb)]


def _inproj(x_prompt, x_sample, modc, g_mix, w_in_r):
    cos, sa, sb = _rope_tables()
    tab_spec = pl.BlockSpec((TM, LANES), lambda i: (jnp.where(i < P_TILES, i % (SEQ // TM), SEQ // TM), 0))

    def out(n, dtype=F32):
        return pl.BlockSpec((TM, n), lambda i: (i, 0)), jax.ShapeDtypeStruct((N_TOK, n), dtype)

    outs = [out(A_WIDTH, BF16), out(2 * A_KV_WIDTH), out(2 * M_WIDTH), out(M_WIDTH, BF16), out(M_WIDTH), out(LANES)]
    return pl.pallas_call(
        _inproj_kernel,
        grid=(N_TILES,),
        in_specs=[_xp_spec(), _xs_spec(), _mod_spec(0), _mod_spec(1),
                  pl.BlockSpec((1, D_MODEL), lambda i: (0, 0)),
                  pl.BlockSpec((D_MODEL, D_IN_PAD), lambda i: (0, 0)),
                  tab_spec, tab_spec, tab_spec],
        out_specs=[o[0] for o in outs],
        out_shape=[o[1] for o in outs],
        compiler_params=pltpu.CompilerParams(vmem_limit_bytes=VMEM_LIMIT),
        name="inproj",
    )(x_prompt, x_sample, modc, modc, g_mix, w_in_r, cos, sa, sb)


ATT_NS = 1


def _attn_chunk_kernel(sink_ref, q_ref, prev_ref, cur_ref, o_ref):
    for st in range(ATT_NS):
        rows = slice(st * CHUNK, (st + 1) * CHUNK)
        _attn_chunk(sink_ref, q_ref[rows, :], prev_ref[st * WINDOW:(st + 1) * WINDOW, :], cur_ref[rows, :],
                    o_ref, rows)


def _attn_chunk(sink_ref, q, prev, cur, o_ref, rows):
    q = q * (A_HEAD_DIM ** -0.5)
    kv = jnp.concatenate([prev, cur], axis=0)
    for g in range(A_KV_HEADS):
        kg = kv[:, g * A_HEAD_DIM:(g + 1) * A_HEAD_DIM].astype(BF16)
        vg = kv[:, A_KV_WIDTH + g * A_HEAD_DIM:A_KV_WIDTH + (g + 1) * A_HEAD_DIM].astype(BF16)
        heads = [g * A_GROUP + i for i in range(A_GROUP)]
        qc = jnp.concatenate([q[:, h * A_HEAD_DIM:(h + 1) * A_HEAD_DIM] for h in heads], axis=0).astype(BF16)
        snk = jnp.concatenate([jnp.full((CHUNK, 1), sink_ref[h], F32) for h in heads], axis=0)
        s = lax.dot_general(qc, kg, (((1,), (1,)), ((), ())), preferred_element_type=F32)
        mx = jnp.maximum(jnp.max(s, axis=-1, keepdims=True), snk)
        p = jnp.exp(s - mx)
        den = jnp.sum(p, axis=-1, keepdims=True) + jnp.exp(snk - mx)
        o = jnp.dot(p.astype(BF16), vg, preferred_element_type=F32) / den
        for i, h in enumerate(heads):
            o_ref[rows, h * A_HEAD_DIM:(h + 1) * A_HEAD_DIM] = o[i * CHUNK:(i + 1) * CHUNK]


ATT_TQ = 1024
ATT_QB = 2 * CHUNK


def _attn_band_kernel(sink_ref, q_ref, prev_ref, cur_ref, o_ref, att_t):
    nk = ATT_QB + WINDOW
    q = (q_ref[...] * (A_HEAD_DIM ** -0.5)).astype(BF16)
    kv = jnp.concatenate([prev_ref[...], cur_ref[...]], axis=0)
    k2 = kv[:, :A_KV_WIDTH]
    k2r = pltpu.roll(k2, A_HEAD_DIM, 1)
    low = lax.broadcasted_iota(jnp.int32, k2.shape, 1) < A_HEAD_DIM
    k_placed = {(0, 0): jnp.where(low, k2, 0.0), (0, 1): jnp.where(low, 0.0, k2r),
                (1, 0): jnp.where(low, k2r, 0.0), (1, 1): jnp.where(low, 0.0, k2)}
    k_placed = {key: val.astype(BF16) for key, val in k_placed.items()}
    v_t = kv[:, A_KV_WIDTH:].T.astype(BF16)
    key_chunk = lax.broadcasted_iota(jnp.int32, (nk, ATT_QB), 0) // CHUNK
    q_chunk = lax.broadcasted_iota(jnp.int32, (nk, ATT_QB), 1) // CHUNK
    band = (key_chunk >= q_chunk) & (key_chunk <= q_chunk + WINDOW // CHUNK)
    has_history = pl.program_id(1) > 0
    for blk in range(ATT_TQ // ATT_QB):
        keys = slice(blk * ATT_QB, blk * ATT_QB + nk)
        qrows = slice(blk * ATT_QB, (blk + 1) * ATT_QB)
        valid = band & ((key_chunk >= WINDOW // CHUNK) | has_history) if blk == 0 else band
        for h in range(A_HEADS):
            g = h // A_GROUP
            s_t = lax.dot_general(k_placed[(g, h % 2)][keys], q[qrows, (h // 2) * LANES:(h // 2 + 1) * LANES],
                                  (((1,), (1,)), ((), ())), preferred_element_type=F32)
            s_t = jnp.where(valid, s_t, -jnp.inf)
            snk = sink_ref[h]
            mx = jnp.maximum(jnp.max(s_t, axis=0, keepdims=True), snk)
            p_t = jnp.exp(s_t - mx)
            den = jnp.sum(p_t, axis=0, keepdims=True) + jnp.exp(snk - mx)
            o_t = jnp.dot(v_t[:, keys], p_t.astype(BF16), preferred_element_type=F32)
            att_t[h * A_HEAD_DIM:(h + 1) * A_HEAD_DIM, qrows] = o_t[g * A_HEAD_DIM:(g + 1) * A_HEAD_DIM] / den
    o_ref[...] = att_t[...].T.astype(BF16)


def _attention(sinks, q, kv, cache_kv):
    smem = pl.BlockSpec(memory_space=pltpu.SMEM)
    tq = ATT_TQ
    nq = SEQ // tq
    att = pl.pallas_call(
        _attn_band_kernel,
        grid=(BATCH, nq),
        scratch_shapes=[pltpu.VMEM((A_WIDTH, ATT_TQ), F32)],
        in_specs=[smem,
                  pl.BlockSpec((tq, A_WIDTH), lambda b, j: (b * nq + j, 0)),
                  pl.BlockSpec((WINDOW, 2 * A_KV_WIDTH),
                               lambda b, j: (jnp.maximum((b * nq + j) * (tq // WINDOW) - 1, 0), 0)),
                  pl.BlockSpec((tq, 2 * A_KV_WIDTH), lambda b, j: (b * nq + j, 0))],
        out_specs=pl.BlockSpec((tq, A_WIDTH), lambda b, j: (b * nq + j, 0)),
        out_shape=jax.ShapeDtypeStruct((N_P, A_WIDTH), BF16),
        name="attn_prompt",
    )(sinks, q, kv, kv)
    rows = ATT_NS * DEC_SEQ
    off = N_P // rows
    att_s = pl.pallas_call(
        _attn_chunk_kernel,
        grid=(DEC_BATCH // ATT_NS,),
        in_specs=[smem,
                  pl.BlockSpec((rows, A_WIDTH), lambda b: (off + b, 0)),
                  pl.BlockSpec((ATT_NS * WINDOW, 2 * A_KV_WIDTH), lambda b: (b, 0)),
                  pl.BlockSpec((rows, 2 * A_KV_WIDTH), lambda b: (off + b, 0))],
        out_specs=pl.BlockSpec((rows, A_WIDTH), lambda b: (b, 0)),
        out_shape=jax.ShapeDtypeStruct((N_S, A_WIDTH), F32),
        name="attn_sample",
    )(sinks, q, cache_kv, kv)
    return att, att_s


def _conv4(x, w, b):
    x1 = pltpu.roll(x, 1, 0)
    near = b + x * w[3:4] + x1 * w[2:3]
    far = x * w[1:2] + x1 * w[0:1]
    return near + pltpu.roll(far, 2, 0)


def _mlstm_kernel(qkm_ref, vm_ref, om_ref, gt_ref, cw_ref, cb_ref, gb_ref, gmh_ref,
                  carry0_ref, c0_ref, n0_ref, m0_ref,
                  hm_ref, cout_ref, nout_ref, mout_ref, c_s, n_s, m_s, carry_s, *, L, NS):
    c = pl.program_id(1)

    @pl.when(c == 0)
    def _():
        c_s[...] = c0_ref[...]
        n_s[...] = n0_ref[...]
        m_s[...] = m0_ref[...]
        carry_s[...] = carry0_ref[...]

    w = cw_ref[...]
    b = cb_ref[...]
    gmh = gmh_ref[...]
    lane = lax.broadcasted_iota(jnp.int32, (L, LANES), 1)
    causal = lax.broadcasted_iota(jnp.int32, (L, L), 0) >= lax.broadcasted_iota(jnp.int32, (L, L), 1)
    ones_causal = causal.astype(F32)
    sel = (lax.broadcasted_iota(jnp.int32, (SUBLANES, LANES), 0)
           == lax.broadcasted_iota(jnp.int32, (SUBLANES, LANES), 1)).astype(F32)
    lane1 = lax.broadcasted_iota(jnp.int32, (1, LANES), 1)
    for st in range(NS):
        _mlstm_stream(st, qkm_ref, vm_ref, om_ref, gt_ref, gb_ref, hm_ref, c_s, n_s, m_s, carry_s,
                      w, b, gmh, lane, causal, ones_causal, sel, lane1, L)

    @pl.when(c == pl.num_programs(1) - 1)
    def _():
        cout_ref[...] = c_s[...]
        nout_ref[...] = n_s[...]
        mout_ref[...] = m_s[...]


def _mlstm_stream(st, qkm_ref, vm_ref, om_ref, gt_ref, gb_ref, hm_ref, c_s, n_s, m_s, carry_s,
                  w, b, gmh, lane, causal, ones_causal, sel, lane1, L):
    x = qkm_ref[st]
    y = _conv4(x, w, b)
    y8 = _conv4(jnp.concatenate([carry_s[st], x[:SUBLANES]], axis=0), w, b)
    y = jnp.concatenate([y8[SUBLANES:], y[SUBLANES:]], axis=0)
    carry_s[st] = x[L - SUBLANES:]
    a = y * _sigmoid(y)
    qa = a[:, :M_WIDTH] * (M_HEAD_DIM ** -0.5)
    ka = a[:, M_WIDTH:]
    v = vm_ref[st]
    om = om_ref[st]

    z = gt_ref[st] + gb_ref[...]
    f_log = jnp.minimum(z, 0.0) - jnp.log1p(jnp.exp(-jnp.abs(z)))
    val = jnp.where(lane < M_HEADS, z, f_log)
    cum = jnp.dot(ones_causal, val, preferred_element_type=F32, precision=HIGHEST)
    G = jnp.where(lane < M_HEADS, val, cum)
    GT = lax.dot_general(sel, G, (((1,), (1,)), ((), ())), preferred_element_type=F32, precision=HIGHEST)

    m_row = m_s[st]
    outs = []
    for h in range(M_HEADS):
        a_col = G[:, M_HEADS + h:M_HEADS + h + 1]
        i_col = G[:, h:h + 1]
        i_row = GT[h:h + 1, :]
        b_row = GT[M_HEADS + h:M_HEADS + h + 1, :]
        m_prev = m_row[:, h:h + 1]
        dm = jnp.where(causal, a_col - b_row + i_row, -jnp.inf)
        inter = a_col + m_prev
        m_t = jnp.maximum(inter, jnp.max(dm, axis=-1, keepdims=True))
        w_intra = jnp.exp(dm - m_t)
        w_inter = jnp.exp(inter - m_t)
        sl = slice(h * M_HEAD_DIM, (h + 1) * M_HEAD_DIM)
        qh, kh, vh = qa[:, sl], ka[:, sl], v[:, sl]
        qb = qh.astype(BF16)
        scores = lax.dot_general(qb, kh.astype(BF16), (((1,), (1,)), ((), ())), preferred_element_type=F32) * w_intra
        ch = c_s[st, h]
        nh = n_s[st, h:h + 1, :]
        num = (jnp.dot(scores.astype(BF16), vh.astype(BF16), preferred_element_type=F32)
               + w_inter * lax.dot_general(qb, ch.astype(BF16), (((1,), (1,)), ((), ())), preferred_element_type=F32))
        den = jnp.sum(scores, axis=-1, keepdims=True) + w_inter * jnp.sum(qh * nh, axis=-1, keepdims=True)
        hv = num / jnp.maximum(jnp.abs(den), jnp.exp(-m_t))
        hn = hv * lax.rsqrt(jnp.mean(hv * hv, axis=-1, keepdims=True) + EPS)
        outs.append(hn * gmh[:, sl] * _sigmoid(om[:, sl]))
        m_new = m_t[L - 1:L, :]
        a_last = a_col[L - 1:L, :]
        w_end = jnp.exp(a_last - a_col + i_col - m_new)
        decay = jnp.exp(a_last + m_prev - m_new)
        kw = kh * w_end
        c_s[st, h] = decay * ch + lax.dot_general(vh.astype(BF16), kw.astype(BF16), (((0,), (0,)), ((), ())),
                                                  preferred_element_type=F32)
        n_s[st, h:h + 1, :] = decay * nh + jnp.sum(kw, axis=0, keepdims=True)
        m_row = jnp.where(lane1 == h, m_new, m_row)
    m_s[st] = m_row
    hm_ref[st] = jnp.concatenate(outs, axis=1).astype(BF16)


def _mlstm(qkm, vm, om, gates, conv_w, conv_b, gbias, g_mh, carry0, c0, n0, m0, *, nb, L, nc, ns, row_off, name):
    S = nc * L
    first = row_off // S // ns

    def tok(arr):
        n = arr.shape[-1]
        return arr.reshape(N_TOK // S, S, n), pl.BlockSpec((ns, L, n), lambda i, c: (first + i, c, 0))

    def const(shape):
        return pl.BlockSpec(shape, lambda i, c: (0,) * len(shape))

    def per_stream(shape):
        return pl.BlockSpec((ns,) + shape, lambda i, c: (i,) + (0,) * len(shape))

    toks = [tok(a) for a in (qkm, vm, om, gates)]
    state_specs = [per_stream((SUBLANES, 2 * M_WIDTH)), per_stream((M_HEADS, M_HEAD_DIM, M_HEAD_DIM)),
                   per_stream((M_HEADS, M_HEAD_DIM)), per_stream((1, LANES))]
    in_specs = ([t[1] for t in toks]
                + [const((CONV_WIDTH, 2 * M_WIDTH)), const((1, 2 * M_WIDTH)), const((1, LANES)), const((1, M_WIDTH))]
                + state_specs)
    args = [t[0] for t in toks] + [conv_w, conv_b, gbias, g_mh, carry0, c0, n0, m0]
    hm, c_out, n_out, m_out = pl.pallas_call(
        functools.partial(_mlstm_kernel, L=L, NS=ns),
        grid=(nb // ns, nc),
        in_specs=in_specs,
        out_specs=[pl.BlockSpec((ns, L, M_WIDTH), lambda i, c: (i, c, 0))] + state_specs[1:],
        out_shape=[jax.ShapeDtypeStruct((nb, S, M_WIDTH), BF16),
                   jax.ShapeDtypeStruct((nb, M_HEADS, M_HEAD_DIM, M_HEAD_DIM), F32),
                   jax.ShapeDtypeStruct((nb, M_HEADS, M_HEAD_DIM), F32),
                   jax.ShapeDtypeStruct((nb, 1, LANES), F32)],
        scratch_shapes=[pltpu.VMEM((ns, M_HEADS, M_HEAD_DIM, M_HEAD_DIM), F32),
                        pltpu.VMEM((ns, M_HEADS, M_HEAD_DIM), F32),
                        pltpu.VMEM((ns, 1, LANES), F32),
                        pltpu.VMEM((ns, SUBLANES, 2 * M_WIDTH), F32)],
        compiler_params=pltpu.CompilerParams(dimension_semantics=("arbitrary", "arbitrary"),
                                             vmem_limit_bytes=VMEM_LIMIT),
        name=name,
    )(*args)
    return hm.reshape(nb * S, M_WIDTH), c_out, n_out, m_out


def _outproj_kernel(xp_ref, xs_ref, attp_ref, atts_ref, hmp_ref, hms_ref, gate_ref, sh_ref, sc_ref, g_ref,
                    w_ref, wr_ref, br_ref, x2_ref, h2_ref, ri_ref, rt_ref, cnt_ref):
    x = _load_x(xp_ref, xs_ref)
    is_prompt = pl.program_id(0) < P_TILES
    att = jnp.where(is_prompt, attp_ref[...], atts_ref[...].astype(BF16))
    hm = jnp.where(is_prompt, hmp_ref[...], hms_ref[...])
    mixed = (jnp.dot(att, w_ref[:A_WIDTH, :], preferred_element_type=F32)
             + jnp.dot(hm, w_ref[A_WIDTH:, :], preferred_element_type=F32))
    x2 = x + _per_chunk(gate_ref) * mixed
    x2_ref[...] = x2
    h2 = _rmsnorm(x2, g_ref[...]) * (1.0 + _per_chunk(sc_ref)) + _per_chunk(sh_ref)
    for s in range(ROW_SUB):
        h2_ref[pl.ds(s, TM, stride=ROW_SUB), :] = h2[:, s * LANES:(s + 1) * LANES]

    h_hi = h2.astype(BF16)
    h_lo = (h2 - h_hi.astype(F32)).astype(BF16)
    hi_w = jnp.dot(h_hi, wr_ref[...], preferred_element_type=F32)
    lg = (hi_w[:, :LANES] + hi_w[:, LANES:]
          + jnp.dot(h_lo, wr_ref[:, :LANES], preferred_element_type=F32)) + br_ref[...]
    lane = lax.broadcasted_iota(jnp.int32, (TM, LANES), 1)
    lanef = lane.astype(F32)
    ninf = -jnp.inf

    def first_argmax(vals):
        mx = jnp.max(vals, axis=-1, keepdims=True)
        return mx, jnp.min(jnp.where(vals == mx, lanef, float(LANES)), axis=-1, keepdims=True)

    is_grp = lane < N_GROUPS
    gmax, grp = first_argmax(jnp.where(is_grp, lg, ninf))
    p_grp = 1.0 / jnp.sum(jnp.where(is_grp, jnp.exp(lg - gmax), 0.0), axis=-1, keepdims=True)
    base = N_GROUPS + grp * EXPERTS_PER_GROUP
    in_grp = (lanef >= base) & (lanef < base + EXPERTS_PER_GROUP)
    el = jnp.where(in_grp, lg, ninf)
    v1, i1 = first_argmax(el)
    v2, i2 = first_argmax(jnp.where(lanef == i1, ninf, el))
    e = jnp.exp(v2 - v1)
    g1 = p_grp / (1.0 + e)
    g2 = p_grp * e / (1.0 + e)
    e1 = i1 - N_GROUPS
    e2 = i2 - N_GROUPS
    oh1 = lanef == e1
    oh2 = lanef == e2
    oh = jnp.where(oh1 | oh2, 1.0, 0.0)
    earlier = (lax.broadcasted_iota(jnp.int32, (TM, TM), 0) > lax.broadcasted_iota(jnp.int32, (TM, TM), 1))
    before = jnp.dot(earlier.astype(BF16), oh.astype(BF16), preferred_element_type=F32)
    r1 = jnp.sum(jnp.where(oh1, before, 0.0), axis=-1, keepdims=True)
    r2 = jnp.sum(jnp.where(oh2, before, 0.0), axis=-1, keepdims=True)
    cnt_ref[...] = jnp.broadcast_to(jnp.sum(oh, axis=0, keepdims=True), (SUBLANES, LANES))
    cols = (e1, e2, g1, g2, r1, r2)
    ri = jnp.zeros((TM, LANES), F32)
    for k, cval in enumerate(cols):
        ri = jnp.where(lane == k, cval, ri)
    ri_ref[...] = ri
    rt_ref[...] = ri.T[:SUBLANES]


def _outproj(x_prompt, x_sample, att_p, att_s, hm_p, hm_s, modc, g_ffn, w_out_b, w_router, b_router):
    def tok(n):
        return pl.BlockSpec((TM, n), lambda i: (i, 0))

    def tok_p(n):
        return pl.BlockSpec((TM, n), lambda i: (jnp.minimum(i, P_TILES - 1), 0))

    def tok_s(n):
        return pl.BlockSpec((TM, n), lambda i: (jnp.maximum(i - P_TILES, 0), 0))

    def const(shape):
        return pl.BlockSpec(shape, lambda i: (0,) * len(shape))

    return pl.pallas_call(
        _outproj_kernel,
        grid=(N_TILES,),
        in_specs=[_xp_spec(), _xs_spec(), tok_p(A_WIDTH), tok_s(A_WIDTH), tok_p(M_WIDTH), tok_s(M_WIDTH),
                  _mod_spec(2), _mod_spec(3), _mod_spec(4),
                  const((1, D_MODEL)), const((D_MODEL, D_MODEL)), const((D_MODEL, 2 * LANES)), const((1, LANES))],
        out_specs=[tok(D_MODEL), pl.BlockSpec((TM * ROW_SUB, LANES), lambda i: (i, 0)), tok(LANES),
                   pl.BlockSpec((SUBLANES, TM), lambda i: (0, i)),
                   pl.BlockSpec((SUBLANES, LANES), lambda i: (i, 0))],
        out_shape=[jax.ShapeDtypeStruct((N_TOK, D_MODEL), F32),
                   jax.ShapeDtypeStruct((N_TOK * ROW_SUB, LANES), F32),
                   jax.ShapeDtypeStruct((N_TOK, LANES), F32),
                   jax.ShapeDtypeStruct((SUBLANES, N_TOK), F32),
                   jax.ShapeDtypeStruct((N_TILES * SUBLANES, LANES), F32)],
        compiler_params=pltpu.CompilerParams(vmem_limit_bytes=VMEM_LIMIT),
        name="outproj",
    )(x_prompt, x_sample, att_p, att_s, hm_p, hm_s, modc, modc, modc, g_ffn, w_out_b, w_router, b_router)


def _row(ref, r):
    return ref.at[pl.ds(pl.multiple_of(r * ROW_SUB, ROW_SUB), ROW_SUB), :]


def _row_copies(i, t, dest_ref, make):
    tok = i * TM + t
    return [make(k, dest_ref[k * N_TOK + tok]) for k in range(2)]


ZCHUNK = 64


def _dispatch_kernel(dest_ref, zlo_ref, zhi_ref, h_ref, xs_out, zbuf, sem, zsem):
    i = pl.program_id(0)
    chunk_rows = ZCHUNK * ROW_SUB

    @pl.when(i == 0)
    def _():
        zbuf[...] = jnp.zeros_like(zbuf)

        def chunk_copy(c):
            return pltpu.make_async_copy(zbuf, xs_out.at[pl.ds(pl.multiple_of(c * chunk_rows, chunk_rows), chunk_rows), :],
                                         zsem)

        def start_range(rng, carry):
            def start(c, cc):
                chunk_copy(c).start()
                return cc

            lax.fori_loop(zlo_ref[rng], zhi_ref[rng], start, 0)
            return carry

        def wait_range(rng, carry):
            def wait(c, cc):
                chunk_copy(c).wait()
                return cc

            lax.fori_loop(zlo_ref[rng], zhi_ref[rng], wait, 0)
            return carry

        lax.fori_loop(0, N_EXPERTS + 1, start_range, 0)
        lax.fori_loop(0, N_EXPERTS + 1, wait_range, 0)

    def copies(t):
        return _row_copies(i, t, dest_ref, lambda k, d: pltpu.make_async_copy(_row(h_ref, t), _row(xs_out, d), sem))

    def issue(t, carry):
        for k, cp in enumerate(copies(t)):
            cp.start(priority=k)
        return carry

    def wait(t, carry):
        for cp in copies(t):
            cp.wait()
        return carry

    lax.fori_loop(0, TM, issue, 0, unroll=8)
    lax.fori_loop(0, TM, wait, 0, unroll=8)


def _dispatch(dest, zlo, zhi, h2t):
    return pl.pallas_call(
        _dispatch_kernel,
        grid_spec=pltpu.PrefetchScalarGridSpec(
            num_scalar_prefetch=3,
            grid=(N_TILES,),
            in_specs=[pl.BlockSpec((TM * ROW_SUB, LANES), lambda i, *_: (i, 0))],
            out_specs=pl.BlockSpec(memory_space=pl.ANY),
            scratch_shapes=[pltpu.VMEM((ZCHUNK * ROW_SUB, LANES), F32),
                            pltpu.SemaphoreType.DMA, pltpu.SemaphoreType.DMA]),
        out_shape=jax.ShapeDtypeStruct((MOE_ROWS * ROW_SUB, LANES), F32),
        compiler_params=pltpu.CompilerParams(dimension_semantics=("arbitrary",)),
        name="moe_dispatch",
    )(dest, zlo, zhi, h2t)


def _expert_kernel(be_ref, bv_ref, first_ref, slot_ref, next_ref, xs_ref, wg_hbm, wu_hbm, wd_hbm, o_ref,
                   wg_f, wu_f, wd_f, wg_s, wu_s, wd_s, wsem):
    b = pl.program_id(0)

    def weight_copies(e, slot):
        return [pltpu.make_async_copy(src.at[e], dst.at[slot], wsem.at[slot])
                for src, dst in ((wg_hbm, wg_f), (wu_hbm, wu_f), (wd_hbm, wd_f))]

    @pl.when(b == 0)
    def _():
        for cp in weight_copies(be_ref[0], 0):
            cp.start()

    @pl.when(first_ref[b] == 1)
    def _():
        slot = slot_ref[b]
        for cp in weight_copies(be_ref[b], slot):
            cp.wait()
        wg_s[...] = wg_f[slot].astype(BF16)
        wu_s[...] = wu_f[slot].astype(BF16)
        wd_s[...] = wd_f[slot].astype(BF16)

        @pl.when(next_ref[b] >= 0)
        def _():
            for cp in weight_copies(next_ref[b], 1 - slot):
                cp.start()

    @pl.when(bv_ref[b] > 0)
    def _():
        x = jnp.concatenate([xs_ref[pl.ds(s, MOE_R, stride=ROW_SUB), :] for s in range(ROW_SUB)], axis=1)
        xb = x.astype(BF16)
        g = jnp.dot(xb, wg_s[...], preferred_element_type=F32)
        u = jnp.dot(xb, wu_s[...], preferred_element_type=F32)
        a = (g * _sigmoid(g) * u).astype(BF16)
        o = jnp.dot(a, wd_s[...], preferred_element_type=F32)
        for s in range(ROW_SUB):
            o_ref[pl.ds(s, MOE_R, stride=ROW_SUB), :] = o[:, s * LANES:(s + 1) * LANES]

    @pl.when(bv_ref[b] == 0)
    def _():
        o_ref[...] = jnp.zeros_like(o_ref)


def _experts(blk_e, blk_valid, xs, w_g, w_u, w_d):
    idx = np.arange(MOE_BLOCKS)
    used = blk_valid > 0
    first = used & ((idx == 0) | (blk_e != jnp.roll(blk_e, 1)))
    ordinal = jnp.sum(first[None, :] & (idx[None, :] <= idx[:, None]), axis=1) - 1
    later_first = jnp.where(first[None, :] & (idx[None, :] > idx[:, None]), idx[None, :], MOE_BLOCKS)
    nxt = jnp.min(later_first, axis=1)
    next_e = jnp.where(nxt < MOE_BLOCKS, jnp.sum(jnp.where(idx[None, :] == nxt[:, None], blk_e[None, :], 0), axis=1), -1)
    plan = [blk_e, blk_valid, first.astype(jnp.int32), (ordinal % 2).astype(jnp.int32), next_e.astype(jnp.int32)]

    def rows(b, be, bv, *_):
        return (jnp.where(bv[b] > 0, b, MOE_BLOCKS - 1), 0)

    hbm = pl.BlockSpec(memory_space=pl.ANY)
    return pl.pallas_call(
        _expert_kernel,
        grid_spec=pltpu.PrefetchScalarGridSpec(
            num_scalar_prefetch=len(plan),
            grid=(MOE_BLOCKS,),
            in_specs=[pl.BlockSpec((MOE_R * ROW_SUB, LANES), rows), hbm, hbm, hbm],
            out_specs=pl.BlockSpec((MOE_R * ROW_SUB, LANES), lambda b, *_: (b, 0)),
            scratch_shapes=[pltpu.VMEM((2, D_MODEL, D_EXPERT), F32),
                            pltpu.VMEM((2, D_MODEL, D_EXPERT), F32),
                            pltpu.VMEM((2, D_EXPERT, D_MODEL), F32),
                            pltpu.VMEM((D_MODEL, D_EXPERT), BF16),
                            pltpu.VMEM((D_MODEL, D_EXPERT), BF16),
                            pltpu.VMEM((D_EXPERT, D_MODEL), BF16),
                            pltpu.SemaphoreType.DMA((2,))]),
        out_shape=jax.ShapeDtypeStruct((MOE_ROWS * ROW_SUB, LANES), F32),
        compiler_params=pltpu.CompilerParams(dimension_semantics=("arbitrary",), vmem_limit_bytes=VMEM_LIMIT),
        name="moe_experts",
    )(*plan, xs, w_g, w_u, w_d)


def _combine_kernel(dest_ref, o_hbm, x2_ref, ri_ref, gate_ref, gf_ref, yp_ref, ys_ref, obuf, sem):
    i = pl.program_id(0)
    slot = i % 2
    tile_rows = TM * ROW_SUB

    def issue_tile(tile, slot_):
        def issue(t, carry):
            cps = _row_copies(tile, t, dest_ref, lambda k, d: pltpu.make_async_copy(
                _row(o_hbm, d), _row(obuf, (2 * slot_ + k) * TM + t), sem.at[slot_]))
            for k, cp in enumerate(cps):
                cp.start(priority=k)
            return carry

        lax.fori_loop(0, TM, issue, 0, unroll=8)

    @pl.when(i == 0)
    def _():
        issue_tile(0, 0)

    @pl.when(i + 1 < N_TILES)
    def _():
        issue_tile(i + 1, 1 - slot)

    for k in range(2):
        start = pl.multiple_of((2 * slot + k) * tile_rows, tile_rows)
        pltpu.make_async_copy(o_hbm.at[pl.ds(0, tile_rows), :], obuf.at[pl.ds(start, tile_rows), :],
                              sem.at[slot]).wait()

    ri = ri_ref[...]
    g1 = ri[:, 2:3]
    g2 = ri[:, 3:4]
    base = 2 * slot * tile_rows
    moe = jnp.concatenate(
        [g1 * obuf[pl.ds(base + s, TM, stride=ROW_SUB), :] + g2 * obuf[pl.ds(base + tile_rows + s, TM, stride=ROW_SUB), :]
         for s in range(ROW_SUB)], axis=1)
    x3 = x2_ref[...] + _per_chunk(gate_ref) * moe
    y = _rmsnorm(x3, gf_ref[...])

    @pl.when(i < P_TILES)
    def _():
        yp_ref[0] = y

    @pl.when(i >= P_TILES)
    def _():
        ys_ref[...] = y.reshape(CH_PER_TILE, DEC_SEQ, D_MODEL)


def _combine(dest, o_rows, x2, rinfo, modc, g_final):
    xp_spec, xs_spec = _xp_spec(), _xs_spec()
    return pl.pallas_call(
        _combine_kernel,
        grid_spec=pltpu.PrefetchScalarGridSpec(
            num_scalar_prefetch=1,
            grid=(N_TILES,),
            in_specs=[pl.BlockSpec(memory_space=pl.ANY),
                      pl.BlockSpec((TM, D_MODEL), lambda i, *_: (i, 0)),
                      pl.BlockSpec((TM, LANES), lambda i, *_: (i, 0)),
                      _mod_spec(5),
                      pl.BlockSpec((1, D_MODEL), lambda i, *_: (0, 0))],
            out_specs=[pl.BlockSpec(xp_spec.block_shape, lambda i, *_: xp_spec.index_map(i)),
                       pl.BlockSpec(xs_spec.block_shape, lambda i, *_: xs_spec.index_map(i))],
            scratch_shapes=[pltpu.VMEM((4 * TM * ROW_SUB, LANES), F32), pltpu.SemaphoreType.DMA((2,))]),
        out_shape=[jax.ShapeDtypeStruct((BATCH, SEQ, D_MODEL), F32),
                   jax.ShapeDtypeStruct((DEC_BATCH, DEC_SEQ, D_MODEL), F32)],
        compiler_params=pltpu.CompilerParams(dimension_semantics=("arbitrary",), vmem_limit_bytes=VMEM_LIMIT),
        name="moe_combine",
    )(dest, o_rows, x2, rinfo, modc, g_final)


def _moe_plan(rt, cnt):
    experts = np.arange(N_EXPERTS)
    tile_cnt = cnt.reshape(N_TILES, SUBLANES, LANES)[:, 0, :N_EXPERTS]
    counts = jnp.sum(tile_cnt, axis=0)
    padded = jnp.ceil(counts / MOE_R) * MOE_R
    pad_end = jnp.sum(padded[:, None] * (experts[:, None] <= experts[None, :]), axis=0)
    base = pad_end - padded
    tiles = np.arange(N_TILES)
    earlier_tiles = (tiles[None, :, None] < tiles[:, None, None])
    tile_base = base[None, :] + jnp.sum(tile_cnt[None, :, :] * earlier_tiles, axis=1)
    eid = rt[0:2].reshape(1, 2, N_TILES, TM)
    rank = rt[4:6].reshape(2, N_TILES, TM)
    pick = eid == experts.astype(np.float32).reshape(N_EXPERTS, 1, 1, 1)
    dest = jnp.sum(jnp.where(pick, tile_base.T[:, None, :, None], 0.0), axis=0) + rank
    blk_start = (np.arange(MOE_BLOCKS) * MOE_R).astype(np.float32)
    blk_e = jnp.minimum(jnp.sum((blk_start[:, None] >= pad_end[None, :]).astype(F32), axis=1), N_EXPERTS - 1)
    mine = blk_e[:, None] == experts.astype(np.float32)[None, :]
    blk_fill = jnp.sum(jnp.where(mine, (counts + base)[None, :], 0.0), axis=1) - blk_start
    blk_valid = jnp.clip(blk_fill, 0, MOE_R)
    zlo = jnp.concatenate([jnp.floor((base + counts) / ZCHUNK), pad_end[-1:] / ZCHUNK])
    zhi = jnp.concatenate([pad_end / ZCHUNK, jnp.full((1,), MOE_ROWS // ZCHUNK, F32)])
    return (dest.reshape(-1).astype(jnp.int32), blk_e.astype(jnp.int32), blk_valid.astype(jnp.int32),
            zlo.astype(jnp.int32), zhi.astype(jnp.int32))


def kernel(x_prompt, x_sample, c_prompt, c_sample, cache_win_k, cache_win_v, state_conv, state_C, state_n, state_m, w_ada, b_ada, g_norm_mix, g_norm_ffn, w_in, attn_sinks, conv_w, conv_b, b_igate, b_fgate, g_mhnorm, w_out, w_router_group, b_router_group, w_router_expert, b_router_expert, w_exp_gate, w_exp_up, w_exp_down, g_final):
    l = 0
    n_streams = BATCH + DEC_BATCH
    pad_streams = -n_streams % (2 * SUBLANES)
    mod = _modulation(jnp.concatenate([c_prompt, c_sample, jnp.zeros((pad_streams, D_MODEL), F32)], axis=0),
                      w_ada[l], b_ada[l])
    chunk_stream = np.concatenate([np.repeat(np.arange(BATCH), SEQ // CHUNK), BATCH + np.arange(DEC_BATCH)])
    modc = mod[chunk_stream]

    wi = w_in[l]
    s_q, s_k, s_v, s_qkm, s_vm, s_ig, s_fg = 0, 512, 640, 768, 1792, 2304, 2308
    s_om = 2312
    w_in_r = jnp.concatenate(
        [wi[:, s_q:s_qkm], wi[:, s_qkm:s_vm], wi[:, s_vm:s_ig], wi[:, s_om:], wi[:, s_ig:s_om],
         jnp.zeros((D_MODEL, LANES - 2 * M_HEADS), F32)], axis=1).astype(BF16)
    q, kv, qkm, vm, om, gates = _inproj(x_prompt, x_sample, modc, g_norm_mix[l].reshape(1, -1), w_in_r)

    cache_kv = jnp.concatenate([cache_win_k[l].reshape(DEC_BATCH * WINDOW, A_KV_WIDTH),
                                cache_win_v[l].reshape(DEC_BATCH * WINDOW, A_KV_WIDTH)], axis=1)
    att_p, att_s = _attention(attn_sinks[l], q, kv, cache_kv)

    gbias = jnp.concatenate([b_igate[l], b_fgate[l], jnp.zeros((LANES - 2 * M_HEADS,), F32)]).reshape(1, LANES)
    common = (qkm, vm, om, gates, conv_w[l], conv_b[l].reshape(1, -1), gbias, g_mhnorm[l].reshape(1, -1))
    zeros_p = (jnp.zeros((BATCH, SUBLANES, 2 * M_WIDTH), F32),
               jnp.zeros((BATCH, M_HEADS, M_HEAD_DIM, M_HEAD_DIM), F32),
               jnp.zeros((BATCH, M_HEADS, M_HEAD_DIM), F32),
               jnp.zeros((BATCH, 1, LANES), F32))
    LP = 256
    hm_p, C_p, n_p, m_p = _mlstm(*common, *zeros_p, nb=BATCH, L=LP, nc=SEQ // LP, ns=1, row_off=0,
                                 name="mlstm_prompt")
    carry_s = jnp.concatenate([jnp.zeros((DEC_BATCH, SUBLANES - (CONV_WIDTH - 1), 2 * M_WIDTH), F32),
                               state_conv[l]], axis=1)
    m0_s = jnp.pad(state_m[l], ((0, 0), (0, LANES - M_HEADS))).reshape(DEC_BATCH, 1, LANES)
    hm_s, C_s, n_s, m_s = _mlstm(*common, carry_s, state_C[l], state_n[l], m0_s,
                                 nb=DEC_BATCH, L=DEC_SEQ, nc=1, ns=4, row_off=N_P, name="mlstm_sample")

    w_router = jnp.concatenate([w_router_group[l], w_router_expert[l],
                                jnp.zeros((D_MODEL, LANES - N_GROUPS - N_EXPERTS), F32)], axis=1)
    w_router_hi = w_router.astype(BF16)
    w_router = jnp.concatenate([w_router_hi, (w_router - w_router_hi.astype(F32)).astype(BF16)], axis=1)
    b_router =jnp.concatenate([b_router_group[l], b_router_expert[l],
                                jnp.zeros((LANES - N_GROUPS - N_EXPERTS,), F32)]).reshape(1, LANES)
    x2, h2t, rinfo, rt, cnt = _outproj(x_prompt, x_sample, att_p, att_s, hm_p, hm_s, modc,
                                       g_norm_ffn[l].reshape(1, -1), w_out[l].astype(BF16), w_router, b_router)

    dest, blk_e, blk_valid, zlo, zhi = _moe_plan(rt, cnt)
    xs_rows = _dispatch(dest, zlo, zhi, h2t)
    o_rows = _experts(blk_e, blk_valid, xs_rows, w_exp_gate[l], w_exp_up[l], w_exp_down[l])
    y_prompt, y_sample = _combine(dest, o_rows, x2, rinfo, modc, g_final.reshape(1, -1))

    kv_p = kv.reshape(N_TOK // WINDOW, WINDOW, 2 * A_KV_WIDTH)[SEQ // WINDOW - 1:N_P // WINDOW:SEQ // WINDOW]
    win_k_p = kv_p[..., :A_KV_WIDTH].reshape(1, BATCH, WINDOW, A_KV_HEADS, A_HEAD_DIM)
    win_v_p = kv_p[..., A_KV_WIDTH:].reshape(1, BATCH, WINDOW, A_KV_HEADS, A_HEAD_DIM)
    kv_s = kv[N_P:].reshape(DEC_BATCH, DEC_SEQ, 2 * A_KV_WIDTH)
    win_k_s = jnp.concatenate([cache_win_k[l][:, DEC_SEQ:],
                               kv_s[..., :A_KV_WIDTH].reshape(DEC_BATCH, DEC_SEQ, A_KV_HEADS, A_HEAD_DIM)], axis=1)[None]
    win_v_s = jnp.concatenate([cache_win_v[l][:, DEC_SEQ:],
                               kv_s[..., A_KV_WIDTH:].reshape(DEC_BATCH, DEC_SEQ, A_KV_HEADS, A_HEAD_DIM)], axis=1)[None]
    qkm_c = qkm.reshape(N_CHUNKS, CHUNK, 2 * M_WIDTH)
    tail = slice(CHUNK - (CONV_WIDTH - 1), CHUNK)
    conv_p = qkm_c[SEQ // CHUNK - 1:N_P // CHUNK:SEQ // CHUNK, tail][None]
    conv_s = qkm_c[N_P // CHUNK:, tail][None]
    return (y_prompt, y_sample,
            win_k_p, win_v_p, conv_p, C_p[None], n_p[None], m_p[:, 0, :M_HEADS][None],
            win_k_s, win_v_s, conv_s, C_s[None], n_s[None], m_s[:, 0, :M_HEADS][None])
```

```python
import functools

import numpy as np
import jax
import jax.numpy as jnp
from jax import lax
from jax.experimental import pallas as pl
from jax.experimental.pallas import tpu as pltpu

F32 = jnp.float32
BF16 = jnp.bfloat16
HIGHEST = lax.Precision.HIGHEST

LANES = 128
SUBLANES = 8

D_MODEL = 1024
BATCH = 8
SEQ = 2048
DEC_BATCH = 32
DEC_SEQ = 64
PAST_LEN = 4096
CHUNK = 64
A_WIDTH = 512
A_HEAD_DIM = 64
A_HEADS = 8
A_KV_HEADS = 2
A_GROUP = 4
A_KV_WIDTH = 128
WINDOW = 128
ROT_DIM = 16
ROPE_THETA = 500000.0
M_WIDTH = 512
M_HEADS = 4
M_HEAD_DIM = 128
CONV_WIDTH = 4
N_GROUPS = 4
EXPERTS_PER_GROUP = 8
N_EXPERTS = 32
D_EXPERT = 512
EPS = 1e-6

N_P = BATCH * SEQ
N_S = DEC_BATCH * DEC_SEQ
N_TOK = N_P + N_S
N_CHUNKS = N_TOK // CHUNK
TM = 512
N_TILES = N_TOK // TM
P_TILES = N_P // TM
CH_PER_TILE = TM // CHUNK
ROW_SUB = D_MODEL // LANES
N_ASSIGN = 2 * N_TOK
MOE_R = 512
MOE_BLOCKS = N_ASSIGN // MOE_R + N_EXPERTS
MOE_ROWS = MOE_BLOCKS * MOE_R
C_Q, C_K, C_V, C_QKM, C_VM, C_OM, C_G = 0, 512, 640, 768, 1792, 2304, 2816
D_IN_PAD = 2944
VMEM_LIMIT = 48 * 1024 * 1024


def _sigmoid(x):
    return 1.0 / (1.0 + jnp.exp(-x))


def _mod_kernel(c_ref, w_ref, b_ref, o_ref):
    c = c_ref[...]
    s = c * _sigmoid(c)
    w = w_ref[...]
    s_hi, w_hi = s.astype(BF16), w.astype(BF16)
    s_lo, w_lo = (s - s_hi.astype(F32)).astype(BF16), (w - w_hi.astype(F32)).astype(BF16)
    o_ref[...] = (jnp.dot(s_hi, w_hi, preferred_element_type=F32) + jnp.dot(s_lo, w_hi, preferred_element_type=F32)
                  + jnp.dot(s_hi, w_lo, preferred_element_type=F32)) + b_ref[...]


def _modulation(c_all, w_ada, b_ada):
    n = c_all.shape[0]
    bn = 512
    return pl.pallas_call(
        _mod_kernel,
        grid=(6 * D_MODEL // bn,),
        in_specs=[pl.BlockSpec((n, D_MODEL), lambda j: (0, 0)),
                  pl.BlockSpec((D_MODEL, bn), lambda j: (0, j)),
                  pl.BlockSpec((1, bn), lambda j: (0, j))],
        out_specs=pl.BlockSpec((n, bn), lambda j: (0, j)),
        out_shape=jax.ShapeDtypeStruct((n, 6 * D_MODEL), F32),
        name="modulation",
    )(c_all, w_ada, b_ada.reshape(1, -1))


def _xp_spec():
    def idx(i):
        t = jnp.minimum(i, P_TILES - 1)
        return (t // (SEQ // TM), t % (SEQ // TM), 0)
    return pl.BlockSpec((1, TM, D_MODEL), idx)


def _xs_spec():
    return pl.BlockSpec((CH_PER_TILE, DEC_SEQ, D_MODEL), lambda i: (jnp.maximum(i - P_TILES, 0), 0, 0))


def _mod_spec(comp):
    return pl.BlockSpec((CH_PER_TILE, D_MODEL), lambda i, *_: (i, comp))


def _load_x(xp_ref, xs_ref):
    return jnp.where(pl.program_id(0) < P_TILES, xp_ref[0], xs_ref[...].reshape(TM, D_MODEL))


def _per_chunk(m_ref):
    m = m_ref[...]
    return jnp.concatenate([jnp.broadcast_to(m[c:c + 1], (CHUNK, D_MODEL)) for c in range(CH_PER_TILE)], axis=0)


def _rmsnorm(x, g):
    return x * lax.rsqrt(jnp.mean(x * x, axis=-1, keepdims=True) + EPS) * g


def _rope(x, cos, sa, sb):
    n = x.shape[1]
    rep = n // LANES
    if rep > 1:
        cos = jnp.concatenate([cos] * rep, axis=1)
        sa = jnp.concatenate([sa] * rep, axis=1)
        sb = jnp.concatenate([sb] * rep, axis=1)
    return x * cos + pltpu.roll(x, n - ROT_DIM // 2, 1) * sa + pltpu.roll(x, ROT_DIM // 2, 1) * sb


def _inproj_kernel(xp_ref, xs_ref, sh_ref, sc_ref, g_ref, w_ref, cos_ref, sa_ref, sb_ref,
                   q_ref, kv_ref, qkm_ref, vm_ref, om_ref, gt_ref):
    x = _load_x(xp_ref, xs_ref)
    h = _rmsnorm(x, g_ref[...]) * (1.0 + _per_chunk(sc_ref)) + _per_chunk(sh_ref)
    hb = h.astype(BF16)

    def proj(a, b):
        return jnp.dot(hb, w_ref[:, a:b], preferred_element_type=F32)

    cos, sa, sb = cos_ref[...], sa_ref[...], sb_ref[...]
    q_ref[...] = _rope(proj(C_Q, C_K), cos, sa, sb).astype(BF16)
    kv_ref[:, :A_KV_WIDTH] = _rope(proj(C_K, C_V), cos, sa, sb)
    kv_ref[:, A_KV_WIDTH:] = proj(C_V, C_QKM)
    qkm_ref[...] = proj(C_QKM, C_VM)
    vm_ref[...] = proj(C_VM, C_OM).astype(BF16)
    om_ref[...] = proj(C_OM, C_G)
    gt_ref[...] = proj(C_G, D_IN_PAD)


def _rope_tables():
    pos = np.concatenate([np.arange(SEQ), np.tile(PAST_LEN + np.arange(DEC_SEQ), CH_PER_TILE)]).astype(np.float64)
    inv_freq = ROPE_THETA ** (-np.arange(0, ROT_DIM, 2, dtype=np.float64) / ROT_DIM)
    lane = np.arange(LANES)
    hl = lane % A_HEAD_DIM
    ang = pos[:, None] * inv_freq[hl % (ROT_DIM // 2)][None, :]
    rot = (hl < ROT_DIM)[None, :]
    lo = (hl < ROT_DIM // 2)[None, :]
    cos = np.where(rot, np.cos(ang), 1.0)
    sa = np.where(lo, -np.sin(ang), 0.0)
    sb = np.where(rot & ~lo, np.sin(ang), 0.0)
    return [jnp.asarray(t, F32) for t in (cos, sa, sb)]


def _inproj(x_prompt, x_sample, modc, g_mix, w_in_r):
    cos, sa, sb = _rope_tables()
    tab_spec = pl.BlockSpec((TM, LANES), lambda i: (jnp.where(i < P_TILES, i % (SEQ // TM), SEQ // TM), 0))

    def out(n, dtype=F32):
        return pl.BlockSpec((TM, n), lambda i: (i, 0)), jax.ShapeDtypeStruct((N_TOK, n), dtype)

    outs = [out(A_WIDTH, BF16), out(2 * A_KV_WIDTH), out(2 * M_WIDTH), out(M_WIDTH, BF16), out(M_WIDTH), out(LANES)]
    return pl.pallas_call(
        _inproj_kernel,
        grid=(N_TILES,),
        in_specs=[_xp_spec(), _xs_spec(), _mod_spec(0), _mod_spec(1),
                  pl.BlockSpec((1, D_MODEL), lambda i: (0, 0)),
                  pl.BlockSpec((D_MODEL, D_IN_PAD), lambda i: (0, 0)),
                  tab_spec, tab_spec, tab_spec],
        out_specs=[o[0] for o in outs],
        out_shape=[o[1] for o in outs],
        compiler_params=pltpu.CompilerParams(vmem_limit_bytes=VMEM_LIMIT),
        name="inproj",
    )(x_prompt, x_sample, modc, modc, g_mix, w_in_r, cos, sa, sb)


ATT_NS = 1


def _attn_chunk_kernel(sink_ref, q_ref, prev_ref, cur_ref, o_ref):
    for st in range(ATT_NS):
        rows = slice(st * CHUNK, (st + 1) * CHUNK)
        _attn_chunk(sink_ref, q_ref[rows, :], prev_ref[st * WINDOW:(st + 1) * WINDOW, :], cur_ref[rows, :],
                    o_ref, rows)


def _attn_chunk(sink_ref, q, prev, cur, o_ref, rows):
    q = q * (A_HEAD_DIM ** -0.5)
    kv = jnp.concatenate([prev, cur], axis=0)
    for g in range(A_KV_HEADS):
        kg = kv[:, g * A_HEAD_DIM:(g + 1) * A_HEAD_DIM].astype(BF16)
        vg = kv[:, A_KV_WIDTH + g * A_HEAD_DIM:A_KV_WIDTH + (g + 1) * A_HEAD_DIM].astype(BF16)
        heads = [g * A_GROUP + i for i in range(A_GROUP)]
        qc = jnp.concatenate([q[:, h * A_HEAD_DIM:(h + 1) * A_HEAD_DIM] for h in heads], axis=0).astype(BF16)
        snk = jnp.concatenate([jnp.full((CHUNK, 1), sink_ref[h], F32) for h in heads], axis=0)
        s = lax.dot_general(qc, kg, (((1,), (1,)), ((), ())), preferred_element_type=F32)
        mx = jnp.maximum(jnp.max(s, axis=-1, keepdims=True), snk)
        p = jnp.exp(s - mx)
        den = jnp.sum(p, axis=-1, keepdims=True) + jnp.exp(snk - mx)
        o = jnp.dot(p.astype(BF16), vg, preferred_element_type=F32) / den
        for i, h in enumerate(heads):
            o_ref[rows, h * A_HEAD_DIM:(h + 1) * A_HEAD_DIM] = o[i * CHUNK:(i + 1) * CHUNK]


ATT_TQ = 2048
ATT_QB = 2 * CHUNK


def _attn_band_kernel(sink_ref, q_ref, prev_ref, cur_ref, o_ref, att_t):
    nk = ATT_QB + WINDOW
    q = (q_ref[...] * (A_HEAD_DIM ** -0.5)).astype(BF16)
    kv = jnp.concatenate([prev_ref[...], cur_ref[...]], axis=0)
    k2 = kv[:, :A_KV_WIDTH]
    k2r = pltpu.roll(k2, A_HEAD_DIM, 1)
    low = lax.broadcasted_iota(jnp.int32, k2.shape, 1) < A_HEAD_DIM
    k_placed = {(0, 0): jnp.where(low, k2, 0.0), (0, 1): jnp.where(low, 0.0, k2r),
                (1, 0): jnp.where(low, k2r, 0.0), (1, 1): jnp.where(low, 0.0, k2)}
    k_placed = {key: val.astype(BF16) for key, val in k_placed.items()}
    v_t = kv[:, A_KV_WIDTH:].T.astype(BF16)
    key_chunk = lax.broadcasted_iota(jnp.int32, (nk, ATT_QB), 0) // CHUNK
    q_chunk = lax.broadcasted_iota(jnp.int32, (nk, ATT_QB), 1) // CHUNK
    band = (key_chunk >= q_chunk) & (key_chunk <= q_chunk + WINDOW // CHUNK)
    has_history = pl.program_id(1) > 0
    for blk in range(ATT_TQ // ATT_QB):
        keys = slice(blk * ATT_QB, blk * ATT_QB + nk)
        qrows = slice(blk * ATT_QB, (blk + 1) * ATT_QB)
        valid = band & ((key_chunk >= WINDOW // CHUNK) | has_history) if blk == 0 else band
        for h in range(A_HEADS):
            g = h // A_GROUP
            s_t = lax.dot_general(k_placed[(g, h % 2)][keys], q[qrows, (h // 2) * LANES:(h // 2 + 1) * LANES],
                                  (((1,), (1,)), ((), ())), preferred_element_type=F32)
            s_t = jnp.where(valid, s_t, -jnp.inf)
            snk = sink_ref[h]
            mx = jnp.maximum(jnp.max(s_t, axis=0, keepdims=True), snk)
            p_t = jnp.exp(s_t - mx)
            den = jnp.sum(p_t, axis=0, keepdims=True) + jnp.exp(snk - mx)
            o_t = jnp.dot(v_t[:, keys], p_t.astype(BF16), preferred_element_type=F32)
            att_t[h * A_HEAD_DIM:(h + 1) * A_HEAD_DIM, qrows] = o_t[g * A_HEAD_DIM:(g + 1) * A_HEAD_DIM] / den
    o_ref[...] = att_t[...].T.astype(BF16)


def _attention(sinks, q, kv, cache_kv):
    smem = pl.BlockSpec(memory_space=pltpu.SMEM)
    tq = ATT_TQ
    nq = SEQ // tq
    att = pl.pallas_call(
        _attn_band_kernel,
        grid=(BATCH, nq),
        scratch_shapes=[pltpu.VMEM((A_WIDTH, ATT_TQ), F32)],
        in_specs=[smem,
                  pl.BlockSpec((tq, A_WIDTH), lambda b, j: (b * nq + j, 0)),
                  pl.BlockSpec((WINDOW, 2 * A_KV_WIDTH),
                               lambda b, j: (jnp.maximum((b * nq + j) * (tq // WINDOW) - 1, 0), 0)),
                  pl.BlockSpec((tq, 2 * A_KV_WIDTH), lambda b, j: (b * nq + j, 0))],
        out_specs=pl.BlockSpec((tq, A_WIDTH), lambda b, j: (b * nq + j, 0)),
        out_shape=jax.ShapeDtypeStruct((N_P, A_WIDTH), BF16),
        name="attn_prompt",
    )(sinks, q, kv, kv)
    rows = ATT_NS * DEC_SEQ
    off = N_P // rows
    att_s = pl.pallas_call(
        _attn_chunk_kernel,
        grid=(DEC_BATCH // ATT_NS,),
        in_specs=[smem,
                  pl.BlockSpec((rows, A_WIDTH), lambda b: (off + b, 0)),
                  pl.BlockSpec((ATT_NS * WINDOW, 2 * A_KV_WIDTH), lambda b: (b, 0)),
                  pl.BlockSpec((rows, 2 * A_KV_WIDTH), lambda b: (off + b, 0))],
        out_specs=pl.BlockSpec((rows, A_WIDTH), lambda b: (b, 0)),
        out_shape=jax.ShapeDtypeStruct((N_S, A_WIDTH), F32),
        name="attn_sample",
    )(sinks, q, cache_kv, kv)
    return att, att_s


def _conv4(x, w, b):
    x1 = pltpu.roll(x, 1, 0)
    near = b + x * w[3:4] + x1 * w[2:3]
    far = x * w[1:2] + x1 * w[0:1]
    return near + pltpu.roll(far, 2, 0)


def _mlstm_kernel(qkm_ref, vm_ref, om_ref, gt_ref, cw_ref, cb_ref, gb_ref, gmh_ref,
                  carry0_ref, c0_ref, n0_ref, m0_ref,
                  hm_ref, cout_ref, nout_ref, mout_ref, c_s, n_s, m_s, carry_s, *, L, NS):
    c = pl.program_id(1)

    @pl.when(c == 0)
    def _():
        c_s[...] = c0_ref[...]
        n_s[...] = n0_ref[...]
        m_s[...] = m0_ref[...]
        carry_s[...] = carry0_ref[...]

    w = cw_ref[...]
    b = cb_ref[...]
    gmh = gmh_ref[...]
    lane = lax.broadcasted_iota(jnp.int32, (L, LANES), 1)
    causal = lax.broadcasted_iota(jnp.int32, (L, L), 0) >= lax.broadcasted_iota(jnp.int32, (L, L), 1)
    ones_causal = causal.astype(F32)
    sel = (lax.broadcasted_iota(jnp.int32, (SUBLANES, LANES), 0)
           == lax.broadcasted_iota(jnp.int32, (SUBLANES, LANES), 1)).astype(F32)
    lane1 = lax.broadcasted_iota(jnp.int32, (1, LANES), 1)
    for st in range(NS):
        _mlstm_stream(st, qkm_ref, vm_ref, om_ref, gt_ref, gb_ref, hm_ref, c_s, n_s, m_s, carry_s,
                      w, b, gmh, lane, causal, ones_causal, sel, lane1, L)

    @pl.when(c == pl.num_programs(1) - 1)
    def _():
        cout_ref[...] = c_s[...]
        nout_ref[...] = n_s[...]
        mout_ref[...] = m_s[...]


def _mlstm_stream(st, qkm_ref, vm_ref, om_ref, gt_ref, gb_ref, hm_ref, c_s, n_s, m_s, carry_s,
                  w, b, gmh, lane, causal, ones_causal, sel, lane1, L):
    x = qkm_ref[st]
    y = _conv4(x, w, b)
    y8 = _conv4(jnp.concatenate([carry_s[st], x[:SUBLANES]], axis=0), w, b)
    y = jnp.concatenate([y8[SUBLANES:], y[SUBLANES:]], axis=0)
    carry_s[st] = x[L - SUBLANES:]
    a = y * _sigmoid(y)
    qa = a[:, :M_WIDTH] * (M_HEAD_DIM ** -0.5)
    ka = a[:, M_WIDTH:]
    v = vm_ref[st]
    om = om_ref[st]

    z = gt_ref[st] + gb_ref[...]
    f_log = jnp.minimum(z, 0.0) - jnp.log1p(jnp.exp(-jnp.abs(z)))
    val = jnp.where(lane < M_HEADS, z, f_log)
    cum = jnp.dot(ones_causal, val, preferred_element_type=F32, precision=HIGHEST)
    G = jnp.where(lane < M_HEADS, val, cum)
    GT = lax.dot_general(sel, G, (((1,), (1,)), ((), ())), preferred_element_type=F32, precision=HIGHEST)

    m_row = m_s[st]
    outs = []
    for h in range(M_HEADS):
        a_col = G[:, M_HEADS + h:M_HEADS + h + 1]
        i_col = G[:, h:h + 1]
        i_row = GT[h:h + 1, :]
        b_row = GT[M_HEADS + h:M_HEADS + h + 1, :]
        m_prev = m_row[:, h:h + 1]
        dm = jnp.where(causal, a_col - b_row + i_row, -jnp.inf)
        inter = a_col + m_prev
        m_t = jnp.maximum(inter, jnp.max(dm, axis=-1, keepdims=True))
        w_intra = jnp.exp(dm - m_t)
        w_inter = jnp.exp(inter - m_t)
        sl = slice(h * M_HEAD_DIM, (h + 1) * M_HEAD_DIM)
        qh, kh, vh = qa[:, sl], ka[:, sl], v[:, sl]
        qb = qh.astype(BF16)
        scores = lax.dot_general(qb, kh.astype(BF16), (((1,), (1,)), ((), ())), preferred_element_type=F32) * w_intra
        ch = c_s[st, h]
        nh = n_s[st, h:h + 1, :]
        num = (jnp.dot(scores.astype(BF16), vh.astype(BF16), preferred_element_type=F32)
               + w_inter * lax.dot_general(qb, ch.astype(BF16), (((1,), (1,)), ((), ())), preferred_element_type=F32))
        den = jnp.sum(scores, axis=-1, keepdims=True) + w_inter * jnp.sum(qh * nh, axis=-1, keepdims=True)
        hv = num / jnp.maximum(jnp.abs(den), jnp.exp(-m_t))
        hn = hv * lax.rsqrt(jnp.mean(hv * hv, axis=-1, keepdims=True) + EPS)
        outs.append(hn * gmh[:, sl] * _sigmoid(om[:, sl]))
        m_new = m_t[L - 1:L, :]
        a_last = a_col[L - 1:L, :]
        w_end = jnp.exp(a_last - a_col + i_col - m_new)
        decay = jnp.exp(a_last + m_prev - m_new)
        kw = kh * w_end
        c_s[st, h] = decay * ch + lax.dot_general(vh.astype(BF16), kw.astype(BF16), (((0,), (0,)), ((), ())),
                                                  preferred_element_type=F32)
        n_s[st, h:h + 1, :] = decay * nh + jnp.sum(kw, axis=0, keepdims=True)
        m_row = jnp.where(lane1 == h, m_new, m_row)
    m_s[st] = m_row
    hm_ref[st] = jnp.concatenate(outs, axis=1).astype(BF16)


def _mlstm(qkm, vm, om, gates, conv_w, conv_b, gbias, g_mh, carry0, c0, n0, m0, *, nb, L, nc, ns, row_off, name):
    S = nc * L
    first = row_off // S // ns

    def tok(arr):
        n = arr.shape[-1]
        return arr.reshape(N_TOK // S, S, n), pl.BlockSpec((ns, L, n), lambda i, c: (first + i, c, 0))

    def const(shape):
        return pl.BlockSpec(shape, lambda i, c: (0,) * len(shape))

    def per_stream(shape):
        return pl.BlockSpec((ns,) + shape, lambda i, c: (i,) + (0,) * len(shape))

    toks = [tok(a) for a in (qkm, vm, om, gates)]
    state_specs = [per_stream((SUBLANES, 2 * M_WIDTH)), per_stream((M_HEADS, M_HEAD_DIM, M_HEAD_DIM)),
                   per_stream((M_HEADS, M_HEAD_DIM)), per_stream((1, LANES))]
    in_specs = ([t[1] for t in toks]
                + [const((CONV_WIDTH, 2 * M_WIDTH)), const((1, 2 * M_WIDTH)), const((1, LANES)), const((1, M_WIDTH))]
                + state_specs)
    args = [t[0] for t in toks] + [conv_w, conv_b, gbias, g_mh, carry0, c0, n0, m0]
    hm, c_out, n_out, m_out = pl.pallas_call(
        functools.partial(_mlstm_kernel, L=L, NS=ns),
        grid=(nb // ns, nc),
        in_specs=in_specs,
        out_specs=[pl.BlockSpec((ns, L, M_WIDTH), lambda i, c: (i, c, 0))] + state_specs[1:],
        out_shape=[jax.ShapeDtypeStruct((nb, S, M_WIDTH), BF16),
                   jax.ShapeDtypeStruct((nb, M_HEADS, M_HEAD_DIM, M_HEAD_DIM), F32),
                   jax.ShapeDtypeStruct((nb, M_HEADS, M_HEAD_DIM), F32),
                   jax.ShapeDtypeStruct((nb, 1, LANES), F32)],
        scratch_shapes=[pltpu.VMEM((ns, M_HEADS, M_HEAD_DIM, M_HEAD_DIM), F32),
                        pltpu.VMEM((ns, M_HEADS, M_HEAD_DIM), F32),
                        pltpu.VMEM((ns, 1, LANES), F32),
                        pltpu.VMEM((ns, SUBLANES, 2 * M_WIDTH), F32)],
        compiler_params=pltpu.CompilerParams(dimension_semantics=("arbitrary", "arbitrary"),
                                             vmem_limit_bytes=VMEM_LIMIT),
        name=name,
    )(*args)
    return hm.reshape(nb * S, M_WIDTH), c_out, n_out, m_out


def _outproj_kernel(xp_ref, xs_ref, attp_ref, atts_ref, hmp_ref, hms_ref, gate_ref, sh_ref, sc_ref, g_ref,
                    w_ref, wr_ref, br_ref, x2_ref, h2_ref, ri_ref, rt_ref, cnt_ref):
    x = _load_x(xp_ref, xs_ref)
    is_prompt = pl.program_id(0) < P_TILES
    att = jnp.where(is_prompt, attp_ref[...], atts_ref[...].astype(BF16))
    hm = jnp.where(is_prompt, hmp_ref[...], hms_ref[...])
    mixed = (jnp.dot(att, w_ref[:A_WIDTH, :], preferred_element_type=F32)
             + jnp.dot(hm, w_ref[A_WIDTH:, :], preferred_element_type=F32))
    x2 = x + _per_chunk(gate_ref) * mixed
    x2_ref[...] = x2
    h2 = _rmsnorm(x2, g_ref[...]) * (1.0 + _per_chunk(sc_ref)) + _per_chunk(sh_ref)
    for s in range(ROW_SUB):
        h2_ref[pl.ds(s, TM, stride=ROW_SUB), :] = h2[:, s * LANES:(s + 1) * LANES]

    h_hi = h2.astype(BF16)
    h_lo = (h2 - h_hi.astype(F32)).astype(BF16)
    hi_w = jnp.dot(h_hi, wr_ref[...], preferred_element_type=F32)
    lg = (hi_w[:, :LANES] + hi_w[:, LANES:]
          + jnp.dot(h_lo, wr_ref[:, :LANES], preferred_element_type=F32)) + br_ref[...]
    lane = lax.broadcasted_iota(jnp.int32, (TM, LANES), 1)
    lanef = lane.astype(F32)
    ninf = -jnp.inf

    def first_argmax(vals):
        mx = jnp.max(vals, axis=-1, keepdims=True)
        return mx, jnp.min(jnp.where(vals == mx, lanef, float(LANES)), axis=-1, keepdims=True)

    is_grp = lane < N_GROUPS
    gmax, grp = first_argmax(jnp.where(is_grp, lg, ninf))
    p_grp = 1.0 / jnp.sum(jnp.where(is_grp, jnp.exp(lg - gmax), 0.0), axis=-1, keepdims=True)
    base = N_GROUPS + grp * EXPERTS_PER_GROUP
    in_grp = (lanef >= base) & (lanef < base + EXPERTS_PER_GROUP)
    el = jnp.where(in_grp, lg, ninf)
    v1, i1 = first_argmax(el)
    v2, i2 = first_argmax(jnp.where(lanef == i1, ninf, el))
    e = jnp.exp(v2 - v1)
    g1 = p_grp / (1.0 + e)
    g2 = p_grp * e / (1.0 + e)
    e1 = i1 - N_GROUPS
    e2 = i2 - N_GROUPS
    oh1 = lanef == e1
    oh2 = lanef == e2
    oh = jnp.where(oh1 | oh2, 1.0, 0.0)
    earlier = (lax.broadcasted_iota(jnp.int32, (TM, TM), 0) > lax.broadcasted_iota(jnp.int32, (TM, TM), 1))
    before = jnp.dot(earlier.astype(BF16), oh.astype(BF16), preferred_element_type=F32)
    r1 = jnp.sum(jnp.where(oh1, before, 0.0), axis=-1, keepdims=True)
    r2 = jnp.sum(jnp.where(oh2, before, 0.0), axis=-1, keepdims=True)
    cnt_ref[...] = jnp.broadcast_to(jnp.sum(oh, axis=0, keepdims=True), (SUBLANES, LANES))
    cols = (e1, e2, g1, g2, r1, r2)
    ri = jnp.zeros((TM, LANES), F32)
    for k, cval in enumerate(cols):
        ri = jnp.where(lane == k, cval, ri)
    ri_ref[...] = ri
    rt_ref[...] = ri.T[:SUBLANES]


def _outproj(x_prompt, x_sample, att_p, att_s, hm_p, hm_s, modc, g_ffn, w_out_b, w_router, b_router):
    def tok(n):
        return pl.BlockSpec((TM, n), lambda i: (i, 0))

    def tok_p(n):
        return pl.BlockSpec((TM, n), lambda i: (jnp.minimum(i, P_TILES - 1), 0))

    def tok_s(n):
        return pl.BlockSpec((TM, n), lambda i: (jnp.maximum(i - P_TILES, 0), 0))

    def const(shape):
        return pl.BlockSpec(shape, lambda i: (0,) * len(shape))

    return pl.pallas_call(
        _outproj_kernel,
        grid=(N_TILES,),
        in_specs=[_xp_spec(), _xs_spec(), tok_p(A_WIDTH), tok_s(A_WIDTH), tok_p(M_WIDTH), tok_s(M_WIDTH),
                  _mod_spec(2), _mod_spec(3), _mod_spec(4),
                  const((1, D_MODEL)), const((D_MODEL, D_MODEL)), const((D_MODEL, 2 * LANES)), const((1, LANES))],
        out_specs=[tok(D_MODEL), pl.BlockSpec((TM * ROW_SUB, LANES), lambda i: (i, 0)), tok(LANES),
                   pl.BlockSpec((SUBLANES, TM), lambda i: (0, i)),
                   pl.BlockSpec((SUBLANES, LANES), lambda i: (i, 0))],
        out_shape=[jax.ShapeDtypeStruct((N_TOK, D_MODEL), F32),
                   jax.ShapeDtypeStruct((N_TOK * ROW_SUB, LANES), F32),
                   jax.ShapeDtypeStruct((N_TOK, LANES), F32),
                   jax.ShapeDtypeStruct((SUBLANES, N_TOK), F32),
                   jax.ShapeDtypeStruct((N_TILES * SUBLANES, LANES), F32)],
        compiler_params=pltpu.CompilerParams(vmem_limit_bytes=VMEM_LIMIT),
        name="outproj",
    )(x_prompt, x_sample, att_p, att_s, hm_p, hm_s, modc, modc, modc, g_ffn, w_out_b, w_router, b_router)


def _row(ref, r):
    return ref.at[pl.ds(pl.multiple_of(r * ROW_SUB, ROW_SUB), ROW_SUB), :]


def _row_copies(i, t, dest_ref, make):
    tok = i * TM + t
    return [make(k, dest_ref[k * N_TOK + tok]) for k in range(2)]


ZCHUNK = 64


def _dispatch_kernel(dest_ref, zlo_ref, zhi_ref, h_ref, xs_out, zbuf, sem, zsem):
    i = pl.program_id(0)
    chunk_rows = ZCHUNK * ROW_SUB

    @pl.when(i == 0)
    def _():
        zbuf[...] = jnp.zeros_like(zbuf)

        def chunk_copy(c):
            return pltpu.make_async_copy(zbuf, xs_out.at[pl.ds(pl.multiple_of(c * chunk_rows, chunk_rows), chunk_rows), :],
                                         zsem)

        def start_range(rng, carry):
            def start(c, cc):
                chunk_copy(c).start()
                return cc

            lax.fori_loop(zlo_ref[rng], zhi_ref[rng], start, 0)
            return carry

        def wait_range(rng, carry):
            def wait(c, cc):
                chunk_copy(c).wait()
                return cc

            lax.fori_loop(zlo_ref[rng], zhi_ref[rng], wait, 0)
            return carry

        lax.fori_loop(0, N_EXPERTS + 1, start_range, 0)
        lax.fori_loop(0, N_EXPERTS + 1, wait_range, 0)

    def copies(t):
        return _row_copies(i, t, dest_ref, lambda k, d: pltpu.make_async_copy(_row(h_ref, t), _row(xs_out, d), sem))

    def issue(t, carry):
        for k, cp in enumerate(copies(t)):
            cp.start(priority=k)
        return carry

    def wait(t, carry):
        for cp in copies(t):
            cp.wait()
        return carry

    lax.fori_loop(0, TM, issue, 0, unroll=8)
    lax.fori_loop(0, TM, wait, 0, unroll=8)


def _dispatch(dest, zlo, zhi, h2t):
    return pl.pallas_call(
        _dispatch_kernel,
        grid_spec=pltpu.PrefetchScalarGridSpec(
            num_scalar_prefetch=3,
            grid=(N_TILES,),
            in_specs=[pl.BlockSpec((TM * ROW_SUB, LANES), lambda i, *_: (i, 0))],
            out_specs=pl.BlockSpec(memory_space=pl.ANY),
            scratch_shapes=[pltpu.VMEM((ZCHUNK * ROW_SUB, LANES), F32),
                            pltpu.SemaphoreType.DMA, pltpu.SemaphoreType.DMA]),
        out_shape=jax.ShapeDtypeStruct((MOE_ROWS * ROW_SUB, LANES), F32),
        compiler_params=pltpu.CompilerParams(dimension_semantics=("arbitrary",)),
        name="moe_dispatch",
    )(dest, zlo, zhi, h2t)


def _expert_kernel(be_ref, bv_ref, first_ref, slot_ref, next_ref, xs_ref, wg_hbm, wu_hbm, wd_hbm, o_ref,
                   wg_f, wu_f, wd_f, wg_s, wu_s, wd_s, wsem):
    b = pl.program_id(0)

    def weight_copies(e, slot):
        return [pltpu.make_async_copy(src.at[e], dst.at[slot], wsem.at[slot])
                for src, dst in ((wg_hbm, wg_f), (wu_hbm, wu_f), (wd_hbm, wd_f))]

    @pl.when(b == 0)
    def _():
        for cp in weight_copies(be_ref[0], 0):
            cp.start()

    @pl.when(first_ref[b] == 1)
    def _():
        slot = slot_ref[b]
        for cp in weight_copies(be_ref[b], slot):
            cp.wait()
        wg_s[...] = wg_f[slot].astype(BF16)
        wu_s[...] = wu_f[slot].astype(BF16)
        wd_s[...] = wd_f[slot].astype(BF16)

        @pl.when(next_ref[b] >= 0)
        def _():
            for cp in weight_copies(next_ref[b], 1 - slot):
                cp.start()

    @pl.when(bv_ref[b] > 0)
    def _():
        x = jnp.concatenate([xs_ref[pl.ds(s, MOE_R, stride=ROW_SUB), :] for s in range(ROW_SUB)], axis=1)
        xb = x.astype(BF16)
        g = jnp.dot(xb, wg_s[...], preferred_element_type=F32)
        u = jnp.dot(xb, wu_s[...], preferred_element_type=F32)
        a = (g * _sigmoid(g) * u).astype(BF16)
        o = jnp.dot(a, wd_s[...], preferred_element_type=F32)
        for s in range(ROW_SUB):
            o_ref[pl.ds(s, MOE_R, stride=ROW_SUB), :] = o[:, s * LANES:(s + 1) * LANES]

    @pl.when(bv_ref[b] == 0)
    def _():
        o_ref[...] = jnp.zeros_like(o_ref)


def _experts(blk_e, blk_valid, xs, w_g, w_u, w_d):
    idx = np.arange(MOE_BLOCKS)
    used = blk_valid > 0
    first = used & ((idx == 0) | (blk_e != jnp.roll(blk_e, 1)))
    ordinal = jnp.sum(first[None, :] & (idx[None, :] <= idx[:, None]), axis=1) - 1
    later_first = jnp.where(first[None, :] & (idx[None, :] > idx[:, None]), idx[None, :], MOE_BLOCKS)
    nxt = jnp.min(later_first, axis=1)
    next_e = jnp.where(nxt < MOE_BLOCKS, jnp.sum(jnp.where(idx[None, :] == nxt[:, None], blk_e[None, :], 0), axis=1), -1)
    plan = [blk_e, blk_valid, first.astype(jnp.int32), (ordinal % 2).astype(jnp.int32), next_e.astype(jnp.int32)]

    def rows(b, be, bv, *_):
        return (jnp.where(bv[b] > 0, b, MOE_BLOCKS - 1), 0)

    hbm = pl.BlockSpec(memory_space=pl.ANY)
    return pl.pallas_call(
        _expert_kernel,
        grid_spec=pltpu.PrefetchScalarGridSpec(
            num_scalar_prefetch=len(plan),
            grid=(MOE_BLOCKS,),
            in_specs=[pl.BlockSpec((MOE_R * ROW_SUB, LANES), rows), hbm, hbm, hbm],
            out_specs=pl.BlockSpec((MOE_R * ROW_SUB, LANES), lambda b, *_: (b, 0)),
            scratch_shapes=[pltpu.VMEM((2, D_MODEL, D_EXPERT), F32),
                            pltpu.VMEM((2, D_MODEL, D_EXPERT), F32),
                            pltpu.VMEM((2, D_EXPERT, D_MODEL), F32),
                            pltpu.VMEM((D_MODEL, D_EXPERT), BF16),
                            pltpu.VMEM((D_MODEL, D_EXPERT), BF16),
                            pltpu.VMEM((D_EXPERT, D_MODEL), BF16),
                            pltpu.SemaphoreType.DMA((2,))]),
        out_shape=jax.ShapeDtypeStruct((MOE_ROWS * ROW_SUB, LANES), F32),
        compiler_params=pltpu.CompilerParams(dimension_semantics=("arbitrary",), vmem_limit_bytes=VMEM_LIMIT),
        name="moe_experts",
    )(*plan, xs, w_g, w_u, w_d)


def _combine_kernel(dest_ref, o_hbm, x2_ref, ri_ref, gate_ref, gf_ref, yp_ref, ys_ref, obuf, sem):
    i = pl.program_id(0)
    slot = i % 2
    tile_rows = TM * ROW_SUB

    def issue_tile(tile, slot_):
        def issue(t, carry):
            cps = _row_copies(tile, t, dest_ref, lambda k, d: pltpu.make_async_copy(
                _row(o_hbm, d), _row(obuf, (2 * slot_ + k) * TM + t), sem.at[slot_]))
            for k, cp in enumerate(cps):
                cp.start(priority=k)
            return carry

        lax.fori_loop(0, TM, issue, 0, unroll=8)

    @pl.when(i == 0)
    def _():
        issue_tile(0, 0)

    @pl.when(i + 1 < N_TILES)
    def _():
        issue_tile(i + 1, 1 - slot)

    for k in range(2):
        start = pl.multiple_of((2 * slot + k) * tile_rows, tile_rows)
        pltpu.make_async_copy(o_hbm.at[pl.ds(0, tile_rows), :], obuf.at[pl.ds(start, tile_rows), :],
                              sem.at[slot]).wait()

    ri = ri_ref[...]
    g1 = ri[:, 2:3]
    g2 = ri[:, 3:4]
    base = 2 * slot * tile_rows
    moe = jnp.concatenate(
        [g1 * obuf[pl.ds(base + s, TM, stride=ROW_SUB), :] + g2 * obuf[pl.ds(base + tile_rows + s, TM, stride=ROW_SUB), :]
         for s in range(ROW_SUB)], axis=1)
    x3 = x2_ref[...] + _per_chunk(gate_ref) * moe
    y = _rmsnorm(x3, gf_ref[...])

    @pl.when(i < P_TILES)
    def _():
        yp_ref[0] = y

    @pl.when(i >= P_TILES)
    def _():
        ys_ref[...] = y.reshape(CH_PER_TILE, DEC_SEQ, D_MODEL)


def _combine(dest, o_rows, x2, rinfo, modc, g_final):
    xp_spec, xs_spec = _xp_spec(), _xs_spec()
    return pl.pallas_call(
        _combine_kernel,
        grid_spec=pltpu.PrefetchScalarGridSpec(
            num_scalar_prefetch=1,
            grid=(N_TILES,),
            in_specs=[pl.BlockSpec(memory_space=pl.ANY),
                      pl.BlockSpec((TM, D_MODEL), lambda i, *_: (i, 0)),
                      pl.BlockSpec((TM, LANES), lambda i, *_: (i, 0)),
                      _mod_spec(5),
                      pl.BlockSpec((1, D_MODEL), lambda i, *_: (0, 0))],
            out_specs=[pl.BlockSpec(xp_spec.block_shape, lambda i, *_: xp_spec.index_map(i)),
                       pl.BlockSpec(xs_spec.block_shape, lambda i, *_: xs_spec.index_map(i))],
            scratch_shapes=[pltpu.VMEM((4 * TM * ROW_SUB, LANES), F32), pltpu.SemaphoreType.DMA((2,))]),
        out_shape=[jax.ShapeDtypeStruct((BATCH, SEQ, D_MODEL), F32),
                   jax.ShapeDtypeStruct((DEC_BATCH, DEC_SEQ, D_MODEL), F32)],
        compiler_params=pltpu.CompilerParams(dimension_semantics=("arbitrary",), vmem_limit_bytes=VMEM_LIMIT),
        name="moe_combine",
    )(dest, o_rows, x2, rinfo, modc, g_final)


def _moe_plan(rt, cnt):
    experts = np.arange(N_EXPERTS)
    tile_cnt = cnt.reshape(N_TILES, SUBLANES, LANES)[:, 0, :N_EXPERTS]
    counts = jnp.sum(tile_cnt, axis=0)
    padded = jnp.ceil(counts / MOE_R) * MOE_R
    pad_end = jnp.sum(padded[:, None] * (experts[:, None] <= experts[None, :]), axis=0)
    base = pad_end - padded
    tiles = np.arange(N_TILES)
    earlier_tiles = (tiles[None, :, None] < tiles[:, None, None])
    tile_base = base[None, :] + jnp.sum(tile_cnt[None, :, :] * earlier_tiles, axis=1)
    eid = rt[0:2].reshape(1, 2, N_TILES, TM)
    rank = rt[4:6].reshape(2, N_TILES, TM)
    pick = eid == experts.astype(np.float32).reshape(N_EXPERTS, 1, 1, 1)
    dest = jnp.sum(jnp.where(pick, tile_base.T[:, None, :, None], 0.0), axis=0) + rank
    blk_start = (np.arange(MOE_BLOCKS) * MOE_R).astype(np.float32)
    blk_e = jnp.minimum(jnp.sum((blk_start[:, None] >= pad_end[None, :]).astype(F32), axis=1), N_EXPERTS - 1)
    mine = blk_e[:, None] == experts.astype(np.float32)[None, :]
    blk_fill = jnp.sum(jnp.where(mine, (counts + base)[None, :], 0.0), axis=1) - blk_start
    blk_valid = jnp.clip(blk_fill, 0, MOE_R)
    zlo = jnp.concatenate([jnp.floor((base + counts) / ZCHUNK), pad_end[-1:] / ZCHUNK])
    zhi = jnp.concatenate([pad_end / ZCHUNK, jnp.full((1,), MOE_ROWS // ZCHUNK, F32)])
    return (dest.reshape(-1).astype(jnp.int32), blk_e.astype(jnp.int32), blk_valid.astype(jnp.int32),
            zlo.astype(jnp.int32), zhi.astype(jnp.int32))


def kernel(x_prompt, x_sample, c_prompt, c_sample, cache_win_k, cache_win_v, state_conv, state_C, state_n, state_m, w_ada, b_ada, g_norm_mix, g_norm_ffn, w_in, attn_sinks, conv_w, conv_b, b_igate, b_fgate, g_mhnorm, w_out, w_router_group, b_router_group, w_router_expert, b_router_expert, w_exp_gate, w_exp_up, w_exp_down, g_final):
    l = 0
    n_streams = BATCH + DEC_BATCH
    pad_streams = -n_streams % (2 * SUBLANES)
    mod = _modulation(jnp.concatenate([c_prompt, c_sample, jnp.zeros((pad_streams, D_MODEL), F32)], axis=0),
                      w_ada[l], b_ada[l])
    chunk_stream = np.concatenate([np.repeat(np.arange(BATCH), SEQ // CHUNK), BATCH + np.arange(DEC_BATCH)])
    modc = mod[chunk_stream]

    wi = w_in[l]
    s_q, s_k, s_v, s_qkm, s_vm, s_ig, s_fg = 0, 512, 640, 768, 1792, 2304, 2308
    s_om = 2312
    w_in_r = jnp.concatenate(
        [wi[:, s_q:s_qkm], wi[:, s_qkm:s_vm], wi[:, s_vm:s_ig], wi[:, s_om:], wi[:, s_ig:s_om],
         jnp.zeros((D_MODEL, LANES - 2 * M_HEADS), F32)], axis=1).astype(BF16)
    q, kv, qkm, vm, om, gates = _inproj(x_prompt, x_sample, modc, g_norm_mix[l].reshape(1, -1), w_in_r)

    cache_kv = jnp.concatenate([cache_win_k[l].reshape(DEC_BATCH * WINDOW, A_KV_WIDTH),
                                cache_win_v[l].reshape(DEC_BATCH * WINDOW, A_KV_WIDTH)], axis=1)
    att_p, att_s = _attention(attn_sinks[l], q, kv, cache_kv)

    gbias = jnp.concatenate([b_igate[l], b_fgate[l], jnp.zeros((LANES - 2 * M_HEADS,), F32)]).reshape(1, LANES)
    common = (qkm, vm, om, gates, conv_w[l], conv_b[l].reshape(1, -1), gbias, g_mhnorm[l].reshape(1, -1))
    zeros_p = (jnp.zeros((BATCH, SUBLANES, 2 * M_WIDTH), F32),
               jnp.zeros((BATCH, M_HEADS, M_HEAD_DIM, M_HEAD_DIM), F32),
               jnp.zeros((BATCH, M_HEADS, M_HEAD_DIM), F32),
               jnp.zeros((BATCH, 1, LANES), F32))
    LP = 256
    hm_p, C_p, n_p, m_p = _mlstm(*common, *zeros_p, nb=BATCH, L=LP, nc=SEQ // LP, ns=1, row_off=0,
                                 name="mlstm_prompt")
    carry_s = jnp.concatenate([jnp.zeros((DEC_BATCH, SUBLANES - (CONV_WIDTH - 1), 2 * M_WIDTH), F32),
                               state_conv[l]], axis=1)
    m0_s = jnp.pad(state_m[l], ((0, 0), (0, LANES - M_HEADS))).reshape(DEC_BATCH, 1, LANES)
    hm_s, C_s, n_s, m_s = _mlstm(*common, carry_s, state_C[l], state_n[l], m0_s,
                                 nb=DEC_BATCH, L=DEC_SEQ, nc=1, ns=8, row_off=N_P, name="mlstm_sample")

    w_router = jnp.concatenate([w_router_group[l], w_router_expert[l],
                                jnp.zeros((D_MODEL, LANES - N_GROUPS - N_EXPERTS), F32)], axis=1)
    w_router_hi = w_router.astype(BF16)
    w_router = jnp.concatenate([w_router_hi, (w_router - w_router_hi.astype(F32)).astype(BF16)], axis=1)
    b_router =jnp.concatenate([b_router_group[l], b_router_expert[l],
                                jnp.zeros((LANES - N_GROUPS - N_EXPERTS,), F32)]).reshape(1, LANES)
    x2, h2t, rinfo, rt, cnt = _outproj(x_prompt, x_sample, att_p, att_s, hm_p, hm_s, modc,
                                       g_norm_ffn[l].reshape(1, -1), w_out[l].astype(BF16), w_router, b_router)

    dest, blk_e, blk_valid, zlo, zhi = _moe_plan(rt, cnt)
    xs_rows = _dispatch(dest, zlo, zhi, h2t)
    o_rows = _experts(blk_e, blk_valid, xs_rows, w_exp_gate[l], w_exp_up[l], w_exp_down[l])
    y_prompt, y_sample = _combine(dest, o_rows, x2, rinfo, modc, g_final.reshape(1, -1))

    kv_p = kv.reshape(N_TOK // WINDOW, WINDOW, 2 * A_KV_WIDTH)[SEQ // WINDOW - 1:N_P // WINDOW:SEQ // WINDOW]
    win_k_p = kv_p[..., :A_KV_WIDTH].reshape(1, BATCH, WINDOW, A_KV_HEADS, A_HEAD_DIM)
    win_v_p = kv_p[..., A_KV_WIDTH:].reshape(1, BATCH, WINDOW, A_KV_HEADS, A_HEAD_DIM)
    kv_s = kv[N_P:].reshape(DEC_BATCH, DEC_SEQ, 2 * A_KV_WIDTH)
    win_k_s = jnp.concatenate([cache_win_k[l][:, DEC_SEQ:],
                               kv_s[..., :A_KV_WIDTH].reshape(DEC_BATCH, DEC_SEQ, A_KV_HEADS, A_HEAD_DIM)], axis=1)[None]
    win_v_s = jnp.concatenate([cache_win_v[l][:, DEC_SEQ:],
                               kv_s[..., A_KV_WIDTH:].reshape(DEC_BATCH, DEC_SEQ, A_KV_HEADS, A_HEAD_DIM)], axis=1)[None]
    qkm_c = qkm.reshape(N_CHUNKS, CHUNK, 2 * M_WIDTH)
    tail = slice(CHUNK - (CONV_WIDTH - 1), CHUNK)
    conv_p = qkm_c[SEQ // CHUNK - 1:N_P // CHUNK:SEQ // CHUNK, tail][None]
    conv_s = qkm_c[N_P // CHUNK:, tail][None]
    return (y_prompt, y_sample,
            win_k_p, win_v_p, conv_p, C_p[None], n_p[None], m_p[:, 0, :M_HEADS][None],
            win_k_s, win_v_s, conv_s, C_s[None], n_s[None], m_s[:, 0, :M_HEADS][None])
```

```python
import functools

import numpy as np
import jax
import jax.numpy as jnp
from jax import lax
from jax.experimental import pallas as pl
from jax.experimental.pallas import tpu as pltpu

F32 = jnp.float32
BF16 = jnp.bfloat16
HIGHEST = lax.Precision.HIGHEST

LANES = 128
SUBLANES = 8

D_MODEL = 1024
BATCH = 8
SEQ = 2048
DEC_BATCH = 32
DEC_SEQ = 64
PAST_LEN = 4096
CHUNK = 64
A_WIDTH = 512
A_HEAD_DIM = 64
A_HEADS = 8
A_KV_HEADS = 2
A_GROUP = 4
A_KV_WIDTH = 128
WINDOW = 128
ROT_DIM = 16
ROPE_THETA = 500000.0
M_WIDTH = 512
M_HEADS = 4
M_HEAD_DIM = 128
CONV_WIDTH = 4
N_GROUPS = 4
EXPERTS_PER_GROUP = 8
N_EXPERTS = 32
D_EXPERT = 512
EPS = 1e-6

N_P = BATCH * SEQ
N_S = DEC_BATCH * DEC_SEQ
N_TOK = N_P + N_S
N_CHUNKS = N_TOK // CHUNK
TM = 512
N_TILES = N_TOK // TM
P_TILES = N_P // TM
CH_PER_TILE = TM // CHUNK
ROW_SUB = D_MODEL // LANES
N_ASSIGN = 2 * N_TOK
MOE_R = 512
MOE_BLOCKS = N_ASSIGN // MOE_R + N_EXPERTS
MOE_ROWS = MOE_BLOCKS * MOE_R
C_Q, C_K, C_V, C_QKM, C_VM, C_OM, C_G = 0, 512, 640, 768, 1792, 2304, 2816
D_IN_PAD = 2944
VMEM_LIMIT = 48 * 1024 * 1024


def _sigmoid(x):
    return 1.0 / (1.0 + jnp.exp(-x))


def _mod_kernel(c_ref, w_ref, b_ref, o_ref):
    c = c_ref[...]
    s = c * _sigmoid(c)
    w = w_ref[...]
    s_hi, w_hi = s.astype(BF16), w.astype(BF16)
    s_lo, w_lo = (s - s_hi.astype(F32)).astype(BF16), (w - w_hi.astype(F32)).astype(BF16)
    o_ref[...] = (jnp.dot(s_hi, w_hi, preferred_element_type=F32) + jnp.dot(s_lo, w_hi, preferred_element_type=F32)
                  + jnp.dot(s_hi, w_lo, preferred_element_type=F32)) + b_ref[...]


def _modulation(c_all, w_ada, b_ada):
    n = c_all.shape[0]
    bn = 512
    return pl.pallas_call(
        _mod_kernel,
        grid=(6 * D_MODEL // bn,),
        in_specs=[pl.BlockSpec((n, D_MODEL), lambda j: (0, 0)),
                  pl.BlockSpec((D_MODEL, bn), lambda j: (0, j)),
                  pl.BlockSpec((1, bn), lambda j: (0, j))],
        out_specs=pl.BlockSpec((n, bn), lambda j: (0, j)),
        out_shape=jax.ShapeDtypeStruct((n, 6 * D_MODEL), F32),
        name="modulation",
    )(c_all, w_ada, b_ada.reshape(1, -1))


def _xp_spec():
    def idx(i):
        t = jnp.minimum(i, P_TILES - 1)
        return (t // (SEQ // TM), t % (SEQ // TM), 0)
    return pl.BlockSpec((1, TM, D_MODEL), idx)


def _xs_spec():
    return pl.BlockSpec((CH_PER_TILE, DEC_SEQ, D_MODEL), lambda i: (jnp.maximum(i - P_TILES, 0), 0, 0))


def _mod_spec(comp):
    return pl.BlockSpec((CH_PER_TILE, D_MODEL), lambda i, *_: (i, comp))


def _load_x(xp_ref, xs_ref):
    return jnp.where(pl.program_id(0) < P_TILES, xp_ref[0], xs_ref[...].reshape(TM, D_MODEL))


def _per_chunk(m_ref):
    m = m_ref[...]
    return jnp.concatenate([jnp.broadcast_to(m[c:c + 1], (CHUNK, D_MODEL)) for c in range(CH_PER_TILE)], axis=0)


def _rmsnorm(x, g):
    return x * lax.rsqrt(jnp.mean(x * x, axis=-1, keepdims=True) + EPS) * g


def _rope(x, cos, sa, sb):
    n = x.shape[1]
    rep = n // LANES
    if rep > 1:
        cos = jnp.concatenate([cos] * rep, axis=1)
        sa = jnp.concatenate([sa] * rep, axis=1)
        sb = jnp.concatenate([sb] * rep, axis=1)
    return x * cos + pltpu.roll(x, n - ROT_DIM // 2, 1) * sa + pltpu.roll(x, ROT_DIM // 2, 1) * sb


def _inproj_kernel(xp_ref, xs_ref, sh_ref, sc_ref, g_ref, w_ref, cos_ref, sa_ref, sb_ref,
                   q_ref, kv_ref, qkm_ref, vm_ref, om_ref, gt_ref):
    x = _load_x(xp_ref, xs_ref)
    h = _rmsnorm(x, g_ref[...]) * (1.0 + _per_chunk(sc_ref)) + _per_chunk(sh_ref)
    half = TM // 2
    for r in range(2):
        rows = slice(r * half, (r + 1) * half)
        hb = h[rows].astype(BF16)

        def proj(a, b, hb=hb):
            return jnp.dot(hb, w_ref[:, a:b], preferred_element_type=F32)

        cos, sa, sb = cos_ref[rows, :], sa_ref[rows, :], sb_ref[rows, :]
        q_ref[rows, :] = _rope(proj(C_Q, C_K), cos, sa, sb).astype(BF16)
        kv_ref[rows, :A_KV_WIDTH] = _rope(proj(C_K, C_V), cos, sa, sb)
        kv_ref[rows, A_KV_WIDTH:] = proj(C_V, C_QKM)
        qkm_ref[rows, :] = proj(C_QKM, C_VM)
        vm_ref[rows, :] = proj(C_VM, C_OM).astype(BF16)
        om_ref[rows, :] = proj(C_OM, C_G)
        gt_ref[rows, :] = proj(C_G, D_IN_PAD)


def _rope_tables():
    pos = np.concatenate([np.arange(SEQ), np.tile(PAST_LEN + np.arange(DEC_SEQ), CH_PER_TILE)]).astype(np.float64)
    inv_freq = ROPE_THETA ** (-np.arange(0, ROT_DIM, 2, dtype=np.float64) / ROT_DIM)
    lane = np.arange(LANES)
    hl = lane % A_HEAD_DIM
    ang = pos[:, None] * inv_freq[hl % (ROT_DIM // 2)][None, :]
    rot = (hl < ROT_DIM)[None, :]
    lo = (hl < ROT_DIM // 2)[None, :]
    cos = np.where(rot, np.cos(ang), 1.0)
    sa = np.where(lo, -np.sin(ang), 0.0)
    sb = np.where(rot & ~lo, np.sin(ang), 0.0)
    return [jnp.asarray(t, F32) for t in (cos, sa, sb)]


def _inproj(x_prompt, x_sample, modc, g_mix, w_in_r):
    cos, sa, sb = _rope_tables()
    tab_spec = pl.BlockSpec((TM, LANES), lambda i: (jnp.where(i < P_TILES, i % (SEQ // TM), SEQ // TM), 0))

    def out(n, dtype=F32):
        return pl.BlockSpec((TM, n), lambda i: (i, 0)), jax.ShapeDtypeStruct((N_TOK, n), dtype)

    outs = [out(A_WIDTH, BF16), out(2 * A_KV_WIDTH), out(2 * M_WIDTH), out(M_WIDTH, BF16), out(M_WIDTH), out(LANES)]
    return pl.pallas_call(
        _inproj_kernel,
        grid=(N_TILES,),
        in_specs=[_xp_spec(), _xs_spec(), _mod_spec(0), _mod_spec(1),
                  pl.BlockSpec((1, D_MODEL), lambda i: (0, 0)),
                  pl.BlockSpec((D_MODEL, D_IN_PAD), lambda i: (0, 0)),
                  tab_spec, tab_spec, tab_spec],
        out_specs=[o[0] for o in outs],
        out_shape=[o[1] for o in outs],
        compiler_params=pltpu.CompilerParams(vmem_limit_bytes=VMEM_LIMIT),
        name="inproj",
    )(x_prompt, x_sample, modc, modc, g_mix, w_in_r, cos, sa, sb)


ATT_NS = 1


def _attn_chunk_kernel(sink_ref, q_ref, prev_ref, cur_ref, o_ref):
    for st in range(ATT_NS):
        rows = slice(st * CHUNK, (st + 1) * CHUNK)
        _attn_chunk(sink_ref, q_ref[rows, :], prev_ref[st * WINDOW:(st + 1) * WINDOW, :], cur_ref[rows, :],
                    o_ref, rows)


def _attn_chunk(sink_ref, q, prev, cur, o_ref, rows):
    q = q * (A_HEAD_DIM ** -0.5)
    kv = jnp.concatenate([prev, cur], axis=0)
    for g in range(A_KV_HEADS):
        kg = kv[:, g * A_HEAD_DIM:(g + 1) * A_HEAD_DIM].astype(BF16)
        vg = kv[:, A_KV_WIDTH + g * A_HEAD_DIM:A_KV_WIDTH + (g + 1) * A_HEAD_DIM].astype(BF16)
        heads = [g * A_GROUP + i for i in range(A_GROUP)]
        qc = jnp.concatenate([q[:, h * A_HEAD_DIM:(h + 1) * A_HEAD_DIM] for h in heads], axis=0).astype(BF16)
        snk = jnp.concatenate([jnp.full((CHUNK, 1), sink_ref[h], F32) for h in heads], axis=0)
        s = lax.dot_general(qc, kg, (((1,), (1,)), ((), ())), preferred_element_type=F32)
        mx = jnp.maximum(jnp.max(s, axis=-1, keepdims=True), snk)
        p = jnp.exp(s - mx)
        den = jnp.sum(p, axis=-1, keepdims=True) + jnp.exp(snk - mx)
        o = jnp.dot(p.astype(BF16), vg, preferred_element_type=F32) / den
        for i, h in enumerate(heads):
            o_ref[rows, h * A_HEAD_DIM:(h + 1) * A_HEAD_DIM] = o[i * CHUNK:(i + 1) * CHUNK]


ATT_TQ = 2048
ATT_QB = 2 * CHUNK


def _attn_band_kernel(sink_ref, q_ref, prev_ref, cur_ref, o_ref, att_t):
    nk = ATT_QB + WINDOW
    q = (q_ref[...] * (A_HEAD_DIM ** -0.5)).astype(BF16)
    kv = jnp.concatenate([prev_ref[...], cur_ref[...]], axis=0)
    k2 = kv[:, :A_KV_WIDTH]
    k2r = pltpu.roll(k2, A_HEAD_DIM, 1)
    low = lax.broadcasted_iota(jnp.int32, k2.shape, 1) < A_HEAD_DIM
    k_placed = {(0, 0): jnp.where(low, k2, 0.0), (0, 1): jnp.where(low, 0.0, k2r),
                (1, 0): jnp.where(low, k2r, 0.0), (1, 1): jnp.where(low, 0.0, k2)}
    k_placed = {key: val.astype(BF16) for key, val in k_placed.items()}
    v_t = kv[:, A_KV_WIDTH:].T.astype(BF16)
    key_chunk = lax.broadcasted_iota(jnp.int32, (nk, ATT_QB), 0) // CHUNK
    q_chunk = lax.broadcasted_iota(jnp.int32, (nk, ATT_QB), 1) // CHUNK
    band = (key_chunk >= q_chunk) & (key_chunk <= q_chunk + WINDOW // CHUNK)
    has_history = pl.program_id(1) > 0
    for blk in range(ATT_TQ // ATT_QB):
        keys = slice(blk * ATT_QB, blk * ATT_QB + nk)
        qrows = slice(blk * ATT_QB, (blk + 1) * ATT_QB)
        valid = band & ((key_chunk >= WINDOW // CHUNK) | has_history) if blk == 0 else band
        for h in range(A_HEADS):
            g = h // A_GROUP
            s_t = lax.dot_general(k_placed[(g, h % 2)][keys], q[qrows, (h // 2) * LANES:(h // 2 + 1) * LANES],
                                  (((1,), (1,)), ((), ())), preferred_element_type=F32)
            s_t = jnp.where(valid, s_t, -jnp.inf)
            snk = sink_ref[h]
            mx = jnp.maximum(jnp.max(s_t, axis=0, keepdims=True), snk)
            p_t = jnp.exp(s_t - mx)
            den = jnp.sum(p_t, axis=0, keepdims=True) + jnp.exp(snk - mx)
            o_t = jnp.dot(v_t[:, keys], p_t.astype(BF16), preferred_element_type=F32)
            att_t[h * A_HEAD_DIM:(h + 1) * A_HEAD_DIM, qrows] = o_t[g * A_HEAD_DIM:(g + 1) * A_HEAD_DIM] / den
    o_ref[...] = att_t[...].T.astype(BF16)


def _attention(sinks, q, kv, cache_kv):
    smem = pl.BlockSpec(memory_space=pltpu.SMEM)
    tq = ATT_TQ
    nq = SEQ // tq
    att = pl.pallas_call(
        _attn_band_kernel,
        grid=(BATCH, nq),
        scratch_shapes=[pltpu.VMEM((A_WIDTH, ATT_TQ), F32)],
        in_specs=[smem,
                  pl.BlockSpec((tq, A_WIDTH), lambda b, j: (b * nq + j, 0)),
                  pl.BlockSpec((WINDOW, 2 * A_KV_WIDTH),
                               lambda b, j: (jnp.maximum((b * nq + j) * (tq // WINDOW) - 1, 0), 0)),
                  pl.BlockSpec((tq, 2 * A_KV_WIDTH), lambda b, j: (b * nq + j, 0))],
        out_specs=pl.BlockSpec((tq, A_WIDTH), lambda b, j: (b * nq + j, 0)),
        out_shape=jax.ShapeDtypeStruct((N_P, A_WIDTH), BF16),
        name="attn_prompt",
    )(sinks, q, kv, kv)
    rows = ATT_NS * DEC_SEQ
    off = N_P // rows
    att_s = pl.pallas_call(
        _attn_chunk_kernel,
        grid=(DEC_BATCH // ATT_NS,),
        in_specs=[smem,
                  pl.BlockSpec((rows, A_WIDTH), lambda b: (off + b, 0)),
                  pl.BlockSpec((ATT_NS * WINDOW, 2 * A_KV_WIDTH), lambda b: (b, 0)),
                  pl.BlockSpec((rows, 2 * A_KV_WIDTH), lambda b: (off + b, 0))],
        out_specs=pl.BlockSpec((rows, A_WIDTH), lambda b: (b, 0)),
        out_shape=jax.ShapeDtypeStruct((N_S, A_WIDTH), F32),
        name="attn_sample",
    )(sinks, q, cache_kv, kv)
    return att, att_s


def _conv4(x, w, b):
    x1 = pltpu.roll(x, 1, 0)
    near = b + x * w[3:4] + x1 * w[2:3]
    far = x * w[1:2] + x1 * w[0:1]
    return near + pltpu.roll(far, 2, 0)


def _mlstm_kernel(qkm_ref, vm_ref, om_ref, gt_ref, cw_ref, cb_ref, gb_ref, gmh_ref,
                  carry0_ref, c0_ref, n0_ref, m0_ref,
                  hm_ref, cout_ref, nout_ref, mout_ref, c_s, n_s, m_s, carry_s, *, L, NS):
    c = pl.program_id(1)

    @pl.when(c == 0)
    def _():
        c_s[...] = c0_ref[...]
        n_s[...] = n0_ref[...]
        m_s[...] = m0_ref[...]
        carry_s[...] = carry0_ref[...]

    w = cw_ref[...]
    b = cb_ref[...]
    gmh = gmh_ref[...]
    lane = lax.broadcasted_iota(jnp.int32, (L, LANES), 1)
    causal = lax.broadcasted_iota(jnp.int32, (L, L), 0) >= lax.broadcasted_iota(jnp.int32, (L, L), 1)
    ones_causal = causal.astype(F32)
    sel = (lax.broadcasted_iota(jnp.int32, (SUBLANES, LANES), 0)
           == lax.broadcasted_iota(jnp.int32, (SUBLANES, LANES), 1)).astype(F32)
    lane1 = lax.broadcasted_iota(jnp.int32, (1, LANES), 1)
    for st in range(NS):
        _mlstm_stream(st, qkm_ref, vm_ref, om_ref, gt_ref, gb_ref, hm_ref, c_s, n_s, m_s, carry_s,
                      w, b, gmh, lane, causal, ones_causal, sel, lane1, L)

    @pl.when(c == pl.num_programs(1) - 1)
    def _():
        cout_ref[...] = c_s[...]
        nout_ref[...] = n_s[...]
        mout_ref[...] = m_s[...]


def _mlstm_stream(st, qkm_ref, vm_ref, om_ref, gt_ref, gb_ref, hm_ref, c_s, n_s, m_s, carry_s,
                  w, b, gmh, lane, causal, ones_causal, sel, lane1, L):
    x = qkm_ref[st]
    y = _conv4(x, w, b)
    y8 = _conv4(jnp.concatenate([carry_s[st], x[:SUBLANES]], axis=0), w, b)
    y = jnp.concatenate([y8[SUBLANES:], y[SUBLANES:]], axis=0)
    carry_s[st] = x[L - SUBLANES:]
    a = y * _sigmoid(y)
    qa = a[:, :M_WIDTH] * (M_HEAD_DIM ** -0.5)
    ka = a[:, M_WIDTH:]
    v = vm_ref[st]
    om = om_ref[st]

    z = gt_ref[st] + gb_ref[...]
    f_log = jnp.minimum(z, 0.0) - jnp.log1p(jnp.exp(-jnp.abs(z)))
    val = jnp.where(lane < M_HEADS, z, f_log)
    cum = jnp.dot(ones_causal, val, preferred_element_type=F32, precision=HIGHEST)
    G = jnp.where(lane < M_HEADS, val, cum)
    GT = lax.dot_general(sel, G, (((1,), (1,)), ((), ())), preferred_element_type=F32, precision=HIGHEST)

    m_row = m_s[st]
    outs = []
    for h in range(M_HEADS):
        a_col = G[:, M_HEADS + h:M_HEADS + h + 1]
        i_col = G[:, h:h + 1]
        i_row = GT[h:h + 1, :]
        b_row = GT[M_HEADS + h:M_HEADS + h + 1, :]
        m_prev = m_row[:, h:h + 1]
        dm = jnp.where(causal, a_col - b_row + i_row, -jnp.inf)
        inter = a_col + m_prev
        m_t = jnp.maximum(inter, jnp.max(dm, axis=-1, keepdims=True))
        w_intra = jnp.exp(dm - m_t)
        w_inter = jnp.exp(inter - m_t)
        sl = slice(h * M_HEAD_DIM, (h + 1) * M_HEAD_DIM)
        qh, kh, vh = qa[:, sl], ka[:, sl], v[:, sl]
        qb = qh.astype(BF16)
        scores = lax.dot_general(qb, kh.astype(BF16), (((1,), (1,)), ((), ())), preferred_element_type=F32) * w_intra
        ch = c_s[st, h]
        nh = n_s[st, h:h + 1, :]
        num = (jnp.dot(scores.astype(BF16), vh.astype(BF16), preferred_element_type=F32)
               + w_inter * lax.dot_general(qb, ch.astype(BF16), (((1,), (1,)), ((), ())), preferred_element_type=F32))
        den = jnp.sum(scores, axis=-1, keepdims=True) + w_inter * jnp.sum(qh * nh, axis=-1, keepdims=True)
        hv = num / jnp.maximum(jnp.abs(den), jnp.exp(-m_t))
        hn = hv * lax.rsqrt(jnp.mean(hv * hv, axis=-1, keepdims=True) + EPS)
        outs.append(hn * gmh[:, sl] * _sigmoid(om[:, sl]))
        m_new = m_t[L - 1:L, :]
        a_last = a_col[L - 1:L, :]
        w_end = jnp.exp(a_last - a_col + i_col - m_new)
        decay = jnp.exp(a_last + m_prev - m_new)
        kw = kh * w_end
        c_s[st, h] = decay * ch + lax.dot_general(vh.astype(BF16), kw.astype(BF16), (((0,), (0,)), ((), ())),
                                                  preferred_element_type=F32)
        n_s[st, h:h + 1, :] = decay * nh + jnp.sum(kw, axis=0, keepdims=True)
        m_row = jnp.where(lane1 == h, m_new, m_row)
    m_s[st] = m_row
    hm_ref[st] = jnp.concatenate(outs, axis=1).astype(BF16)


def _mlstm(qkm, vm, om, gates, conv_w, conv_b, gbias, g_mh, carry0, c0, n0, m0, *, nb, L, nc, ns, row_off, name):
    S = nc * L
    first = row_off // S // ns

    def tok(arr):
        n = arr.shape[-1]
        return arr.reshape(N_TOK // S, S, n), pl.BlockSpec((ns, L, n), lambda i, c: (first + i, c, 0))

    def const(shape):
        return pl.BlockSpec(shape, lambda i, c: (0,) * len(shape))

    def per_stream(shape):
        return pl.BlockSpec((ns,) + shape, lambda i, c: (i,) + (0,) * len(shape))

    toks = [tok(a) for a in (qkm, vm, om, gates)]
    state_specs = [per_stream((SUBLANES, 2 * M_WIDTH)), per_stream((M_HEADS, M_HEAD_DIM, M_HEAD_DIM)),
                   per_stream((M_HEADS, M_HEAD_DIM)), per_stream((1, LANES))]
    in_specs = ([t[1] for t in toks]
                + [const((CONV_WIDTH, 2 * M_WIDTH)), const((1, 2 * M_WIDTH)), const((1, LANES)), const((1, M_WIDTH))]
                + state_specs)
    args = [t[0] for t in toks] + [conv_w, conv_b, gbias, g_mh, carry0, c0, n0, m0]
    hm, c_out, n_out, m_out = pl.pallas_call(
        functools.partial(_mlstm_kernel, L=L, NS=ns),
        grid=(nb // ns, nc),
        in_specs=in_specs,
        out_specs=[pl.BlockSpec((ns, L, M_WIDTH), lambda i, c: (i, c, 0))] + state_specs[1:],
        out_shape=[jax.ShapeDtypeStruct((nb, S, M_WIDTH), BF16),
                   jax.ShapeDtypeStruct((nb, M_HEADS, M_HEAD_DIM, M_HEAD_DIM), F32),
                   jax.ShapeDtypeStruct((nb, M_HEADS, M_HEAD_DIM), F32),
                   jax.ShapeDtypeStruct((nb, 1, LANES), F32)],
        scratch_shapes=[pltpu.VMEM((ns, M_HEADS, M_HEAD_DIM, M_HEAD_DIM), F32),
                        pltpu.VMEM((ns, M_HEADS, M_HEAD_DIM), F32),
                        pltpu.VMEM((ns, 1, LANES), F32),
                        pltpu.VMEM((ns, SUBLANES, 2 * M_WIDTH), F32)],
        compiler_params=pltpu.CompilerParams(dimension_semantics=("arbitrary", "arbitrary"),
                                             vmem_limit_bytes=VMEM_LIMIT),
        name=name,
    )(*args)
    return hm.reshape(nb * S, M_WIDTH), c_out, n_out, m_out


def _outproj_kernel(xp_ref, xs_ref, attp_ref, atts_ref, hmp_ref, hms_ref, gate_ref, sh_ref, sc_ref, g_ref,
                    w_ref, wr_ref, br_ref, x2_ref, h2_ref, ri_ref, rt_ref, cnt_ref):
    x = _load_x(xp_ref, xs_ref)
    is_prompt = pl.program_id(0) < P_TILES
    att = jnp.where(is_prompt, attp_ref[...], atts_ref[...].astype(BF16))
    hm = jnp.where(is_prompt, hmp_ref[...], hms_ref[...])
    mixed = (jnp.dot(att, w_ref[:A_WIDTH, :], preferred_element_type=F32)
             + jnp.dot(hm, w_ref[A_WIDTH:, :], preferred_element_type=F32))
    x2 = x + _per_chunk(gate_ref) * mixed
    x2_ref[...] = x2
    h2 = _rmsnorm(x2, g_ref[...]) * (1.0 + _per_chunk(sc_ref)) + _per_chunk(sh_ref)
    for s in range(ROW_SUB):
        h2_ref[pl.ds(s, TM, stride=ROW_SUB), :] = h2[:, s * LANES:(s + 1) * LANES]

    h_hi = h2.astype(BF16)
    h_lo = (h2 - h_hi.astype(F32)).astype(BF16)
    hi_w = jnp.dot(h_hi, wr_ref[...], preferred_element_type=F32)
    lg = (hi_w[:, :LANES] + hi_w[:, LANES:]
          + jnp.dot(h_lo, wr_ref[:, :LANES], preferred_element_type=F32)) + br_ref[...]
    lane = lax.broadcasted_iota(jnp.int32, (TM, LANES), 1)
    lanef = lane.astype(F32)
    ninf = -jnp.inf

    def first_argmax(vals):
        mx = jnp.max(vals, axis=-1, keepdims=True)
        return mx, jnp.min(jnp.where(vals == mx, lanef, float(LANES)), axis=-1, keepdims=True)

    is_grp = lane < N_GROUPS
    gmax, grp = first_argmax(jnp.where(is_grp, lg, ninf))
    p_grp = 1.0 / jnp.sum(jnp.where(is_grp, jnp.exp(lg - gmax), 0.0), axis=-1, keepdims=True)
    base = N_GROUPS + grp * EXPERTS_PER_GROUP
    in_grp = (lanef >= base) & (lanef < base + EXPERTS_PER_GROUP)
    el = jnp.where(in_grp, lg, ninf)
    v1, i1 = first_argmax(el)
    v2, i2 = first_argmax(jnp.where(lanef == i1, ninf, el))
    e = jnp.exp(v2 - v1)
    g1 = p_grp / (1.0 + e)
    g2 = p_grp * e / (1.0 + e)
    e1 = i1 - N_GROUPS
    e2 = i2 - N_GROUPS
    oh1 = lanef == e1
    oh2 = lanef == e2
    oh = jnp.where(oh1 | oh2, 1.0, 0.0)
    earlier = (lax.broadcasted_iota(jnp.int32, (TM, TM), 0) > lax.broadcasted_iota(jnp.int32, (TM, TM), 1))
    before = jnp.dot(earlier.astype(BF16), oh.astype(BF16), preferred_element_type=F32)
    r1 = jnp.sum(jnp.where(oh1, before, 0.0), axis=-1, keepdims=True)
    r2 = jnp.sum(jnp.where(oh2, before, 0.0), axis=-1, keepdims=True)
    cnt_ref[...] = jnp.broadcast_to(jnp.sum(oh, axis=0, keepdims=True), (SUBLANES, LANES))
    cols = (e1, e2, g1, g2, r1, r2)
    ri = jnp.zeros((TM, LANES), F32)
    for k, cval in enumerate(cols):
        ri = jnp.where(lane == k, cval, ri)
    ri_ref[...] = ri
    rt_ref[...] = ri.T[:SUBLANES]


def _outproj(x_prompt, x_sample, att_p, att_s, hm_p, hm_s, modc, g_ffn, w_out_b, w_router, b_router):
    def tok(n):
        return pl.BlockSpec((TM, n), lambda i: (i, 0))

    def tok_p(n):
        return pl.BlockSpec((TM, n), lambda i: (jnp.minimum(i, P_TILES - 1), 0))

    def tok_s(n):
        return pl.BlockSpec((TM, n), lambda i: (jnp.maximum(i - P_TILES, 0), 0))

    def const(shape):
        return pl.BlockSpec(shape, lambda i: (0,) * len(shape))

    return pl.pallas_call(
        _outproj_kernel,
        grid=(N_TILES,),
        in_specs=[_xp_spec(), _xs_spec(), tok_p(A_WIDTH), tok_s(A_WIDTH), tok_p(M_WIDTH), tok_s(M_WIDTH),
                  _mod_spec(2), _mod_spec(3), _mod_spec(4),
                  const((1, D_MODEL)), const((D_MODEL, D_MODEL)), const((D_MODEL, 2 * LANES)), const((1, LANES))],
        out_specs=[tok(D_MODEL), pl.BlockSpec((TM * ROW_SUB, LANES), lambda i: (i, 0)), tok(LANES),
                   pl.BlockSpec((SUBLANES, TM), lambda i: (0, i)),
                   pl.BlockSpec((SUBLANES, LANES), lambda i: (i, 0))],
        out_shape=[jax.ShapeDtypeStruct((N_TOK, D_MODEL), F32),
                   jax.ShapeDtypeStruct((N_TOK * ROW_SUB, LANES), F32),
                   jax.ShapeDtypeStruct((N_TOK, LANES), F32),
                   jax.ShapeDtypeStruct((SUBLANES, N_TOK), F32),
                   jax.ShapeDtypeStruct((N_TILES * SUBLANES, LANES), F32)],
        compiler_params=pltpu.CompilerParams(vmem_limit_bytes=VMEM_LIMIT),
        name="outproj",
    )(x_prompt, x_sample, att_p, att_s, hm_p, hm_s, modc, modc, modc, g_ffn, w_out_b, w_router, b_router)


def _row(ref, r):
    return ref.at[pl.ds(pl.multiple_of(r * ROW_SUB, ROW_SUB), ROW_SUB), :]


def _row_copies(i, t, dest_ref, make):
    tok = i * TM + t
    return [make(k, dest_ref[k * N_TOK + tok]) for k in range(2)]


ZCHUNK = 64


def _dispatch_kernel(dest_ref, zlo_ref, zhi_ref, h_ref, xs_out, zbuf, sem, zsem):
    i = pl.program_id(0)
    chunk_rows = ZCHUNK * ROW_SUB

    @pl.when(i == 0)
    def _():
        zbuf[...] = jnp.zeros_like(zbuf)

        def chunk_copy(c):
            return pltpu.make_async_copy(zbuf, xs_out.at[pl.ds(pl.multiple_of(c * chunk_rows, chunk_rows), chunk_rows), :],
                                         zsem)

        def start_range(rng, carry):
            def start(c, cc):
                chunk_copy(c).start()
                return cc

            lax.fori_loop(zlo_ref[rng], zhi_ref[rng], start, 0)
            return carry

        def wait_range(rng, carry):
            def wait(c, cc):
                chunk_copy(c).wait()
                return cc

            lax.fori_loop(zlo_ref[rng], zhi_ref[rng], wait, 0)
            return carry

        lax.fori_loop(0, N_EXPERTS + 1, start_range, 0)
        lax.fori_loop(0, N_EXPERTS + 1, wait_range, 0)

    def copies(t):
        return _row_copies(i, t, dest_ref, lambda k, d: pltpu.make_async_copy(_row(h_ref, t), _row(xs_out, d), sem))

    def issue(t, carry):
        for k, cp in enumerate(copies(t)):
            cp.start(priority=k)
        return carry

    def wait(t, carry):
        for cp in copies(t):
            cp.wait()
        return carry

    lax.fori_loop(0, TM, issue, 0, unroll=8)
    lax.fori_loop(0, TM, wait, 0, unroll=8)


def _dispatch(dest, zlo, zhi, h2t):
    return pl.pallas_call(
        _dispatch_kernel,
        grid_spec=pltpu.PrefetchScalarGridSpec(
            num_scalar_prefetch=3,
            grid=(N_TILES,),
            in_specs=[pl.BlockSpec((TM * ROW_SUB, LANES), lambda i, *_: (i, 0))],
            out_specs=pl.BlockSpec(memory_space=pl.ANY),
            scratch_shapes=[pltpu.VMEM((ZCHUNK * ROW_SUB, LANES), F32),
                            pltpu.SemaphoreType.DMA, pltpu.SemaphoreType.DMA]),
        out_shape=jax.ShapeDtypeStruct((MOE_ROWS * ROW_SUB, LANES), F32),
        compiler_params=pltpu.CompilerParams(dimension_semantics=("arbitrary",)),
        name="moe_dispatch",
    )(dest, zlo, zhi, h2t)


def _expert_kernel(be_ref, bv_ref, first_ref, slot_ref, next_ref, xs_ref, wg_hbm, wu_hbm, wd_hbm, o_ref,
                   wg_f, wu_f, wd_f, wg_s, wu_s, wd_s, wsem):
    b = pl.program_id(0)

    def weight_copies(e, slot):
        return [pltpu.make_async_copy(src.at[e], dst.at[slot], wsem.at[slot])
                for src, dst in ((wg_hbm, wg_f), (wu_hbm, wu_f), (wd_hbm, wd_f))]

    @pl.when(b == 0)
    def _():
        for cp in weight_copies(be_ref[0], 0):
            cp.start()

    @pl.when(first_ref[b] == 1)
    def _():
        slot = slot_ref[b]
        for cp in weight_copies(be_ref[b], slot):
            cp.wait()
        wg_s[...] = wg_f[slot].astype(BF16)
        wu_s[...] = wu_f[slot].astype(BF16)
        wd_s[...] = wd_f[slot].astype(BF16)

        @pl.when(next_ref[b] >= 0)
        def _():
            for cp in weight_copies(next_ref[b], 1 - slot):
                cp.start()

    @pl.when(bv_ref[b] > 0)
    def _():
        x = jnp.concatenate([xs_ref[pl.ds(s, MOE_R, stride=ROW_SUB), :] for s in range(ROW_SUB)], axis=1)
        xb = x.astype(BF16)
        g = jnp.dot(xb, wg_s[...], preferred_element_type=F32)
        u = jnp.dot(xb, wu_s[...], preferred_element_type=F32)
        a = (g * _sigmoid(g) * u).astype(BF16)
        o = jnp.dot(a, wd_s[...], preferred_element_type=F32)
        for s in range(ROW_SUB):
            o_ref[pl.ds(s, MOE_R, stride=ROW_SUB), :] = o[:, s * LANES:(s + 1) * LANES]

    @pl.when(bv_ref[b] == 0)
    def _():
        o_ref[...] = jnp.zeros_like(o_ref)


def _experts(blk_e, blk_valid, xs, w_g, w_u, w_d):
    idx = np.arange(MOE_BLOCKS)
    used = blk_valid > 0
    first = used & ((idx == 0) | (blk_e != jnp.roll(blk_e, 1)))
    ordinal = jnp.sum(first[None, :] & (idx[None, :] <= idx[:, None]), axis=1) - 1
    later_first = jnp.where(first[None, :] & (idx[None, :] > idx[:, None]), idx[None, :], MOE_BLOCKS)
    nxt = jnp.min(later_first, axis=1)
    next_e = jnp.where(nxt < MOE_BLOCKS, jnp.sum(jnp.where(idx[None, :] == nxt[:, None], blk_e[None, :], 0), axis=1), -1)
    plan = [blk_e, blk_valid, first.astype(jnp.int32), (ordinal % 2).astype(jnp.int32), next_e.astype(jnp.int32)]

    def rows(b, be, bv, *_):
        return (jnp.where(bv[b] > 0, b, MOE_BLOCKS - 1), 0)

    hbm = pl.BlockSpec(memory_space=pl.ANY)
    return pl.pallas_call(
        _expert_kernel,
        grid_spec=pltpu.PrefetchScalarGridSpec(
            num_scalar_prefetch=len(plan),
            grid=(MOE_BLOCKS,),
            in_specs=[pl.BlockSpec((MOE_R * ROW_SUB, LANES), rows), hbm, hbm, hbm],
            out_specs=pl.BlockSpec((MOE_R * ROW_SUB, LANES), lambda b, *_: (b, 0)),
            scratch_shapes=[pltpu.VMEM((2, D_MODEL, D_EXPERT), F32),
                            pltpu.VMEM((2, D_MODEL, D_EXPERT), F32),
                            pltpu.VMEM((2, D_EXPERT, D_MODEL), F32),
                            pltpu.VMEM((D_MODEL, D_EXPERT), BF16),
                            pltpu.VMEM((D_MODEL, D_EXPERT), BF16),
                            pltpu.VMEM((D_EXPERT, D_MODEL), BF16),
                            pltpu.SemaphoreType.DMA((2,))]),
        out_shape=jax.ShapeDtypeStruct((MOE_ROWS * ROW_SUB, LANES), F32),
        compiler_params=pltpu.CompilerParams(dimension_semantics=("arbitrary",), vmem_limit_bytes=VMEM_LIMIT),
        name="moe_experts",
    )(*plan, xs, w_g, w_u, w_d)


def _combine_kernel(dest_ref, o_hbm, x2_ref, ri_ref, gate_ref, gf_ref, yp_ref, ys_ref, obuf, sem):
    i = pl.program_id(0)
    slot = i % 2
    tile_rows = TM * ROW_SUB

    def issue_tile(tile, slot_):
        def issue(t, carry):
            cps = _row_copies(tile, t, dest_ref, lambda k, d: pltpu.make_async_copy(
                _row(o_hbm, d), _row(obuf, (2 * slot_ + k) * TM + t), sem.at[slot_]))
            for k, cp in enumerate(cps):
                cp.start(priority=k)
            return carry

        lax.fori_loop(0, TM, issue, 0, unroll=8)

    @pl.when(i == 0)
    def _():
        issue_tile(0, 0)

    @pl.when(i + 1 < N_TILES)
    def _():
        issue_tile(i + 1, 1 - slot)

    for k in range(2):
        start = pl.multiple_of((2 * slot + k) * tile_rows, tile_rows)
        pltpu.make_async_copy(o_hbm.at[pl.ds(0, tile_rows), :], obuf.at[pl.ds(start, tile_rows), :],
                              sem.at[slot]).wait()

    ri = ri_ref[...]
    g1 = ri[:, 2:3]
    g2 = ri[:, 3:4]
    base = 2 * slot * tile_rows
    moe = jnp.concatenate(
        [g1 * obuf[pl.ds(base + s, TM, stride=ROW_SUB), :] + g2 * obuf[pl.ds(base + tile_rows + s, TM, stride=ROW_SUB), :]
         for s in range(ROW_SUB)], axis=1)
    x3 = x2_ref[...] + _per_chunk(gate_ref) * moe
    y = _rmsnorm(x3, gf_ref[...])

    @pl.when(i < P_TILES)
    def _():
        yp_ref[0] = y

    @pl.when(i >= P_TILES)
    def _():
        ys_ref[...] = y.reshape(CH_PER_TILE, DEC_SEQ, D_MODEL)


def _combine(dest, o_rows, x2, rinfo, modc, g_final):
    xp_spec, xs_spec = _xp_spec(), _xs_spec()
    return pl.pallas_call(
        _combine_kernel,
        grid_spec=pltpu.PrefetchScalarGridSpec(
            num_scalar_prefetch=1,
            grid=(N_TILES,),
            in_specs=[pl.BlockSpec(memory_space=pl.ANY),
                      pl.BlockSpec((TM, D_MODEL), lambda i, *_: (i, 0)),
                      pl.BlockSpec((TM, LANES), lambda i, *_: (i, 0)),
                      _mod_spec(5),
                      pl.BlockSpec((1, D_MODEL), lambda i, *_: (0, 0))],
            out_specs=[pl.BlockSpec(xp_spec.block_shape, lambda i, *_: xp_spec.index_map(i)),
                       pl.BlockSpec(xs_spec.block_shape, lambda i, *_: xs_spec.index_map(i))],
            scratch_shapes=[pltpu.VMEM((4 * TM * ROW_SUB, LANES), F32), pltpu.SemaphoreType.DMA((2,))]),
        out_shape=[jax.ShapeDtypeStruct((BATCH, SEQ, D_MODEL), F32),
                   jax.ShapeDtypeStruct((DEC_BATCH, DEC_SEQ, D_MODEL), F32)],
        compiler_params=pltpu.CompilerParams(dimension_semantics=("arbitrary",), vmem_limit_bytes=VMEM_LIMIT),
        name="moe_combine",
    )(dest, o_rows, x2, rinfo, modc, g_final)


def _moe_plan(rt, cnt):
    experts = np.arange(N_EXPERTS)
    tile_cnt = cnt.reshape(N_TILES, SUBLANES, LANES)[:, 0, :N_EXPERTS]
    counts = jnp.sum(tile_cnt, axis=0)
    padded = jnp.ceil(counts / MOE_R) * MOE_R
    pad_end = jnp.sum(padded[:, None] * (experts[:, None] <= experts[None, :]), axis=0)
    base = pad_end - padded
    tiles = np.arange(N_TILES)
    earlier_tiles = (tiles[None, :, None] < tiles[:, None, None])
    tile_base = base[None, :] + jnp.sum(tile_cnt[None, :, :] * earlier_tiles, axis=1)
    eid = rt[0:2].reshape(1, 2, N_TILES, TM)
    rank = rt[4:6].reshape(2, N_TILES, TM)
    pick = eid == experts.astype(np.float32).reshape(N_EXPERTS, 1, 1, 1)
    dest = jnp.sum(jnp.where(pick, tile_base.T[:, None, :, None], 0.0), axis=0) + rank
    blk_start = (np.arange(MOE_BLOCKS) * MOE_R).astype(np.float32)
    blk_e = jnp.minimum(jnp.sum((blk_start[:, None] >= pad_end[None, :]).astype(F32), axis=1), N_EXPERTS - 1)
    mine = blk_e[:, None] == experts.astype(np.float32)[None, :]
    blk_fill = jnp.sum(jnp.where(mine, (counts + base)[None, :], 0.0), axis=1) - blk_start
    blk_valid = jnp.clip(blk_fill, 0, MOE_R)
    zlo = jnp.concatenate([jnp.floor((base + counts) / ZCHUNK), pad_end[-1:] / ZCHUNK])
    zhi = jnp.concatenate([pad_end / ZCHUNK, jnp.full((1,), MOE_ROWS // ZCHUNK, F32)])
    return (dest.reshape(-1).astype(jnp.int32), blk_e.astype(jnp.int32), blk_valid.astype(jnp.int32),
            zlo.astype(jnp.int32), zhi.astype(jnp.int32))


def kernel(x_prompt, x_sample, c_prompt, c_sample, cache_win_k, cache_win_v, state_conv, state_C, state_n, state_m, w_ada, b_ada, g_norm_mix, g_norm_ffn, w_in, attn_sinks, conv_w, conv_b, b_igate, b_fgate, g_mhnorm, w_out, w_router_group, b_router_group, w_router_expert, b_router_expert, w_exp_gate, w_exp_up, w_exp_down, g_final):
    l = 0
    n_streams = BATCH + DEC_BATCH
    pad_streams = -n_streams % (2 * SUBLANES)
    mod = _modulation(jnp.concatenate([c_prompt, c_sample, jnp.zeros((pad_streams, D_MODEL), F32)], axis=0),
                      w_ada[l], b_ada[l])
    chunk_stream = np.concatenate([np.repeat(np.arange(BATCH), SEQ // CHUNK), BATCH + np.arange(DEC_BATCH)])
    modc = mod[chunk_stream]

    wi = w_in[l]
    s_q, s_k, s_v, s_qkm, s_vm, s_ig, s_fg = 0, 512, 640, 768, 1792, 2304, 2308
    s_om = 2312
    w_in_r = jnp.concatenate(
        [wi[:, s_q:s_qkm], wi[:, s_qkm:s_vm], wi[:, s_vm:s_ig], wi[:, s_om:], wi[:, s_ig:s_om],
         jnp.zeros((D_MODEL, LANES - 2 * M_HEADS), F32)], axis=1).astype(BF16)
    q, kv, qkm, vm, om, gates = _inproj(x_prompt, x_sample, modc, g_norm_mix[l].reshape(1, -1), w_in_r)

    cache_kv = jnp.concatenate([cache_win_k[l].reshape(DEC_BATCH * WINDOW, A_KV_WIDTH),
                                cache_win_v[l].reshape(DEC_BATCH * WINDOW, A_KV_WIDTH)], axis=1)
    att_p, att_s = _attention(attn_sinks[l], q, kv, cache_kv)

    gbias = jnp.concatenate([b_igate[l], b_fgate[l], jnp.zeros((LANES - 2 * M_HEADS,), F32)]).reshape(1, LANES)
    common = (qkm, vm, om, gates, conv_w[l], conv_b[l].reshape(1, -1), gbias, g_mhnorm[l].reshape(1, -1))
    zeros_p = (jnp.zeros((BATCH, SUBLANES, 2 * M_WIDTH), F32),
               jnp.zeros((BATCH, M_HEADS, M_HEAD_DIM, M_HEAD_DIM), F32),
               jnp.zeros((BATCH, M_HEADS, M_HEAD_DIM), F32),
               jnp.zeros((BATCH, 1, LANES), F32))
    LP = 256
    hm_p, C_p, n_p, m_p = _mlstm(*common, *zeros_p, nb=BATCH, L=LP, nc=SEQ // LP, ns=1, row_off=0,
                                 name="mlstm_prompt")
    carry_s = jnp.concatenate([jnp.zeros((DEC_BATCH, SUBLANES - (CONV_WIDTH - 1), 2 * M_WIDTH), F32),
                               state_conv[l]], axis=1)
    m0_s = jnp.pad(state_m[l], ((0, 0), (0, LANES - M_HEADS))).reshape(DEC_BATCH, 1, LANES)
    hm_s, C_s, n_s, m_s = _mlstm(*common, carry_s, state_C[l], state_n[l], m0_s,
                                 nb=DEC_BATCH, L=DEC_SEQ, nc=1, ns=8, row_off=N_P, name="mlstm_sample")

    w_router = jnp.concatenate([w_router_group[l], w_router_expert[l],
                                jnp.zeros((D_MODEL, LANES - N_GROUPS - N_EXPERTS), F32)], axis=1)
    w_router_hi = w_router.astype(BF16)
    w_router = jnp.concatenate([w_router_hi, (w_router - w_router_hi.astype(F32)).astype(BF16)], axis=1)
    b_router =jnp.concatenate([b_router_group[l], b_router_expert[l],
                                jnp.zeros((LANES - N_GROUPS - N_EXPERTS,), F32)]).reshape(1, LANES)
    x2, h2t, rinfo, rt, cnt = _outproj(x_prompt, x_sample, att_p, att_s, hm_p, hm_s, modc,
                                       g_norm_ffn[l].reshape(1, -1), w_out[l].astype(BF16), w_router, b_router)

    dest, blk_e, blk_valid, zlo, zhi = _moe_plan(rt, cnt)
    xs_rows = _dispatch(dest, zlo, zhi, h2t)
    o_rows = _experts(blk_e, blk_valid, xs_rows, w_exp_gate[l], w_exp_up[l], w_exp_down[l])
    y_prompt, y_sample = _combine(dest, o_rows, x2, rinfo, modc, g_final.reshape(1, -1))

    kv_p = kv.reshape(N_TOK // WINDOW, WINDOW, 2 * A_KV_WIDTH)[SEQ // WINDOW - 1:N_P // WINDOW:SEQ // WINDOW]
    win_k_p = kv_p[..., :A_KV_WIDTH].reshape(1, BATCH, WINDOW, A_KV_HEADS, A_HEAD_DIM)
    win_v_p = kv_p[..., A_KV_WIDTH:].reshape(1, BATCH, WINDOW, A_KV_HEADS, A_HEAD_DIM)
    kv_s = kv[N_P:].reshape(DEC_BATCH, DEC_SEQ, 2 * A_KV_WIDTH)
    win_k_s = jnp.concatenate([cache_win_k[l][:, DEC_SEQ:],
                               kv_s[..., :A_KV_WIDTH].reshape(DEC_BATCH, DEC_SEQ, A_KV_HEADS, A_HEAD_DIM)], axis=1)[None]
    win_v_s = jnp.concatenate([cache_win_v[l][:, DEC_SEQ:],
                               kv_s[..., A_KV_WIDTH:].reshape(DEC_BATCH, DEC_SEQ, A_KV_HEADS, A_HEAD_DIM)], axis=1)[None]
    qkm_c = qkm.reshape(N_CHUNKS, CHUNK, 2 * M_WIDTH)
    tail = slice(CHUNK - (CONV_WIDTH - 1), CHUNK)
    conv_p = qkm_c[SEQ // CHUNK - 1:N_P // CHUNK:SEQ // CHUNK, tail][None]
    conv_s = qkm_c[N_P // CHUNK:, tail][None]
    return (y_prompt, y_sample,
            win_k_p, win_v_p, conv_p, C_p[None], n_p[None], m_p[:, 0, :M_HEADS][None],
            win_k_s, win_v_s, conv_s, C_s[None], n_s[None], m_s[:, 0, :M_HEADS][None])
```
